```python
import jax, jax.numpy as jnp
from jax import lax
import numpy as np

D_MODEL = 1024
BATCH = 32
SEQ = 2048
DEPTH = 2

CHUNK = 64
EPS = 1e-6
NEG_INF = -1e30

CONV_WIDTH = D_MODEL // 2
CONV_KERNEL = 31
HEAD_DIM = 64
ATTN_HEADS = D_MODEL // 128
ATTN_WIDTH = ATTN_HEADS * HEAD_DIM
LEFT_CHUNKS = 8
BAND = (LEFT_CHUNKS + 1) * CHUNK
KEY_PAD = BAND - CHUNK
MAX_REL = 256
POOL_WINDOWS = (2, 4, 8, 16)
POOL_GROUPS = len(POOL_WINDOWS)
POOL_WIDTH = D_MODEL // 2
POOL_GROUP_DIM = POOL_WIDTH // POOL_GROUPS
N_BRANCH = 3

IN_SPLITS = (CONV_WIDTH, CONV_WIDTH, CONV_WIDTH,
             ATTN_WIDTH, ATTN_WIDTH, ATTN_WIDTH, ATTN_WIDTH,
             POOL_WIDTH, POOL_WIDTH,
             N_BRANCH * D_MODEL)
IN_COLS = sum(IN_SPLITS)

kernel_name = "hybrid_conv_chunkattn_pool_gated_block"


def rms_norm(x, g):
    xf = x.astype(jnp.float32)
    y = xf * lax.rsqrt(jnp.mean(xf * xf, axis=-1, keepdims=True) + EPS)
    return (y * g.astype(jnp.float32)).astype(x.dtype)


def layer_norm(x, g, b):
    xf = x.astype(jnp.float32)
    mu = jnp.mean(xf, axis=-1, keepdims=True)
    xc = xf - mu
    y = xc * lax.rsqrt(jnp.mean(xc * xc, axis=-1, keepdims=True) + EPS)
    return (y * g.astype(jnp.float32) + b.astype(jnp.float32)).astype(x.dtype)


def conv_branch(a, b, gate, dw, dw_b, ln_g, ln_b, w_o):
    u = a * jax.nn.sigmoid(b)
    u = lax.conv_general_dilated(
        u, dw[:, None, :], window_strides=(1,),
        padding=[(CONV_KERNEL - 1, 0)],
        dimension_numbers=('NWC', 'WIO', 'NWC'),
        feature_group_count=CONV_WIDTH) + dw_b
    u = jax.nn.silu(layer_norm(u, ln_g, ln_b))
    return (u * jax.nn.silu(gate)) @ w_o


def chunk_attention_branch(q, k, v, gate, rel_table, w_o):
    B, S, _ = q.shape
    n_chunks = S // CHUNK
    qc = (q * (HEAD_DIM ** -0.5)).reshape(B, n_chunks, CHUNK, ATTN_HEADS, HEAD_DIM)
    qc = qc.transpose(1, 0, 3, 2, 4)
    kh = k.reshape(B, S, ATTN_HEADS, HEAD_DIM).transpose(0, 2, 1, 3)
    vh = v.reshape(B, S, ATTN_HEADS, HEAD_DIM).transpose(0, 2, 1, 3)
    kh = jnp.pad(kh, ((0, 0), (0, 0), (KEY_PAD, 0), (0, 0)))
    vh = jnp.pad(vh, ((0, 0), (0, 0), (KEY_PAD, 0), (0, 0)))
    rel = jnp.arange(CHUNK)[:, None] + KEY_PAD - jnp.arange(BAND)[None, :]
    bias = rel_table[:, jnp.clip(rel, -MAX_REL, MAX_REL) + MAX_REL].astype(jnp.float32)
    key_offsets = jnp.arange(BAND) - KEY_PAD

    def one_chunk(args):
        q_blk, c = args
        start = c * CHUNK
        kb = lax.dynamic_slice_in_dim(kh, start, BAND, axis=2)
        vb = lax.dynamic_slice_in_dim(vh, start, BAND, axis=2)
        s = jnp.einsum('bhqd,bhkd->bhqk', q_blk, kb).astype(jnp.float32) + bias
        valid = (start + key_offsets) >= 0
        s = jnp.where(valid, s, NEG_INF)
        p = jax.nn.softmax(s, axis=-1).astype(vb.dtype)
        return jnp.einsum('bhqk,bhkd->bhqd', p, vb)

    o = lax.map(one_chunk, (qc, jnp.arange(n_chunks)))
    o = o.transpose(1, 0, 3, 2, 4).reshape(B, S, ATTN_WIDTH)
    return (o * jax.nn.silu(gate)) @ w_o


def pool_branch(u, gate, w_grp, b_grp, scale, w_o):
    B, S, _ = u.shape
    uf = u.astype(jnp.float32)
    cs = jnp.pad(jnp.cumsum(uf, axis=1), ((0, 0), (1, 0), (0, 0)))
    t = jnp.arange(S)
    outs = []
    for g, w in enumerate(POOL_WINDOWS):
        sl = slice(g * POOL_GROUP_DIM, (g + 1) * POOL_GROUP_DIM)
        csg = cs[..., sl]
        lower = jnp.concatenate(
            [jnp.zeros((B, w - 1, POOL_GROUP_DIM), jnp.float32), csg[:, :S + 1 - w]], axis=1)
        cnt = jnp.minimum(t + 1, w).astype(jnp.float32)[None, :, None]
        outs.append((csg[:, 1:] - lower) / cnt - uf[..., sl])
    pooled = jnp.stack(outs, axis=2).astype(u.dtype)
    mixed = jnp.einsum('bsgc,gcd->bsgd', pooled, w_grp) + b_grp
    mixed = mixed.reshape(B, S, POOL_WIDTH) * scale
    return (mixed * jax.nn.silu(gate)) @ w_o


def hybrid_layer(x, pre_g, post_g, w_in, conv_dw, conv_dw_b, conv_ln_g, conv_ln_b,
                 w_conv_out, rel_bias, w_attn_out, pool_w, pool_b, pool_scale,
                 w_pool_out, w_out):
    B, S, D = x.shape
    h = rms_norm(x, pre_g)
    z = h @ w_in
    (c_a, c_b, c_gate, q, k, v, a_gate, p_in, p_gate, g_merge) = jnp.split(
        z, np.cumsum(IN_SPLITS)[:-1].tolist(), axis=-1)
    y_conv = conv_branch(c_a, c_b, c_gate, conv_dw, conv_dw_b, conv_ln_g, conv_ln_b, w_conv_out)
    y_attn = chunk_attention_branch(q, k, v, a_gate, rel_bias, w_attn_out)
    y_pool = pool_branch(p_in, p_gate, pool_w, pool_b, pool_scale, w_pool_out)
    gates = jax.nn.sigmoid(g_merge).reshape(B, S, N_BRANCH, D)
    merged = gates[:, :, 0] * y_conv + gates[:, :, 1] * y_attn + gates[:, :, 2] * y_pool
    y = merged @ w_out
    return x + rms_norm(y, post_g)


def _fwd_setup_inputs(seed: int = 0) -> dict:
    key = jax.random.key(seed)
    ks = jax.random.split(key, 20)
    n = lambda k, shape, s: jax.random.normal(k, shape, jnp.float32) * s
    L, D = DEPTH, D_MODEL
    return {
        "x": n(ks[0], (BATCH, SEQ, D), 1.0),
        "pre_norm_g": 1.0 + n(ks[1], (L, D), 0.05),
        "post_norm_g": 1.0 + n(ks[2], (L, D), 0.05),
        "w_in": n(ks[3], (L, D, IN_COLS), D ** -0.5),
        "conv_dw": n(ks[4], (L, CONV_KERNEL, CONV_WIDTH), CONV_KERNEL ** -0.5),
        "conv_dw_b": n(ks[5], (L, CONV_WIDTH), 0.02),
        "conv_ln_g": 1.0 + n(ks[6], (L, CONV_WIDTH), 0.05),
        "conv_ln_b": n(ks[7], (L, CONV_WIDTH), 0.02),
        "w_conv_out": n(ks[8], (L, CONV_WIDTH, D), CONV_WIDTH ** -0.5),
        "rel_bias": n(ks[9], (L, ATTN_HEADS, 2 * MAX_REL + 1), 0.1),
        "w_attn_out": n(ks[10], (L, ATTN_WIDTH, D), ATTN_WIDTH ** -0.5),
        "pool_w": n(ks[11], (L, POOL_GROUPS, POOL_GROUP_DIM, POOL_GROUP_DIM), POOL_GROUP_DIM ** -0.5),
        "pool_b": n(ks[12], (L, POOL_GROUPS, POOL_GROUP_DIM), 0.02),
        "pool_scale": 1.0 + n(ks[13], (L, POOL_WIDTH), 0.1),
        "w_pool_out": n(ks[14], (L, POOL_WIDTH, D), POOL_WIDTH ** -0.5),
        "w_out": n(ks[15], (L, D, D), D ** -0.5),
    }


def _fwd_reference(x, pre_norm_g, post_norm_g, w_in, conv_dw, conv_dw_b, conv_ln_g, conv_ln_b,
              w_conv_out, rel_bias, w_attn_out, pool_w, pool_b, pool_scale, w_pool_out, w_out):
    for l in range(DEPTH):
        x = hybrid_layer(x, pre_norm_g[l], post_norm_g[l], w_in[l], conv_dw[l], conv_dw_b[l],
                         conv_ln_g[l], conv_ln_b[l], w_conv_out[l], rel_bias[l], w_attn_out[l],
                         pool_w[l], pool_b[l], pool_scale[l], w_pool_out[l], w_out[l])
    return x


import jax as _jax
import jax.numpy as _jnp

TWIN_FORMAT = 'train_step'
FWD_PARAMS = ['x', 'pre_norm_g', 'post_norm_g', 'w_in', 'conv_dw', 'conv_dw_b', 'conv_ln_g', 'conv_ln_b', 'w_conv_out', 'rel_bias', 'w_attn_out', 'pool_w', 'pool_b', 'pool_scale', 'w_pool_out', 'w_out']
TWIN_WEIGHTS = ['pre_norm_g', 'post_norm_g', 'w_in', 'conv_dw', 'conv_dw_b', 'conv_ln_g', 'conv_ln_b', 'w_conv_out', 'rel_bias', 'w_attn_out', 'pool_w', 'pool_b', 'pool_scale', 'w_pool_out', 'w_out']
TWIN_DIFF_INPUT = 'x'
TWIN_INPUTS = ['x', 'pre_norm_g', 'post_norm_g', 'w_in', 'conv_dw', 'conv_dw_b', 'conv_ln_g', 'conv_ln_b', 'w_conv_out', 'rel_bias', 'w_attn_out', 'pool_w', 'pool_b', 'pool_scale', 'w_pool_out', 'w_out', 'loss_target', 'm_pre_norm_g', 'm_post_norm_g', 'm_w_in', 'm_conv_dw', 'm_conv_dw_b', 'm_conv_ln_g', 'm_conv_ln_b', 'm_w_conv_out', 'm_rel_bias', 'm_w_attn_out', 'm_pool_w', 'm_pool_b', 'm_pool_scale', 'm_w_pool_out', 'm_w_out', 'v_pre_norm_g', 'v_post_norm_g', 'v_w_in', 'v_conv_dw', 'v_conv_dw_b', 'v_conv_ln_g', 'v_conv_ln_b', 'v_w_conv_out', 'v_rel_bias', 'v_w_attn_out', 'v_pool_w', 'v_pool_b', 'v_pool_scale', 'v_w_pool_out', 'v_w_out']
TWIN_OUTPUTS = ['loss', 'grad_x', 'grad_pre_norm_g', 'grad_post_norm_g', 'grad_w_in', 'grad_conv_dw', 'grad_conv_dw_b', 'grad_conv_ln_g', 'grad_conv_ln_b', 'grad_w_conv_out', 'grad_rel_bias', 'grad_w_attn_out', 'grad_pool_w', 'grad_pool_b', 'grad_pool_scale', 'grad_w_pool_out', 'grad_w_out', 'delta_pre_norm_g', 'delta_post_norm_g', 'delta_w_in', 'delta_conv_dw', 'delta_conv_dw_b', 'delta_conv_ln_g', 'delta_conv_ln_b', 'delta_w_conv_out', 'delta_rel_bias', 'delta_w_attn_out', 'delta_pool_w', 'delta_pool_b', 'delta_pool_scale', 'delta_w_pool_out', 'delta_w_out', 'new_m_pre_norm_g', 'new_m_post_norm_g', 'new_m_w_in', 'new_m_conv_dw', 'new_m_conv_dw_b', 'new_m_conv_ln_g', 'new_m_conv_ln_b', 'new_m_w_conv_out', 'new_m_rel_bias', 'new_m_w_attn_out', 'new_m_pool_w', 'new_m_pool_b', 'new_m_pool_scale', 'new_m_w_pool_out', 'new_m_w_out', 'new_v_pre_norm_g', 'new_v_post_norm_g', 'new_v_w_in', 'new_v_conv_dw', 'new_v_conv_dw_b', 'new_v_conv_ln_g', 'new_v_conv_ln_b', 'new_v_w_conv_out', 'new_v_rel_bias', 'new_v_w_attn_out', 'new_v_pool_w', 'new_v_pool_b', 'new_v_pool_scale', 'new_v_w_pool_out', 'new_v_w_out']
TWIN_LEAF_KINDS = {'loss': 'loss', 'grad_x': 'grad_x', 'grad_pre_norm_g': 'grad_w', 'grad_post_norm_g': 'grad_w', 'grad_w_in': 'grad_w', 'grad_conv_dw': 'grad_w', 'grad_conv_dw_b': 'grad_w', 'grad_conv_ln_g': 'grad_w', 'grad_conv_ln_b': 'grad_w', 'grad_w_conv_out': 'grad_w', 'grad_rel_bias': 'grad_w', 'grad_w_attn_out': 'grad_w', 'grad_pool_w': 'grad_w', 'grad_pool_b': 'grad_w', 'grad_pool_scale': 'grad_w', 'grad_w_pool_out': 'grad_w', 'grad_w_out': 'grad_w', 'delta_pre_norm_g': 'delta_w', 'delta_post_norm_g': 'delta_w', 'delta_w_in': 'delta_w', 'delta_conv_dw': 'delta_w', 'delta_conv_dw_b': 'delta_w', 'delta_conv_ln_g': 'delta_w', 'delta_conv_ln_b': 'delta_w', 'delta_w_conv_out': 'delta_w', 'delta_rel_bias': 'delta_w', 'delta_w_attn_out': 'delta_w', 'delta_pool_w': 'delta_w', 'delta_pool_b': 'delta_w', 'delta_pool_scale': 'delta_w', 'delta_w_pool_out': 'delta_w', 'delta_w_out': 'delta_w', 'new_m_pre_norm_g': 'new_m', 'new_m_post_norm_g': 'new_m', 'new_m_w_in': 'new_m', 'new_m_conv_dw': 'new_m', 'new_m_conv_dw_b': 'new_m', 'new_m_conv_ln_g': 'new_m', 'new_m_conv_ln_b': 'new_m', 'new_m_w_conv_out': 'new_m', 'new_m_rel_bias': 'new_m', 'new_m_w_attn_out': 'new_m', 'new_m_pool_w': 'new_m', 'new_m_pool_b': 'new_m', 'new_m_pool_scale': 'new_m', 'new_m_w_pool_out': 'new_m', 'new_m_w_out': 'new_m', 'new_v_pre_norm_g': 'new_v', 'new_v_post_norm_g': 'new_v', 'new_v_w_in': 'new_v', 'new_v_conv_dw': 'new_v', 'new_v_conv_dw_b': 'new_v', 'new_v_conv_ln_g': 'new_v', 'new_v_conv_ln_b': 'new_v', 'new_v_w_conv_out': 'new_v', 'new_v_rel_bias': 'new_v', 'new_v_w_attn_out': 'new_v', 'new_v_pool_w': 'new_v', 'new_v_pool_b': 'new_v', 'new_v_pool_scale': 'new_v', 'new_v_w_pool_out': 'new_v', 'new_v_w_out': 'new_v'}


def _forward(args):
    return _fwd_reference(*[args[k] for k in FWD_PARAMS])


def _output_shape():
    out = _jax.eval_shape(lambda: _forward(_fwd_setup_inputs(0)))
    return out.shape, out.dtype

N_MICROBATCH = 1
ADAM_LR = 0.001
ADAM_B1 = 0.9
ADAM_B2 = 0.999
ADAM_EPS = 1e-08
ADAM_WD = 0.01
ADAM_STEP = 10
PER_EXAMPLE_BATCH_AXIS = {'x': 0, 'loss_target': 0}
SHARED_INPUTS = []
_WEIGHT_DTYPES = {'pre_norm_g': _jnp.float32, 'post_norm_g': _jnp.float32, 'w_in': _jnp.float32, 'conv_dw': _jnp.float32, 'conv_dw_b': _jnp.float32, 'conv_ln_g': _jnp.float32, 'conv_ln_b': _jnp.float32, 'w_conv_out': _jnp.float32, 'rel_bias': _jnp.float32, 'w_attn_out': _jnp.float32, 'pool_w': _jnp.float32, 'pool_b': _jnp.float32, 'pool_scale': _jnp.float32, 'w_pool_out': _jnp.float32, 'w_out': _jnp.float32}
MOMENT_SCALE = {'pre_norm_g': 9.112725e-01, 'post_norm_g': 6.397581e+01, 'w_in': 3.260620e-01, 'conv_dw': 4.504938e-01, 'conv_dw_b': 1.653812e+00, 'conv_ln_g': 8.230135e-01, 'conv_ln_b': 1.229437e+00, 'w_conv_out': 3.967464e-01, 'rel_bias': 3.671953e-02, 'w_attn_out': 8.136952e-02, 'pool_w': 7.619684e-01, 'pool_b': 2.259443e+00, 'pool_scale': 8.095866e-01, 'w_pool_out': 5.633484e-01, 'w_out': 7.152948e-01}


def _to_microbatches(a, axis):
    t = _jnp.moveaxis(a, axis, 0)
    t = t.reshape((N_MICROBATCH, t.shape[0] // N_MICROBATCH) + t.shape[1:])
    return _jnp.moveaxis(t, 1, axis + 1)


def setup_inputs(seed: int = 0) -> dict:
    inp = _fwd_setup_inputs(seed)
    key = _jax.random.fold_in(_jax.random.key(seed), 7919)
    shape, _ = _output_shape()
    out = dict(inp)
    out["loss_target"] = _jax.random.normal(_jax.random.fold_in(key, 0), shape, _jnp.float32)
    for i, name in enumerate(TWIN_WEIGHTS):
        w = inp[name].astype(_jnp.float32)
        if MOMENT_SCALE is None:
            s = _jnp.sqrt(_jnp.mean(_jnp.square(w)) + 1e-30)
        else:
            s = MOMENT_SCALE[name]
        km, kv = _jax.random.split(_jax.random.fold_in(key, i + 1))
        out[name] = w
        out["m_" + name] = s * _jax.random.normal(km, w.shape, _jnp.float32)
        out["v_" + name] = (s * s) * _jax.random.uniform(kv, w.shape, _jnp.float32, 0.5, 1.5)
    if N_MICROBATCH > 1:
        for name, axis in PER_EXAMPLE_BATCH_AXIS.items():
            out[name] = _to_microbatches(out[name], axis)
    return {'x': out['x'], 'pre_norm_g': out['pre_norm_g'], 'post_norm_g': out['post_norm_g'], 'w_in': out['w_in'], 'conv_dw': out['conv_dw'], 'conv_dw_b': out['conv_dw_b'], 'conv_ln_g': out['conv_ln_g'], 'conv_ln_b': out['conv_ln_b'], 'w_conv_out': out['w_conv_out'], 'rel_bias': out['rel_bias'], 'w_attn_out': out['w_attn_out'], 'pool_w': out['pool_w'], 'pool_b': out['pool_b'], 'pool_scale': out['pool_scale'], 'w_pool_out': out['w_pool_out'], 'w_out': out['w_out'], 'loss_target': out['loss_target'], 'm_pre_norm_g': out['m_pre_norm_g'], 'm_post_norm_g': out['m_post_norm_g'], 'm_w_in': out['m_w_in'], 'm_conv_dw': out['m_conv_dw'], 'm_conv_dw_b': out['m_conv_dw_b'], 'm_conv_ln_g': out['m_conv_ln_g'], 'm_conv_ln_b': out['m_conv_ln_b'], 'm_w_conv_out': out['m_w_conv_out'], 'm_rel_bias': out['m_rel_bias'], 'm_w_attn_out': out['m_w_attn_out'], 'm_pool_w': out['m_pool_w'], 'm_pool_b': out['m_pool_b'], 'm_pool_scale': out['m_pool_scale'], 'm_w_pool_out': out['m_w_pool_out'], 'm_w_out': out['m_w_out'], 'v_pre_norm_g': out['v_pre_norm_g'], 'v_post_norm_g': out['v_post_norm_g'], 'v_w_in': out['v_w_in'], 'v_conv_dw': out['v_conv_dw'], 'v_conv_dw_b': out['v_conv_dw_b'], 'v_conv_ln_g': out['v_conv_ln_g'], 'v_conv_ln_b': out['v_conv_ln_b'], 'v_w_conv_out': out['v_w_conv_out'], 'v_rel_bias': out['v_rel_bias'], 'v_w_attn_out': out['v_w_attn_out'], 'v_pool_w': out['v_pool_w'], 'v_pool_b': out['v_pool_b'], 'v_pool_scale': out['v_pool_scale'], 'v_w_pool_out': out['v_w_pool_out'], 'v_w_out': out['v_w_out']}


def _loss(weights, diff, rest, loss_target):
    with _jax.named_scope("forward"):
        args = {**rest, TWIN_DIFF_INPUT: diff, **{k: w.astype(_WEIGHT_DTYPES[k]) for k, w in weights.items()}}
        y = _forward(args)
    with _jax.named_scope("loss_head"):
        err = _jnp.square(y.astype(_jnp.float32) - loss_target)
        return 0.5 * _jnp.sum(_jnp.mean(err, axis=-1)) if err.ndim else 0.5 * err


def _adamw(w, g, m, v):
    m = ADAM_B1 * m + (1.0 - ADAM_B1) * g
    v = ADAM_B2 * v + (1.0 - ADAM_B2) * _jnp.square(g)
    m_hat = m / (1.0 - ADAM_B1 ** ADAM_STEP)
    v_hat = v / (1.0 - ADAM_B2 ** ADAM_STEP)
    delta = -ADAM_LR * (m_hat / (_jnp.sqrt(v_hat) + ADAM_EPS) + ADAM_WD * w)
    return delta, m, v


def reference(x, pre_norm_g, post_norm_g, w_in, conv_dw, conv_dw_b, conv_ln_g, conv_ln_b, w_conv_out, rel_bias, w_attn_out, pool_w, pool_b, pool_scale, w_pool_out, w_out, loss_target, m_pre_norm_g, m_post_norm_g, m_w_in, m_conv_dw, m_conv_dw_b, m_conv_ln_g, m_conv_ln_b, m_w_conv_out, m_rel_bias, m_w_attn_out, m_pool_w, m_pool_b, m_pool_scale, m_w_pool_out, m_w_out, v_pre_norm_g, v_post_norm_g, v_w_in, v_conv_dw, v_conv_dw_b, v_conv_ln_g, v_conv_ln_b, v_w_conv_out, v_rel_bias, v_w_attn_out, v_pool_w, v_pool_b, v_pool_scale, v_w_pool_out, v_w_out):
    given = dict(x=x, pre_norm_g=pre_norm_g, post_norm_g=post_norm_g, w_in=w_in, conv_dw=conv_dw, conv_dw_b=conv_dw_b, conv_ln_g=conv_ln_g, conv_ln_b=conv_ln_b, w_conv_out=w_conv_out, rel_bias=rel_bias, w_attn_out=w_attn_out, pool_w=pool_w, pool_b=pool_b, pool_scale=pool_scale, w_pool_out=w_pool_out, w_out=w_out, loss_target=loss_target, m_pre_norm_g=m_pre_norm_g, m_post_norm_g=m_post_norm_g, m_w_in=m_w_in, m_conv_dw=m_conv_dw, m_conv_dw_b=m_conv_dw_b, m_conv_ln_g=m_conv_ln_g, m_conv_ln_b=m_conv_ln_b, m_w_conv_out=m_w_conv_out, m_rel_bias=m_rel_bias, m_w_attn_out=m_w_attn_out, m_pool_w=m_pool_w, m_pool_b=m_pool_b, m_pool_scale=m_pool_scale, m_w_pool_out=m_w_pool_out, m_w_out=m_w_out, v_pre_norm_g=v_pre_norm_g, v_post_norm_g=v_post_norm_g, v_w_in=v_w_in, v_conv_dw=v_conv_dw, v_conv_dw_b=v_conv_dw_b, v_conv_ln_g=v_conv_ln_g, v_conv_ln_b=v_conv_ln_b, v_w_conv_out=v_w_conv_out, v_rel_bias=v_rel_bias, v_w_attn_out=v_w_attn_out, v_pool_w=v_pool_w, v_pool_b=v_pool_b, v_pool_scale=v_pool_scale, v_w_pool_out=v_w_pool_out, v_w_out=v_w_out)
    weights = {n: given[n] for n in TWIN_WEIGHTS}
    shared = {n: given[n] for n in SHARED_INPUTS}
    per_example = {n: given[n] for n in ['x']}
    grad_fn = _jax.value_and_grad(_loss, argnums=(0, 1))

    def one_microbatch(ex, loss_target):
        ex = dict(ex)
        diff = ex.pop(TWIN_DIFF_INPUT)
        return grad_fn(weights, diff, {**shared, **ex}, loss_target)

    if N_MICROBATCH == 1:
        loss, (grad_w, grad_x) = one_microbatch(per_example, given["loss_target"])
    else:
        def body(carry, xs):
            loss_sum, grad_sum = carry
            l_k, (gw_k, gx_k) = one_microbatch(xs[0], xs[1])
            with _jax.named_scope("update"):
                return (loss_sum + l_k, _jax.tree.map(_jnp.add, grad_sum, gw_k)), gx_k

        init = (_jnp.zeros((), _jnp.float32), _jax.tree.map(_jnp.zeros_like, weights))
        (loss, grad_w), grad_x = _jax.lax.scan(body, init, (per_example, given["loss_target"]))
    with _jax.named_scope("update"):
        delta_w, new_m, new_v = {}, {}, {}
        for n in TWIN_WEIGHTS:
            delta_w[n], new_m[n], new_v[n] = _adamw(weights[n], grad_w[n], given["m_" + n], given["v_" + n])
    return (loss, grad_x, *[grad_w[n] for n in TWIN_WEIGHTS], *[delta_w[n] for n in TWIN_WEIGHTS],
            *[new_m[n] for n in TWIN_WEIGHTS], *[new_v[n] for n in TWIN_WEIGHTS])
```

```python
import functools

import numpy as np
import jax
import jax.numpy as jnp
from jax import lax
from jax.experimental import pallas as pl
from jax.experimental.pallas import tpu as pltpu

F32 = jnp.float32
BF16 = jnp.bfloat16
SDS = jax.ShapeDtypeStruct

D = 1024
BW = 512
NCOL = 7680
EPS = 1e-6
NEG = -1e30
HEADS = 8
HD = 64
CHUNK = 64
LEFT = 8
MAX_REL = 256
TQ = 256
KW = 768
CONV_K = 31
WINDOWS = (2, 4, 8, 16)
HALO = 32
RC = 32
N_DEV = 8
MESH_AXES = ("x", "y", "c")

ADAM_LR = 0.001
ADAM_B1 = 0.9
ADAM_B2 = 0.999
ADAM_EPS = 1e-08
ADAM_WD = 0.01
ADAM_STEP = 10

VMEM_LIMIT = 56 * 1024 * 1024

CB_CA, CB_CB, CB_CG, CB_Q, CB_K, CB_V, CB_AG, CB_PI, CB_PG = range(9)
DZ_PIECES = (("conv", 1536), ("q", 512), ("k", 512), ("v", 512), ("ag", 512), ("pool", 1024), ("gm", 3072))


def _cparams(sem):
    return pltpu.CompilerParams(dimension_semantics=sem, vmem_limit_bytes=VMEM_LIMIT)


def _sig(x):
    return 1.0 / (1.0 + jnp.exp(-x))


def _dsilu(x, s):
    return s * (1.0 + x * (1.0 - s))


def _nt(a, b):
    return lax.dot_general(a, b, (((1,), (1,)), ((), ())), preferred_element_type=F32)


def _tn(a, b):
    return lax.dot_general(a, b, (((0,), (0,)), ((), ())), preferred_element_type=F32)


def _nn(a, b):
    return jnp.dot(a, b, preferred_element_type=F32)


def _rows8(x):
    return x[0:8] + x[8:16] + x[16:24] + x[24:32]


def _in_proj(x, g, w):
    T = x.shape[0]
    tm = min(T, 1024)
    tn = 1536

    def body(x_ref, g_ref, w_ref, z_ref, ht_ref, h_scr):
        @pl.when(pl.program_id(1) == 0)
        def _():
            xv = x_ref[...]
            r = lax.rsqrt(jnp.mean(xv * xv, axis=-1, keepdims=True) + EPS)
            h = xv * r * g_ref[...]
            h_scr[...] = h.astype(BF16)
            ht_ref[...] = h.T.astype(BF16)
        z_ref[...] = _nn(h_scr[...], w_ref[...]).astype(BF16)

    return pl.pallas_call(
        body, name="in_proj", grid=(T // tm, NCOL // tn),
        in_specs=[pl.BlockSpec((tm, D), lambda i, j: (i, 0)),
                  pl.BlockSpec((1, D), lambda i, j: (0, 0)),
                  pl.BlockSpec((D, tn), lambda i, j: (0, j))],
        out_specs=[pl.BlockSpec((tm, tn), lambda i, j: (i, j)),
                   pl.BlockSpec((D, tm), lambda i, j: (0, i))],
        out_shape=[SDS((T, NCOL), BF16), SDS((D, T), BF16)],
        scratch_shapes=[pltpu.VMEM((tm, D), BF16)],
        compiler_params=_cparams(("parallel", "arbitrary")),
    )(x, g, w)


def _conv_taps(win, dw_ref, lo):
    acc = None
    for j in range(CONV_K):
        o = lo(j)
        term = dw_ref[j:j + 1, :] * win[o:o + RC, :]
        acc = term if acc is None else acc + term
    return acc


def _layer_norm_fwd(u1):
    mu = jnp.mean(u1, axis=-1, keepdims=True)
    xc = u1 - mu
    rstd = lax.rsqrt(jnp.mean(xc * xc, axis=-1, keepdims=True) + EPS)
    return xc * rstd, rstd


def _pool_chunk(pwin, t_first):
    t = t_first + lax.broadcasted_iota(jnp.int32, (RC, 128), 0)
    outs = []
    for g, w in enumerate(WINDOWS):
        sl = slice(g * 128, (g + 1) * 128)
        s = pwin[HALO:HALO + RC, sl]
        for k in range(1, w):
            s = s + pwin[HALO - k:HALO - k + RC, sl]
        cnt = jnp.minimum(t + 1, w).astype(F32)
        outs.append(s / cnt - pwin[HALO:HALO + RC, sl])
    return outs


def _mix_fwd(z, dw, dwb, lng, lnb, pw, pb, ps, S):
    T = z.shape[0]
    tm = 512
    ts = S // tm
    nh = tm // HALO

    def body(ca_ref, cb_ref, cg_ref, pi_ref, pg_ref, cah_ref, cbh_ref, pih_ref,
             dw_ref, dwb_ref, lng_ref, lnb_ref, pw_ref, pb_ref, ps_ref,
             ac_ref, ap_ref, ubuf, pbuf, pooled):
        i = pl.program_id(0)
        keep = jnp.where((i % ts) == 0, 0.0, 1.0)
        ubuf[0:HALO, :] = cah_ref[...].astype(F32) * _sig(cbh_ref[...].astype(F32)) * keep
        ubuf[HALO:HALO + tm, :] = ca_ref[...].astype(F32) * _sig(cb_ref[...].astype(F32))
        pbuf[0:HALO, :] = pih_ref[...].astype(F32) * keep
        pbuf[HALO:HALO + tm, :] = pi_ref[...].astype(F32)
        t0 = (i % ts) * tm

        def chunk(c, carry):
            base = pl.multiple_of(c * RC, RC)
            win = ubuf[pl.ds(base, RC + HALO), :]
            u1 = _conv_taps(win, dw_ref, lambda j: 2 + j) + dwb_ref[...]
            n, _ = _layer_norm_fwd(u1)
            u2 = n * lng_ref[...] + lnb_ref[...]
            u3 = u2 * _sig(u2)
            cg = cg_ref[pl.ds(base, RC), :].astype(F32)
            ac_ref[pl.ds(base, RC), :] = (u3 * cg * _sig(cg)).astype(BF16)
            pwin = pbuf[pl.ds(base, RC + HALO), :]
            outs = _pool_chunk(pwin, t0 + base)
            for g in range(4):
                pooled[pl.ds(base, RC), g * 128:(g + 1) * 128] = outs[g].astype(BF16)
            return carry

        lax.fori_loop(0, tm // RC, chunk, 0)
        pg = pg_ref[...].astype(F32)
        spg = pg * _sig(pg)
        for g in range(4):
            sl = slice(g * 128, (g + 1) * 128)
            mixed = (_nn(pooled[:, sl], pw_ref[g]) + pb_ref[:, sl]) * ps_ref[:, sl]
            ap_ref[:, sl] = (mixed * spg[:, sl]).astype(BF16)

    def zmain(cb):
        return pl.BlockSpec((tm, BW), lambda i: (i, cb))

    def zprev(cb):
        return pl.BlockSpec((HALO, BW), lambda i: (jnp.maximum(i * nh - 1, 0), cb))

    full = lambda shape: pl.BlockSpec(shape, lambda i: (0,) * len(shape))
    return pl.pallas_call(
        body, name="mix_fwd", grid=(T // tm,),
        in_specs=[zmain(CB_CA), zmain(CB_CB), zmain(CB_CG), zmain(CB_PI), zmain(CB_PG),
                  zprev(CB_CA), zprev(CB_CB), zprev(CB_PI),
                  full((CONV_K, BW)), full((1, BW)), full((1, BW)), full((1, BW)),
                  full((4, 128, 128)), full((1, BW)), full((1, BW))],
        out_specs=[pl.BlockSpec((tm, BW), lambda i: (i, 0)), pl.BlockSpec((tm, BW), lambda i: (i, 0))],
        out_shape=[SDS((T, BW), BF16), SDS((T, BW), BF16)],
        scratch_shapes=[pltpu.VMEM((HALO + tm, BW), F32), pltpu.VMEM((HALO + tm, BW), F32),
                        pltpu.VMEM((tm, BW), BF16)],
        compiler_params=_cparams(("parallel",)),
    )(z, z, z, z, z, z, z, z, dw, dwb, lng, lnb, pw, pb, ps)


def _attn_specs(nq):
    def kv(cb, off):
        return pl.BlockSpec((TQ, BW), lambda i: (i - jnp.minimum(off, i % nq), cb))
    return [pl.BlockSpec((TQ, BW), lambda i: (i, CB_Q)),
            kv(CB_K, 2), kv(CB_K, 1), kv(CB_K, 0), kv(CB_V, 2), kv(CB_V, 1), kv(CB_V, 0)]


def _softmax_rows(s):
    m = jnp.max(s, axis=-1, keepdims=True)
    e = jnp.exp(s - m)
    return e / jnp.sum(e, axis=-1, keepdims=True)


def _attn_fwd(z, bias, S):
    T = z.shape[0]
    nq = S // TQ

    def body(q_ref, k2_ref, k1_ref, k0_ref, v2_ref, v1_ref, v0_ref, b_ref, o_ref, kbuf, vbuf):
        qb = pl.program_id(0) % nq
        kbuf[0:TQ, :] = k2_ref[...]
        kbuf[TQ:2 * TQ, :] = k1_ref[...]
        kbuf[2 * TQ:KW, :] = k0_ref[...]
        vbuf[0:TQ, :] = v2_ref[...]
        vbuf[TQ:2 * TQ, :] = v1_ref[...]
        vbuf[2 * TQ:KW, :] = v0_ref[...]
        pad = lax.broadcasted_iota(jnp.int32, (TQ, KW), 1) < (2 - qb) * TQ
        lane = lax.broadcasted_iota(jnp.int32, (1, 128), 1)
        for hp in range(HEADS // 2):
            sl = slice(hp * 128, (hp + 1) * 128)
            qp = q_ref[:, sl] * 0.125
            kp = kbuf[:, sl]
            vp = vbuf[:, sl]
            acc = None
            for e in range(2):
                msk = (lane < HD) if e == 0 else (lane >= HD)
                qm = jnp.where(msk, qp, jnp.zeros_like(qp))
                s = _nt(qm, kp) + b_ref[2 * hp + e]
                p = _softmax_rows(jnp.where(pad, NEG, s)).astype(BF16)
                vm = jnp.where(msk, vp, jnp.zeros_like(vp))
                o = _nn(p, vm)
                acc = o if acc is None else acc + o
            o_ref[:, sl] = acc.astype(BF16)

    return pl.pallas_call(
        body, name="attn_fwd", grid=(T // TQ,),
        in_specs=_attn_specs(nq) + [pl.BlockSpec((HEADS, TQ, KW), lambda i: (0, 0, 0))],
        out_specs=pl.BlockSpec((TQ, BW), lambda i: (i, 0)),
        out_shape=SDS((T, BW), BF16),
        scratch_shapes=[pltpu.VMEM((KW, BW), BF16), pltpu.VMEM((KW, BW), BF16)],
        compiler_params=_cparams(("parallel",)),
    )(z, z, z, z, z, z, z, bias)


def _gates(gl_ref, gh_ref):
    gl = _sig(gl_ref[...].astype(F32))
    gh = _sig(gh_ref[...].astype(F32))
    return (gl[:, 0:D], jnp.concatenate([gl[:, D:1536], gh[:, 0:512]], axis=1), gh[:, 512:1536])


def _out_specs_in(tm):
    row = lambda w: pl.BlockSpec((tm, w), lambda i: (i, 0))
    full = lambda shape: pl.BlockSpec(shape, lambda i: (0,) * len(shape))
    return [row(BW), row(BW), row(BW),
            pl.BlockSpec((tm, BW), lambda i: (i, CB_AG)),
            pl.BlockSpec((tm, 1536), lambda i: (i, 3)),
            pl.BlockSpec((tm, 1536), lambda i: (i, 4)),
            full((BW, D)), full((BW, D)), full((BW, D)), full((D, D)), full((1, D))]


def _out_fwd(x, ac, o, ap, z, wco, wao, wpo, wout, postg):
    T = x.shape[0]
    tm = 512

    def body(ac_ref, o_ref, ap_ref, ag_ref, gl_ref, gh_ref, wco_ref, wao_ref, wpo_ref, wout_ref, pg_ref,
             x_ref, out_ref):
        ag = ag_ref[...].astype(F32)
        aat = (o_ref[...].astype(F32) * ag * _sig(ag)).astype(BF16)
        g0, g1, g2 = _gates(gl_ref, gh_ref)
        merged = g0 * _nn(ac_ref[...], wco_ref[...])
        merged = merged + g1 * _nn(aat, wao_ref[...])
        merged = merged + g2 * _nn(ap_ref[...], wpo_ref[...])
        y = _nn(merged.astype(BF16), wout_ref[...])
        ry = lax.rsqrt(jnp.mean(y * y, axis=-1, keepdims=True) + EPS)
        out_ref[...] = x_ref[...] + y * ry * pg_ref[...]

    return pl.pallas_call(
        body, name="out_fwd", grid=(T // tm,),
        in_specs=_out_specs_in(tm) + [pl.BlockSpec((tm, D), lambda i: (i, 0))],
        out_specs=pl.BlockSpec((tm, D), lambda i: (i, 0)),
        out_shape=SDS((T, D), F32),
        compiler_params=_cparams(("parallel",)),
    )(ac, o, ap, z, z, z, wco, wao, wpo, wout, postg, x)


def _loss_grad(out, tgt):
    T = out.shape[0]
    tm = 1024 if T % 1024 == 0 else T

    def body(o_ref, t_ref, l_ref, d_ref):
        @pl.when(pl.program_id(0) == 0)
        def _():
            l_ref[...] = jnp.zeros_like(l_ref)
        d = o_ref[...] - t_ref[...]
        d_ref[...] = d * (1.0 / D)
        sq = jnp.sum(d * d, axis=0, keepdims=True)
        l_ref[...] += jnp.sum(sq, axis=1, keepdims=True)

    return pl.pallas_call(
        body, name="loss_grad", grid=(T // tm,),
        in_specs=[pl.BlockSpec((tm, D), lambda i: (i, 0)), pl.BlockSpec((tm, D), lambda i: (i, 0))],
        out_specs=[pl.BlockSpec((1, 128), lambda i: (0, 0)), pl.BlockSpec((tm, D), lambda i: (i, 0))],
        out_shape=[SDS((1, 128), F32), SDS((T, D), F32)],
        compiler_params=_cparams(("arbitrary",)),
    )(out, tgt)


def _out_bwd(dout, ac, o, ap, z, wco, wao, wpo, wout, postg):
    T = dout.shape[0]
    tm = 256

    def body(ac_ref, o_ref, ap_ref, ag_ref, gl_ref, gh_ref, wco_ref, wao_ref, wpo_ref, wout_ref, pg_ref,
             do_ref,
             dac_ref, dao_ref, dag_ref, dap_ref, dgm_ref, dwco_ref, dwao_ref, dwpo_ref, dwout_ref, dpg_ref):
        @pl.when(pl.program_id(0) == 0)
        def _():
            for r in (dwco_ref, dwao_ref, dwpo_ref, dwout_ref, dpg_ref):
                r[...] = jnp.zeros_like(r)

        ag = ag_ref[...].astype(F32)
        sag = _sig(ag)
        ov = o_ref[...].astype(F32)
        acts = (ac_ref[...], (ov * ag * sag).astype(BF16), ap_ref[...])
        ws = (wco_ref, wao_ref, wpo_ref)
        gates = _gates(gl_ref, gh_ref)
        ys = [_nn(acts[b], ws[b][...]) for b in range(3)]
        merged = (gates[0] * ys[0] + gates[1] * ys[1] + gates[2] * ys[2]).astype(BF16)
        y = _nn(merged, wout_ref[...])
        ry = lax.rsqrt(jnp.mean(y * y, axis=-1, keepdims=True) + EPS)
        yn = y * ry
        dout_v = do_ref[...]
        dpg_ref[...] += jnp.sum(dout_v * yn, axis=0, keepdims=True)
        dyn = dout_v * pg_ref[...]
        dy = (ry * (dyn - yn * jnp.mean(dyn * yn, axis=-1, keepdims=True))).astype(BF16)
        dmerged = _nt(dy, wout_ref[...])
        dwout_ref[...] += _tn(merged, dy)
        dws = (dwco_ref, dwao_ref, dwpo_ref)
        das = []
        for b in range(3):
            gb = gates[b]
            dgm_ref[:, b * D:(b + 1) * D] = (dmerged * ys[b] * gb * (1.0 - gb)).astype(BF16)
            dyb = (dmerged * gb).astype(BF16)
            dws[b][...] += _tn(acts[b], dyb)
            das.append(_nt(dyb, ws[b][...]))
        dac_ref[...] = das[0].astype(BF16)
        dap_ref[...] = das[2].astype(BF16)
        dao_ref[...] = (das[1] * ag * sag).astype(BF16)
        dag_ref[...] = (das[1] * ov * _dsilu(ag, sag)).astype(BF16)

    row = lambda w: pl.BlockSpec((tm, w), lambda i: (i, 0))
    full = lambda shape: pl.BlockSpec(shape, lambda i: (0,) * len(shape))
    return pl.pallas_call(
        body, name="out_bwd", grid=(T // tm,),
        in_specs=_out_specs_in(tm) + [row(D)],
        out_specs=[row(BW), row(BW), row(BW), row(BW), row(3 * D),
                   full((BW, D)), full((BW, D)), full((BW, D)), full((D, D)), full((1, D))],
        out_shape=[SDS((T, BW), BF16)] * 4 + [SDS((T, 3 * D), BF16)]
                  + [SDS((BW, D), F32)] * 3 + [SDS((D, D), F32), SDS((1, D), F32)],
        compiler_params=_cparams(("arbitrary",)),
    )(ac, o, ap, z, z, z, wco, wao, wpo, wout, postg, dout)


def _attn_bwd(z, dao, bias, S):
    T = z.shape[0]
    nq = S // TQ

    def body(q_ref, k2_ref, k1_ref, k0_ref, v2_ref, v1_ref, v0_ref, do_ref, b_ref,
             dq_ref, dk_ref, dv_ref, db_ref, kbuf, vbuf, dkacc, dvacc):
        i = pl.program_id(0)
        qb = i % nq

        @pl.when(i == 0)
        def _():
            db_ref[...] = jnp.zeros_like(db_ref)

        @pl.when(qb == 0)
        def _():
            dkacc[...] = jnp.zeros_like(dkacc)
            dvacc[...] = jnp.zeros_like(dvacc)

        kbuf[0:TQ, :] = k2_ref[...]
        kbuf[TQ:2 * TQ, :] = k1_ref[...]
        kbuf[2 * TQ:KW, :] = k0_ref[...]
        vbuf[0:TQ, :] = v2_ref[...]
        vbuf[TQ:2 * TQ, :] = v1_ref[...]
        vbuf[2 * TQ:KW, :] = v0_ref[...]
        pad = lax.broadcasted_iota(jnp.int32, (TQ, KW), 1) < (2 - qb) * TQ
        lane = lax.broadcasted_iota(jnp.int32, (1, 128), 1)
        row0 = pl.multiple_of(qb * TQ, TQ)
        for hp in range(HEADS // 2):
            sl = slice(hp * 128, (hp + 1) * 128)
            qp = q_ref[:, sl] * 0.125
            kp = kbuf[:, sl]
            vp = vbuf[:, sl]
            dop = do_ref[:, sl]
            dq_acc = dk_acc = dv_acc = None
            for e in range(2):
                h = 2 * hp + e
                msk = (lane < HD) if e == 0 else (lane >= HD)
                qm = jnp.where(msk, qp, jnp.zeros_like(qp))
                dom = jnp.where(msk, dop, jnp.zeros_like(dop))
                km = jnp.where(msk, kp, jnp.zeros_like(kp))
                s = _nt(qm, kp) + b_ref[h]
                p = _softmax_rows(jnp.where(pad, NEG, s))
                dp = _nt(dom, vp)
                ds = p * (dp - jnp.sum(p * dp, axis=-1, keepdims=True))
                db_ref[h] += ds
                dsb = ds.astype(BF16)
                dq_h = _nn(dsb, km) * 0.125
                dk_h = _tn(dsb, qm)
                dv_h = _tn(p.astype(BF16), dom)
                dq_acc = dq_h if dq_acc is None else dq_acc + dq_h
                dk_acc = dk_h if dk_acc is None else dk_acc + dk_h
                dv_acc = dv_h if dv_acc is None else dv_acc + dv_h
            dq_ref[:, sl] = dq_acc.astype(BF16)
            dkacc[pl.ds(row0, KW), sl] += dk_acc
            dvacc[pl.ds(row0, KW), sl] += dv_acc

        @pl.when(qb == nq - 1)
        def _():
            dk_ref[...] = dkacc[2 * TQ:2 * TQ + S, :].astype(BF16)
            dv_ref[...] = dvacc[2 * TQ:2 * TQ + S, :].astype(BF16)

    return pl.pallas_call(
        body, name="attn_bwd", grid=(T // TQ,),
        in_specs=_attn_specs(nq) + [pl.BlockSpec((TQ, BW), lambda i: (i, 0)),
                                    pl.BlockSpec((HEADS, TQ, KW), lambda i: (0, 0, 0))],
        out_specs=[pl.BlockSpec((TQ, BW), lambda i: (i, 0)),
                   pl.BlockSpec((S, BW), lambda i: (i // nq, 0)),
                   pl.BlockSpec((S, BW), lambda i: (i // nq, 0)),
                   pl.BlockSpec((HEADS, TQ, KW), lambda i: (0, 0, 0))],
        out_shape=[SDS((T, BW), BF16)] * 3 + [SDS((HEADS, TQ, KW), F32)],
        scratch_shapes=[pltpu.VMEM((KW, BW), BF16), pltpu.VMEM((KW, BW), BF16),
                        pltpu.VMEM((S + 2 * TQ, BW), F32), pltpu.VMEM((S + 2 * TQ, BW), F32)],
        compiler_params=_cparams(("arbitrary",)),
    )(z, z, z, z, z, z, z, dao, bias)


def _mix_bwd(z, dac, dap, dw, dwb, lng, lnb, pw, pb, ps, S):
    T = z.shape[0]
    tm = 512
    ts = S // tm
    nh = tm // HALO
    nsteps = T // tm
    nblk32 = T // HALO

    def body(ca_ref, cb_ref, cg_ref, pi_ref, pg_ref, dac_ref, dap_ref,
             cah_ref, cbh_ref, pih_ref,
             can_ref, cbn_ref, cgn_ref, pgn_ref, dacn_ref, dapn_ref,
             dw_ref, dwb_ref, lng_ref, lnb_ref, pw_ref, pb_ref, ps_ref,
             dzc_ref, dzp_ref, ddw_ref, ddwb_ref, dlng_ref, dlnb_ref, dpw_ref, dpb_ref, dps_ref,
             ubuf, gbuf, pbuf, qbuf, pooled, dwacc, vacc):
        i = pl.program_id(0)
        keep_prev = jnp.where((i % ts) == 0, 0.0, 1.0)
        keep_next = jnp.where((i % ts) == ts - 1, 0.0, 1.0)
        t0 = (i % ts) * tm

        @pl.when(i == 0)
        def _():
            dwacc[...] = jnp.zeros_like(dwacc)
            vacc[...] = jnp.zeros_like(vacc)
            dpw_ref[...] = jnp.zeros_like(dpw_ref)
            dpb_ref[...] = jnp.zeros_like(dpb_ref)
            dps_ref[...] = jnp.zeros_like(dps_ref)

        ubuf[0:HALO, :] = cah_ref[...].astype(F32) * _sig(cbh_ref[...].astype(F32)) * keep_prev
        ubuf[HALO:HALO + tm, :] = ca_ref[...].astype(F32) * _sig(cb_ref[...].astype(F32))
        ubuf[HALO + tm:2 * HALO + tm, :] = can_ref[...].astype(F32) * _sig(cbn_ref[...].astype(F32))
        pbuf[0:HALO, :] = pih_ref[...].astype(F32) * keep_prev
        pbuf[HALO:HALO + tm, :] = pi_ref[...].astype(F32)

        def conv_back(win, cg, dacv, main, base):
            u1 = _conv_taps(win, dw_ref, lambda j: 2 + j) + dwb_ref[...]
            n, rstd = _layer_norm_fwd(u1)
            u2 = n * lng_ref[...] + lnb_ref[...]
            s2 = _sig(u2)
            scg = _sig(cg)
            du2 = dacv * cg * scg * _dsilu(u2, s2)
            dn = du2 * lng_ref[...]
            du1 = rstd * (dn - jnp.mean(dn, axis=-1, keepdims=True)
                          - n * jnp.mean(dn * n, axis=-1, keepdims=True))
            if main:
                dzc_ref[pl.ds(base, RC), 2 * BW:3 * BW] = (dacv * u2 * s2 * _dsilu(cg, scg)).astype(BF16)
                vacc[0:8, :] += _rows8(du2 * n)
                vacc[8:16, :] += _rows8(du2)
                vacc[16:24, :] += _rows8(du1)
                for j in range(CONV_K):
                    dwacc[8 * j:8 * j + 8, :] += _rows8(du1 * win[2 + j:2 + j + RC, :])
            return du1

        def chunk_a(c, carry):
            base = pl.multiple_of(c * RC, RC)
            win = ubuf[pl.ds(base, RC + HALO), :]
            du1 = conv_back(win, cg_ref[pl.ds(base, RC), :].astype(F32),
                            dac_ref[pl.ds(base, RC), :].astype(F32), True, base)
            gbuf[pl.ds(base, RC), :] = du1
            return carry

        lax.fori_loop(0, tm // RC, chunk_a, 0)
        du1n = conv_back(ubuf[tm:tm + RC + HALO, :], cgn_ref[...].astype(F32),
                         dacn_ref[...].astype(F32), False, None)
        gbuf[tm:tm + HALO, :] = du1n * keep_next

        def chunk_p(c, carry):
            base = pl.multiple_of(c * RC, RC)
            outs = _pool_chunk(pbuf[pl.ds(base, RC + HALO), :], t0 + base)
            for g in range(4):
                pooled[pl.ds(base, RC), g * 128:(g + 1) * 128] = outs[g].astype(BF16)
            return carry

        lax.fori_loop(0, tm // RC, chunk_p, 0)

        def cnt_of(t_first, rows, w):
            t = t_first + lax.broadcasted_iota(jnp.int32, (rows, 128), 0)
            return jnp.minimum(t + 1, w).astype(F32)

        pg = pg_ref[...].astype(F32)
        spg_s = _sig(pg)
        dapv = dap_ref[...].astype(F32)
        pgn = pgn_ref[...].astype(F32)
        dmixn = dapn_ref[...].astype(F32) * pgn * _sig(pgn) * ps_ref[...] * keep_next
        for g, w in enumerate(WINDOWS):
            sl = slice(g * 128, (g + 1) * 128)
            mixed_u = _nn(pooled[:, sl], pw_ref[g]) + pb_ref[:, sl]
            dap_g = dapv[:, sl]
            pg_g = pg[:, sl]
            s_g = spg_s[:, sl]
            silu_g = pg_g * s_g
            dps_ref[:, sl] += jnp.sum(dap_g * silu_g * mixed_u, axis=0, keepdims=True)
            dzp_ref[:, BW + g * 128:BW + (g + 1) * 128] = (
                dap_g * mixed_u * ps_ref[:, sl] * _dsilu(pg_g, s_g)).astype(BF16)
            dmix = dap_g * silu_g * ps_ref[:, sl]
            dpb_ref[:, sl] += jnp.sum(dmix, axis=0, keepdims=True)
            dmixb = dmix.astype(BF16)
            dpw_ref[g] += _tn(pooled[:, sl], dmixb)
            qbuf[0:tm, sl] = _nt(dmixb, pw_ref[g]) / cnt_of(t0, tm, w)
            qbuf[tm:tm + HALO, sl] = _nt(dmixn[:, sl].astype(BF16), pw_ref[g]) / cnt_of(t0 + tm, HALO, w)

        def chunk_b(c, carry):
            base = pl.multiple_of(c * RC, RC)
            gwin = gbuf[pl.ds(base, RC + HALO), :]
            du0 = _conv_taps(gwin, dw_ref, lambda j: CONV_K - 1 - j)
            ca = ca_ref[pl.ds(base, RC), :].astype(F32)
            sb = _sig(cb_ref[pl.ds(base, RC), :].astype(F32))
            dzc_ref[pl.ds(base, RC), 0:BW] = (du0 * sb).astype(BF16)
            dzc_ref[pl.ds(base, RC), BW:2 * BW] = (du0 * ca * sb * (1.0 - sb)).astype(BF16)
            qwin = qbuf[pl.ds(base, RC + HALO), :]
            t = t0 + base + lax.broadcasted_iota(jnp.int32, (RC, 128), 0)
            for g, w in enumerate(WINDOWS):
                sl = slice(g * 128, (g + 1) * 128)
                acc = qwin[0:RC, sl] * (1.0 - jnp.minimum(t + 1, w).astype(F32))
                for k in range(1, w):
                    acc = acc + qwin[k:k + RC, sl]
                dzp_ref[pl.ds(base, RC), sl] = acc.astype(BF16)
            return carry

        lax.fori_loop(0, tm // RC, chunk_b, 0)

        @pl.when(i == nsteps - 1)
        def _():
            dlng_ref[...] = jnp.sum(vacc[0:8, :], axis=0, keepdims=True)
            dlnb_ref[...] = jnp.sum(vacc[8:16, :], axis=0, keepdims=True)
            ddwb_ref[...] = jnp.sum(vacc[16:24, :], axis=0, keepdims=True)
            for j in range(CONV_K):
                ddw_ref[j:j + 1, :] = jnp.sum(dwacc[8 * j:8 * j + 8, :], axis=0, keepdims=True)

    def zmain(cb):
        return pl.BlockSpec((tm, BW), lambda i: (i, cb))

    def zprev(cb):
        return pl.BlockSpec((HALO, BW), lambda i: (jnp.maximum(i * nh - 1, 0), cb))

    def znext(cb):
        return pl.BlockSpec((HALO, BW), lambda i: (jnp.minimum((i + 1) * nh, nblk32 - 1), cb))

    row = lambda w: pl.BlockSpec((tm, w), lambda i: (i, 0))
    full = lambda shape: pl.BlockSpec(shape, lambda i: (0,) * len(shape))
    return pl.pallas_call(
        body, name="mix_bwd", grid=(nsteps,),
        in_specs=[zmain(CB_CA), zmain(CB_CB), zmain(CB_CG), zmain(CB_PI), zmain(CB_PG), row(BW), row(BW),
                  zprev(CB_CA), zprev(CB_CB), zprev(CB_PI),
                  znext(CB_CA), znext(CB_CB), znext(CB_CG), znext(CB_PG), znext(0), znext(0),
                  full((CONV_K, BW)), full((1, BW)), full((1, BW)), full((1, BW)),
                  full((4, 128, 128)), full((1, BW)), full((1, BW))],
        out_specs=[row(3 * BW), row(2 * BW), full((CONV_K, BW)), full((1, BW)), full((1, BW)), full((1, BW)),
                   full((4, 128, 128)), full((1, BW)), full((1, BW))],
        out_shape=[SDS((T, 3 * BW), BF16), SDS((T, 2 * BW), BF16), SDS((CONV_K, BW), F32),
                   SDS((1, BW), F32), SDS((1, BW), F32), SDS((1, BW), F32),
                   SDS((4, 128, 128), F32), SDS((1, BW), F32), SDS((1, BW), F32)],
        scratch_shapes=[pltpu.VMEM((2 * HALO + tm, BW), F32), pltpu.VMEM((tm + HALO, BW), F32),
                        pltpu.VMEM((HALO + tm, BW), F32), pltpu.VMEM((tm + HALO, BW), F32),
                        pltpu.VMEM((tm, BW), BF16), pltpu.VMEM((8 * CONV_K, BW), F32),
                        pltpu.VMEM((24, BW), F32)],
        compiler_params=_cparams(("arbitrary",)),
    )(z, z, z, z, z, dac, dap, z, z, z, z, z, z, z, dac, dap, dw, dwb, lng, lnb, pw, pb, ps)


def _in_bwd_x(pieces, w, x, g, dout):
    T = x.shape[0]
    tm = 256
    widths = [p.shape[1] for p in pieces]
    offs = np.cumsum([0] + widths)
    npc = len(pieces)

    def body(*refs):
        p_refs = refs[:npc]
        w_ref, x_ref, g_ref, do_ref, dx_ref, dg_ref = refs[npc:]

        @pl.when(pl.program_id(0) == 0)
        def _():
            dg_ref[...] = jnp.zeros_like(dg_ref)

        dh = None
        for k in range(npc):
            t = _nt(p_refs[k][...], w_ref[:, int(offs[k]):int(offs[k + 1])])
            dh = t if dh is None else dh + t
        xv = x_ref[...]
        r = lax.rsqrt(jnp.mean(xv * xv, axis=-1, keepdims=True) + EPS)
        xn = xv * r
        dg_ref[...] += jnp.sum(dh * xn, axis=0, keepdims=True)
        dxn = dh * g_ref[...]
        dx_ref[...] = do_ref[...] + r * (dxn - xn * jnp.mean(dxn * xn, axis=-1, keepdims=True))

    row = lambda wd: pl.BlockSpec((tm, wd), lambda i: (i, 0))
    return pl.pallas_call(
        body, name="in_bwd_x", grid=(T // tm,),
        in_specs=[row(wd) for wd in widths]
                 + [pl.BlockSpec((D, NCOL), lambda i: (0, 0), pipeline_mode=pl.Buffered(1)),
                    row(D), pl.BlockSpec((1, D), lambda i: (0, 0)), row(D)],
        out_specs=[row(D), pl.BlockSpec((1, D), lambda i: (0, 0))],
        out_shape=[SDS((T, D), F32), SDS((1, D), F32)],
        compiler_params=_cparams(("arbitrary",)),
    )(*pieces, w, x, g, dout)


def _in_bwd_w(ht, piece):
    T = ht.shape[1]
    wd = piece.shape[1]
    tn = 512
    tk = min(T, 2048)
    nk = T // tk

    def body(ht_ref, p_ref, o_ref, acc):
        k = pl.program_id(1)

        @pl.when(k == 0)
        def _():
            acc[...] = jnp.zeros_like(acc)
        acc[...] += _nn(ht_ref[...], p_ref[...])

        @pl.when(k == nk - 1)
        def _():
            o_ref[...] = acc[...].astype(BF16)

    return pl.pallas_call(
        body, name="in_bwd_w", grid=(wd // tn, nk),
        in_specs=[pl.BlockSpec((D, tk), lambda j, k: (0, k)), pl.BlockSpec((tk, tn), lambda j, k: (k, j))],
        out_specs=pl.BlockSpec((D, tn), lambda j, k: (0, j)),
        out_shape=SDS((D, wd), BF16),
        scratch_shapes=[pltpu.VMEM((D, tn), F32)],
        compiler_params=_cparams(("parallel", "arbitrary")),
    )(ht, piece)


def _band_mask():
    qc = np.arange(TQ)[:, None] // CHUNK
    kc = (np.arange(KW)[None, :] - 2 * TQ) // CHUNK
    band = (kc <= qc) & (kc >= qc - LEFT)
    return np.where(band, 0.0, NEG).astype(np.float32)


def _bias_block(table, band_mask):
    nd = TQ + KW - 1
    d = np.arange(nd)
    idx = np.clip(3 * TQ - 1 - d, -MAX_REL, MAX_REL) + MAX_REL
    sel = np.zeros((2 * MAX_REL + 1, nd), np.float32)
    sel[idx, d] = 1.0
    f = jnp.dot(table, jnp.asarray(sel), precision=lax.Precision.HIGHEST)
    fp = jnp.concatenate([f, jnp.zeros((HEADS, 1), F32)], axis=1)
    rows = jnp.tile(fp, (1, TQ))[:, :TQ * nd].reshape(HEADS, TQ, nd)
    return rows[:, :, TQ - 1:TQ - 1 + KW] + band_mask[None]


def _my_id():
    return 4 * lax.axis_index("x") + 2 * lax.axis_index("y") + lax.axis_index("c")


def _peers():
    x, y, c = lax.axis_index("x"), lax.axis_index("y"), lax.axis_index("c")
    out = []
    for k in range(1, N_DEV):
        fx, fy, fc = (k >> 2) & 1, (k >> 1) & 1, k & 1
        px, py, pc = x ^ fx, y ^ fy, c ^ fc
        out.append(((px, py, pc), 4 * px + 2 * py + pc))
    return out


def _exchange(arrays, scatter, name):
    n = len(arrays)

    def body(*refs):
        in_refs = refs[:n]
        out_refs = refs[n:2 * n]
        send_sems, recv_sems, local_sems = refs[2 * n:]
        me = _my_id()
        peers = _peers()

        def remote(k, p, slab):
            pos, pid = peers[p]
            return pltpu.make_async_remote_copy(
                src_ref=in_refs[k].at[pid] if scatter else in_refs[k], dst_ref=out_refs[k].at[slab],
                send_sem=send_sems.at[p, k], recv_sem=recv_sems.at[p, k],
                device_id=pos, device_id_type=pl.DeviceIdType.MESH)

        own = [pltpu.make_async_copy(in_refs[k].at[me] if scatter else in_refs[k], out_refs[k].at[me],
                                     local_sems.at[k]) for k in range(n)]
        for cp in own:
            cp.start()
        sends = [remote(k, p, me) for p in range(N_DEV - 1) for k in range(n)]
        for cp in sends:
            cp.start()
        for p in range(N_DEV - 1):
            for k in range(n):
                remote(k, p, peers[p][1]).wait_recv()
        for cp in sends:
            cp.wait_send()
        for cp in own:
            cp.wait()

    hbm = pl.BlockSpec(memory_space=pltpu.HBM)
    shapes = [a.shape[1:] if scatter else a.shape for a in arrays]
    return pl.pallas_call(
        body, name=name,
        in_specs=[hbm] * n, out_specs=[hbm] * n,
        out_shape=[SDS((N_DEV,) + tuple(s), a.dtype) for s, a in zip(shapes, arrays)],
        scratch_shapes=[pltpu.SemaphoreType.DMA((N_DEV - 1, n)), pltpu.SemaphoreType.DMA((N_DEV - 1, n)),
                        pltpu.SemaphoreType.DMA((n,))],
    )(*arrays)


def _adamw_sum(parts0, parts1, w, m, v, name):
    _, R, C = w.shape
    tr = R
    while tr * C > 256 * 1024 and tr % 32 == 0:
        tr //= 2
    c1 = 1.0 / (1.0 - ADAM_B1 ** ADAM_STEP)
    c2 = 1.0 / (1.0 - ADAM_B2 ** ADAM_STEP)

    def body(p0_ref, p1_ref, w_ref, m_ref, v_ref, g_ref, d_ref, mo_ref, vo_ref):
        def update(p_ref):
            g = p_ref[0].astype(F32)
            for s in range(1, N_DEV):
                g = g + p_ref[s].astype(F32)
            mn = ADAM_B1 * m_ref[...] + (1.0 - ADAM_B1) * g
            vn = ADAM_B2 * v_ref[...] + (1.0 - ADAM_B2) * (g * g)
            g_ref[...] = g
            mo_ref[...] = mn
            vo_ref[...] = vn
            d_ref[...] = -ADAM_LR * ((mn * c1) / (jnp.sqrt(vn * c2) + ADAM_EPS) + ADAM_WD * w_ref[...])

        @pl.when(pl.program_id(0) == 0)
        def _():
            update(p0_ref)

        @pl.when(pl.program_id(0) == 1)
        def _():
            update(p1_ref)

    blk = pl.BlockSpec((None, tr, C), lambda l, i: (l, i, 0))
    return pl.pallas_call(
        body, name=name, grid=(2, R // tr),
        in_specs=[pl.BlockSpec((N_DEV, tr, C), lambda l, i: (0, i * (1 - l), 0)),
                  pl.BlockSpec((N_DEV, tr, C), lambda l, i: (0, i * l, 0)), blk, blk, blk],
        out_specs=[blk, blk, blk, blk],
        out_shape=[SDS((2, R, C), F32)] * 4,
        compiler_params=_cparams(("arbitrary", "arbitrary")),
    )(parts0, parts1, w, m, v)


def _layer_fwd(x, P, bias, S):
    z, ht = _in_proj(x, P["pre_g"], P["w_in"])
    ac, ap = _mix_fwd(z, P["conv_dw"], P["conv_dw_b"], P["conv_ln_g"], P["conv_ln_b"],
                      P["pool_w"], P["pool_b"], P["pool_scale"], S)
    o = _attn_fwd(z, bias, S)
    out = _out_fwd(x, ac, o, ap, z, P["w_conv_out"], P["w_attn_out"], P["w_pool_out"], P["w_out"], P["post_g"])
    return out, (x, z, ht, ac, o, ap)


def _layer_bwd(dout, saved, P, bias, S):
    x, z, ht, ac, o, ap = saved
    (dac, dao, dag, dap, dgm, dwco, dwao, dwpo, dwout, dpostg) = _out_bwd(
        dout, ac, o, ap, z, P["w_conv_out"], P["w_attn_out"], P["w_pool_out"], P["w_out"], P["post_g"])
    dq, dk, dv, dbias = _attn_bwd(z, dao, bias, S)
    (dzc, dzp, ddw, ddwb, dlng, dlnb, dpw, dpb, dps) = _mix_bwd(
        z, dac, dap, P["conv_dw"], P["conv_dw_b"], P["conv_ln_g"], P["conv_ln_b"],
        P["pool_w"], P["pool_b"], P["pool_scale"], S)
    pieces = [dzc, dq, dk, dv, dag, dzp, dgm]
    dx, dpreg = _in_bwd_x(pieces, P["w_in"], x, P["pre_g"], dout)
    dwin = [_in_bwd_w(ht, p) for p in pieces]
    grads = dict(pre_norm_g=dpreg, post_norm_g=dpostg, w_in=dwin, conv_dw=ddw, conv_dw_b=ddwb,
                 conv_ln_g=dlng, conv_ln_b=dlnb, w_conv_out=dwco, dbias=dbias, w_attn_out=dwao,
                 pool_w=dpw, pool_b=dpb, pool_scale=dps, w_pool_out=dwpo, w_out=dwout)
    return dx, grads


WEIGHT_NAMES = ("pre_norm_g", "post_norm_g", "w_in", "conv_dw", "conv_dw_b", "conv_ln_g", "conv_ln_b",
                "w_conv_out", "rel_bias", "w_attn_out", "pool_w", "pool_b", "pool_scale", "w_pool_out", "w_out")
COL_SHARDED = ("w_in", "w_conv_out", "w_attn_out", "w_pool_out", "conv_dw")
ROW_SHARDED = ("w_out",)
REPLICATED = tuple(n for n in WEIGHT_NAMES if n not in COL_SHARDED + ROW_SHARDED)


def _cols_from_slabs(g):
    return g.transpose(1, 0, 2).reshape(g.shape[1], N_DEV * g.shape[2])


def _slabs_from_cols(full):
    r, wd = full.shape
    return full.reshape(r, N_DEV, wd // N_DEV).transpose(1, 0, 2)


def kernel(x, pre_norm_g, post_norm_g, w_in, conv_dw, conv_dw_b, conv_ln_g, conv_ln_b, w_conv_out, rel_bias, w_attn_out, pool_w, pool_b, pool_scale, w_pool_out, w_out, loss_target, m_pre_norm_g, m_post_norm_g, m_w_in, m_conv_dw, m_conv_dw_b, m_conv_ln_g, m_conv_ln_b, m_w_conv_out, m_rel_bias, m_w_attn_out, m_pool_w, m_pool_b, m_pool_scale, m_w_pool_out, m_w_out, v_pre_norm_g, v_post_norm_g, v_w_in, v_conv_dw, v_conv_dw_b, v_conv_ln_g, v_conv_ln_b, v_w_conv_out, v_rel_bias, v_w_attn_out, v_pool_w, v_pool_b, v_pool_scale, v_w_pool_out, v_w_out):
    given = dict(locals())
    weights = {n: given[n] for n in WEIGHT_NAMES}
    nb, S, _ = x.shape
    T = nb * S
    L = pre_norm_g.shape[0]
    x2 = x.reshape(T, D)
    tgt2 = loss_target.reshape(T, D)

    shard_names = ("w_in", "w_conv_out", "w_attn_out", "w_pool_out", "w_out")
    shards = [weights[n].astype(BF16).reshape(-1, weights[n].shape[-1]) for n in shard_names]
    shards.append(conv_dw.reshape(-1, conv_dw.shape[-1]))
    gathered = dict(zip(shard_names + ("conv_dw",), _exchange(shards, False, "gather_weights")))
    band = jnp.asarray(_band_mask())

    def layer_rows(g, l):
        r = g.shape[1] // L
        return g[:, l * r:(l + 1) * r, :]

    params, biases, bias_vjps = [], [], []
    for l in range(L):
        P = {n: _cols_from_slabs(layer_rows(gathered[n], l)) for n in COL_SHARDED}
        P["w_out"] = layer_rows(gathered["w_out"], l).reshape(D, D)
        P["pre_g"] = pre_norm_g[l:l + 1]
        P["post_g"] = post_norm_g[l:l + 1]
        P["conv_dw_b"] = conv_dw_b[l:l + 1]
        P["conv_ln_g"] = conv_ln_g[l:l + 1]
        P["conv_ln_b"] = conv_ln_b[l:l + 1]
        P["pool_w"] = pool_w[l].astype(BF16)
        P["pool_b"] = pool_b[l].reshape(1, BW)
        P["pool_scale"] = pool_scale[l:l + 1]
        params.append(P)
        b, vjp = jax.vjp(lambda t: _bias_block(t, band), rel_bias[l])
        biases.append(b)
        bias_vjps.append(vjp)

    h = x2
    saved = []
    for l in range(L):
        h, sv = _layer_fwd(h, params[l], biases[l], S)
        saved.append(sv)
    lsum, dout = _loss_grad(h, tgt2)
    loss = lax.psum(lsum[0, 0], MESH_AXES) * (0.5 / D)
    parts = [None] * L
    for l in reversed(range(L)):
        dout, g = _layer_bwd(dout, saved[l], params[l], biases[l], S)
        scat = {"w_in": _slabs_from_cols(jnp.concatenate(g["w_in"], axis=1)),
                "w_conv_out": _slabs_from_cols(g["w_conv_out"].astype(BF16)),
                "w_attn_out": _slabs_from_cols(g["w_attn_out"].astype(BF16)),
                "w_pool_out": _slabs_from_cols(g["w_pool_out"].astype(BF16)),
                "conv_dw": _slabs_from_cols(g["conv_dw"].astype(BF16)),
                "w_out": g["w_out"].astype(BF16).reshape(N_DEV, D // N_DEV, D)}
        names_s = tuple(scat)
        got_s = _exchange([scat[n] for n in names_s], True, "scatter_grads_%d" % l)
        (d_table,) = bias_vjps[l](g["dbias"])
        rep = {"pre_norm_g": g["pre_norm_g"], "post_norm_g": g["post_norm_g"], "conv_dw_b": g["conv_dw_b"],
               "conv_ln_g": g["conv_ln_g"], "conv_ln_b": g["conv_ln_b"], "rel_bias": d_table,
               "pool_w": g["pool_w"].reshape(4 * 128, 128), "pool_b": g["pool_b"].reshape(4, 128),
               "pool_scale": g["pool_scale"]}
        names_r = tuple(rep)
        got_r = _exchange([rep[n] for n in names_r], False, "gather_grads_%d" % l)
        parts[l] = {**dict(zip(names_s, got_s)), **dict(zip(names_r, got_r))}
    grad_x = dout.reshape(x.shape)

    outs = {}
    for n in WEIGHT_NAMES:
        w = weights[n]
        r, c = parts[0][n].shape[1:]
        shape3 = (L, r, c)
        res = _adamw_sum(parts[0][n], parts[1][n], w.reshape(shape3), given["m_" + n].reshape(shape3),
                         given["v_" + n].reshape(shape3), "adamw_" + n)
        outs[n] = [a.reshape(w.shape) for a in res]
    return (loss, grad_x, *[outs[n][0] for n in WEIGHT_NAMES], *[outs[n][1] for n in WEIGHT_NAMES],
            *[outs[n][2] for n in WEIGHT_NAMES], *[outs[n][3] for n in WEIGHT_NAMES])
```

```python
import functools

import numpy as np
import jax
import jax.numpy as jnp
from jax import lax
from jax.experimental import pallas as pl
from jax.experimental.pallas import tpu as pltpu

F32 = jnp.float32
BF16 = jnp.bfloat16
SDS = jax.ShapeDtypeStruct

D = 1024
BW = 512
NCOL = 7680
EPS = 1e-6
NEG = -1e30
HEADS = 8
HD = 64
CHUNK = 64
LEFT = 8
MAX_REL = 256
TQ = 256
KW = 768
CONV_K = 31
WINDOWS = (2, 4, 8, 16)
HALO = 32
RC = 32
N_DEV = 8
MESH_AXES = ("x", "y", "c")

ADAM_LR = 0.001
ADAM_B1 = 0.9
ADAM_B2 = 0.999
ADAM_EPS = 1e-08
ADAM_WD = 0.01
ADAM_STEP = 10

VMEM_LIMIT = 56 * 1024 * 1024

CB_CA, CB_CB, CB_CG, CB_Q, CB_K, CB_V, CB_AG, CB_PI, CB_PG = range(9)
DZ_PIECES = (("conv", 1536), ("q", 512), ("k", 512), ("v", 512), ("ag", 512), ("pool", 1024), ("gm", 3072))


def _cparams(sem):
    return pltpu.CompilerParams(dimension_semantics=sem, vmem_limit_bytes=VMEM_LIMIT)


def _sig(x):
    return 1.0 / (1.0 + jnp.exp(-x))


def _dsilu(x, s):
    return s * (1.0 + x * (1.0 - s))


def _nt(a, b):
    return lax.dot_general(a, b, (((1,), (1,)), ((), ())), preferred_element_type=F32)


def _tn(a, b):
    return lax.dot_general(a, b, (((0,), (0,)), ((), ())), preferred_element_type=F32)


def _nn(a, b):
    return jnp.dot(a, b, preferred_element_type=F32)


def _rows8(x):
    return x[0:8] + x[8:16] + x[16:24] + x[24:32]


def _unpack(ex, refs, n_in, n_out):
    if ex is None:
        return refs[:n_in], refs[n_in:n_in + n_out], refs[n_in + n_out:], None
    return ex.split(refs, n_in, n_out)


def _carried(ex, first, last, exr):
    if ex is None:
        return lambda: None

    @pl.when(first)
    def _():
        ex.start(exr)

    def finish():
        @pl.when(last)
        def _():
            ex.finish(exr)
    return finish


def _ex_lists(ex):
    if ex is None:
        return [], [], [], [], []
    return ex.arrays, ex.in_specs, ex.out_specs, ex.out_shape, ex.scratch


def _in_proj(x, g, wt, ex=None):
    T = x.shape[0]
    tm = min(T, 1024)
    tn = 1536

    ni, nj = T // tm, NCOL // tn
    ex_arrays, ex_in, ex_out, ex_shape, ex_scr = _ex_lists(ex)

    def body(*refs):
        (x_ref, g_ref, w_ref), (z_ref, ht_ref), (h_scr,), exr = _unpack(ex, refs, 3, 2)
        i, j = pl.program_id(0), pl.program_id(1)
        finish = _carried(ex, (i == 0) & (j == 0), (i == ni - 1) & (j == nj - 1), exr)

        @pl.when(j == 0)
        def _():
            xv = x_ref[...]
            r = lax.rsqrt(jnp.mean(xv * xv, axis=-1, keepdims=True) + EPS)
            h = xv * r * g_ref[...]
            h_scr[...] = h.astype(BF16)
            ht_ref[...] = h.T.astype(BF16)
        z_ref[...] = _nt(h_scr[...], w_ref[...]).astype(BF16)
        finish()

    return pl.pallas_call(
        body, name="in_proj" if ex is None else "in_proj_carrier", grid=(ni, nj),
        in_specs=[pl.BlockSpec((tm, D), lambda i, j: (i, 0)),
                  pl.BlockSpec((1, D), lambda i, j: (0, 0)),
                  pl.BlockSpec((tn, D), lambda i, j: (j, 0))] + ex_in,
        out_specs=[pl.BlockSpec((tm, tn), lambda i, j: (i, j)),
                   pl.BlockSpec((D, tm), lambda i, j: (0, i))] + ex_out,
        out_shape=[SDS((T, NCOL), BF16), SDS((D, T), BF16)] + ex_shape,
        scratch_shapes=[pltpu.VMEM((tm, D), BF16)] + ex_scr,
        compiler_params=_cparams(("arbitrary", "arbitrary")),
    )(x, g, wt, *ex_arrays)


def _stencil(load, w_ref, lo, hi, tap_of):
    out = None
    for r in range(8):
        n = RC if r == 0 else RC + 8
        v = None
        for q in range((hi - r) // 8 + 1):
            o = 8 * q + r
            if o < lo:
                continue
            j = tap_of(o)
            term = w_ref[j:j + 1, :] * load(q, n)
            v = term if v is None else v + term
        if v is None:
            continue
        if r:
            v = pltpu.roll(v, n - r, axis=0)[0:RC]
        out = v if out is None else out + v
    return out


def _layer_norm_fwd(u1):
    mu = jnp.mean(u1, axis=-1, keepdims=True)
    xc = u1 - mu
    rstd = lax.rsqrt(jnp.mean(xc * xc, axis=-1, keepdims=True) + EPS)
    return xc * rstd, rstd


def _window_sums(x, w, back):
    n = x.shape[0]
    s = x
    k = 1
    while k < w:
        s = s + pltpu.roll(s, k if back else n - k, axis=0)
        k *= 2
    return s


def _pool_chunk(pwin, t_first):
    t = t_first + lax.broadcasted_iota(jnp.int32, (RC, 128), 0)
    outs = []
    for g, w in enumerate(WINDOWS):
        x = pwin[:, g * 128:(g + 1) * 128]
        s = _window_sums(x, w, True)
        cnt = jnp.minimum(t + 1, w).astype(F32)
        outs.append(s[HALO:HALO + RC] / cnt - x[HALO:HALO + RC])
    return outs


def _mix_fwd(z, dw, dwb, lng, lnb, pw, pb, ps, S):
    T = z.shape[0]
    tm = 512
    ts = S // tm
    nh = tm // HALO

    def body(ca_ref, cb_ref, cg_ref, pi_ref, pg_ref, cah_ref, cbh_ref, pih_ref,
             dw_ref, dwb_ref, lng_ref, lnb_ref, pw_ref, pb_ref, ps_ref,
             ac_ref, ap_ref, u1_ref, ubuf, pbuf, pooled):
        i = pl.program_id(0)
        keep = jnp.where((i % ts) == 0, 0.0, 1.0)
        ubuf[0:HALO, :] = cah_ref[...].astype(F32) * _sig(cbh_ref[...].astype(F32)) * keep
        ubuf[HALO:HALO + tm, :] = ca_ref[...].astype(F32) * _sig(cb_ref[...].astype(F32))
        pbuf[0:HALO, :] = pih_ref[...].astype(F32) * keep
        pbuf[HALO:HALO + tm, :] = pi_ref[...].astype(F32)
        t0 = (i % ts) * tm

        def chunk(c, carry):
            base = pl.multiple_of(c * RC, RC)
            load = lambda q, n: ubuf[pl.ds(base + 8 * q, n), :]
            u1 = _stencil(load, dw_ref, 2, CONV_K + 1, lambda o: o - 2) + dwb_ref[...]
            u1_ref[pl.ds(base, RC), :] = u1
            n, _ = _layer_norm_fwd(u1)
            u2 = n * lng_ref[...] + lnb_ref[...]
            u3 = u2 * _sig(u2)
            cg = cg_ref[pl.ds(base, RC), :].astype(F32)
            ac_ref[pl.ds(base, RC), :] = (u3 * cg * _sig(cg)).astype(BF16)
            pwin = pbuf[pl.ds(base, RC + HALO), :]
            outs = _pool_chunk(pwin, t0 + base)
            for g in range(4):
                pooled[pl.ds(base, RC), g * 128:(g + 1) * 128] = outs[g].astype(BF16)
            return carry

        lax.fori_loop(0, tm // RC, chunk, 0)
        pg = pg_ref[...].astype(F32)
        spg = pg * _sig(pg)
        for g in range(4):
            sl = slice(g * 128, (g + 1) * 128)
            mixed = (_nn(pooled[:, sl], pw_ref[g]) + pb_ref[:, sl]) * ps_ref[:, sl]
            ap_ref[:, sl] = (mixed * spg[:, sl]).astype(BF16)

    def zmain(cb):
        return pl.BlockSpec((tm, BW), lambda i: (i, cb))

    def zprev(cb):
        return pl.BlockSpec((HALO, BW), lambda i: (jnp.maximum(i * nh - 1, 0), cb))

    full = lambda shape: pl.BlockSpec(shape, lambda i: (0,) * len(shape))
    row = pl.BlockSpec((tm, BW), lambda i: (i, 0))
    return pl.pallas_call(
        body, name="mix_fwd", grid=(T // tm,),
        in_specs=[zmain(CB_CA), zmain(CB_CB), zmain(CB_CG), zmain(CB_PI), zmain(CB_PG),
                  zprev(CB_CA), zprev(CB_CB), zprev(CB_PI),
                  full((CONV_K, BW)), full((1, BW)), full((1, BW)), full((1, BW)),
                  full((4, 128, 128)), full((1, BW)), full((1, BW))],
        out_specs=[row, row, row],
        out_shape=[SDS((T, BW), BF16), SDS((T, BW), BF16), SDS((T, BW), F32)],
        scratch_shapes=[pltpu.VMEM((HALO + tm, BW), F32), pltpu.VMEM((HALO + tm, BW), F32),
                        pltpu.VMEM((tm, BW), BF16)],
        compiler_params=_cparams(("parallel",)),
    )(z, z, z, z, z, z, z, z, dw, dwb, lng, lnb, pw, pb, ps)


def _attn_specs(nq):
    def kv(cb, off):
        return pl.BlockSpec((TQ, BW), lambda i: (i - jnp.minimum(off, i % nq), cb))
    return [pl.BlockSpec((TQ, BW), lambda i: (i, CB_Q)),
            kv(CB_K, 2), kv(CB_K, 1), kv(CB_K, 0), kv(CB_V, 2), kv(CB_V, 1), kv(CB_V, 0)]


def _softmax_rows(s):
    m = jnp.max(s, axis=-1, keepdims=True)
    e = jnp.exp(s - m)
    return e / jnp.sum(e, axis=-1, keepdims=True)


def _attn_fwd(z, bias, S):
    T = z.shape[0]
    nq = S // TQ

    def body(q_ref, k2_ref, k1_ref, k0_ref, v2_ref, v1_ref, v0_ref, b_ref, o_ref, kbuf, vbuf):
        qb = pl.program_id(0) % nq
        kbuf[0:TQ, :] = k2_ref[...]
        kbuf[TQ:2 * TQ, :] = k1_ref[...]
        kbuf[2 * TQ:KW, :] = k0_ref[...]
        vbuf[0:TQ, :] = v2_ref[...]
        vbuf[TQ:2 * TQ, :] = v1_ref[...]
        vbuf[2 * TQ:KW, :] = v0_ref[...]
        pad = lax.broadcasted_iota(jnp.int32, (TQ, KW), 1) < (2 - qb) * TQ
        lane = lax.broadcasted_iota(jnp.int32, (1, 128), 1)
        for hp in range(HEADS // 2):
            sl = slice(hp * 128, (hp + 1) * 128)
            qp = q_ref[:, sl] * 0.125
            kp = kbuf[:, sl]
            vp = vbuf[:, sl]
            acc = None
            for e in range(2):
                msk = (lane < HD) if e == 0 else (lane >= HD)
                qm = jnp.where(msk, qp, jnp.zeros_like(qp))
                s = _nt(qm, kp) + b_ref[2 * hp + e]
                p = _softmax_rows(jnp.where(pad, NEG, s)).astype(BF16)
                vm = jnp.where(msk, vp, jnp.zeros_like(vp))
                o = _nn(p, vm)
                acc = o if acc is None else acc + o
            o_ref[:, sl] = acc.astype(BF16)

    return pl.pallas_call(
        body, name="attn_fwd", grid=(T // TQ,),
        in_specs=_attn_specs(nq) + [pl.BlockSpec((HEADS, TQ, KW), lambda i: (0, 0, 0))],
        out_specs=pl.BlockSpec((TQ, BW), lambda i: (i, 0)),
        out_shape=SDS((T, BW), BF16),
        scratch_shapes=[pltpu.VMEM((KW, BW), BF16), pltpu.VMEM((KW, BW), BF16)],
        compiler_params=_cparams(("parallel",)),
    )(z, z, z, z, z, z, z, bias)


def _gates(gl_ref, gh_ref):
    gl = _sig(gl_ref[...].astype(F32))
    gh = _sig(gh_ref[...].astype(F32))
    return (gl[:, 0:D], jnp.concatenate([gl[:, D:1536], gh[:, 0:512]], axis=1), gh[:, 512:1536])


def _out_specs_in(tm):
    row = lambda w: pl.BlockSpec((tm, w), lambda i: (i, 0))
    full = lambda shape: pl.BlockSpec(shape, lambda i: (0,) * len(shape))
    return [row(BW), row(BW), row(BW),
            pl.BlockSpec((tm, BW), lambda i: (i, CB_AG)),
            pl.BlockSpec((tm, 1536), lambda i: (i, 3)),
            pl.BlockSpec((tm, 1536), lambda i: (i, 4)),
            full((BW, D)), full((BW, D)), full((BW, D)), full((D, D)), full((1, D))]


def _out_fwd(x, ac, o, ap, z, wco, wao, wpo, wout, postg):
    T = x.shape[0]
    tm = 512

    def body(ac_ref, o_ref, ap_ref, ag_ref, gl_ref, gh_ref, wco_ref, wao_ref, wpo_ref, wout_ref, pg_ref,
             x_ref, out_ref):
        ag = ag_ref[...].astype(F32)
        aat = (o_ref[...].astype(F32) * ag * _sig(ag)).astype(BF16)
        g0, g1, g2 = _gates(gl_ref, gh_ref)
        merged = g0 * _nn(ac_ref[...], wco_ref[...])
        merged = merged + g1 * _nn(aat, wao_ref[...])
        merged = merged + g2 * _nn(ap_ref[...], wpo_ref[...])
        y = _nn(merged.astype(BF16), wout_ref[...])
        ry = lax.rsqrt(jnp.mean(y * y, axis=-1, keepdims=True) + EPS)
        out_ref[...] = x_ref[...] + y * ry * pg_ref[...]

    return pl.pallas_call(
        body, name="out_fwd", grid=(T // tm,),
        in_specs=_out_specs_in(tm) + [pl.BlockSpec((tm, D), lambda i: (i, 0))],
        out_specs=pl.BlockSpec((tm, D), lambda i: (i, 0)),
        out_shape=SDS((T, D), F32),
        compiler_params=_cparams(("parallel",)),
    )(ac, o, ap, z, z, z, wco, wao, wpo, wout, postg, x)


def _loss_grad(out, tgt):
    T = out.shape[0]
    tm = 1024 if T % 1024 == 0 else T

    def body(o_ref, t_ref, l_ref, d_ref):
        @pl.when(pl.program_id(0) == 0)
        def _():
            l_ref[...] = jnp.zeros_like(l_ref)
        d = o_ref[...] - t_ref[...]
        d_ref[...] = d * (1.0 / D)
        sq = jnp.sum(d * d, axis=0, keepdims=True)
        l_ref[...] += jnp.sum(sq, axis=1, keepdims=True)

    return pl.pallas_call(
        body, name="loss_grad", grid=(T // tm,),
        in_specs=[pl.BlockSpec((tm, D), lambda i: (i, 0)), pl.BlockSpec((tm, D), lambda i: (i, 0))],
        out_specs=[pl.BlockSpec((1, 128), lambda i: (0, 0)), pl.BlockSpec((tm, D), lambda i: (i, 0))],
        out_shape=[SDS((1, 128), F32), SDS((T, D), F32)],
        compiler_params=_cparams(("arbitrary",)),
    )(out, tgt)


def _out_bwd(dout, ac, o, ap, z, wco, wao, wpo, wout, postg, ex=None):
    T = dout.shape[0]
    tm = 256
    nsteps = T // tm
    ex_arrays, ex_in, ex_out, ex_shape, ex_scr = _ex_lists(ex)

    def body(*refs):
        ins, outs, _, exr = _unpack(ex, refs, 12, 10)
        (ac_ref, o_ref, ap_ref, ag_ref, gl_ref, gh_ref, wco_ref, wao_ref, wpo_ref, wout_ref, pg_ref, do_ref) = ins
        (dac_ref, dao_ref, dag_ref, dap_ref, dgm_ref, dwco_ref, dwao_ref, dwpo_ref, dwout_ref, dpg_ref) = outs
        finish = _carried(ex, pl.program_id(0) == 0, pl.program_id(0) == nsteps - 1, exr)

        @pl.when(pl.program_id(0) == 0)
        def _():
            for r in (dwco_ref, dwao_ref, dwpo_ref, dwout_ref, dpg_ref):
                r[...] = jnp.zeros_like(r)

        ag = ag_ref[...].astype(F32)
        sag = _sig(ag)
        ov = o_ref[...].astype(F32)
        acts = (ac_ref[...], (ov * ag * sag).astype(BF16), ap_ref[...])
        ws = (wco_ref, wao_ref, wpo_ref)
        gates = _gates(gl_ref, gh_ref)
        ys = [_nn(acts[b], ws[b][...]) for b in range(3)]
        merged = (gates[0] * ys[0] + gates[1] * ys[1] + gates[2] * ys[2]).astype(BF16)
        y = _nn(merged, wout_ref[...])
        ry = lax.rsqrt(jnp.mean(y * y, axis=-1, keepdims=True) + EPS)
        yn = y * ry
        dout_v = do_ref[...]
        dpg_ref[...] += jnp.sum(dout_v * yn, axis=0, keepdims=True)
        dyn = dout_v * pg_ref[...]
        dy = (ry * (dyn - yn * jnp.mean(dyn * yn, axis=-1, keepdims=True))).astype(BF16)
        dmerged = _nt(dy, wout_ref[...])
        dwout_ref[...] += _tn(merged, dy)
        dws = (dwco_ref, dwao_ref, dwpo_ref)
        das = []
        for b in range(3):
            gb = gates[b]
            dgm_ref[:, b * D:(b + 1) * D] = (dmerged * ys[b] * gb * (1.0 - gb)).astype(BF16)
            dyb = (dmerged * gb).astype(BF16)
            dws[b][...] += _tn(acts[b], dyb)
            das.append(_nt(dyb, ws[b][...]))
        dac_ref[...] = das[0].astype(BF16)
        dap_ref[...] = das[2].astype(BF16)
        dao_ref[...] = (das[1] * ag * sag).astype(BF16)
        dag_ref[...] = (das[1] * ov * _dsilu(ag, sag)).astype(BF16)
        finish()

    row = lambda w: pl.BlockSpec((tm, w), lambda i: (i, 0))
    full = lambda shape: pl.BlockSpec(shape, lambda i: (0,) * len(shape))
    return pl.pallas_call(
        body, name="out_bwd" if ex is None else "out_bwd_carrier", grid=(nsteps,),
        in_specs=_out_specs_in(tm) + [row(D)] + ex_in,
        out_specs=[row(BW), row(BW), row(BW), row(BW), row(3 * D),
                   full((BW, D)), full((BW, D)), full((BW, D)), full((D, D)), full((1, D))] + ex_out,
        out_shape=[SDS((T, BW), BF16)] * 4 + [SDS((T, 3 * D), BF16)]
                  + [SDS((BW, D), F32)] * 3 + [SDS((D, D), F32), SDS((1, D), F32)] + ex_shape,
        scratch_shapes=ex_scr,
        compiler_params=_cparams(("arbitrary",)),
    )(ac, o, ap, z, z, z, wco, wao, wpo, wout, postg, dout, *ex_arrays)


def _attn_bwd(z, dao, bias, S):
    T = z.shape[0]
    nq = S // TQ

    def body(q_ref, k2_ref, k1_ref, k0_ref, v2_ref, v1_ref, v0_ref, do_ref, b_ref,
             dq_ref, dk_ref, dv_ref, db_ref, kbuf, vbuf, dkacc, dvacc):
        i = pl.program_id(0)
        qb = i % nq

        @pl.when(i == 0)
        def _():
            db_ref[...] = jnp.zeros_like(db_ref)

        @pl.when(qb == 0)
        def _():
            dkacc[...] = jnp.zeros_like(dkacc)
            dvacc[...] = jnp.zeros_like(dvacc)

        kbuf[0:TQ, :] = k2_ref[...]
        kbuf[TQ:2 * TQ, :] = k1_ref[...]
        kbuf[2 * TQ:KW, :] = k0_ref[...]
        vbuf[0:TQ, :] = v2_ref[...]
        vbuf[TQ:2 * TQ, :] = v1_ref[...]
        vbuf[2 * TQ:KW, :] = v0_ref[...]
        pad = lax.broadcasted_iota(jnp.int32, (TQ, KW), 1) < (2 - qb) * TQ
        lane = lax.broadcasted_iota(jnp.int32, (1, 128), 1)
        row0 = pl.multiple_of(qb * TQ, TQ)
        for hp in range(HEADS // 2):
            sl = slice(hp * 128, (hp + 1) * 128)
            qp = q_ref[:, sl] * 0.125
            kp = kbuf[:, sl]
            vp = vbuf[:, sl]
            dop = do_ref[:, sl]
            dq_acc = dk_acc = dv_acc = None
            for e in range(2):
                h = 2 * hp + e
                msk = (lane < HD) if e == 0 else (lane >= HD)
                qm = jnp.where(msk, qp, jnp.zeros_like(qp))
                dom = jnp.where(msk, dop, jnp.zeros_like(dop))
                km = jnp.where(msk, kp, jnp.zeros_like(kp))
                s = _nt(qm, kp) + b_ref[h]
                p = _softmax_rows(jnp.where(pad, NEG, s))
                dp = _nt(dom, vp)
                ds = p * (dp - jnp.sum(p * dp, axis=-1, keepdims=True))
                db_ref[h] += ds
                dsb = ds.astype(BF16)
                dq_h = _nn(dsb, km) * 0.125
                dk_h = _tn(dsb, qm)
                dv_h = _tn(p.astype(BF16), dom)
                dq_acc = dq_h if dq_acc is None else dq_acc + dq_h
                dk_acc = dk_h if dk_acc is None else dk_acc + dk_h
                dv_acc = dv_h if dv_acc is None else dv_acc + dv_h
            dq_ref[:, sl] = dq_acc.astype(BF16)
            dkacc[pl.ds(row0, KW), sl] += dk_acc
            dvacc[pl.ds(row0, KW), sl] += dv_acc

        @pl.when(qb == nq - 1)
        def _():
            dk_ref[...] = dkacc[2 * TQ:2 * TQ + S, :].astype(BF16)
            dv_ref[...] = dvacc[2 * TQ:2 * TQ + S, :].astype(BF16)

    return pl.pallas_call(
        body, name="attn_bwd", grid=(T // TQ,),
        in_specs=_attn_specs(nq) + [pl.BlockSpec((TQ, BW), lambda i: (i, 0)),
                                    pl.BlockSpec((HEADS, TQ, KW), lambda i: (0, 0, 0))],
        out_specs=[pl.BlockSpec((TQ, BW), lambda i: (i, 0)),
                   pl.BlockSpec((S, BW), lambda i: (i // nq, 0)),
                   pl.BlockSpec((S, BW), lambda i: (i // nq, 0)),
                   pl.BlockSpec((HEADS, TQ, KW), lambda i: (0, 0, 0))],
        out_shape=[SDS((T, BW), BF16)] * 3 + [SDS((HEADS, TQ, KW), F32)],
        scratch_shapes=[pltpu.VMEM((KW, BW), BF16), pltpu.VMEM((KW, BW), BF16),
                        pltpu.VMEM((S + 2 * TQ, BW), F32), pltpu.VMEM((S + 2 * TQ, BW), F32)],
        compiler_params=_cparams(("arbitrary",)),
    )(z, z, z, z, z, z, z, dao, bias)


def _mix_bwd(z, u1, dac, dap, dw, dwb, lng, lnb, pw, pb, ps, S):
    T = z.shape[0]
    tm = 512
    ts = S // tm
    nh = tm // HALO
    nsteps = T // tm
    nblk32 = T // HALO

    def body(ca_ref, cb_ref, cg_ref, pi_ref, pg_ref, u1_ref, dac_ref, dap_ref,
             cah_ref, cbh_ref, pih_ref,
             cgn_ref, pgn_ref, u1n_ref, dacn_ref, dapn_ref,
             dw_ref, dwb_ref, lng_ref, lnb_ref, pw_ref, pb_ref, ps_ref,
             dzc_ref, dzp_ref, ddw_ref, ddwb_ref, dlng_ref, dlnb_ref, dpw_ref, dpb_ref, dps_ref,
             ubuf, gbuf, pbuf, qbuf, pooled, dwacc, vacc):
        i = pl.program_id(0)
        keep_prev = jnp.where((i % ts) == 0, 0.0, 1.0)
        keep_next = jnp.where((i % ts) == ts - 1, 0.0, 1.0)
        t0 = (i % ts) * tm

        @pl.when(i == 0)
        def _():
            dwacc[...] = jnp.zeros_like(dwacc)
            vacc[...] = jnp.zeros_like(vacc)
            dpw_ref[...] = jnp.zeros_like(dpw_ref)
            dpb_ref[...] = jnp.zeros_like(dpb_ref)
            dps_ref[...] = jnp.zeros_like(dps_ref)

        ubuf[0:HALO, :] = cah_ref[...].astype(F32) * _sig(cbh_ref[...].astype(F32)) * keep_prev
        ubuf[HALO:HALO + tm, :] = ca_ref[...].astype(F32) * _sig(cb_ref[...].astype(F32))
        pbuf[0:HALO, :] = pih_ref[...].astype(F32) * keep_prev
        pbuf[HALO:HALO + tm, :] = pi_ref[...].astype(F32)

        def norm_back(u1v, cg, dacv):
            n, rstd = _layer_norm_fwd(u1v)
            u2 = n * lng_ref[...] + lnb_ref[...]
            s2 = _sig(u2)
            scg = _sig(cg)
            du2 = dacv * cg * scg * _dsilu(u2, s2)
            dn = du2 * lng_ref[...]
            du1 = rstd * (dn - jnp.mean(dn, axis=-1, keepdims=True)
                          - n * jnp.mean(dn * n, axis=-1, keepdims=True))
            return du1, du2, n, dacv * u2 * s2 * _dsilu(cg, scg)

        def chunk_a(c, carry):
            base = pl.multiple_of(c * RC, RC)
            du1, du2, n, dcg = norm_back(u1_ref[pl.ds(base, RC), :], cg_ref[pl.ds(base, RC), :].astype(F32),
                                         dac_ref[pl.ds(base, RC), :].astype(F32))
            gbuf[pl.ds(base, RC), :] = du1
            dzc_ref[pl.ds(base, RC), 2 * BW:3 * BW] = dcg.astype(BF16)
            vacc[0:8, :] += _rows8(du2 * n)
            vacc[8:16, :] += _rows8(du2)
            vacc[16:24, :] += _rows8(du1)
            padded = jnp.concatenate([du1, jnp.zeros((8, BW), F32)], axis=0)
            for r in range(8):
                nrow = RC if r == 0 else RC + 8
                g = du1 if r == 0 else pltpu.roll(padded, r, axis=0)
                for q in range((CONV_K + 1 - r) // 8 + 1):
                    o = 8 * q + r
                    if o < 2:
                        continue
                    prod = g * ubuf[pl.ds(base + 8 * q, nrow), :]
                    red = prod[0:8]
                    for k in range(1, nrow // 8):
                        red = red + prod[8 * k:8 * k + 8]
                    dwacc[8 * (o - 2):8 * (o - 2) + 8, :] += red
            return carry

        lax.fori_loop(0, tm // RC, chunk_a, 0)
        du1n, _, _, _ = norm_back(u1n_ref[...], cgn_ref[...].astype(F32), dacn_ref[...].astype(F32))
        gbuf[tm:tm + HALO, :] = du1n * keep_next

        def chunk_p(c, carry):
            base = pl.multiple_of(c * RC, RC)
            outs = _pool_chunk(pbuf[pl.ds(base, RC + HALO), :], t0 + base)
            for g in range(4):
                pooled[pl.ds(base, RC), g * 128:(g + 1) * 128] = outs[g].astype(BF16)
            return carry

        lax.fori_loop(0, tm // RC, chunk_p, 0)

        def cnt_of(t_first, rows, w):
            t = t_first + lax.broadcasted_iota(jnp.int32, (rows, 128), 0)
            return jnp.minimum(t + 1, w).astype(F32)

        pg = pg_ref[...].astype(F32)
        spg_s = _sig(pg)
        dapv = dap_ref[...].astype(F32)
        pgn = pgn_ref[...].astype(F32)
        dmixn = dapn_ref[...].astype(F32) * pgn * _sig(pgn) * ps_ref[...] * keep_next
        for g, w in enumerate(WINDOWS):
            sl = slice(g * 128, (g + 1) * 128)
            mixed_u = _nn(pooled[:, sl], pw_ref[g]) + pb_ref[:, sl]
            dap_g = dapv[:, sl]
            pg_g = pg[:, sl]
            s_g = spg_s[:, sl]
            silu_g = pg_g * s_g
            dps_ref[:, sl] += jnp.sum(dap_g * silu_g * mixed_u, axis=0, keepdims=True)
            dzp_ref[:, BW + g * 128:BW + (g + 1) * 128] = (
                dap_g * mixed_u * ps_ref[:, sl] * _dsilu(pg_g, s_g)).astype(BF16)
            dmix = dap_g * silu_g * ps_ref[:, sl]
            dpb_ref[:, sl] += jnp.sum(dmix, axis=0, keepdims=True)
            dmixb = dmix.astype(BF16)
            dpw_ref[g] += _tn(pooled[:, sl], dmixb)
            qbuf[0:tm, sl] = _nt(dmixb, pw_ref[g]) / cnt_of(t0, tm, w)
            qbuf[tm:tm + HALO, sl] = _nt(dmixn[:, sl].astype(BF16), pw_ref[g]) / cnt_of(t0 + tm, HALO, w)

        def chunk_b(c, carry):
            base = pl.multiple_of(c * RC, RC)
            load = lambda q, n: gbuf[pl.ds(base + 8 * q, n), :]
            du0 = _stencil(load, dw_ref, 0, CONV_K - 1, lambda o: CONV_K - 1 - o)
            ca = ca_ref[pl.ds(base, RC), :].astype(F32)
            sb = _sig(cb_ref[pl.ds(base, RC), :].astype(F32))
            dzc_ref[pl.ds(base, RC), 0:BW] = (du0 * sb).astype(BF16)
            dzc_ref[pl.ds(base, RC), BW:2 * BW] = (du0 * ca * sb * (1.0 - sb)).astype(BF16)
            qwin = qbuf[pl.ds(base, RC + HALO), :]
            t = t0 + base + lax.broadcasted_iota(jnp.int32, (RC, 128), 0)
            for g, w in enumerate(WINDOWS):
                x = qwin[:, g * 128:(g + 1) * 128]
                s = _window_sums(x, w, False)
                cnt = jnp.minimum(t + 1, w).astype(F32)
                dzp_ref[pl.ds(base, RC), g * 128:(g + 1) * 128] = (s[0:RC] - cnt * x[0:RC]).astype(BF16)
            return carry

        lax.fori_loop(0, tm // RC, chunk_b, 0)

        @pl.when(i == nsteps - 1)
        def _():
            dlng_ref[...] = jnp.sum(vacc[0:8, :], axis=0, keepdims=True)
            dlnb_ref[...] = jnp.sum(vacc[8:16, :], axis=0, keepdims=True)
            ddwb_ref[...] = jnp.sum(vacc[16:24, :], axis=0, keepdims=True)
            for j in range(CONV_K):
                ddw_ref[j:j + 1, :] = jnp.sum(dwacc[8 * j:8 * j + 8, :], axis=0, keepdims=True)

    def zmain(cb):
        return pl.BlockSpec((tm, BW), lambda i: (i, cb))

    def zprev(cb):
        return pl.BlockSpec((HALO, BW), lambda i: (jnp.maximum(i * nh - 1, 0), cb))

    def znext(cb):
        return pl.BlockSpec((HALO, BW), lambda i: (jnp.minimum((i + 1) * nh, nblk32 - 1), cb))

    row = lambda w: pl.BlockSpec((tm, w), lambda i: (i, 0))
    full = lambda shape: pl.BlockSpec(shape, lambda i: (0,) * len(shape))
    return pl.pallas_call(
        body, name="mix_bwd", grid=(nsteps,),
        in_specs=[zmain(CB_CA), zmain(CB_CB), zmain(CB_CG), zmain(CB_PI), zmain(CB_PG), row(BW), row(BW), row(BW),
                  zprev(CB_CA), zprev(CB_CB), zprev(CB_PI),
                  znext(CB_CG), znext(CB_PG), znext(0), znext(0), znext(0),
                  full((CONV_K, BW)), full((1, BW)), full((1, BW)), full((1, BW)),
                  full((4, 128, 128)), full((1, BW)), full((1, BW))],
        out_specs=[row(3 * BW), row(2 * BW), full((CONV_K, BW)), full((1, BW)), full((1, BW)), full((1, BW)),
                   full((4, 128, 128)), full((1, BW)), full((1, BW))],
        out_shape=[SDS((T, 3 * BW), BF16), SDS((T, 2 * BW), BF16), SDS((CONV_K, BW), F32),
                   SDS((1, BW), F32), SDS((1, BW), F32), SDS((1, BW), F32),
                   SDS((4, 128, 128), F32), SDS((1, BW), F32), SDS((1, BW), F32)],
        scratch_shapes=[pltpu.VMEM((HALO + tm, BW), F32), pltpu.VMEM((tm + HALO, BW), F32),
                        pltpu.VMEM((HALO + tm, BW), F32), pltpu.VMEM((tm + HALO, BW), F32),
                        pltpu.VMEM((tm, BW), BF16), pltpu.VMEM((8 * CONV_K, BW), F32),
                        pltpu.VMEM((24, BW), F32)],
        compiler_params=_cparams(("arbitrary",)),
    )(z, z, z, z, z, u1, dac, dap, z, z, z, z, z, u1, dac, dap, dw, dwb, lng, lnb, pw, pb, ps)


def _in_bwd_x(pieces, wt, x, g, dout):
    T = x.shape[0]
    tm = 256
    widths = [p.shape[1] for p in pieces]
    offs = np.cumsum([0] + widths)
    npc = len(pieces)

    def body(*refs):
        p_refs = refs[:npc]
        w_ref, x_ref, g_ref, do_ref, dx_ref, dg_ref = refs[npc:]

        @pl.when(pl.program_id(0) == 0)
        def _():
            dg_ref[...] = jnp.zeros_like(dg_ref)

        dh = None
        for k in range(npc):
            t = _nn(p_refs[k][...], w_ref[int(offs[k]):int(offs[k + 1]), :])
            dh = t if dh is None else dh + t
        xv = x_ref[...]
        r = lax.rsqrt(jnp.mean(xv * xv, axis=-1, keepdims=True) + EPS)
        xn = xv * r
        dg_ref[...] += jnp.sum(dh * xn, axis=0, keepdims=True)
        dxn = dh * g_ref[...]
        dx_ref[...] = do_ref[...] + r * (dxn - xn * jnp.mean(dxn * xn, axis=-1, keepdims=True))

    row = lambda wd: pl.BlockSpec((tm, wd), lambda i: (i, 0))
    return pl.pallas_call(
        body, name="in_bwd_x", grid=(T // tm,),
        in_specs=[row(wd) for wd in widths]
                 + [pl.BlockSpec((NCOL, D), lambda i: (0, 0)),
                    row(D), pl.BlockSpec((1, D), lambda i: (0, 0)), row(D)],
        out_specs=[row(D), pl.BlockSpec((1, D), lambda i: (0, 0))],
        out_shape=[SDS((T, D), F32), SDS((1, D), F32)],
        compiler_params=_cparams(("arbitrary",)),
    )(*pieces, wt, x, g, dout)


def _in_bwd_w(ht, piece):
    T = ht.shape[1]
    wd = piece.shape[1]
    tn = 512
    tk = min(T, 2048)
    nk = T // tk

    def body(ht_ref, p_ref, o_ref, acc):
        k = pl.program_id(1)

        @pl.when(k == 0)
        def _():
            acc[...] = jnp.zeros_like(acc)
        acc[...] += _nn(ht_ref[...], p_ref[...])

        @pl.when(k == nk - 1)
        def _():
            o_ref[...] = acc[...].T.astype(BF16)

    return pl.pallas_call(
        body, name="in_bwd_w", grid=(wd // tn, nk),
        in_specs=[pl.BlockSpec((D, tk), lambda j, k: (0, k)), pl.BlockSpec((tk, tn), lambda j, k: (k, j))],
        out_specs=pl.BlockSpec((tn, D), lambda j, k: (j, 0)),
        out_shape=SDS((wd, D), BF16),
        scratch_shapes=[pltpu.VMEM((D, tn), F32)],
        compiler_params=_cparams(("parallel", "arbitrary")),
    )(ht, piece)


def _band_mask():
    qc = np.arange(TQ)[:, None] // CHUNK
    kc = (np.arange(KW)[None, :] - 2 * TQ) // CHUNK
    band = (kc <= qc) & (kc >= qc - LEFT)
    return np.where(band, 0.0, NEG).astype(np.float32)


def _bias_block(table, band_mask):
    nd = TQ + KW - 1
    d = np.arange(nd)
    idx = np.clip(3 * TQ - 1 - d, -MAX_REL, MAX_REL) + MAX_REL
    sel = np.zeros((2 * MAX_REL + 1, nd), np.float32)
    sel[idx, d] = 1.0
    f = jnp.dot(table, jnp.asarray(sel), precision=lax.Precision.HIGHEST)
    fp = jnp.concatenate([f, jnp.zeros((HEADS, 1), F32)], axis=1)
    rows = jnp.tile(fp, (1, TQ))[:, :TQ * nd].reshape(HEADS, TQ, nd)
    return rows[:, :, TQ - 1:TQ - 1 + KW] + band_mask[None]


def _my_id():
    return 4 * lax.axis_index("x") + 2 * lax.axis_index("y") + lax.axis_index("c")


def _peers():
    x, y, c = lax.axis_index("x"), lax.axis_index("y"), lax.axis_index("c")
    out = []
    for k in range(1, N_DEV):
        fx, fy, fc = (k >> 2) & 1, (k >> 1) & 1, k & 1
        px, py, pc = x ^ fx, y ^ fy, c ^ fc
        out.append(((px, py, pc), 4 * px + 2 * py + pc))
    return out


class _Exchange:
    def __init__(self, arrays, scatter):
        self.arrays = list(arrays)
        self.scatter = list(scatter)
        self.n = n = len(arrays)
        hbm = pl.BlockSpec(memory_space=pltpu.HBM)
        self.in_specs = [hbm] * n
        self.out_specs = [hbm] * n
        self.out_shape = [SDS((N_DEV,) + tuple(a.shape[1:] if s else a.shape), a.dtype)
                          for a, s in zip(arrays, scatter)]
        self.scratch = [pltpu.SemaphoreType.DMA((N_DEV - 1, n)), pltpu.SemaphoreType.DMA((N_DEV - 1, n)),
                        pltpu.SemaphoreType.DMA((n,))]

    def split(self, refs, n_in, n_out):
        n = self.n
        own_in = refs[:n_in]
        ex_in = refs[n_in:n_in + n]
        own_out = refs[n_in + n:n_in + n + n_out]
        ex_out = refs[n_in + n + n_out:n_in + 2 * n + n_out]
        rest = refs[n_in + 2 * n + n_out:]
        return own_in, own_out, rest[:-3], (ex_in, ex_out, rest[-3:])

    def _copy(self, ex, k, p, landing):
        in_refs, out_refs, (send_sems, recv_sems, _) = ex
        pos, pid = _peers()[p]
        return pltpu.make_async_remote_copy(
            src_ref=in_refs[k].at[pid] if self.scatter[k] else in_refs[k],
            dst_ref=out_refs[k].at[pid if landing else _my_id()],
            send_sem=send_sems.at[p, k], recv_sem=recv_sems.at[p, k],
            device_id=pos, device_id_type=pl.DeviceIdType.MESH)

    def _own(self, ex, k):
        in_refs, out_refs, (_, _, local_sems) = ex
        me = _my_id()
        return pltpu.make_async_copy(in_refs[k].at[me] if self.scatter[k] else in_refs[k], out_refs[k].at[me],
                                     local_sems.at[k])

    def start(self, ex):
        for k in range(self.n):
            self._own(ex, k).start()
        for p in range(N_DEV - 1):
            for k in range(self.n):
                self._copy(ex, k, p, False).start()

    def finish(self, ex):
        for p in range(N_DEV - 1):
            for k in range(self.n):
                self._copy(ex, k, p, True).wait_recv()
        for p in range(N_DEV - 1):
            for k in range(self.n):
                self._copy(ex, k, p, False).wait_send()
        for k in range(self.n):
            self._own(ex, k).wait()


def _exchange(arrays, scatter, name):
    ex = _Exchange(arrays, scatter)

    def body(*refs):
        _, _, _, exr = ex.split(refs, 0, 0)
        ex.start(exr)
        ex.finish(exr)

    return pl.pallas_call(body, name=name, in_specs=ex.in_specs, out_specs=ex.out_specs,
                          out_shape=ex.out_shape, scratch_shapes=ex.scratch)(*ex.arrays)


def _adamw_sum(parts0, parts1, w, m, v, name):
    _, R, C = w.shape
    tr = R
    while tr * C > 256 * 1024 and tr % 32 == 0:
        tr //= 2
    c1 = 1.0 / (1.0 - ADAM_B1 ** ADAM_STEP)
    c2 = 1.0 / (1.0 - ADAM_B2 ** ADAM_STEP)

    def body(p0_ref, p1_ref, w_ref, m_ref, v_ref, g_ref, d_ref, mo_ref, vo_ref):
        def update(p_ref):
            g = p_ref[0].astype(F32)
            for s in range(1, N_DEV):
                g = g + p_ref[s].astype(F32)
            mn = ADAM_B1 * m_ref[...] + (1.0 - ADAM_B1) * g
            vn = ADAM_B2 * v_ref[...] + (1.0 - ADAM_B2) * (g * g)
            g_ref[...] = g
            mo_ref[...] = mn
            vo_ref[...] = vn
            d_ref[...] = -ADAM_LR * ((mn * c1) / (jnp.sqrt(vn * c2) + ADAM_EPS) + ADAM_WD * w_ref[...])

        @pl.when(pl.program_id(0) == 0)
        def _():
            update(p0_ref)

        @pl.when(pl.program_id(0) == 1)
        def _():
            update(p1_ref)

    blk = pl.BlockSpec((None, tr, C), lambda l, i: (l, i, 0))
    return pl.pallas_call(
        body, name=name, grid=(2, R // tr),
        in_specs=[pl.BlockSpec((N_DEV, tr, C), lambda l, i: (0, i * (1 - l), 0)),
                  pl.BlockSpec((N_DEV, tr, C), lambda l, i: (0, i * l, 0)), blk, blk, blk],
        out_specs=[blk, blk, blk, blk],
        out_shape=[SDS((2, R, C), F32)] * 4,
        compiler_params=_cparams(("arbitrary", "arbitrary")),
    )(parts0, parts1, w, m, v)


def _layer_fwd(x, P, bias, S, ex=None):
    z, ht, *got = _in_proj(x, P["pre_g"], P["w_in_t"], ex)
    ac, ap, u1 = _mix_fwd(z, P["conv_dw"], P["conv_dw_b"], P["conv_ln_g"], P["conv_ln_b"],
                          P["pool_w"], P["pool_b"], P["pool_scale"], S)
    o = _attn_fwd(z, bias, S)
    out = _out_fwd(x, ac, o, ap, z, P["w_conv_out"], P["w_attn_out"], P["w_pool_out"], P["w_out"], P["post_g"])
    return out, (x, z, ht, ac, o, ap, u1), got


def _layer_bwd(dout, saved, P, bias, S, ex=None):
    x, z, ht, ac, o, ap, u1 = saved
    (dac, dao, dag, dap, dgm, dwco, dwao, dwpo, dwout, dpostg, *got) = _out_bwd(
        dout, ac, o, ap, z, P["w_conv_out"], P["w_attn_out"], P["w_pool_out"], P["w_out"], P["post_g"], ex)
    dq, dk, dv, dbias = _attn_bwd(z, dao, bias, S)
    (dzc, dzp, ddw, ddwb, dlng, dlnb, dpw, dpb, dps) = _mix_bwd(
        z, u1, dac, dap, P["conv_dw"], P["conv_dw_b"], P["conv_ln_g"], P["conv_ln_b"],
        P["pool_w"], P["pool_b"], P["pool_scale"], S)
    pieces = [dzc, dq, dk, dv, dag, dzp, dgm]
    dx, dpreg = _in_bwd_x(pieces, P["w_in_t"], x, P["pre_g"], dout)
    dwin_t = [_in_bwd_w(ht, p) for p in pieces]
    grads = dict(pre_norm_g=dpreg, post_norm_g=dpostg, w_in_t=dwin_t, conv_dw=ddw, conv_dw_b=ddwb,
                 conv_ln_g=dlng, conv_ln_b=dlnb, w_conv_out=dwco, dbias=dbias, w_attn_out=dwao,
                 pool_w=dpw, pool_b=dpb, pool_scale=dps, w_pool_out=dwpo, w_out=dwout)
    return dx, grads, got


WEIGHT_NAMES = ("pre_norm_g", "post_norm_g", "w_in", "conv_dw", "conv_dw_b", "conv_ln_g", "conv_ln_b",
                "w_conv_out", "rel_bias", "w_attn_out", "pool_w", "pool_b", "pool_scale", "w_pool_out", "w_out")
SHARDED = ("w_in", "w_conv_out", "w_attn_out", "w_pool_out", "w_out", "conv_dw")
REPLICATED = tuple(n for n in WEIGHT_NAMES if n not in SHARDED)


def _cols_from_slabs(g):
    return g.transpose(1, 0, 2).reshape(g.shape[1], N_DEV * g.shape[2])


def _slabs_from_cols(full):
    r, wd = full.shape
    return full.reshape(r, N_DEV, wd // N_DEV).transpose(1, 0, 2)


def _weight_shards(weights, l):
    return [weights["w_in"][l].T.astype(BF16), weights["w_conv_out"][l].astype(BF16),
            weights["w_attn_out"][l].astype(BF16), weights["w_pool_out"][l].astype(BF16),
            weights["w_out"][l].astype(BF16), weights["conv_dw"][l]]


def _full_weights(got):
    w_in_t, wco, wao, wpo, wout, cdw = got
    return dict(w_in_t=w_in_t.reshape(NCOL, D), w_conv_out=_cols_from_slabs(wco), w_attn_out=_cols_from_slabs(wao),
                w_pool_out=_cols_from_slabs(wpo), w_out=wout.reshape(D, D), conv_dw=_cols_from_slabs(cdw))


def _grad_exchange(g, d_table):
    arrays = {"w_in": jnp.concatenate(g["w_in_t"], axis=0).reshape(N_DEV, NCOL // N_DEV, D),
              "w_conv_out": _slabs_from_cols(g["w_conv_out"].astype(BF16)),
              "w_attn_out": _slabs_from_cols(g["w_attn_out"].astype(BF16)),
              "w_pool_out": _slabs_from_cols(g["w_pool_out"].astype(BF16)),
              "w_out": g["w_out"].astype(BF16).reshape(N_DEV, D // N_DEV, D),
              "conv_dw": _slabs_from_cols(g["conv_dw"].astype(BF16)),
              "pre_norm_g": g["pre_norm_g"], "post_norm_g": g["post_norm_g"], "conv_dw_b": g["conv_dw_b"],
              "conv_ln_g": g["conv_ln_g"], "conv_ln_b": g["conv_ln_b"], "rel_bias": d_table,
              "pool_w": g["pool_w"].reshape(4 * 128, 128), "pool_b": g["pool_b"].reshape(4, 128),
              "pool_scale": g["pool_scale"]}
    names = tuple(arrays)
    return names, [arrays[n] for n in names], [n in SHARDED for n in names]


def kernel(x, pre_norm_g, post_norm_g, w_in, conv_dw, conv_dw_b, conv_ln_g, conv_ln_b, w_conv_out, rel_bias, w_attn_out, pool_w, pool_b, pool_scale, w_pool_out, w_out, loss_target, m_pre_norm_g, m_post_norm_g, m_w_in, m_conv_dw, m_conv_dw_b, m_conv_ln_g, m_conv_ln_b, m_w_conv_out, m_rel_bias, m_w_attn_out, m_pool_w, m_pool_b, m_pool_scale, m_w_pool_out, m_w_out, v_pre_norm_g, v_post_norm_g, v_w_in, v_conv_dw, v_conv_dw_b, v_conv_ln_g, v_conv_ln_b, v_w_conv_out, v_rel_bias, v_w_attn_out, v_pool_w, v_pool_b, v_pool_scale, v_w_pool_out, v_w_out):
    given = dict(locals())
    weights = {n: given[n] for n in WEIGHT_NAMES}
    nb, S, _ = x.shape
    T = nb * S
    L = pre_norm_g.shape[0]
    assert L == 2
    x2 = x.reshape(T, D)
    tgt2 = loss_target.reshape(T, D)
    band = jnp.asarray(_band_mask())

    def local_params(l):
        return dict(pre_g=pre_norm_g[l:l + 1], post_g=post_norm_g[l:l + 1], conv_dw_b=conv_dw_b[l:l + 1],
                    conv_ln_g=conv_ln_g[l:l + 1], conv_ln_b=conv_ln_b[l:l + 1], pool_w=pool_w[l].astype(BF16),
                    pool_b=pool_b[l].reshape(1, BW), pool_scale=pool_scale[l:l + 1])

    biases, bias_vjps = [], []
    for l in range(L):
        b, vjp = jax.vjp(lambda t: _bias_block(t, band), rel_bias[l])
        biases.append(b)
        bias_vjps.append(vjp)

    params = [None] * L
    saved = [None] * L
    params[0] = {**local_params(0),
                 **_full_weights(_exchange(_weight_shards(weights, 0), [False] * 6, "gather_weights_0"))}
    h, saved[0], got = _layer_fwd(x2, params[0], biases[0], S, _Exchange(_weight_shards(weights, 1), [False] * 6))
    params[1] = {**local_params(1), **_full_weights(got)}
    h, saved[1], _ = _layer_fwd(h, params[1], biases[1], S)
    lsum, dout = _loss_grad(h, tgt2)
    loss = lax.psum(lsum[0, 0], MESH_AXES) * (0.5 / D)

    dout, g1, _ = _layer_bwd(dout, saved[1], params[1], biases[1], S)
    names, arrays1, flags = _grad_exchange(g1, bias_vjps[1](g1["dbias"])[0])
    dout, g0, got1 = _layer_bwd(dout, saved[0], params[0], biases[0], S, _Exchange(arrays1, flags))
    _, arrays0, _ = _grad_exchange(g0, bias_vjps[0](g0["dbias"])[0])
    got0 = _exchange(arrays0, flags, "exchange_grads_0")
    parts = [dict(zip(names, got0)), dict(zip(names, got1))]
    grad_x = dout.reshape(x.shape)

    outs = {}
    for n in WEIGHT_NAMES:
        view = (lambda a: a.transpose(0, 2, 1)) if n == "w_in" else (lambda a: a)
        w = view(weights[n])
        shape3 = (L,) + parts[0][n].shape[1:]
        res = _adamw_sum(parts[0][n], parts[1][n], w.reshape(shape3), view(given["m_" + n]).reshape(shape3),
                         view(given["v_" + n]).reshape(shape3), "adamw_" + n)
        outs[n] = [view(a.reshape(w.shape)) for a in res]
    return (loss, grad_x, *[outs[n][0] for n in WEIGHT_NAMES], *[outs[n][1] for n in WEIGHT_NAMES],
            *[outs[n][2] for n in WEIGHT_NAMES], *[outs[n][3] for n in WEIGHT_NAMES])
```

```python
import functools

import numpy as np
import jax
import jax.numpy as jnp
from jax import lax
from jax.experimental import pallas as pl
from jax.experimental.pallas import tpu as pltpu

F32 = jnp.float32
BF16 = jnp.bfloat16
SDS = jax.ShapeDtypeStruct

D = 1024
BW = 512
NCOL = 7680
EPS = 1e-6
NEG = -1e30
HEADS = 8
HD = 64
CHUNK = 64
LEFT = 8
MAX_REL = 256
TQ = 256
KW = 768
CONV_K = 31
WINDOWS = (2, 4, 8, 16)
HALO = 32
RC = 32
N_DEV = 8
MESH_AXES = ("x", "y", "c")

ADAM_LR = 0.001
ADAM_B1 = 0.9
ADAM_B2 = 0.999
ADAM_EPS = 1e-08
ADAM_WD = 0.01
ADAM_STEP = 10

VMEM_LIMIT = 56 * 1024 * 1024

CB_CA, CB_CB, CB_CG, CB_Q, CB_K, CB_V, CB_AG, CB_PI, CB_PG = range(9)
DZ_PIECES = (("conv", 1536), ("q", 512), ("k", 512), ("v", 512), ("ag", 512), ("pool", 1024), ("gm", 3072))


def _cparams(sem):
    return pltpu.CompilerParams(dimension_semantics=sem, vmem_limit_bytes=VMEM_LIMIT)


def _sig(x):
    return 1.0 / (1.0 + jnp.exp(-x))


def _dsilu(x, s):
    return s * (1.0 + x * (1.0 - s))


def _nt(a, b):
    return lax.dot_general(a, b, (((1,), (1,)), ((), ())), preferred_element_type=F32)


def _tn(a, b):
    return lax.dot_general(a, b, (((0,), (0,)), ((), ())), preferred_element_type=F32)


def _nn(a, b):
    return jnp.dot(a, b, preferred_element_type=F32)


def _rows8(x):
    return x[0:8] + x[8:16] + x[16:24] + x[24:32]


def _call(body, name, grid, in_specs, out_specs, out_shape, scratch, args, sem, ex=None):
    if ex is None:
        return pl.pallas_call(body, name=name, grid=grid, in_specs=in_specs, out_specs=out_specs,
                              out_shape=out_shape, scratch_shapes=scratch, compiler_params=_cparams(sem))(*args)
    n_in, n_out = len(in_specs), len(out_specs)
    steps = int(np.prod(grid))

    def carrier(*refs):
        own_in, own_out, own_scr, exr = ex.split(refs, n_in, n_out)
        step = pl.program_id(0)
        for axis in range(1, len(grid)):
            step = step * grid[axis] + pl.program_id(axis)

        @pl.when(step == 0)
        def _():
            ex.start(exr)

        body(*own_in, *own_out, *own_scr)

        @pl.when(step == steps - 1)
        def _():
            ex.finish(exr)

    return pl.pallas_call(
        carrier, name=name + "_carrier", grid=grid, in_specs=in_specs + ex.in_specs,
        out_specs=out_specs + ex.out_specs, out_shape=out_shape + ex.out_shape,
        scratch_shapes=scratch + ex.scratch, compiler_params=_cparams(("arbitrary",) * len(grid)),
    )(*args, *ex.arrays)


def _in_proj(x, g, wt, ex=None):
    T = x.shape[0]
    tm = min(T, 1024)
    tn = 1536

    def body(x_ref, g_ref, w_ref, z_ref, ht_ref, h_scr):
        @pl.when(pl.program_id(1) == 0)
        def _():
            xv = x_ref[...]
            r = lax.rsqrt(jnp.mean(xv * xv, axis=-1, keepdims=True) + EPS)
            h = xv * r * g_ref[...]
            h_scr[...] = h.astype(BF16)
            ht_ref[...] = h.T.astype(BF16)
        z_ref[...] = _nt(h_scr[...], w_ref[...]).astype(BF16)

    return _call(
        body, "in_proj", (T // tm, NCOL // tn),
        [pl.BlockSpec((tm, D), lambda i, j: (i, 0)), pl.BlockSpec((1, D), lambda i, j: (0, 0)),
         pl.BlockSpec((tn, D), lambda i, j: (j, 0))],
        [pl.BlockSpec((tm, tn), lambda i, j: (i, j)), pl.BlockSpec((D, tm), lambda i, j: (0, i))],
        [SDS((T, NCOL), BF16), SDS((D, T), BF16)], [pltpu.VMEM((tm, D), BF16)],
        (x, g, wt), ("parallel", "arbitrary"), ex)


def _stencil(load, w_ref, lo, hi, tap_of):
    out = None
    for r in range(8):
        n = RC if r == 0 else RC + 8
        v = None
        for q in range((hi - r) // 8 + 1):
            o = 8 * q + r
            if o < lo:
                continue
            j = tap_of(o)
            term = w_ref[j:j + 1, :] * load(q, n)
            v = term if v is None else v + term
        if v is None:
            continue
        if r:
            v = pltpu.roll(v, n - r, axis=0)[0:RC]
        out = v if out is None else out + v
    return out


def _layer_norm_fwd(u1):
    mu = jnp.mean(u1, axis=-1, keepdims=True)
    xc = u1 - mu
    rstd = lax.rsqrt(jnp.mean(xc * xc, axis=-1, keepdims=True) + EPS)
    return xc * rstd, rstd


def _window_sums(x, w, back):
    n = x.shape[0]
    s = x
    k = 1
    while k < w:
        s = s + pltpu.roll(s, k if back else n - k, axis=0)
        k *= 2
    return s


def _pool_chunk(pwin, t_first):
    t = t_first + lax.broadcasted_iota(jnp.int32, (RC, 128), 0)
    outs = []
    for g, w in enumerate(WINDOWS):
        x = pwin[:, g * 128:(g + 1) * 128]
        s = _window_sums(x, w, True)
        cnt = jnp.minimum(t + 1, w).astype(F32)
        outs.append(s[HALO:HALO + RC] / cnt - x[HALO:HALO + RC])
    return outs


def _mix_fwd(z, dw, dwb, lng, lnb, pw, pb, ps, S, ex=None):
    T = z.shape[0]
    tm = 512
    ts = S // tm
    nh = tm // HALO

    def body(ca_ref, cb_ref, cg_ref, pi_ref, pg_ref, cah_ref, cbh_ref, pih_ref,
             dw_ref, dwb_ref, lng_ref, lnb_ref, pw_ref, pb_ref, ps_ref,
             ac_ref, ap_ref, u1_ref, ubuf, pbuf, pooled):
        i = pl.program_id(0)
        keep = jnp.where((i % ts) == 0, 0.0, 1.0)
        ubuf[0:HALO, :] = cah_ref[...].astype(F32) * _sig(cbh_ref[...].astype(F32)) * keep
        ubuf[HALO:HALO + tm, :] = ca_ref[...].astype(F32) * _sig(cb_ref[...].astype(F32))
        pbuf[0:HALO, :] = pih_ref[...].astype(F32) * keep
        pbuf[HALO:HALO + tm, :] = pi_ref[...].astype(F32)
        t0 = (i % ts) * tm

        def chunk(c, carry):
            base = pl.multiple_of(c * RC, RC)
            load = lambda q, n: ubuf[pl.ds(base + 8 * q, n), :]
            u1 = _stencil(load, dw_ref, 2, CONV_K + 1, lambda o: o - 2) + dwb_ref[...]
            u1_ref[pl.ds(base, RC), :] = u1
            n, _ = _layer_norm_fwd(u1)
            u2 = n * lng_ref[...] + lnb_ref[...]
            u3 = u2 * _sig(u2)
            cg = cg_ref[pl.ds(base, RC), :].astype(F32)
            ac_ref[pl.ds(base, RC), :] = (u3 * cg * _sig(cg)).astype(BF16)
            pwin = pbuf[pl.ds(base, RC + HALO), :]
            outs = _pool_chunk(pwin, t0 + base)
            for g in range(4):
                pooled[pl.ds(base, RC), g * 128:(g + 1) * 128] = outs[g].astype(BF16)
            return carry

        lax.fori_loop(0, tm // RC, chunk, 0)
        pg = pg_ref[...].astype(F32)
        spg = pg * _sig(pg)
        for g in range(4):
            sl = slice(g * 128, (g + 1) * 128)
            mixed = (_nn(pooled[:, sl], pw_ref[g]) + pb_ref[:, sl]) * ps_ref[:, sl]
            ap_ref[:, sl] = (mixed * spg[:, sl]).astype(BF16)

    def zmain(cb):
        return pl.BlockSpec((tm, BW), lambda i: (i, cb))

    def zprev(cb):
        return pl.BlockSpec((HALO, BW), lambda i: (jnp.maximum(i * nh - 1, 0), cb))

    full = lambda shape: pl.BlockSpec(shape, lambda i: (0,) * len(shape))
    row = pl.BlockSpec((tm, BW), lambda i: (i, 0))
    return _call(
        body, "mix_fwd", (T // tm,),
        [zmain(CB_CA), zmain(CB_CB), zmain(CB_CG), zmain(CB_PI), zmain(CB_PG),
         zprev(CB_CA), zprev(CB_CB), zprev(CB_PI),
         full((CONV_K, BW)), full((1, BW)), full((1, BW)), full((1, BW)),
         full((4, 128, 128)), full((1, BW)), full((1, BW))],
        [row, row, row], [SDS((T, BW), BF16), SDS((T, BW), BF16), SDS((T, BW), F32)],
        [pltpu.VMEM((HALO + tm, BW), F32), pltpu.VMEM((HALO + tm, BW), F32), pltpu.VMEM((tm, BW), BF16)],
        (z, z, z, z, z, z, z, z, dw, dwb, lng, lnb, pw, pb, ps), ("parallel",), ex)


def _attn_specs(nq):
    def kv(cb, off):
        return pl.BlockSpec((TQ, BW), lambda i: (i - jnp.minimum(off, i % nq), cb))
    return [pl.BlockSpec((TQ, BW), lambda i: (i, CB_Q)),
            kv(CB_K, 2), kv(CB_K, 1), kv(CB_K, 0), kv(CB_V, 2), kv(CB_V, 1), kv(CB_V, 0)]


def _softmax_rows(s):
    m = jnp.max(s, axis=-1, keepdims=True)
    e = jnp.exp(s - m)
    return e / jnp.sum(e, axis=-1, keepdims=True)


NSKEW = 1024


def _skew_table(table):
    return jnp.dot(table, jnp.asarray(_skew_select()), precision=lax.Precision.HIGHEST)


def _skew_select():
    d = np.arange(TQ + KW - 1)
    idx = np.clip(3 * TQ - 1 - d, -MAX_REL, MAX_REL) + MAX_REL
    sel = np.zeros((2 * MAX_REL + 1, NSKEW), np.float32)
    sel[idx, d] = 1.0
    return sel


def _bias_from_skew(f_ref, band_ref, bias_scr):
    for h in range(HEADS):
        rows = jnp.broadcast_to(f_ref[h:h + 1, :], (TQ, NSKEW))
        rows = pltpu.roll(rows, NSKEW - (TQ - 1), axis=1, stride=1, stride_axis=0)
        bias_scr[h] = rows[:, 0:KW] + band_ref[...]


def _skew_from_bias(db):
    i = lax.broadcasted_iota(jnp.int32, (TQ, TQ), 0)
    j = lax.broadcasted_iota(jnp.int32, (TQ, TQ), 1)
    flip = jnp.where(i + j == TQ - 1, 1.0, 0.0).astype(BF16)
    hi = db.astype(BF16)
    lo = (db - hi.astype(F32)).astype(BF16)
    rev = _nn(flip, hi) + _nn(flip, lo)
    rev = jnp.concatenate([rev, jnp.zeros((TQ, NSKEW - KW), F32)], axis=1)
    return jnp.sum(pltpu.roll(rev, 0, axis=1, stride=1, stride_axis=0), axis=0, keepdims=True)


def _attn_fwd(z, f, band, S, ex=None):
    T = z.shape[0]
    nq = S // TQ

    def body(q_ref, k2_ref, k1_ref, k0_ref, v2_ref, v1_ref, v0_ref, f_ref, band_ref, o_ref, kbuf, vbuf, b_scr):
        @pl.when(pl.program_id(0) == 0)
        def _():
            _bias_from_skew(f_ref, band_ref, b_scr)

        qb = pl.program_id(0) % nq
        kbuf[0:TQ, :] = k2_ref[...]
        kbuf[TQ:2 * TQ, :] = k1_ref[...]
        kbuf[2 * TQ:KW, :] = k0_ref[...]
        vbuf[0:TQ, :] = v2_ref[...]
        vbuf[TQ:2 * TQ, :] = v1_ref[...]
        vbuf[2 * TQ:KW, :] = v0_ref[...]
        pad = lax.broadcasted_iota(jnp.int32, (TQ, KW), 1) < (2 - qb) * TQ
        lane = lax.broadcasted_iota(jnp.int32, (1, 128), 1)
        for hp in range(HEADS // 2):
            sl = slice(hp * 128, (hp + 1) * 128)
            qp = q_ref[:, sl] * 0.125
            kp = kbuf[:, sl]
            vp = vbuf[:, sl]
            acc = None
            for e in range(2):
                msk = (lane < HD) if e == 0 else (lane >= HD)
                qm = jnp.where(msk, qp, jnp.zeros_like(qp))
                s = _nt(qm, kp) + b_scr[2 * hp + e]
                p = _softmax_rows(jnp.where(pad, NEG, s)).astype(BF16)
                vm = jnp.where(msk, vp, jnp.zeros_like(vp))
                o = _nn(p, vm)
                acc = o if acc is None else acc + o
            o_ref[:, sl] = acc.astype(BF16)

    full = lambda shape: pl.BlockSpec(shape, lambda i: (0,) * len(shape))
    return _call(
        body, "attn_fwd", (T // TQ,),
        _attn_specs(nq) + [full((HEADS, NSKEW)), full((TQ, KW))],
        [pl.BlockSpec((TQ, BW), lambda i: (i, 0))], [SDS((T, BW), BF16)],
        [pltpu.VMEM((KW, BW), BF16), pltpu.VMEM((KW, BW), BF16), pltpu.VMEM((HEADS, TQ, KW), F32)],
        (z, z, z, z, z, z, z, f, band), ("arbitrary",), ex)


def _gates(gl_ref, gh_ref):
    gl = _sig(gl_ref[...].astype(F32))
    gh = _sig(gh_ref[...].astype(F32))
    return (gl[:, 0:D], jnp.concatenate([gl[:, D:1536], gh[:, 0:512]], axis=1), gh[:, 512:1536])


def _out_specs_in(tm):
    row = lambda w: pl.BlockSpec((tm, w), lambda i: (i, 0))
    full = lambda shape: pl.BlockSpec(shape, lambda i: (0,) * len(shape))
    return [row(BW), row(BW), row(BW),
            pl.BlockSpec((tm, BW), lambda i: (i, CB_AG)),
            pl.BlockSpec((tm, 1536), lambda i: (i, 3)),
            pl.BlockSpec((tm, 1536), lambda i: (i, 4)),
            full((BW, D)), full((BW, D)), full((BW, D)), full((D, D)), full((1, D))]


def _out_fwd(x, ac, o, ap, z, wco, wao, wpo, wout, postg, ex=None):
    T = x.shape[0]
    tm = 512

    def body(ac_ref, o_ref, ap_ref, ag_ref, gl_ref, gh_ref, wco_ref, wao_ref, wpo_ref, wout_ref, pg_ref,
             x_ref, out_ref):
        ag = ag_ref[...].astype(F32)
        aat = (o_ref[...].astype(F32) * ag * _sig(ag)).astype(BF16)
        g0, g1, g2 = _gates(gl_ref, gh_ref)
        merged = g0 * _nn(ac_ref[...], wco_ref[...])
        merged = merged + g1 * _nn(aat, wao_ref[...])
        merged = merged + g2 * _nn(ap_ref[...], wpo_ref[...])
        y = _nn(merged.astype(BF16), wout_ref[...])
        ry = lax.rsqrt(jnp.mean(y * y, axis=-1, keepdims=True) + EPS)
        out_ref[...] = x_ref[...] + y * ry * pg_ref[...]

    return _call(
        body, "out_fwd", (T // tm,), _out_specs_in(tm) + [pl.BlockSpec((tm, D), lambda i: (i, 0))],
        [pl.BlockSpec((tm, D), lambda i: (i, 0))], [SDS((T, D), F32)], [],
        (ac, o, ap, z, z, z, wco, wao, wpo, wout, postg, x), ("parallel",), ex)


def _loss_grad(out, tgt):
    T = out.shape[0]
    tm = 1024 if T % 1024 == 0 else T

    def body(o_ref, t_ref, l_ref, d_ref):
        @pl.when(pl.program_id(0) == 0)
        def _():
            l_ref[...] = jnp.zeros_like(l_ref)
        d = o_ref[...] - t_ref[...]
        d_ref[...] = d * (1.0 / D)
        sq = jnp.sum(d * d, axis=0, keepdims=True)
        l_ref[...] += jnp.sum(sq, axis=1, keepdims=True)

    return pl.pallas_call(
        body, name="loss_grad", grid=(T // tm,),
        in_specs=[pl.BlockSpec((tm, D), lambda i: (i, 0)), pl.BlockSpec((tm, D), lambda i: (i, 0))],
        out_specs=[pl.BlockSpec((1, 128), lambda i: (0, 0)), pl.BlockSpec((tm, D), lambda i: (i, 0))],
        out_shape=[SDS((1, 128), F32), SDS((T, D), F32)],
        compiler_params=_cparams(("arbitrary",)),
    )(out, tgt)


def _out_bwd(dout, ac, o, ap, z, wco, wao, wpo, wout, postg, ex=None):
    T = dout.shape[0]
    tm = 256

    def body(ac_ref, o_ref, ap_ref, ag_ref, gl_ref, gh_ref, wco_ref, wao_ref, wpo_ref, wout_ref, pg_ref, do_ref,
             dac_ref, dao_ref, dag_ref, dap_ref, dgm_ref, dwco_ref, dwao_ref, dwpo_ref, dwout_ref, dpg_ref):
        @pl.when(pl.program_id(0) == 0)
        def _():
            for r in (dwco_ref, dwao_ref, dwpo_ref, dwout_ref, dpg_ref):
                r[...] = jnp.zeros_like(r)

        ag = ag_ref[...].astype(F32)
        sag = _sig(ag)
        ov = o_ref[...].astype(F32)
        acts = (ac_ref[...], (ov * ag * sag).astype(BF16), ap_ref[...])
        ws = (wco_ref, wao_ref, wpo_ref)
        gates = _gates(gl_ref, gh_ref)
        ys = [_nn(acts[b], ws[b][...]) for b in range(3)]
        merged = (gates[0] * ys[0] + gates[1] * ys[1] + gates[2] * ys[2]).astype(BF16)
        y = _nn(merged, wout_ref[...])
        ry = lax.rsqrt(jnp.mean(y * y, axis=-1, keepdims=True) + EPS)
        yn = y * ry
        dout_v = do_ref[...]
        dpg_ref[...] += jnp.sum(dout_v * yn, axis=0, keepdims=True)
        dyn = dout_v * pg_ref[...]
        dy = (ry * (dyn - yn * jnp.mean(dyn * yn, axis=-1, keepdims=True))).astype(BF16)
        dmerged = _nt(dy, wout_ref[...])
        dwout_ref[...] += _tn(merged, dy)
        dws = (dwco_ref, dwao_ref, dwpo_ref)
        das = []
        for b in range(3):
            gb = gates[b]
            dgm_ref[:, b * D:(b + 1) * D] = (dmerged * ys[b] * gb * (1.0 - gb)).astype(BF16)
            dyb = (dmerged * gb).astype(BF16)
            dws[b][...] += _tn(acts[b], dyb)
            das.append(_nt(dyb, ws[b][...]))
        dac_ref[...] = das[0].astype(BF16)
        dap_ref[...] = das[2].astype(BF16)
        dao_ref[...] = (das[1] * ag * sag).astype(BF16)
        dag_ref[...] = (das[1] * ov * _dsilu(ag, sag)).astype(BF16)

    row = lambda w: pl.BlockSpec((tm, w), lambda i: (i, 0))
    full = lambda shape: pl.BlockSpec(shape, lambda i: (0,) * len(shape))
    return _call(
        body, "out_bwd", (T // tm,), _out_specs_in(tm) + [row(D)],
        [row(BW), row(BW), row(BW), row(BW), row(3 * D),
         full((BW, D)), full((BW, D)), full((BW, D)), full((D, D)), full((1, D))],
        [SDS((T, BW), BF16)] * 4 + [SDS((T, 3 * D), BF16)]
        + [SDS((BW, D), F32)] * 3 + [SDS((D, D), F32), SDS((1, D), F32)], [],
        (ac, o, ap, z, z, z, wco, wao, wpo, wout, postg, dout), ("arbitrary",), ex)


def _attn_bwd(z, dao, f, band, S, ex=None):
    T = z.shape[0]
    nq = S // TQ
    nsteps = T // TQ

    def body(q_ref, k2_ref, k1_ref, k0_ref, v2_ref, v1_ref, v0_ref, do_ref, f_ref, band_ref,
             dq_ref, dk_ref, dv_ref, df_ref, kbuf, vbuf, dkacc, dvacc, b_scr, db_scr):
        i = pl.program_id(0)
        qb = i % nq

        @pl.when(i == 0)
        def _():
            _bias_from_skew(f_ref, band_ref, b_scr)
            db_scr[...] = jnp.zeros_like(db_scr)

        @pl.when(qb == 0)
        def _():
            dkacc[...] = jnp.zeros_like(dkacc)
            dvacc[...] = jnp.zeros_like(dvacc)

        kbuf[0:TQ, :] = k2_ref[...]
        kbuf[TQ:2 * TQ, :] = k1_ref[...]
        kbuf[2 * TQ:KW, :] = k0_ref[...]
        vbuf[0:TQ, :] = v2_ref[...]
        vbuf[TQ:2 * TQ, :] = v1_ref[...]
        vbuf[2 * TQ:KW, :] = v0_ref[...]
        pad = lax.broadcasted_iota(jnp.int32, (TQ, KW), 1) < (2 - qb) * TQ
        lane = lax.broadcasted_iota(jnp.int32, (1, 128), 1)
        row0 = pl.multiple_of(qb * TQ, TQ)
        for hp in range(HEADS // 2):
            sl = slice(hp * 128, (hp + 1) * 128)
            qp = q_ref[:, sl] * 0.125
            kp = kbuf[:, sl]
            vp = vbuf[:, sl]
            dop = do_ref[:, sl]
            dq_acc = dk_acc = dv_acc = None
            for e in range(2):
                h = 2 * hp + e
                msk = (lane < HD) if e == 0 else (lane >= HD)
                qm = jnp.where(msk, qp, jnp.zeros_like(qp))
                dom = jnp.where(msk, dop, jnp.zeros_like(dop))
                km = jnp.where(msk, kp, jnp.zeros_like(kp))
                s = _nt(qm, kp) + b_scr[h]
                p = _softmax_rows(jnp.where(pad, NEG, s))
                dp = _nt(dom, vp)
                ds = p * (dp - jnp.sum(p * dp, axis=-1, keepdims=True))
                db_scr[h] += ds
                dsb = ds.astype(BF16)
                dq_h = _nn(dsb, km) * 0.125
                dk_h = _tn(dsb, qm)
                dv_h = _tn(p.astype(BF16), dom)
                dq_acc = dq_h if dq_acc is None else dq_acc + dq_h
                dk_acc = dk_h if dk_acc is None else dk_acc + dk_h
                dv_acc = dv_h if dv_acc is None else dv_acc + dv_h
            dq_ref[:, sl] = dq_acc.astype(BF16)
            dkacc[pl.ds(row0, KW), sl] += dk_acc
            dvacc[pl.ds(row0, KW), sl] += dv_acc

        @pl.when(qb == nq - 1)
        def _():
            dk_ref[...] = dkacc[2 * TQ:2 * TQ + S, :].astype(BF16)
            dv_ref[...] = dvacc[2 * TQ:2 * TQ + S, :].astype(BF16)

        @pl.when(i == nsteps - 1)
        def _():
            for h in range(HEADS):
                df_ref[h:h + 1, :] = _skew_from_bias(db_scr[h])

    full = lambda shape: pl.BlockSpec(shape, lambda i: (0,) * len(shape))
    return _call(
        body, "attn_bwd", (nsteps,),
        _attn_specs(nq) + [pl.BlockSpec((TQ, BW), lambda i: (i, 0)), full((HEADS, NSKEW)), full((TQ, KW))],
        [pl.BlockSpec((TQ, BW), lambda i: (i, 0)), pl.BlockSpec((S, BW), lambda i: (i // nq, 0)),
         pl.BlockSpec((S, BW), lambda i: (i // nq, 0)), full((HEADS, NSKEW))],
        [SDS((T, BW), BF16)] * 3 + [SDS((HEADS, NSKEW), F32)],
        [pltpu.VMEM((KW, BW), BF16), pltpu.VMEM((KW, BW), BF16),
         pltpu.VMEM((S + 2 * TQ, BW), F32), pltpu.VMEM((S + 2 * TQ, BW), F32),
         pltpu.VMEM((HEADS, TQ, KW), F32), pltpu.VMEM((HEADS, TQ, KW), F32)],
        (z, z, z, z, z, z, z, dao, f, band), ("arbitrary",), ex)


def _mix_bwd(z, u1, dac, dap, dw, dwb, lng, lnb, pw, pb, ps, S):
    T = z.shape[0]
    tm = 512
    ts = S // tm
    nh = tm // HALO
    nsteps = T // tm
    nblk32 = T // HALO

    def body(ca_ref, cb_ref, cg_ref, pi_ref, pg_ref, u1_ref, dac_ref, dap_ref,
             cah_ref, cbh_ref, pih_ref,
             cgn_ref, pgn_ref, u1n_ref, dacn_ref, dapn_ref,
             dw_ref, dwb_ref, lng_ref, lnb_ref, pw_ref, pb_ref, ps_ref,
             dzc_ref, dzp_ref, ddw_ref, ddwb_ref, dlng_ref, dlnb_ref, dpw_ref, dpb_ref, dps_ref,
             ubuf, gbuf, pbuf, qbuf, pooled, dwacc, vacc):
        i = pl.program_id(0)
        keep_prev = jnp.where((i % ts) == 0, 0.0, 1.0)
        keep_next = jnp.where((i % ts) == ts - 1, 0.0, 1.0)
        t0 = (i % ts) * tm

        @pl.when(i == 0)
        def _():
            dwacc[...] = jnp.zeros_like(dwacc)
            vacc[...] = jnp.zeros_like(vacc)
            dpw_ref[...] = jnp.zeros_like(dpw_ref)
            dpb_ref[...] = jnp.zeros_like(dpb_ref)
            dps_ref[...] = jnp.zeros_like(dps_ref)

        ubuf[0:HALO, :] = cah_ref[...].astype(F32) * _sig(cbh_ref[...].astype(F32)) * keep_prev
        ubuf[HALO:HALO + tm, :] = ca_ref[...].astype(F32) * _sig(cb_ref[...].astype(F32))
        pbuf[0:HALO, :] = pih_ref[...].astype(F32) * keep_prev
        pbuf[HALO:HALO + tm, :] = pi_ref[...].astype(F32)

        def norm_back(u1v, cg, dacv):
            n, rstd = _layer_norm_fwd(u1v)
            u2 = n * lng_ref[...] + lnb_ref[...]
            s2 = _sig(u2)
            scg = _sig(cg)
            du2 = dacv * cg * scg * _dsilu(u2, s2)
            dn = du2 * lng_ref[...]
            du1 = rstd * (dn - jnp.mean(dn, axis=-1, keepdims=True)
                          - n * jnp.mean(dn * n, axis=-1, keepdims=True))
            return du1, du2, n, dacv * u2 * s2 * _dsilu(cg, scg)

        def chunk_a(c, carry):
            base = pl.multiple_of(c * RC, RC)
            du1, du2, n, dcg = norm_back(u1_ref[pl.ds(base, RC), :], cg_ref[pl.ds(base, RC), :].astype(F32),
                                         dac_ref[pl.ds(base, RC), :].astype(F32))
            gbuf[pl.ds(base, RC), :] = du1
            dzc_ref[pl.ds(base, RC), 2 * BW:3 * BW] = dcg.astype(BF16)
            vacc[0:8, :] += _rows8(du2 * n)
            vacc[8:16, :] += _rows8(du2)
            vacc[16:24, :] += _rows8(du1)
            padded = jnp.concatenate([du1, jnp.zeros((8, BW), F32)], axis=0)
            for r in range(8):
                nrow = RC if r == 0 else RC + 8
                g = du1 if r == 0 else pltpu.roll(padded, r, axis=0)
                for q in range((CONV_K + 1 - r) // 8 + 1):
                    o = 8 * q + r
                    if o < 2:
                        continue
                    prod = g * ubuf[pl.ds(base + 8 * q, nrow), :]
                    red = prod[0:8]
                    for k in range(1, nrow // 8):
                        red = red + prod[8 * k:8 * k + 8]
                    dwacc[8 * (o - 2):8 * (o - 2) + 8, :] += red
            return carry

        lax.fori_loop(0, tm // RC, chunk_a, 0)
        du1n, _, _, _ = norm_back(u1n_ref[...], cgn_ref[...].astype(F32), dacn_ref[...].astype(F32))
        gbuf[tm:tm + HALO, :] = du1n * keep_next

        def chunk_p(c, carry):
            base = pl.multiple_of(c * RC, RC)
            outs = _pool_chunk(pbuf[pl.ds(base, RC + HALO), :], t0 + base)
            for g in range(4):
                pooled[pl.ds(base, RC), g * 128:(g + 1) * 128] = outs[g].astype(BF16)
            return carry

        lax.fori_loop(0, tm // RC, chunk_p, 0)

        def cnt_of(t_first, rows, w):
            t = t_first + lax.broadcasted_iota(jnp.int32, (rows, 128), 0)
            return jnp.minimum(t + 1, w).astype(F32)

        pg = pg_ref[...].astype(F32)
        spg_s = _sig(pg)
        dapv = dap_ref[...].astype(F32)
        pgn = pgn_ref[...].astype(F32)
        dmixn = dapn_ref[...].astype(F32) * pgn * _sig(pgn) * ps_ref[...] * keep_next
        for g, w in enumerate(WINDOWS):
            sl = slice(g * 128, (g + 1) * 128)
            mixed_u = _nn(pooled[:, sl], pw_ref[g]) + pb_ref[:, sl]
            dap_g = dapv[:, sl]
            pg_g = pg[:, sl]
            s_g = spg_s[:, sl]
            silu_g = pg_g * s_g
            dps_ref[:, sl] += jnp.sum(dap_g * silu_g * mixed_u, axis=0, keepdims=True)
            dzp_ref[:, BW + g * 128:BW + (g + 1) * 128] = (
                dap_g * mixed_u * ps_ref[:, sl] * _dsilu(pg_g, s_g)).astype(BF16)
            dmix = dap_g * silu_g * ps_ref[:, sl]
            dpb_ref[:, sl] += jnp.sum(dmix, axis=0, keepdims=True)
            dmixb = dmix.astype(BF16)
            dpw_ref[g] += _tn(pooled[:, sl], dmixb)
            qbuf[0:tm, sl] = _nt(dmixb, pw_ref[g]) / cnt_of(t0, tm, w)
            qbuf[tm:tm + HALO, sl] = _nt(dmixn[:, sl].astype(BF16), pw_ref[g]) / cnt_of(t0 + tm, HALO, w)

        def chunk_b(c, carry):
            base = pl.multiple_of(c * RC, RC)
            load = lambda q, n: gbuf[pl.ds(base + 8 * q, n), :]
            du0 = _stencil(load, dw_ref, 0, CONV_K - 1, lambda o: CONV_K - 1 - o)
            ca = ca_ref[pl.ds(base, RC), :].astype(F32)
            sb = _sig(cb_ref[pl.ds(base, RC), :].astype(F32))
            dzc_ref[pl.ds(base, RC), 0:BW] = (du0 * sb).astype(BF16)
            dzc_ref[pl.ds(base, RC), BW:2 * BW] = (du0 * ca * sb * (1.0 - sb)).astype(BF16)
            qwin = qbuf[pl.ds(base, RC + HALO), :]
            t = t0 + base + lax.broadcasted_iota(jnp.int32, (RC, 128), 0)
            for g, w in enumerate(WINDOWS):
                x = qwin[:, g * 128:(g + 1) * 128]
                s = _window_sums(x, w, False)
                cnt = jnp.minimum(t + 1, w).astype(F32)
                dzp_ref[pl.ds(base, RC), g * 128:(g + 1) * 128] = (s[0:RC] - cnt * x[0:RC]).astype(BF16)
            return carry

        lax.fori_loop(0, tm // RC, chunk_b, 0)

        @pl.when(i == nsteps - 1)
        def _():
            dlng_ref[...] = jnp.sum(vacc[0:8, :], axis=0, keepdims=True)
            dlnb_ref[...] = jnp.sum(vacc[8:16, :], axis=0, keepdims=True)
            ddwb_ref[...] = jnp.sum(vacc[16:24, :], axis=0, keepdims=True)
            for j in range(CONV_K):
                ddw_ref[j:j + 1, :] = jnp.sum(dwacc[8 * j:8 * j + 8, :], axis=0, keepdims=True)

    def zmain(cb):
        return pl.BlockSpec((tm, BW), lambda i: (i, cb))

    def zprev(cb):
        return pl.BlockSpec((HALO, BW), lambda i: (jnp.maximum(i * nh - 1, 0), cb))

    def znext(cb):
        return pl.BlockSpec((HALO, BW), lambda i: (jnp.minimum((i + 1) * nh, nblk32 - 1), cb))

    row = lambda w: pl.BlockSpec((tm, w), lambda i: (i, 0))
    full = lambda shape: pl.BlockSpec(shape, lambda i: (0,) * len(shape))
    return pl.pallas_call(
        body, name="mix_bwd", grid=(nsteps,),
        in_specs=[zmain(CB_CA), zmain(CB_CB), zmain(CB_CG), zmain(CB_PI), zmain(CB_PG), row(BW), row(BW), row(BW),
                  zprev(CB_CA), zprev(CB_CB), zprev(CB_PI),
                  znext(CB_CG), znext(CB_PG), znext(0), znext(0), znext(0),
                  full((CONV_K, BW)), full((1, BW)), full((1, BW)), full((1, BW)),
                  full((4, 128, 128)), full((1, BW)), full((1, BW))],
        out_specs=[row(3 * BW), row(2 * BW), full((CONV_K, BW)), full((1, BW)), full((1, BW)), full((1, BW)),
                   full((4, 128, 128)), full((1, BW)), full((1, BW))],
        out_shape=[SDS((T, 3 * BW), BF16), SDS((T, 2 * BW), BF16), SDS((CONV_K, BW), F32),
                   SDS((1, BW), F32), SDS((1, BW), F32), SDS((1, BW), F32),
                   SDS((4, 128, 128), F32), SDS((1, BW), F32), SDS((1, BW), F32)],
        scratch_shapes=[pltpu.VMEM((HALO + tm, BW), F32), pltpu.VMEM((tm + HALO, BW), F32),
                        pltpu.VMEM((HALO + tm, BW), F32), pltpu.VMEM((tm + HALO, BW), F32),
                        pltpu.VMEM((tm, BW), BF16), pltpu.VMEM((8 * CONV_K, BW), F32),
                        pltpu.VMEM((24, BW), F32)],
        compiler_params=_cparams(("arbitrary",)),
    )(z, z, z, z, z, u1, dac, dap, z, z, z, z, z, u1, dac, dap, dw, dwb, lng, lnb, pw, pb, ps)


def _in_bwd_x(pieces, wt, x, g, dout, ex=None):
    T = x.shape[0]
    tm = 256
    widths = [p.shape[1] for p in pieces]
    offs = np.cumsum([0] + widths)
    npc = len(pieces)

    def body(*refs):
        p_refs = refs[:npc]
        w_ref, x_ref, g_ref, do_ref, dx_ref, dg_ref = refs[npc:]

        @pl.when(pl.program_id(0) == 0)
        def _():
            dg_ref[...] = jnp.zeros_like(dg_ref)

        dh = None
        for k in range(npc):
            t = _nn(p_refs[k][...], w_ref[int(offs[k]):int(offs[k + 1]), :])
            dh = t if dh is None else dh + t
        xv = x_ref[...]
        r = lax.rsqrt(jnp.mean(xv * xv, axis=-1, keepdims=True) + EPS)
        xn = xv * r
        dg_ref[...] += jnp.sum(dh * xn, axis=0, keepdims=True)
        dxn = dh * g_ref[...]
        dx_ref[...] = do_ref[...] + r * (dxn - xn * jnp.mean(dxn * xn, axis=-1, keepdims=True))

    row = lambda wd: pl.BlockSpec((tm, wd), lambda i: (i, 0))
    return _call(
        body, "in_bwd_x", (T // tm,),
        [row(wd) for wd in widths] + [pl.BlockSpec((NCOL, D), lambda i: (0, 0)),
                                      row(D), pl.BlockSpec((1, D), lambda i: (0, 0)), row(D)],
        [row(D), pl.BlockSpec((1, D), lambda i: (0, 0))], [SDS((T, D), F32), SDS((1, D), F32)], [],
        (*pieces, wt, x, g, dout), ("arbitrary",), ex)


def _in_bwd_w(ht, piece, ex=None):
    T = ht.shape[1]
    wd = piece.shape[1]
    tn = 512
    tk = min(T, 2048)
    nk = T // tk

    def body(ht_ref, p_ref, o_ref, acc):
        k = pl.program_id(1)

        @pl.when(k == 0)
        def _():
            acc[...] = jnp.zeros_like(acc)
        acc[...] += _nn(ht_ref[...], p_ref[...])

        @pl.when(k == nk - 1)
        def _():
            o_ref[...] = acc[...].T.astype(BF16)

    return _call(
        body, "in_bwd_w", (wd // tn, nk),
        [pl.BlockSpec((D, tk), lambda j, k: (0, k)), pl.BlockSpec((tk, tn), lambda j, k: (k, j))],
        [pl.BlockSpec((tn, D), lambda j, k: (j, 0))], [SDS((wd, D), BF16)], [pltpu.VMEM((D, tn), F32)],
        (ht, piece), ("parallel", "arbitrary"), ex)


def _band_mask():
    qc = np.arange(TQ)[:, None] // CHUNK
    kc = (np.arange(KW)[None, :] - 2 * TQ) // CHUNK
    band = (kc <= qc) & (kc >= qc - LEFT)
    return np.where(band, 0.0, NEG).astype(np.float32)


def _my_id():
    return 4 * lax.axis_index("x") + 2 * lax.axis_index("y") + lax.axis_index("c")


def _peers():
    x, y, c = lax.axis_index("x"), lax.axis_index("y"), lax.axis_index("c")
    out = []
    for k in range(1, N_DEV):
        fx, fy, fc = (k >> 2) & 1, (k >> 1) & 1, k & 1
        px, py, pc = x ^ fx, y ^ fy, c ^ fc
        out.append(((px, py, pc), 4 * px + 2 * py + pc))
    return out


class _Exchange:
    def __init__(self, arrays, scatter):
        self.arrays = list(arrays)
        self.scatter = list(scatter)
        self.n = n = len(arrays)
        hbm = pl.BlockSpec(memory_space=pltpu.HBM)
        self.in_specs = [hbm] * n
        self.out_specs = [hbm] * n
        self.out_shape = [SDS((N_DEV,) + tuple(a.shape[1:] if s else a.shape), a.dtype)
                          for a, s in zip(arrays, scatter)]
        self.scratch = [pltpu.SemaphoreType.DMA((N_DEV - 1, n)), pltpu.SemaphoreType.DMA((N_DEV - 1, n)),
                        pltpu.SemaphoreType.DMA((n,))]

    def split(self, refs, n_in, n_out):
        n = self.n
        own_in = refs[:n_in]
        ex_in = refs[n_in:n_in + n]
        own_out = refs[n_in + n:n_in + n + n_out]
        ex_out = refs[n_in + n + n_out:n_in + 2 * n + n_out]
        rest = refs[n_in + 2 * n + n_out:]
        return own_in, own_out, rest[:-3], (ex_in, ex_out, rest[-3:])

    def _copy(self, ex, k, p, landing):
        in_refs, out_refs, (send_sems, recv_sems, _) = ex
        pos, pid = _peers()[p]
        return pltpu.make_async_remote_copy(
            src_ref=in_refs[k].at[pid] if self.scatter[k] else in_refs[k],
            dst_ref=out_refs[k].at[pid if landing else _my_id()],
            send_sem=send_sems.at[p, k], recv_sem=recv_sems.at[p, k],
            device_id=pos, device_id_type=pl.DeviceIdType.MESH)

    def _own(self, ex, k):
        in_refs, out_refs, (_, _, local_sems) = ex
        me = _my_id()
        return pltpu.make_async_copy(in_refs[k].at[me] if self.scatter[k] else in_refs[k], out_refs[k].at[me],
                                     local_sems.at[k])

    def start(self, ex):
        for k in range(self.n):
            self._own(ex, k).start()
        for p in range(N_DEV - 1):
            for k in range(self.n):
                self._copy(ex, k, p, False).start()

    def finish(self, ex):
        for p in range(N_DEV - 1):
            for k in range(self.n):
                self._copy(ex, k, p, True).wait_recv()
        for p in range(N_DEV - 1):
            for k in range(self.n):
                self._copy(ex, k, p, False).wait_send()
        for k in range(self.n):
            self._own(ex, k).wait()


def _exchange(arrays, scatter, name):
    ex = _Exchange(arrays, scatter)

    def body(*refs):
        _, _, _, exr = ex.split(refs, 0, 0)
        ex.start(exr)
        ex.finish(exr)

    return pl.pallas_call(body, name=name, in_specs=ex.in_specs, out_specs=ex.out_specs,
                          out_shape=ex.out_shape, scratch_shapes=ex.scratch)(*ex.arrays)


def _adamw_sum(parts0, parts1, w, m, v, name):
    _, R, C = w.shape
    tr = R
    while tr * C > 256 * 1024 and tr % 32 == 0:
        tr //= 2
    c1 = 1.0 / (1.0 - ADAM_B1 ** ADAM_STEP)
    c2 = 1.0 / (1.0 - ADAM_B2 ** ADAM_STEP)

    def body(p0_ref, p1_ref, w_ref, m_ref, v_ref, g_ref, d_ref, mo_ref, vo_ref):
        def update(p_ref):
            g = p_ref[0].astype(F32)
            for s in range(1, N_DEV):
                g = g + p_ref[s].astype(F32)
            mn = ADAM_B1 * m_ref[...] + (1.0 - ADAM_B1) * g
            vn = ADAM_B2 * v_ref[...] + (1.0 - ADAM_B2) * (g * g)
            g_ref[...] = g
            mo_ref[...] = mn
            vo_ref[...] = vn
            d_ref[...] = -ADAM_LR * ((mn * c1) / (jnp.sqrt(vn * c2) + ADAM_EPS) + ADAM_WD * w_ref[...])

        @pl.when(pl.program_id(0) == 0)
        def _():
            update(p0_ref)

        @pl.when(pl.program_id(0) == 1)
        def _():
            update(p1_ref)

    blk = pl.BlockSpec((None, tr, C), lambda l, i: (l, i, 0))
    return pl.pallas_call(
        body, name=name, grid=(2, R // tr),
        in_specs=[pl.BlockSpec((N_DEV, tr, C), lambda l, i: (0, i * (1 - l), 0)),
                  pl.BlockSpec((N_DEV, tr, C), lambda l, i: (0, i * l, 0)), blk, blk, blk],
        out_specs=[blk, blk, blk, blk],
        out_shape=[SDS((2, R, C), F32)] * 4,
        compiler_params=_cparams(("arbitrary", "arbitrary")),
    )(parts0, parts1, w, m, v)


def _layer_fwd(x, P, skew, band, S, rest, ex):
    z, ht, *got0 = _in_proj(x, P["pre_g"], P["w_in_t"], ex[0])
    P = {**P, **rest(got0)}
    ac, ap, u1, *got1 = _mix_fwd(z, P["conv_dw"], P["conv_dw_b"], P["conv_ln_g"], P["conv_ln_b"],
                                 P["pool_w"], P["pool_b"], P["pool_scale"], S, ex[1])
    o, *got2 = _attn_fwd(z, skew, band, S, ex[2])
    out, *got3 = _out_fwd(x, ac, o, ap, z, P["w_conv_out"], P["w_attn_out"], P["w_pool_out"], P["w_out"],
                          P["post_g"], ex[3])
    return out, (x, z, ht, ac, o, ap, u1), P, (got0, got1, got2, got3)


def _layer_bwd(dout, saved, P, skew, band, S, ex, late_ex):
    x, z, ht, ac, o, ap, u1 = saved
    (dac, dao, dag, dap, dgm, dwco, dwao, dwpo, dwout, dpostg, *got0) = _out_bwd(
        dout, ac, o, ap, z, P["w_conv_out"], P["w_attn_out"], P["w_pool_out"], P["w_out"], P["post_g"], ex[0])
    dq, dk, dv, dskew, *got1 = _attn_bwd(z, dao, skew, band, S, ex[1])
    (dzc, dzp, ddw, ddwb, dlng, dlnb, dpw, dpb, dps) = _mix_bwd(
        z, u1, dac, dap, P["conv_dw"], P["conv_dw_b"], P["conv_ln_g"], P["conv_ln_b"],
        P["pool_w"], P["pool_b"], P["pool_scale"], S)
    grads = dict(post_norm_g=dpostg, conv_dw=ddw, conv_dw_b=ddwb, conv_ln_g=dlng, conv_ln_b=dlnb,
                 w_conv_out=dwco, dskew=dskew, w_attn_out=dwao, pool_w=dpw, pool_b=dpb, pool_scale=dps,
                 w_pool_out=dwpo, w_out=dwout)
    pieces = [dzc, dq, dk, dv, dag, dzp, dgm]
    ex_rest, ex_win = late_ex(grads)
    dwin_t = [_in_bwd_w(ht, p)[0] for p in pieces[:-1]]
    last, *got2 = _in_bwd_w(ht, pieces[-1], ex_rest)
    grads["w_in_t"] = jnp.concatenate(dwin_t + [last], axis=0)
    dx, dpreg, *got3 = _in_bwd_x(pieces, P["w_in_t"], x, P["pre_g"], dout, ex_win(grads))
    grads["pre_norm_g"] = dpreg
    return dx, grads, (got0, got1, got2, got3)


WEIGHT_NAMES = ("pre_norm_g", "post_norm_g", "w_in", "conv_dw", "conv_dw_b", "conv_ln_g", "conv_ln_b",
                "w_conv_out", "rel_bias", "w_attn_out", "pool_w", "pool_b", "pool_scale", "w_pool_out", "w_out")
SHARDED = ("w_in", "w_conv_out", "w_attn_out", "w_pool_out", "w_out", "conv_dw")
REST = tuple(n for n in WEIGHT_NAMES if n not in ("w_in", "pre_norm_g"))


def _cols_from_slabs(g):
    return g.transpose(1, 0, 2).reshape(g.shape[1], N_DEV * g.shape[2])


def _slabs_from_cols(full):
    r, wd = full.shape
    return full.reshape(r, N_DEV, wd // N_DEV).transpose(1, 0, 2)


def _rest_shards(weights, l):
    return [weights["w_conv_out"][l].astype(BF16), weights["w_attn_out"][l].astype(BF16),
            weights["w_pool_out"][l].astype(BF16), weights["w_out"][l].astype(BF16), weights["conv_dw"][l]]


def _rest_weights(got):
    wco, wao, wpo, wout, cdw = got
    return dict(w_conv_out=_cols_from_slabs(wco), w_attn_out=_cols_from_slabs(wao),
                w_pool_out=_cols_from_slabs(wpo), w_out=wout.reshape(D, D), conv_dw=_cols_from_slabs(cdw))


def _grad_arrays(g, names):
    make = {"w_in": lambda: g["w_in_t"].reshape(N_DEV, NCOL // N_DEV, D),
            "w_conv_out": lambda: _slabs_from_cols(g["w_conv_out"].astype(BF16)),
            "w_attn_out": lambda: _slabs_from_cols(g["w_attn_out"].astype(BF16)),
            "w_pool_out": lambda: _slabs_from_cols(g["w_pool_out"].astype(BF16)),
            "w_out": lambda: g["w_out"].astype(BF16).reshape(N_DEV, D // N_DEV, D),
            "conv_dw": lambda: _slabs_from_cols(g["conv_dw"].astype(BF16)),
            "rel_bias": lambda: jnp.dot(g["dskew"], jnp.asarray(_skew_select().T), precision=lax.Precision.HIGHEST),
            "pool_w": lambda: g["pool_w"].reshape(4 * 128, 128),
            "pool_b": lambda: g["pool_b"].reshape(4, 128)}
    return [make[n]() if n in make else g[n] for n in names]


def _grad_exchange(g, names):
    return _Exchange(_grad_arrays(g, names), [n in SHARDED for n in names])


def kernel(x, pre_norm_g, post_norm_g, w_in, conv_dw, conv_dw_b, conv_ln_g, conv_ln_b, w_conv_out, rel_bias, w_attn_out, pool_w, pool_b, pool_scale, w_pool_out, w_out, loss_target, m_pre_norm_g, m_post_norm_g, m_w_in, m_conv_dw, m_conv_dw_b, m_conv_ln_g, m_conv_ln_b, m_w_conv_out, m_rel_bias, m_w_attn_out, m_pool_w, m_pool_b, m_pool_scale, m_w_pool_out, m_w_out, v_pre_norm_g, v_post_norm_g, v_w_in, v_conv_dw, v_conv_dw_b, v_conv_ln_g, v_conv_ln_b, v_w_conv_out, v_rel_bias, v_w_attn_out, v_pool_w, v_pool_b, v_pool_scale, v_w_pool_out, v_w_out):
    given = dict(locals())
    weights = {n: given[n] for n in WEIGHT_NAMES}
    nb, S, _ = x.shape
    T = nb * S
    L = pre_norm_g.shape[0]
    assert L == 2
    x2 = x.reshape(T, D)
    tgt2 = loss_target.reshape(T, D)
    band = jnp.asarray(_band_mask())
    skews = [_skew_table(rel_bias[l]) for l in range(L)]

    def local_params(l):
        return dict(pre_g=pre_norm_g[l:l + 1], post_g=post_norm_g[l:l + 1], conv_dw_b=conv_dw_b[l:l + 1],
                    conv_ln_g=conv_ln_g[l:l + 1], conv_ln_b=conv_ln_b[l:l + 1], pool_w=pool_w[l].astype(BF16),
                    pool_b=pool_b[l].reshape(1, BW), pool_scale=pool_scale[l:l + 1])

    win0 = w_in[0].T.astype(BF16)
    win1 = w_in[1].T.astype(BF16)
    half = win1.shape[0] // 2
    (w_in_t0,) = _exchange([win0], [False], "gather_w_in_0")
    gather = lambda arrays: _Exchange(arrays, [False] * len(arrays))
    h, saved0, P0, (got_rest0, got_a, got_b, got_rest1) = _layer_fwd(
        x2, {**local_params(0), "w_in_t": w_in_t0.reshape(NCOL, D)}, skews[0], band, S, _rest_weights,
        (gather(_rest_shards(weights, 0)), gather([win1[:half]]), gather([win1[half:]]),
         gather(_rest_shards(weights, 1))))
    w_in_t1 = jnp.concatenate([got_a[0], got_b[0]], axis=1).reshape(NCOL, D)
    h, saved1, P1, _ = _layer_fwd(h, {**local_params(1), "w_in_t": w_in_t1}, skews[1], band, S,
                                  lambda _: _rest_weights(got_rest1), (None,) * 4)
    lsum, dout = _loss_grad(h, tgt2)
    loss = lax.psum(lsum[0, 0], MESH_AXES) * (0.5 / D)

    no_ex = lambda grads: (None, lambda g: None)
    dout, g1, _ = _layer_bwd(dout, saved1, P1, skews[1], band, S, (None, None), no_ex)
    late0 = lambda grads: (_grad_exchange(grads, REST), lambda g: _grad_exchange(g, ("w_in",)))
    dout, g0, (got_win1, got_rest1g, got_rest0g, got_win0) = _layer_bwd(
        dout, saved0, P0, skews[0], band, S,
        (_grad_exchange(g1, ("w_in",)), _grad_exchange(g1, REST + ("pre_norm_g",))), late0)
    (got_pre0,) = _exchange([g0["pre_norm_g"]], [False], "gather_grad_pre_norm_g_0")
    parts = [{"w_in": got_win0[0], "pre_norm_g": got_pre0, **dict(zip(REST, got_rest0g))},
             {"w_in": got_win1[0], **dict(zip(REST + ("pre_norm_g",), got_rest1g))}]
    grad_x = dout.reshape(x.shape)

    outs = {}
    for n in WEIGHT_NAMES:
        view = (lambda a: a.transpose(0, 2, 1)) if n == "w_in" else (lambda a: a)
        w = view(weights[n])
        shape3 = (L,) + parts[0][n].shape[1:]
        res = _adamw_sum(parts[0][n], parts[1][n], w.reshape(shape3), view(given["m_" + n]).reshape(shape3),
                         view(given["v_" + n]).reshape(shape3), "adamw_" + n)
        outs[n] = [view(a.reshape(w.shape)) for a in res]
    return (loss, grad_x, *[outs[n][0] for n in WEIGHT_NAMES], *[outs[n][1] for n in WEIGHT_NAMES],
            *[outs[n][2] for n in WEIGHT_NAMES], *[outs[n][3] for n in WEIGHT_NAMES])
```

```python
import functools

import numpy as np
import jax
import jax.numpy as jnp
from jax import lax
from jax.experimental import pallas as pl
from jax.experimental.pallas import tpu as pltpu

F32 = jnp.float32
BF16 = jnp.bfloat16
SDS = jax.ShapeDtypeStruct

D = 1024
BW = 512
NCOL = 7680
EPS = 1e-6
NEG = -1e30
HEADS = 8
HD = 64
CHUNK = 64
LEFT = 8
MAX_REL = 256
TQ = 256
KW = 768
CONV_K = 31
WINDOWS = (2, 4, 8, 16)
HALO = 32
RC = 32
N_DEV = 8
MESH_AXES = ("x", "y", "c")

ADAM_LR = 0.001
ADAM_B1 = 0.9
ADAM_B2 = 0.999
ADAM_EPS = 1e-08
ADAM_WD = 0.01
ADAM_STEP = 10

VMEM_LIMIT = 56 * 1024 * 1024

CB_CA, CB_CB, CB_CG, CB_Q, CB_K, CB_V, CB_AG, CB_PI, CB_PG = range(9)
DZ_PIECES = (("conv", 1536), ("q", 512), ("k", 512), ("v", 512), ("ag", 512), ("pool", 1024), ("gm", 3072))


def _cparams(sem):
    return pltpu.CompilerParams(dimension_semantics=sem, vmem_limit_bytes=VMEM_LIMIT)


def _sig(x):
    return 1.0 / (1.0 + jnp.exp(-x))


def _dsilu(x, s):
    return s * (1.0 + x * (1.0 - s))


def _nt(a, b):
    return lax.dot_general(a, b, (((1,), (1,)), ((), ())), preferred_element_type=F32)


def _tn(a, b):
    return lax.dot_general(a, b, (((0,), (0,)), ((), ())), preferred_element_type=F32)


def _nn(a, b):
    return jnp.dot(a, b, preferred_element_type=F32)


def _rows8(x):
    return x[0:8] + x[8:16] + x[16:24] + x[24:32]


def _call(body, name, grid, in_specs, out_specs, out_shape, scratch, args, sem, ex=None, aliases=None):
    aliases = aliases or {}
    if ex is None:
        return pl.pallas_call(body, name=name, grid=grid, in_specs=in_specs, out_specs=out_specs,
                              out_shape=out_shape, scratch_shapes=scratch, input_output_aliases=aliases,
                              compiler_params=_cparams(sem))(*args)
    n_in, n_out = len(in_specs), len(out_specs)
    steps = int(np.prod(grid))

    def carrier(*refs):
        own_in, own_out, own_scr, exr = ex.split(refs, n_in, n_out)
        step = pl.program_id(0)
        for axis in range(1, len(grid)):
            step = step * grid[axis] + pl.program_id(axis)

        @pl.when(step == 0)
        def _():
            ex.start(exr)

        body(*own_in, *own_out, *own_scr)

        @pl.when(step == steps - 1)
        def _():
            ex.finish(exr)

    return pl.pallas_call(
        carrier, name=name + "_carrier", grid=grid, in_specs=in_specs + ex.in_specs,
        out_specs=out_specs + ex.out_specs, out_shape=out_shape + ex.out_shape,
        scratch_shapes=scratch + ex.scratch, input_output_aliases=aliases,
        compiler_params=_cparams(("arbitrary",) * len(grid)),
    )(*args, *ex.arrays)


def _in_proj(x, g, wt, ex=None):
    T = x.shape[0]
    tm = min(T, 1024)
    tn = 1536

    def body(x_ref, g_ref, w_ref, z_ref, ht_ref, h_scr):
        @pl.when(pl.program_id(1) == 0)
        def _():
            xv = x_ref[...]
            r = lax.rsqrt(jnp.mean(xv * xv, axis=-1, keepdims=True) + EPS)
            h = xv * r * g_ref[...]
            h_scr[...] = h.astype(BF16)
            ht_ref[...] = h.T.astype(BF16)
        z_ref[...] = _nt(h_scr[...], w_ref[...]).astype(BF16)

    return _call(
        body, "in_proj", (T // tm, NCOL // tn),
        [pl.BlockSpec((tm, D), lambda i, j: (i, 0)), pl.BlockSpec((1, D), lambda i, j: (0, 0)),
         pl.BlockSpec((tn, D), lambda i, j: (j, 0))],
        [pl.BlockSpec((tm, tn), lambda i, j: (i, j)), pl.BlockSpec((D, tm), lambda i, j: (0, i))],
        [SDS((T, NCOL), BF16), SDS((D, T), BF16)], [pltpu.VMEM((tm, D), BF16)],
        (x, g, wt), ("parallel", "arbitrary"), ex)


def _stencil(load, w_ref, lo, hi, tap_of):
    out = None
    for r in range(8):
        n = RC if r == 0 else RC + 8
        v = None
        for q in range((hi - r) // 8 + 1):
            o = 8 * q + r
            if o < lo:
                continue
            j = tap_of(o)
            term = w_ref[j:j + 1, :] * load(q, n)
            v = term if v is None else v + term
        if v is None:
            continue
        if r:
            v = pltpu.roll(v, n - r, axis=0)[0:RC]
        out = v if out is None else out + v
    return out


def _layer_norm_fwd(u1):
    mu = jnp.mean(u1, axis=-1, keepdims=True)
    xc = u1 - mu
    rstd = lax.rsqrt(jnp.mean(xc * xc, axis=-1, keepdims=True) + EPS)
    return xc * rstd, rstd


def _window_sums(x, w, back):
    n = x.shape[0]
    s = x
    k = 1
    while k < w:
        s = s + pltpu.roll(s, k if back else n - k, axis=0)
        k *= 2
    return s


def _pool_chunk(pwin, t_first):
    t = t_first + lax.broadcasted_iota(jnp.int32, (RC, 128), 0)
    outs = []
    for g, w in enumerate(WINDOWS):
        x = pwin[:, g * 128:(g + 1) * 128]
        s = _window_sums(x, w, True)
        cnt = jnp.minimum(t + 1, w).astype(F32)
        outs.append(s[HALO:HALO + RC] / cnt - x[HALO:HALO + RC])
    return outs


def _mix_fwd(z, dw, dwb, lng, lnb, pw, pb, ps, S, ex=None):
    T = z.shape[0]
    tm = 512
    ts = S // tm
    nh = tm // HALO

    def body(ca_ref, cb_ref, cg_ref, pi_ref, pg_ref, cah_ref, cbh_ref, pih_ref,
             dw_ref, dwb_ref, lng_ref, lnb_ref, pw_ref, pb_ref, ps_ref,
             ac_ref, ap_ref, u1_ref, ubuf, pbuf, pooled):
        i = pl.program_id(0)
        keep = jnp.where((i % ts) == 0, 0.0, 1.0)
        ubuf[0:HALO, :] = cah_ref[...].astype(F32) * _sig(cbh_ref[...].astype(F32)) * keep
        ubuf[HALO:HALO + tm, :] = ca_ref[...].astype(F32) * _sig(cb_ref[...].astype(F32))
        pbuf[0:HALO, :] = pih_ref[...].astype(F32) * keep
        pbuf[HALO:HALO + tm, :] = pi_ref[...].astype(F32)
        t0 = (i % ts) * tm

        def chunk(c, carry):
            base = pl.multiple_of(c * RC, RC)
            load = lambda q, n: ubuf[pl.ds(base + 8 * q, n), :]
            u1 = _stencil(load, dw_ref, 2, CONV_K + 1, lambda o: o - 2) + dwb_ref[...]
            u1_ref[pl.ds(base, RC), :] = u1
            n, _ = _layer_norm_fwd(u1)
            u2 = n * lng_ref[...] + lnb_ref[...]
            u3 = u2 * _sig(u2)
            cg = cg_ref[pl.ds(base, RC), :].astype(F32)
            ac_ref[pl.ds(base, RC), :] = (u3 * cg * _sig(cg)).astype(BF16)
            pwin = pbuf[pl.ds(base, RC + HALO), :]
            outs = _pool_chunk(pwin, t0 + base)
            for g in range(4):
                pooled[pl.ds(base, RC), g * 128:(g + 1) * 128] = outs[g].astype(BF16)
            return carry

        lax.fori_loop(0, tm // RC, chunk, 0)
        pg = pg_ref[...].astype(F32)
        spg = pg * _sig(pg)
        for g in range(4):
            sl = slice(g * 128, (g + 1) * 128)
            mixed = (_nn(pooled[:, sl], pw_ref[g]) + pb_ref[:, sl]) * ps_ref[:, sl]
            ap_ref[:, sl] = (mixed * spg[:, sl]).astype(BF16)

    def zmain(cb):
        return pl.BlockSpec((tm, BW), lambda i: (i, cb))

    def zprev(cb):
        return pl.BlockSpec((HALO, BW), lambda i: (jnp.maximum(i * nh - 1, 0), cb))

    full = lambda shape: pl.BlockSpec(shape, lambda i: (0,) * len(shape))
    row = pl.BlockSpec((tm, BW), lambda i: (i, 0))
    return _call(
        body, "mix_fwd", (T // tm,),
        [zmain(CB_CA), zmain(CB_CB), zmain(CB_CG), zmain(CB_PI), zmain(CB_PG),
         zprev(CB_CA), zprev(CB_CB), zprev(CB_PI),
         full((CONV_K, BW)), full((1, BW)), full((1, BW)), full((1, BW)),
         full((4, 128, 128)), full((1, BW)), full((1, BW))],
        [row, row, row], [SDS((T, BW), BF16), SDS((T, BW), BF16), SDS((T, BW), F32)],
        [pltpu.VMEM((HALO + tm, BW), F32), pltpu.VMEM((HALO + tm, BW), F32), pltpu.VMEM((tm, BW), BF16)],
        (z, z, z, z, z, z, z, z, dw, dwb, lng, lnb, pw, pb, ps), ("parallel",), ex)


def _attn_specs(nq):
    def kv(cb, off):
        return pl.BlockSpec((TQ, BW), lambda i: (i - jnp.minimum(off, i % nq), cb))
    return [pl.BlockSpec((TQ, BW), lambda i: (i, CB_Q)),
            kv(CB_K, 2), kv(CB_K, 1), kv(CB_K, 0), kv(CB_V, 2), kv(CB_V, 1), kv(CB_V, 0)]


def _softmax_rows(s):
    m = jnp.max(s, axis=-1, keepdims=True)
    e = jnp.exp(s - m)
    return e / jnp.sum(e, axis=-1, keepdims=True)


NSKEW = 1024


def _skew_table(table):
    return jnp.dot(table, jnp.asarray(_skew_select()), precision=lax.Precision.HIGHEST)


def _skew_select():
    d = np.arange(TQ + KW - 1)
    idx = np.clip(3 * TQ - 1 - d, -MAX_REL, MAX_REL) + MAX_REL
    sel = np.zeros((2 * MAX_REL + 1, NSKEW), np.float32)
    sel[idx, d] = 1.0
    return sel


def _bias_from_skew(f_ref, band_ref, bias_scr):
    for h in range(HEADS):
        rows = jnp.broadcast_to(f_ref[h:h + 1, :], (TQ, NSKEW))
        rows = pltpu.roll(rows, NSKEW - (TQ - 1), axis=1, stride=1, stride_axis=0)
        bias_scr[h] = rows[:, 0:KW] + band_ref[...]


def _skew_from_bias(db):
    i = lax.broadcasted_iota(jnp.int32, (TQ, TQ), 0)
    j = lax.broadcasted_iota(jnp.int32, (TQ, TQ), 1)
    flip = jnp.where(i + j == TQ - 1, 1.0, 0.0).astype(BF16)
    hi = db.astype(BF16)
    lo = (db - hi.astype(F32)).astype(BF16)
    rev = _nn(flip, hi) + _nn(flip, lo)
    rev = jnp.concatenate([rev, jnp.zeros((TQ, NSKEW - KW), F32)], axis=1)
    return jnp.sum(pltpu.roll(rev, 0, axis=1, stride=1, stride_axis=0), axis=0, keepdims=True)


def _attn_fwd(z, f, band, S, ex=None):
    T = z.shape[0]
    nq = S // TQ

    def body(q_ref, k2_ref, k1_ref, k0_ref, v2_ref, v1_ref, v0_ref, f_ref, band_ref, o_ref, kbuf, vbuf, b_scr):
        @pl.when(pl.program_id(0) == 0)
        def _():
            _bias_from_skew(f_ref, band_ref, b_scr)

        qb = pl.program_id(0) % nq
        kbuf[0:TQ, :] = k2_ref[...]
        kbuf[TQ:2 * TQ, :] = k1_ref[...]
        kbuf[2 * TQ:KW, :] = k0_ref[...]
        vbuf[0:TQ, :] = v2_ref[...]
        vbuf[TQ:2 * TQ, :] = v1_ref[...]
        vbuf[2 * TQ:KW, :] = v0_ref[...]
        lane = lax.broadcasted_iota(jnp.int32, (1, 128), 1)

        def attend(lo):
            def scores(h):
                sl = slice((h // 2) * 128, (h // 2 + 1) * 128)
                qp = q_ref[:, sl] * 0.125
                qm = jnp.where((lane < HD) if h % 2 == 0 else (lane >= HD), qp, jnp.zeros_like(qp))
                return _nt(qm, kbuf[lo:KW, sl]) + b_scr[h, :, lo:KW]

            s = scores(0)
            acc = None
            for h in range(HEADS):
                s_next = scores(h + 1) if h + 1 < HEADS else None
                sl = slice((h // 2) * 128, (h // 2 + 1) * 128)
                e = jnp.exp(s - jnp.max(s, axis=-1, keepdims=True))
                vp = vbuf[lo:KW, sl]
                vm = jnp.where((lane < HD) if h % 2 == 0 else (lane >= HD), vp, jnp.zeros_like(vp))
                o = _nn(e.astype(BF16), vm) * (1.0 / jnp.sum(e, axis=-1, keepdims=True))
                acc = o if h % 2 == 0 else acc + o
                if h % 2 == 1:
                    o_ref[:, sl] = acc.astype(BF16)
                s = s_next

        for nblk in (1, 2, 3):
            pl.when(jnp.minimum(qb, 2) == nblk - 1)(functools.partial(attend, (3 - nblk) * TQ))

    full = lambda shape: pl.BlockSpec(shape, lambda i: (0,) * len(shape))
    return _call(
        body, "attn_fwd", (T // TQ,),
        _attn_specs(nq) + [full((HEADS, NSKEW)), full((TQ, KW))],
        [pl.BlockSpec((TQ, BW), lambda i: (i, 0))], [SDS((T, BW), BF16)],
        [pltpu.VMEM((KW, BW), BF16), pltpu.VMEM((KW, BW), BF16), pltpu.VMEM((HEADS, TQ, KW), F32)],
        (z, z, z, z, z, z, z, f, band), ("arbitrary",), ex)


def _gates(gl_ref, gh_ref):
    gl = _sig(gl_ref[...].astype(F32))
    gh = _sig(gh_ref[...].astype(F32))
    return (gl[:, 0:D], jnp.concatenate([gl[:, D:1536], gh[:, 0:512]], axis=1), gh[:, 512:1536])


def _out_specs_in(tm):
    row = lambda w: pl.BlockSpec((tm, w), lambda i: (i, 0))
    full = lambda shape: pl.BlockSpec(shape, lambda i: (0,) * len(shape))
    return [row(BW), row(BW), row(BW),
            pl.BlockSpec((tm, BW), lambda i: (i, CB_AG)),
            pl.BlockSpec((tm, 1536), lambda i: (i, 3)),
            pl.BlockSpec((tm, 1536), lambda i: (i, 4)),
            full((BW, D)), full((BW, D)), full((BW, D)), full((D, D)), full((1, D))]


def _out_fwd(x, ac, o, ap, z, wco, wao, wpo, wout, postg, ex=None, tgt=None):
    T = x.shape[0]
    tm = 512
    last = tgt is not None

    def body(ac_ref, o_ref, ap_ref, ag_ref, gl_ref, gh_ref, wco_ref, wao_ref, wpo_ref, wout_ref, pg_ref,
             x_ref, *rest):
        ag = ag_ref[...].astype(F32)
        aat = (o_ref[...].astype(F32) * ag * _sig(ag)).astype(BF16)
        g0, g1, g2 = _gates(gl_ref, gh_ref)
        merged = g0 * _nn(ac_ref[...], wco_ref[...])
        merged = merged + g1 * _nn(aat, wao_ref[...])
        merged = merged + g2 * _nn(ap_ref[...], wpo_ref[...])
        y = _nn(merged.astype(BF16), wout_ref[...])
        ry = lax.rsqrt(jnp.mean(y * y, axis=-1, keepdims=True) + EPS)
        out = x_ref[...] + y * ry * pg_ref[...]
        if not last:
            rest[0][...] = out
            return
        t_ref, d_ref, l_ref = rest

        @pl.when(pl.program_id(0) == 0)
        def _():
            l_ref[...] = jnp.zeros_like(l_ref)
        d = out - t_ref[...]
        d_ref[...] = d * (1.0 / D)
        l_ref[...] += jnp.sum(jnp.sum(d * d, axis=0, keepdims=True), axis=1, keepdims=True)

    row = pl.BlockSpec((tm, D), lambda i: (i, 0))
    if not last:
        return _call(body, "out_fwd", (T // tm,), _out_specs_in(tm) + [row], [row], [SDS((T, D), F32)], [],
                     (ac, o, ap, z, z, z, wco, wao, wpo, wout, postg, x), ("parallel",), ex)
    return _call(body, "out_fwd_loss", (T // tm,), _out_specs_in(tm) + [row, row],
                 [row, pl.BlockSpec((1, 128), lambda i: (0, 0))], [SDS((T, D), F32), SDS((1, 128), F32)], [],
                 (ac, o, ap, z, z, z, wco, wao, wpo, wout, postg, x, tgt), ("arbitrary",), ex)


def _out_bwd(dout, ac, o, ap, z, wco, wao, wpo, wout, postg, ex=None):
    T = dout.shape[0]
    tm = 256

    def body(ac_ref, o_ref, ap_ref, ag_ref, gl_ref, gh_ref, wco_ref, wao_ref, wpo_ref, wout_ref, pg_ref, do_ref,
             dac_ref, dao_ref, dag_ref, dap_ref, dgm_ref, dwco_ref, dwao_ref, dwpo_ref, dwout_ref, dpg_ref):
        @pl.when(pl.program_id(0) == 0)
        def _():
            for r in (dwco_ref, dwao_ref, dwpo_ref, dwout_ref, dpg_ref):
                r[...] = jnp.zeros_like(r)

        ag = ag_ref[...].astype(F32)
        sag = _sig(ag)
        ov = o_ref[...].astype(F32)
        acts = (ac_ref[...], (ov * ag * sag).astype(BF16), ap_ref[...])
        ws = (wco_ref, wao_ref, wpo_ref)
        gates = _gates(gl_ref, gh_ref)
        ys = [_nn(acts[b], ws[b][...]) for b in range(3)]
        merged = (gates[0] * ys[0] + gates[1] * ys[1] + gates[2] * ys[2]).astype(BF16)
        y = _nn(merged, wout_ref[...])
        ry = lax.rsqrt(jnp.mean(y * y, axis=-1, keepdims=True) + EPS)
        yn = y * ry
        dout_v = do_ref[...]
        dpg_ref[...] += jnp.sum(dout_v * yn, axis=0, keepdims=True)
        dyn = dout_v * pg_ref[...]
        dy = (ry * (dyn - yn * jnp.mean(dyn * yn, axis=-1, keepdims=True))).astype(BF16)
        dmerged = _nt(dy, wout_ref[...])
        dwout_ref[...] += _tn(merged, dy)
        dws = (dwco_ref, dwao_ref, dwpo_ref)
        das = []
        for b in range(3):
            gb = gates[b]
            dgm_ref[:, b * D:(b + 1) * D] = (dmerged * ys[b] * gb * (1.0 - gb)).astype(BF16)
            dyb = (dmerged * gb).astype(BF16)
            dws[b][...] += _tn(acts[b], dyb)
            das.append(_nt(dyb, ws[b][...]))
        dac_ref[...] = das[0].astype(BF16)
        dap_ref[...] = das[2].astype(BF16)
        dao_ref[...] = (das[1] * ag * sag).astype(BF16)
        dag_ref[...] = (das[1] * ov * _dsilu(ag, sag)).astype(BF16)

    row = lambda w: pl.BlockSpec((tm, w), lambda i: (i, 0))
    full = lambda shape: pl.BlockSpec(shape, lambda i: (0,) * len(shape))
    return _call(
        body, "out_bwd", (T // tm,), _out_specs_in(tm) + [row(D)],
        [row(BW), row(BW), row(BW), row(BW), row(3 * D),
         full((BW, D)), full((BW, D)), full((BW, D)), full((D, D)), full((1, D))],
        [SDS((T, BW), BF16)] * 4 + [SDS((T, 3 * D), BF16)]
        + [SDS((BW, D), F32)] * 3 + [SDS((D, D), F32), SDS((1, D), F32)], [],
        (ac, o, ap, z, z, z, wco, wao, wpo, wout, postg, dout), ("arbitrary",), ex)


def _attn_bwd(z, dao, f, band, S, ex=None):
    T = z.shape[0]
    nq = S // TQ
    nsteps = T // TQ

    def body(q_ref, k2_ref, k1_ref, k0_ref, v2_ref, v1_ref, v0_ref, do_ref, f_ref, band_ref,
             dq_ref, dk_ref, dv_ref, df_ref, kbuf, vbuf, dkacc, dvacc, b_scr, db_scr):
        i = pl.program_id(0)
        qb = i % nq

        @pl.when(i == 0)
        def _():
            _bias_from_skew(f_ref, band_ref, b_scr)
            db_scr[...] = jnp.zeros_like(db_scr)

        @pl.when(qb == 0)
        def _():
            dkacc[...] = jnp.zeros_like(dkacc)
            dvacc[...] = jnp.zeros_like(dvacc)

        kbuf[0:TQ, :] = k2_ref[...]
        kbuf[TQ:2 * TQ, :] = k1_ref[...]
        kbuf[2 * TQ:KW, :] = k0_ref[...]
        vbuf[0:TQ, :] = v2_ref[...]
        vbuf[TQ:2 * TQ, :] = v1_ref[...]
        vbuf[2 * TQ:KW, :] = v0_ref[...]
        lane = lax.broadcasted_iota(jnp.int32, (1, 128), 1)
        row0 = pl.multiple_of(qb * TQ, TQ)

        def attend(lo):
            def first_matmuls(h):
                sl = slice((h // 2) * 128, (h // 2 + 1) * 128)
                msk = (lane < HD) if h % 2 == 0 else (lane >= HD)
                qp = q_ref[:, sl] * 0.125
                dop = do_ref[:, sl]
                qm = jnp.where(msk, qp, jnp.zeros_like(qp))
                dom = jnp.where(msk, dop, jnp.zeros_like(dop))
                s = _nt(qm, kbuf[lo:KW, sl]) + b_scr[h, :, lo:KW]
                return s, _nt(dom, vbuf[lo:KW, sl]), qm, dom

            cur = first_matmuls(0)
            dq_acc = dk_acc = dv_acc = None
            for h in range(HEADS):
                nxt = first_matmuls(h + 1) if h + 1 < HEADS else None
                s, dp, qm, dom = cur
                sl = slice((h // 2) * 128, (h // 2 + 1) * 128)
                e = jnp.exp(s - jnp.max(s, axis=-1, keepdims=True))
                p = e * (1.0 / jnp.sum(e, axis=-1, keepdims=True))
                ds = p * (dp - jnp.sum(p * dp, axis=-1, keepdims=True))
                db_scr[h, :, lo:KW] += ds
                dsb = ds.astype(BF16)
                kp = kbuf[lo:KW, sl]
                km = jnp.where((lane < HD) if h % 2 == 0 else (lane >= HD), kp, jnp.zeros_like(kp))
                dq_h = _nn(dsb, km) * 0.125
                dk_h = _tn(dsb, qm)
                dv_h = _tn(p.astype(BF16), dom)
                if h % 2 == 0:
                    dq_acc, dk_acc, dv_acc = dq_h, dk_h, dv_h
                else:
                    dq_ref[:, sl] = (dq_acc + dq_h).astype(BF16)
                    dkacc[pl.ds(row0 + lo, KW - lo), sl] += dk_acc + dk_h
                    dvacc[pl.ds(row0 + lo, KW - lo), sl] += dv_acc + dv_h
                cur = nxt

        for nblk in (1, 2, 3):
            pl.when(jnp.minimum(qb, 2) == nblk - 1)(functools.partial(attend, (3 - nblk) * TQ))

        @pl.when(qb == nq - 1)
        def _():
            dk_ref[...] = dkacc[2 * TQ:2 * TQ + S, :].astype(BF16)
            dv_ref[...] = dvacc[2 * TQ:2 * TQ + S, :].astype(BF16)

        @pl.when(i == nsteps - 1)
        def _():
            for h in range(HEADS):
                df_ref[h:h + 1, :] = _skew_from_bias(db_scr[h])

    full = lambda shape: pl.BlockSpec(shape, lambda i: (0,) * len(shape))
    return _call(
        body, "attn_bwd", (nsteps,),
        _attn_specs(nq) + [pl.BlockSpec((TQ, BW), lambda i: (i, 0)), full((HEADS, NSKEW)), full((TQ, KW))],
        [pl.BlockSpec((TQ, BW), lambda i: (i, 0)), pl.BlockSpec((S, BW), lambda i: (i // nq, 0)),
         pl.BlockSpec((S, BW), lambda i: (i // nq, 0)), full((HEADS, NSKEW))],
        [SDS((T, BW), BF16)] * 3 + [SDS((HEADS, NSKEW), F32)],
        [pltpu.VMEM((KW, BW), BF16), pltpu.VMEM((KW, BW), BF16),
         pltpu.VMEM((S + 2 * TQ, BW), F32), pltpu.VMEM((S + 2 * TQ, BW), F32),
         pltpu.VMEM((HEADS, TQ, KW), F32), pltpu.VMEM((HEADS, TQ, KW), F32)],
        (z, z, z, z, z, z, z, dao, f, band), ("arbitrary",), ex)


def _mix_bwd(z, u1, dac, dap, dw, dwb, lng, lnb, pw, pb, ps, S):
    T = z.shape[0]
    tm = 512
    ts = S // tm
    nh = tm // HALO
    nsteps = T // tm
    nblk32 = T // HALO

    def body(ca_ref, cb_ref, cg_ref, pi_ref, pg_ref, u1_ref, dac_ref, dap_ref,
             cah_ref, cbh_ref, pih_ref,
             cgn_ref, pgn_ref, u1n_ref, dacn_ref, dapn_ref,
             dw_ref, dwb_ref, lng_ref, lnb_ref, pw_ref, pb_ref, ps_ref,
             dzc_ref, dzp_ref, ddw_ref, ddwb_ref, dlng_ref, dlnb_ref, dpw_ref, dpb_ref, dps_ref,
             ubuf, gbuf, pbuf, qbuf, pooled, dwacc, vacc):
        i = pl.program_id(0)
        keep_prev = jnp.where((i % ts) == 0, 0.0, 1.0)
        keep_next = jnp.where((i % ts) == ts - 1, 0.0, 1.0)
        t0 = (i % ts) * tm

        @pl.when(i == 0)
        def _():
            dwacc[...] = jnp.zeros_like(dwacc)
            vacc[...] = jnp.zeros_like(vacc)
            dpw_ref[...] = jnp.zeros_like(dpw_ref)
            dpb_ref[...] = jnp.zeros_like(dpb_ref)
            dps_ref[...] = jnp.zeros_like(dps_ref)

        ubuf[0:HALO, :] = cah_ref[...].astype(F32) * _sig(cbh_ref[...].astype(F32)) * keep_prev
        ubuf[HALO:HALO + tm, :] = ca_ref[...].astype(F32) * _sig(cb_ref[...].astype(F32))
        pbuf[0:HALO, :] = pih_ref[...].astype(F32) * keep_prev
        pbuf[HALO:HALO + tm, :] = pi_ref[...].astype(F32)

        def norm_back(u1v, cg, dacv):
            n, rstd = _layer_norm_fwd(u1v)
            u2 = n * lng_ref[...] + lnb_ref[...]
            s2 = _sig(u2)
            scg = _sig(cg)
            du2 = dacv * cg * scg * _dsilu(u2, s2)
            dn = du2 * lng_ref[...]
            du1 = rstd * (dn - jnp.mean(dn, axis=-1, keepdims=True)
                          - n * jnp.mean(dn * n, axis=-1, keepdims=True))
            return du1, du2, n, dacv * u2 * s2 * _dsilu(cg, scg)

        def chunk_a(c, carry):
            base = pl.multiple_of(c * RC, RC)
            du1, du2, n, dcg = norm_back(u1_ref[pl.ds(base, RC), :], cg_ref[pl.ds(base, RC), :].astype(F32),
                                         dac_ref[pl.ds(base, RC), :].astype(F32))
            gbuf[pl.ds(base, RC), :] = du1
            dzc_ref[pl.ds(base, RC), 2 * BW:3 * BW] = dcg.astype(BF16)
            vacc[0:8, :] += _rows8(du2 * n)
            vacc[8:16, :] += _rows8(du2)
            vacc[16:24, :] += _rows8(du1)
            padded = jnp.concatenate([du1, jnp.zeros((8, BW), F32)], axis=0)
            for r in range(8):
                nrow = RC if r == 0 else RC + 8
                g = du1 if r == 0 else pltpu.roll(padded, r, axis=0)
                for q in range((CONV_K + 1 - r) // 8 + 1):
                    o = 8 * q + r
                    if o < 2:
                        continue
                    prod = g * ubuf[pl.ds(base + 8 * q, nrow), :]
                    red = prod[0:8]
                    for k in range(1, nrow // 8):
                        red = red + prod[8 * k:8 * k + 8]
                    dwacc[8 * (o - 2):8 * (o - 2) + 8, :] += red
            return carry

        lax.fori_loop(0, tm // RC, chunk_a, 0)
        du1n, _, _, _ = norm_back(u1n_ref[...], cgn_ref[...].astype(F32), dacn_ref[...].astype(F32))
        gbuf[tm:tm + HALO, :] = du1n * keep_next

        def chunk_p(c, carry):
            base = pl.multiple_of(c * RC, RC)
            outs = _pool_chunk(pbuf[pl.ds(base, RC + HALO), :], t0 + base)
            for g in range(4):
                pooled[pl.ds(base, RC), g * 128:(g + 1) * 128] = outs[g].astype(BF16)
            return carry

        lax.fori_loop(0, tm // RC, chunk_p, 0)

        def cnt_of(t_first, rows, w):
            t = t_first + lax.broadcasted_iota(jnp.int32, (rows, 128), 0)
            return jnp.minimum(t + 1, w).astype(F32)

        pg = pg_ref[...].astype(F32)
        spg_s = _sig(pg)
        dapv = dap_ref[...].astype(F32)
        pgn = pgn_ref[...].astype(F32)
        dmixn = dapn_ref[...].astype(F32) * pgn * _sig(pgn) * ps_ref[...] * keep_next
        for g, w in enumerate(WINDOWS):
            sl = slice(g * 128, (g + 1) * 128)
            mixed_u = _nn(pooled[:, sl], pw_ref[g]) + pb_ref[:, sl]
            dap_g = dapv[:, sl]
            pg_g = pg[:, sl]
            s_g = spg_s[:, sl]
            silu_g = pg_g * s_g
            dps_ref[:, sl] += jnp.sum(dap_g * silu_g * mixed_u, axis=0, keepdims=True)
            dzp_ref[:, BW + g * 128:BW + (g + 1) * 128] = (
                dap_g * mixed_u * ps_ref[:, sl] * _dsilu(pg_g, s_g)).astype(BF16)
            dmix = dap_g * silu_g * ps_ref[:, sl]
            dpb_ref[:, sl] += jnp.sum(dmix, axis=0, keepdims=True)
            dmixb = dmix.astype(BF16)
            dpw_ref[g] += _tn(pooled[:, sl], dmixb)
            qbuf[0:tm, sl] = _nt(dmixb, pw_ref[g]) / cnt_of(t0, tm, w)
            qbuf[tm:tm + HALO, sl] = _nt(dmixn[:, sl].astype(BF16), pw_ref[g]) / cnt_of(t0 + tm, HALO, w)

        def chunk_b(c, carry):
            base = pl.multiple_of(c * RC, RC)
            load = lambda q, n: gbuf[pl.ds(base + 8 * q, n), :]
            du0 = _stencil(load, dw_ref, 0, CONV_K - 1, lambda o: CONV_K - 1 - o)
            ca = ca_ref[pl.ds(base, RC), :].astype(F32)
            sb = _sig(cb_ref[pl.ds(base, RC), :].astype(F32))
            dzc_ref[pl.ds(base, RC), 0:BW] = (du0 * sb).astype(BF16)
            dzc_ref[pl.ds(base, RC), BW:2 * BW] = (du0 * ca * sb * (1.0 - sb)).astype(BF16)
            qwin = qbuf[pl.ds(base, RC + HALO), :]
            t = t0 + base + lax.broadcasted_iota(jnp.int32, (RC, 128), 0)
            for g, w in enumerate(WINDOWS):
                x = qwin[:, g * 128:(g + 1) * 128]
                s = _window_sums(x, w, False)
                cnt = jnp.minimum(t + 1, w).astype(F32)
                dzp_ref[pl.ds(base, RC), g * 128:(g + 1) * 128] = (s[0:RC] - cnt * x[0:RC]).astype(BF16)
            return carry

        lax.fori_loop(0, tm // RC, chunk_b, 0)

        @pl.when(i == nsteps - 1)
        def _():
            dlng_ref[...] = jnp.sum(vacc[0:8, :], axis=0, keepdims=True)
            dlnb_ref[...] = jnp.sum(vacc[8:16, :], axis=0, keepdims=True)
            ddwb_ref[...] = jnp.sum(vacc[16:24, :], axis=0, keepdims=True)
            for j in range(CONV_K):
                ddw_ref[j:j + 1, :] = jnp.sum(dwacc[8 * j:8 * j + 8, :], axis=0, keepdims=True)

    def zmain(cb):
        return pl.BlockSpec((tm, BW), lambda i: (i, cb))

    def zprev(cb):
        return pl.BlockSpec((HALO, BW), lambda i: (jnp.maximum(i * nh - 1, 0), cb))

    def znext(cb):
        return pl.BlockSpec((HALO, BW), lambda i: (jnp.minimum((i + 1) * nh, nblk32 - 1), cb))

    row = lambda w: pl.BlockSpec((tm, w), lambda i: (i, 0))
    full = lambda shape: pl.BlockSpec(shape, lambda i: (0,) * len(shape))
    return pl.pallas_call(
        body, name="mix_bwd", grid=(nsteps,),
        in_specs=[zmain(CB_CA), zmain(CB_CB), zmain(CB_CG), zmain(CB_PI), zmain(CB_PG), row(BW), row(BW), row(BW),
                  zprev(CB_CA), zprev(CB_CB), zprev(CB_PI),
                  znext(CB_CG), znext(CB_PG), znext(0), znext(0), znext(0),
                  full((CONV_K, BW)), full((1, BW)), full((1, BW)), full((1, BW)),
                  full((4, 128, 128)), full((1, BW)), full((1, BW))],
        out_specs=[row(3 * BW), row(2 * BW), full((CONV_K, BW)), full((1, BW)), full((1, BW)), full((1, BW)),
                   full((4, 128, 128)), full((1, BW)), full((1, BW))],
        out_shape=[SDS((T, 3 * BW), BF16), SDS((T, 2 * BW), BF16), SDS((CONV_K, BW), F32),
                   SDS((1, BW), F32), SDS((1, BW), F32), SDS((1, BW), F32),
                   SDS((4, 128, 128), F32), SDS((1, BW), F32), SDS((1, BW), F32)],
        scratch_shapes=[pltpu.VMEM((HALO + tm, BW), F32), pltpu.VMEM((tm + HALO, BW), F32),
                        pltpu.VMEM((HALO + tm, BW), F32), pltpu.VMEM((tm + HALO, BW), F32),
                        pltpu.VMEM((tm, BW), BF16), pltpu.VMEM((8 * CONV_K, BW), F32),
                        pltpu.VMEM((24, BW), F32)],
        compiler_params=_cparams(("arbitrary",)),
    )(z, z, z, z, z, u1, dac, dap, z, z, z, z, z, u1, dac, dap, dw, dwb, lng, lnb, pw, pb, ps)


def _in_bwd_x(pieces, wt, x, g, dout, ex=None):
    T = x.shape[0]
    tm = 256
    widths = [p.shape[1] for p in pieces]
    offs = np.cumsum([0] + widths)
    npc = len(pieces)

    def body(*refs):
        p_refs = refs[:npc]
        w_ref, x_ref, g_ref, do_ref, dx_ref, dg_ref = refs[npc:]

        @pl.when(pl.program_id(0) == 0)
        def _():
            dg_ref[...] = jnp.zeros_like(dg_ref)

        dh = None
        for k in range(npc):
            t = _nn(p_refs[k][...], w_ref[int(offs[k]):int(offs[k + 1]), :])
            dh = t if dh is None else dh + t
        xv = x_ref[...]
        r = lax.rsqrt(jnp.mean(xv * xv, axis=-1, keepdims=True) + EPS)
        xn = xv * r
        dg_ref[...] += jnp.sum(dh * xn, axis=0, keepdims=True)
        dxn = dh * g_ref[...]
        dx_ref[...] = do_ref[...] + r * (dxn - xn * jnp.mean(dxn * xn, axis=-1, keepdims=True))

    row = lambda wd: pl.BlockSpec((tm, wd), lambda i: (i, 0))
    return _call(
        body, "in_bwd_x", (T // tm,),
        [row(wd) for wd in widths] + [pl.BlockSpec((NCOL, D), lambda i: (0, 0)),
                                      row(D), pl.BlockSpec((1, D), lambda i: (0, 0)), row(D)],
        [row(D), pl.BlockSpec((1, D), lambda i: (0, 0))], [SDS((T, D), F32), SDS((1, D), F32)], [],
        (*pieces, wt, x, g, dout), ("arbitrary",), ex)


def _in_bwd_w(ht, piece, row0, buf=None, ex=None):
    T = ht.shape[1]
    wd = piece.shape[1]
    tn = 512
    tk = min(T, 2048)
    nk = T // tk
    j0 = row0 // tn

    def body(ht_ref, p_ref, *rest):
        o_ref, acc = rest[-2:]
        k = pl.program_id(1)

        @pl.when(k == 0)
        def _():
            acc[...] = jnp.zeros_like(acc)
        acc[...] += _nn(ht_ref[...], p_ref[...])

        @pl.when(k == nk - 1)
        def _():
            o_ref[...] = acc[...].T.astype(BF16)

    in_specs = [pl.BlockSpec((D, tk), lambda j, k: (0, k)), pl.BlockSpec((tk, tn), lambda j, k: (k, j))]
    args = (ht, piece)
    if buf is not None:
        in_specs.append(pl.BlockSpec(memory_space=pl.ANY))
        args += (buf,)
    return _call(
        body, "in_bwd_w", (wd // tn, nk), in_specs,
        [pl.BlockSpec((tn, D), lambda j, k: (j + j0, 0))], [SDS((NCOL, D), BF16)], [pltpu.VMEM((D, tn), F32)],
        args, ("parallel", "arbitrary"), ex, None if buf is None else {2: 0})


def _band_mask():
    qc = np.arange(TQ)[:, None] // CHUNK
    kc = (np.arange(KW)[None, :] - 2 * TQ) // CHUNK
    band = (kc <= qc) & (kc >= qc - LEFT)
    return np.where(band, 0.0, NEG).astype(np.float32)


def _my_id():
    return 4 * lax.axis_index("x") + 2 * lax.axis_index("y") + lax.axis_index("c")


def _peers():
    x, y, c = lax.axis_index("x"), lax.axis_index("y"), lax.axis_index("c")
    out = []
    for k in range(1, N_DEV):
        fx, fy, fc = (k >> 2) & 1, (k >> 1) & 1, k & 1
        px, py, pc = x ^ fx, y ^ fy, c ^ fc
        out.append(((px, py, pc), 4 * px + 2 * py + pc))
    return out


class _Exchange:
    def __init__(self, arrays, scatter):
        self.arrays = list(arrays)
        self.scatter = list(scatter)
        self.n = n = len(arrays)
        hbm = pl.BlockSpec(memory_space=pltpu.HBM)
        self.in_specs = [hbm] * n
        self.out_specs = [hbm] * n
        self.out_shape = [SDS((N_DEV,) + tuple(a.shape[1:] if s else a.shape), a.dtype)
                          for a, s in zip(arrays, scatter)]
        self.scratch = [pltpu.SemaphoreType.DMA((N_DEV - 1, n)), pltpu.SemaphoreType.DMA((N_DEV - 1, n)),
                        pltpu.SemaphoreType.DMA((n,))]

    def split(self, refs, n_in, n_out):
        n = self.n
        own_in = refs[:n_in]
        ex_in = refs[n_in:n_in + n]
        own_out = refs[n_in + n:n_in + n + n_out]
        ex_out = refs[n_in + n + n_out:n_in + 2 * n + n_out]
        rest = refs[n_in + 2 * n + n_out:]
        return own_in, own_out, rest[:-3], (ex_in, ex_out, rest[-3:])

    def _copy(self, ex, k, p, landing):
        in_refs, out_refs, (send_sems, recv_sems, _) = ex
        pos, pid = _peers()[p]
        return pltpu.make_async_remote_copy(
            src_ref=in_refs[k].at[pid] if self.scatter[k] else in_refs[k],
            dst_ref=out_refs[k].at[pid if landing else _my_id()],
            send_sem=send_sems.at[p, k], recv_sem=recv_sems.at[p, k],
            device_id=pos, device_id_type=pl.DeviceIdType.MESH)

    def _own(self, ex, k):
        in_refs, out_refs, (_, _, local_sems) = ex
        me = _my_id()
        return pltpu.make_async_copy(in_refs[k].at[me] if self.scatter[k] else in_refs[k], out_refs[k].at[me],
                                     local_sems.at[k])

    def start(self, ex):
        for k in range(self.n):
            self._own(ex, k).start()
        for p in range(N_DEV - 1):
            for k in range(self.n):
                self._copy(ex, k, p, False).start()

    def finish(self, ex):
        for p in range(N_DEV - 1):
            for k in range(self.n):
                self._copy(ex, k, p, True).wait_recv()
        for p in range(N_DEV - 1):
            for k in range(self.n):
                self._copy(ex, k, p, False).wait_send()
        for k in range(self.n):
            self._own(ex, k).wait()


def _exchange(arrays, scatter, name):
    ex = _Exchange(arrays, scatter)

    def body(*refs):
        _, _, _, exr = ex.split(refs, 0, 0)
        ex.start(exr)
        ex.finish(exr)

    return pl.pallas_call(body, name=name, in_specs=ex.in_specs, out_specs=ex.out_specs,
                          out_shape=ex.out_shape, scratch_shapes=ex.scratch)(*ex.arrays)


def _gather_two_level(shard, name):
    def body(x_ref, out_ref, send_sems, recv_sems, local_sem):
        x, y, c = lax.axis_index("x"), lax.axis_index("y"), lax.axis_index("c")
        me, sibling = (x, y, c), (x, y, 1 - c)
        chips = [(1 - x, y), (x, 1 - y), (1 - x, 1 - y)]
        slab = lambda px, py, pc: out_ref.at[4 * px + 2 * py + pc]

        def copy(k, block, to, src=None):
            return pltpu.make_async_remote_copy(
                src_ref=slab(*block) if src is None else src, dst_ref=slab(*block),
                send_sem=send_sems.at[k], recv_sem=recv_sems.at[k],
                device_id=to, device_id_type=pl.DeviceIdType.MESH)

        mine = pltpu.make_async_copy(x_ref, slab(*me), local_sem)
        mine.start()
        first = [copy(0, me, sibling, src=x_ref)] + [copy(1 + j, me, (*chip, c), src=x_ref)
                                                     for j, chip in enumerate(chips)]
        for cp in first:
            cp.start()
        passed = [copy(4 + j, (*chip, c), sibling) for j, chip in enumerate(chips)]
        for j, chip in enumerate(chips):
            copy(1 + j, (*chip, c), me).wait_recv()
            passed[j].start()
        copy(0, sibling, me).wait_recv()
        for j, chip in enumerate(chips):
            copy(4 + j, (*chip, 1 - c), me).wait_recv()
        for cp in first + passed:
            cp.wait_send()
        mine.wait()

    hbm = pl.BlockSpec(memory_space=pltpu.HBM)
    return pl.pallas_call(
        body, name=name, in_specs=[hbm], out_specs=hbm,
        out_shape=SDS((N_DEV,) + shard.shape, shard.dtype),
        scratch_shapes=[pltpu.SemaphoreType.DMA((N_DEV - 1,)), pltpu.SemaphoreType.DMA((N_DEV - 1,)),
                        pltpu.SemaphoreType.DMA],
    )(shard)


def _adamw_sum(parts0, parts1, w, m, v, name):
    _, R, C = w.shape
    tr = R
    while tr * C > 256 * 1024 and tr % 32 == 0:
        tr //= 2
    c1 = 1.0 / (1.0 - ADAM_B1 ** ADAM_STEP)
    c2 = 1.0 / (1.0 - ADAM_B2 ** ADAM_STEP)

    def body(p0_ref, p1_ref, w_ref, m_ref, v_ref, g_ref, d_ref, mo_ref, vo_ref):
        def update(p_ref):
            g = p_ref[0].astype(F32)
            for s in range(1, N_DEV):
                g = g + p_ref[s].astype(F32)
            mn = ADAM_B1 * m_ref[...] + (1.0 - ADAM_B1) * g
            vn = ADAM_B2 * v_ref[...] + (1.0 - ADAM_B2) * (g * g)
            g_ref[...] = g
            mo_ref[...] = mn
            vo_ref[...] = vn
            d_ref[...] = -ADAM_LR * ((mn * c1) / (jnp.sqrt(vn * c2) + ADAM_EPS) + ADAM_WD * w_ref[...])

        @pl.when(pl.program_id(0) == 0)
        def _():
            update(p0_ref)

        @pl.when(pl.program_id(0) == 1)
        def _():
            update(p1_ref)

    blk = pl.BlockSpec((None, tr, C), lambda l, i: (l, i, 0))
    return pl.pallas_call(
        body, name=name, grid=(2, R // tr),
        in_specs=[pl.BlockSpec((N_DEV, tr, C), lambda l, i: (0, i * (1 - l), 0)),
                  pl.BlockSpec((N_DEV, tr, C), lambda l, i: (0, i * l, 0)), blk, blk, blk],
        out_specs=[blk, blk, blk, blk],
        out_shape=[SDS((2, R, C), F32)] * 4,
        compiler_params=_cparams(("arbitrary", "arbitrary")),
    )(parts0, parts1, w, m, v)


def _layer_fwd(x, P, skew, band, S, rest, ex, tgt=None):
    z, ht, *got0 = _in_proj(x, P["pre_g"], P["w_in_t"], ex[0])
    P = {**P, **rest(got0)}
    ac, ap, u1, *got1 = _mix_fwd(z, P["conv_dw"], P["conv_dw_b"], P["conv_ln_g"], P["conv_ln_b"],
                                 P["pool_w"], P["pool_b"], P["pool_scale"], S, ex[1])
    o, *got2 = _attn_fwd(z, skew, band, S, ex[2])
    out, *got3 = _out_fwd(x, ac, o, ap, z, P["w_conv_out"], P["w_attn_out"], P["w_pool_out"], P["w_out"],
                          P["post_g"], ex[3], tgt)
    return out, (x, z, ht, ac, o, ap, u1), P, (got0, got1, got2, got3)


def _layer_bwd(dout, saved, P, skew, band, S, ex, late_ex):
    x, z, ht, ac, o, ap, u1 = saved
    (dac, dao, dag, dap, dgm, dwco, dwao, dwpo, dwout, dpostg, *got0) = _out_bwd(
        dout, ac, o, ap, z, P["w_conv_out"], P["w_attn_out"], P["w_pool_out"], P["w_out"], P["post_g"], ex[0])
    dq, dk, dv, dskew, *got1 = _attn_bwd(z, dao, skew, band, S, ex[1])
    (dzc, dzp, ddw, ddwb, dlng, dlnb, dpw, dpb, dps) = _mix_bwd(
        z, u1, dac, dap, P["conv_dw"], P["conv_dw_b"], P["conv_ln_g"], P["conv_ln_b"],
        P["pool_w"], P["pool_b"], P["pool_scale"], S)
    grads = dict(post_norm_g=dpostg, conv_dw=ddw, conv_dw_b=ddwb, conv_ln_g=dlng, conv_ln_b=dlnb,
                 w_conv_out=dwco, dskew=dskew, w_attn_out=dwao, pool_w=dpw, pool_b=dpb, pool_scale=dps,
                 w_pool_out=dwpo, w_out=dwout)
    pieces = [dzc, dq, dk, dv, dag, dzp, dgm]
    ex_rest, ex_win = late_ex(grads)
    buf, row0 = None, 0
    for p in pieces[:-1]:
        (buf,) = _in_bwd_w(ht, p, row0, buf)
        row0 += p.shape[1]
    grads["w_in_t"], *got2 = _in_bwd_w(ht, pieces[-1], row0, buf, ex_rest)
    dx, dpreg, *got3 = _in_bwd_x(pieces, P["w_in_t"], x, P["pre_g"], dout, ex_win(grads))
    grads["pre_norm_g"] = dpreg
    return dx, grads, (got0, got1, got2, got3)


WEIGHT_NAMES = ("pre_norm_g", "post_norm_g", "w_in", "conv_dw", "conv_dw_b", "conv_ln_g", "conv_ln_b",
                "w_conv_out", "rel_bias", "w_attn_out", "pool_w", "pool_b", "pool_scale", "w_pool_out", "w_out")
SHARDED = ("w_in", "w_conv_out", "w_attn_out", "w_pool_out", "w_out", "conv_dw")
REST = tuple(n for n in WEIGHT_NAMES if n not in ("w_in", "pre_norm_g"))


def _cols_from_slabs(g):
    return g.transpose(1, 0, 2).reshape(g.shape[1], N_DEV * g.shape[2])


def _slabs_from_cols(full):
    r, wd = full.shape
    return full.reshape(r, N_DEV, wd // N_DEV).transpose(1, 0, 2)


def _rest_shards(weights, l):
    return [weights["w_conv_out"][l].astype(BF16), weights["w_attn_out"][l].astype(BF16),
            weights["w_pool_out"][l].astype(BF16), weights["w_out"][l].astype(BF16), weights["conv_dw"][l]]


def _rest_weights(got):
    wco, wao, wpo, wout, cdw = got
    return dict(w_conv_out=_cols_from_slabs(wco), w_attn_out=_cols_from_slabs(wao),
                w_pool_out=_cols_from_slabs(wpo), w_out=wout.reshape(D, D), conv_dw=_cols_from_slabs(cdw))


def _grad_arrays(g, names):
    make = {"w_in": lambda: g["w_in_t"].reshape(N_DEV, NCOL // N_DEV, D),
            "w_conv_out": lambda: _slabs_from_cols(g["w_conv_out"].astype(BF16)),
            "w_attn_out": lambda: _slabs_from_cols(g["w_attn_out"].astype(BF16)),
            "w_pool_out": lambda: _slabs_from_cols(g["w_pool_out"].astype(BF16)),
            "w_out": lambda: g["w_out"].astype(BF16).reshape(N_DEV, D // N_DEV, D),
            "conv_dw": lambda: _slabs_from_cols(g["conv_dw"].astype(BF16)),
            "rel_bias": lambda: jnp.dot(g["dskew"], jnp.asarray(_skew_select().T), precision=lax.Precision.HIGHEST),
            "pool_w": lambda: g["pool_w"].reshape(4 * 128, 128),
            "pool_b": lambda: g["pool_b"].reshape(4, 128)}
    return [make[n]() if n in make else g[n] for n in names]


def _grad_exchange(g, names):
    return _Exchange(_grad_arrays(g, names), [n in SHARDED for n in names])


def kernel(x, pre_norm_g, post_norm_g, w_in, conv_dw, conv_dw_b, conv_ln_g, conv_ln_b, w_conv_out, rel_bias, w_attn_out, pool_w, pool_b, pool_scale, w_pool_out, w_out, loss_target, m_pre_norm_g, m_post_norm_g, m_w_in, m_conv_dw, m_conv_dw_b, m_conv_ln_g, m_conv_ln_b, m_w_conv_out, m_rel_bias, m_w_attn_out, m_pool_w, m_pool_b, m_pool_scale, m_w_pool_out, m_w_out, v_pre_norm_g, v_post_norm_g, v_w_in, v_conv_dw, v_conv_dw_b, v_conv_ln_g, v_conv_ln_b, v_w_conv_out, v_rel_bias, v_w_attn_out, v_pool_w, v_pool_b, v_pool_scale, v_w_pool_out, v_w_out):
    given = dict(locals())
    weights = {n: given[n] for n in WEIGHT_NAMES}
    nb, S, _ = x.shape
    T = nb * S
    L = pre_norm_g.shape[0]
    assert L == 2
    x2 = x.reshape(T, D)
    tgt2 = loss_target.reshape(T, D)
    band = jnp.asarray(_band_mask())
    skews = [_skew_table(rel_bias[l]) for l in range(L)]

    def local_params(l):
        return dict(pre_g=pre_norm_g[l:l + 1], post_g=post_norm_g[l:l + 1], conv_dw_b=conv_dw_b[l:l + 1],
                    conv_ln_g=conv_ln_g[l:l + 1], conv_ln_b=conv_ln_b[l:l + 1], pool_w=pool_w[l].astype(BF16),
                    pool_b=pool_b[l].reshape(1, BW), pool_scale=pool_scale[l:l + 1])

    win0 = w_in[0].T.astype(BF16)
    win1 = w_in[1].T.astype(BF16)
    half = win1.shape[0] // 2
    w_in_t0 = _gather_two_level(win0, "gather_w_in_0")
    gather = lambda arrays: _Exchange(arrays, [False] * len(arrays))
    h, saved0, P0, (got_rest0, got_a, got_b, got_rest1) = _layer_fwd(
        x2, {**local_params(0), "w_in_t": w_in_t0.reshape(NCOL, D)}, skews[0], band, S, _rest_weights,
        (gather(_rest_shards(weights, 0)), gather([win1[:half]]), gather([win1[half:]]),
         gather(_rest_shards(weights, 1))))
    w_in_t1 = jnp.concatenate([got_a[0], got_b[0]], axis=1).reshape(NCOL, D)
    dout, saved1, P1, (_, _, _, (lsum,)) = _layer_fwd(h, {**local_params(1), "w_in_t": w_in_t1}, skews[1], band, S,
                                                      lambda _: _rest_weights(got_rest1), (None,) * 4, tgt2)
    loss = lax.psum(lsum[0, 0], MESH_AXES) * (0.5 / D)

    no_ex = lambda grads: (None, lambda g: None)
    dout, g1, _ = _layer_bwd(dout, saved1, P1, skews[1], band, S, (None, None), no_ex)
    late0 = lambda grads: (_grad_exchange(grads, REST), lambda g: _grad_exchange(g, ("w_in",)))
    dout, g0, (got_win1, got_rest1g, got_rest0g, got_win0) = _layer_bwd(
        dout, saved0, P0, skews[0], band, S,
        (_grad_exchange(g1, ("w_in",)), _grad_exchange(g1, REST + ("pre_norm_g",))), late0)
    (got_pre0,) = _exchange([g0["pre_norm_g"]], [False], "gather_grad_pre_norm_g_0")
    parts = [{"w_in": got_win0[0], "pre_norm_g": got_pre0, **dict(zip(REST, got_rest0g))},
             {"w_in": got_win1[0], **dict(zip(REST + ("pre_norm_g",), got_rest1g))}]
    grad_x = dout.reshape(x.shape)

    outs = {}
    for n in WEIGHT_NAMES:
        view = (lambda a: a.transpose(0, 2, 1)) if n == "w_in" else (lambda a: a)
        w = view(weights[n])
        shape3 = (L,) + parts[0][n].shape[1:]
        res = _adamw_sum(parts[0][n], parts[1][n], w.reshape(shape3), view(given["m_" + n]).reshape(shape3),
                         view(given["v_" + n]).reshape(shape3), "adamw_" + n)
        outs[n] = [view(a.reshape(w.shape)) for a in res]
    return (loss, grad_x, *[outs[n][0] for n in WEIGHT_NAMES], *[outs[n][1] for n in WEIGHT_NAMES],
            *[outs[n][2] for n in WEIGHT_NAMES], *[outs[n][3] for n in WEIGHT_NAMES])
```

```python
import functools

import numpy as np
import jax
import jax.numpy as jnp
from jax import lax
from jax.experimental import pallas as pl
from jax.experimental.pallas import tpu as pltpu

F32 = jnp.float32
BF16 = jnp.bfloat16
SDS = jax.ShapeDtypeStruct

D = 1024
BW = 512
NCOL = 7680
EPS = 1e-6
NEG = -1e30
HEADS = 8
HD = 64
CHUNK = 64
LEFT = 8
MAX_REL = 256
TQ = 256
KW = 768
CONV_K = 31
WINDOWS = (2, 4, 8, 16)
HALO = 32
RC = 32
N_DEV = 8
MESH_AXES = ("x", "y", "c")

ADAM_LR = 0.001
ADAM_B1 = 0.9
ADAM_B2 = 0.999
ADAM_EPS = 1e-08
ADAM_WD = 0.01
ADAM_STEP = 10

VMEM_LIMIT = 56 * 1024 * 1024

CB_CA, CB_CB, CB_CG, CB_Q, CB_K, CB_V, CB_AG, CB_PI, CB_PG = range(9)
DZ_PIECES = (("conv", 1536), ("q", 512), ("k", 512), ("v", 512), ("ag", 512), ("pool", 1024), ("gm", 3072))


def _cparams(sem):
    return pltpu.CompilerParams(dimension_semantics=sem, vmem_limit_bytes=VMEM_LIMIT)


def _sig(x):
    return 0.5 * jnp.tanh(0.5 * x) + 0.5


def _dsilu(x, s):
    return s * (1.0 + x * (1.0 - s))


def _nt(a, b):
    return lax.dot_general(a, b, (((1,), (1,)), ((), ())), preferred_element_type=F32)


def _tn(a, b):
    return lax.dot_general(a, b, (((0,), (0,)), ((), ())), preferred_element_type=F32)


def _nn(a, b):
    return jnp.dot(a, b, preferred_element_type=F32)


def _rows8(x):
    return x[0:8] + x[8:16] + x[16:24] + x[24:32]


def _call(body, name, grid, in_specs, out_specs, out_shape, scratch, args, sem, ex=None, aliases=None):
    aliases = aliases or {}
    if ex is None:
        return pl.pallas_call(body, name=name, grid=grid, in_specs=in_specs, out_specs=out_specs,
                              out_shape=out_shape, scratch_shapes=scratch, input_output_aliases=aliases,
                              compiler_params=_cparams(sem))(*args)
    n_in, n_out = len(in_specs), len(out_specs)
    steps = int(np.prod(grid))

    def carrier(*refs):
        own_in, own_out, own_scr, exr = ex.split(refs, n_in, n_out)
        step = pl.program_id(0)
        for axis in range(1, len(grid)):
            step = step * grid[axis] + pl.program_id(axis)

        @pl.when(step == 0)
        def _():
            ex.start(exr)

        body(*own_in, *own_out, *own_scr)

        @pl.when(step == steps - 1)
        def _():
            ex.finish(exr)

    return pl.pallas_call(
        carrier, name=name + "_carrier", grid=grid, in_specs=in_specs + ex.in_specs,
        out_specs=out_specs + ex.out_specs, out_shape=out_shape + ex.out_shape,
        scratch_shapes=scratch + ex.scratch, input_output_aliases=aliases,
        compiler_params=_cparams(("arbitrary",) * len(grid)),
    )(*args, *ex.arrays)


def _in_proj(x, g, wt, ex=None):
    T = x.shape[0]
    tm = min(T, 1024)
    tn = 1536

    def body(x_ref, g_ref, w_ref, z_ref, ht_ref, h_scr):
        @pl.when(pl.program_id(1) == 0)
        def _():
            xv = x_ref[...]
            r = lax.rsqrt(jnp.mean(xv * xv, axis=-1, keepdims=True) + EPS)
            h = xv * r * g_ref[...]
            h_scr[...] = h.astype(BF16)
            ht_ref[...] = h.T.astype(BF16)
        z_ref[...] = _nt(h_scr[...], w_ref[...]).astype(BF16)

    return _call(
        body, "in_proj", (T // tm, NCOL // tn),
        [pl.BlockSpec((tm, D), lambda i, j: (i, 0)), pl.BlockSpec((1, D), lambda i, j: (0, 0)),
         pl.BlockSpec((tn, D), lambda i, j: (j, 0))],
        [pl.BlockSpec((tm, tn), lambda i, j: (i, j)), pl.BlockSpec((D, tm), lambda i, j: (0, i))],
        [SDS((T, NCOL), BF16), SDS((D, T), BF16)], [pltpu.VMEM((tm, D), BF16)],
        (x, g, wt), ("parallel", "arbitrary"), ex)


def _stencil(load, w_ref, lo, hi, tap_of):
    out = None
    for r in range(8):
        n = RC if r == 0 else RC + 8
        v = None
        for q in range((hi - r) // 8 + 1):
            o = 8 * q + r
            if o < lo:
                continue
            j = tap_of(o)
            term = w_ref[j:j + 1, :] * load(q, n)
            v = term if v is None else v + term
        if v is None:
            continue
        if r:
            v = pltpu.roll(v, n - r, axis=0)[0:RC]
        out = v if out is None else out + v
    return out


def _layer_norm_fwd(u1):
    mu = jnp.mean(u1, axis=-1, keepdims=True)
    xc = u1 - mu
    rstd = lax.rsqrt(jnp.mean(xc * xc, axis=-1, keepdims=True) + EPS)
    return xc * rstd, rstd


def _window_sums(x, w, back):
    n = x.shape[0]
    s = x
    k = 1
    while k < w:
        s = s + pltpu.roll(s, k if back else n - k, axis=0)
        k *= 2
    return s


def _pool_chunk(pwin, t_first):
    t = t_first + lax.broadcasted_iota(jnp.int32, (RC, 128), 0)
    outs = []
    for g, w in enumerate(WINDOWS):
        x = pwin[:, g * 128:(g + 1) * 128]
        s = _window_sums(x, w, True)
        cnt = jnp.minimum(t + 1, w).astype(F32)
        outs.append(s[HALO:HALO + RC] / cnt - x[HALO:HALO + RC])
    return outs


def _mix_fwd(z, dw, dwb, lng, lnb, pw, pb, ps, S, ex=None):
    T = z.shape[0]
    tm = 512
    ts = S // tm
    nh = tm // HALO

    def body(ca_ref, cb_ref, cg_ref, pi_ref, pg_ref, cah_ref, cbh_ref, pih_ref,
             dw_ref, dwb_ref, lng_ref, lnb_ref, pw_ref, pb_ref, ps_ref,
             ac_ref, ap_ref, u1_ref, ubuf, pbuf, pooled):
        i = pl.program_id(0)
        keep = jnp.where((i % ts) == 0, 0.0, 1.0)
        ubuf[0:HALO, :] = cah_ref[...].astype(F32) * _sig(cbh_ref[...].astype(F32)) * keep
        ubuf[HALO:HALO + tm, :] = ca_ref[...].astype(F32) * _sig(cb_ref[...].astype(F32))
        pbuf[0:HALO, :] = pih_ref[...].astype(F32) * keep
        pbuf[HALO:HALO + tm, :] = pi_ref[...].astype(F32)
        t0 = (i % ts) * tm

        def chunk(c, carry):
            base = pl.multiple_of(c * RC, RC)
            load = lambda q, n: ubuf[pl.ds(base + 8 * q, n), :]
            u1 = _stencil(load, dw_ref, 2, CONV_K + 1, lambda o: o - 2) + dwb_ref[...]
            u1_ref[pl.ds(base, RC), :] = u1
            n, _ = _layer_norm_fwd(u1)
            u2 = n * lng_ref[...] + lnb_ref[...]
            u3 = u2 * _sig(u2)
            cg = cg_ref[pl.ds(base, RC), :].astype(F32)
            ac_ref[pl.ds(base, RC), :] = (u3 * cg * _sig(cg)).astype(BF16)
            pwin = pbuf[pl.ds(base, RC + HALO), :]
            outs = _pool_chunk(pwin, t0 + base)
            for g in range(4):
                pooled[pl.ds(base, RC), g * 128:(g + 1) * 128] = outs[g].astype(BF16)
            return carry

        lax.fori_loop(0, tm // RC, chunk, 0, unroll=4)
        pg = pg_ref[...].astype(F32)
        spg = pg * _sig(pg)
        for g in range(4):
            sl = slice(g * 128, (g + 1) * 128)
            mixed = (_nn(pooled[:, sl], pw_ref[g]) + pb_ref[:, sl]) * ps_ref[:, sl]
            ap_ref[:, sl] = (mixed * spg[:, sl]).astype(BF16)

    def zmain(cb):
        return pl.BlockSpec((tm, BW), lambda i: (i, cb))

    def zprev(cb):
        return pl.BlockSpec((HALO, BW), lambda i: (jnp.maximum(i * nh - 1, 0), cb))

    full = lambda shape: pl.BlockSpec(shape, lambda i: (0,) * len(shape))
    row = pl.BlockSpec((tm, BW), lambda i: (i, 0))
    return _call(
        body, "mix_fwd", (T // tm,),
        [zmain(CB_CA), zmain(CB_CB), zmain(CB_CG), zmain(CB_PI), zmain(CB_PG),
         zprev(CB_CA), zprev(CB_CB), zprev(CB_PI),
         full((CONV_K, BW)), full((1, BW)), full((1, BW)), full((1, BW)),
         full((4, 128, 128)), full((1, BW)), full((1, BW))],
        [row, row, row], [SDS((T, BW), BF16), SDS((T, BW), BF16), SDS((T, BW), F32)],
        [pltpu.VMEM((HALO + tm, BW), F32), pltpu.VMEM((HALO + tm, BW), F32), pltpu.VMEM((tm, BW), BF16)],
        (z, z, z, z, z, z, z, z, dw, dwb, lng, lnb, pw, pb, ps), ("parallel",), ex)


def _attn_specs(nq):
    def kv(cb, off):
        return pl.BlockSpec((TQ, BW), lambda i: (i - jnp.minimum(off, i % nq), cb))
    return [pl.BlockSpec((TQ, BW), lambda i: (i, CB_Q)),
            kv(CB_K, 2), kv(CB_K, 1), kv(CB_K, 0), kv(CB_V, 2), kv(CB_V, 1), kv(CB_V, 0)]


def _softmax_rows(s):
    m = jnp.max(s, axis=-1, keepdims=True)
    e = jnp.exp(s - m)
    return e / jnp.sum(e, axis=-1, keepdims=True)


NSKEW = 1024


def _skew_table(table):
    return jnp.dot(table, jnp.asarray(_skew_select()), precision=lax.Precision.HIGHEST)


def _skew_select():
    d = np.arange(TQ + KW - 1)
    idx = np.clip(3 * TQ - 1 - d, -MAX_REL, MAX_REL) + MAX_REL
    sel = np.zeros((2 * MAX_REL + 1, NSKEW), np.float32)
    sel[idx, d] = 1.0
    return sel


def _bias_from_skew(f_ref, band_ref, bias_scr):
    for h in range(HEADS):
        rows = jnp.broadcast_to(f_ref[h:h + 1, :], (TQ, NSKEW))
        rows = pltpu.roll(rows, NSKEW - (TQ - 1), axis=1, stride=1, stride_axis=0)
        bias_scr[h] = rows[:, 0:KW] + band_ref[...]


def _skew_from_bias(db):
    i = lax.broadcasted_iota(jnp.int32, (TQ, TQ), 0)
    j = lax.broadcasted_iota(jnp.int32, (TQ, TQ), 1)
    flip = jnp.where(i + j == TQ - 1, 1.0, 0.0).astype(BF16)
    hi = db.astype(BF16)
    lo = (db - hi.astype(F32)).astype(BF16)
    rev = _nn(flip, hi) + _nn(flip, lo)
    rev = jnp.concatenate([rev, jnp.zeros((TQ, NSKEW - KW), F32)], axis=1)
    return jnp.sum(pltpu.roll(rev, 0, axis=1, stride=1, stride_axis=0), axis=0, keepdims=True)


def _attn_fwd(z, f, band, S, ex=None):
    T = z.shape[0]
    nq = S // TQ

    def body(q_ref, k2_ref, k1_ref, k0_ref, v2_ref, v1_ref, v0_ref, f_ref, band_ref, o_ref, kbuf, vbuf, b_scr):
        @pl.when(pl.program_id(0) == 0)
        def _():
            _bias_from_skew(f_ref, band_ref, b_scr)

        qb = pl.program_id(0) % nq
        kbuf[0:TQ, :] = k2_ref[...]
        kbuf[TQ:2 * TQ, :] = k1_ref[...]
        kbuf[2 * TQ:KW, :] = k0_ref[...]
        vbuf[0:TQ, :] = v2_ref[...]
        vbuf[TQ:2 * TQ, :] = v1_ref[...]
        vbuf[2 * TQ:KW, :] = v0_ref[...]
        lane = lax.broadcasted_iota(jnp.int32, (1, 128), 1)

        def attend(lo):
            def scores(h):
                sl = slice((h // 2) * 128, (h // 2 + 1) * 128)
                qp = q_ref[:, sl] * 0.125
                qm = jnp.where((lane < HD) if h % 2 == 0 else (lane >= HD), qp, jnp.zeros_like(qp))
                return _nt(qm, kbuf[lo:KW, sl]) + b_scr[h, :, lo:KW]

            s = scores(0)
            acc = None
            for h in range(HEADS):
                s_next = scores(h + 1) if h + 1 < HEADS else None
                sl = slice((h // 2) * 128, (h // 2 + 1) * 128)
                e = jnp.exp(s - jnp.max(s, axis=-1, keepdims=True))
                vp = vbuf[lo:KW, sl]
                vm = jnp.where((lane < HD) if h % 2 == 0 else (lane >= HD), vp, jnp.zeros_like(vp))
                o = _nn(e.astype(BF16), vm) * (1.0 / jnp.sum(e, axis=-1, keepdims=True))
                acc = o if h % 2 == 0 else acc + o
                if h % 2 == 1:
                    o_ref[:, sl] = acc.astype(BF16)
                s = s_next

        for nblk in (1, 2, 3):
            pl.when(jnp.minimum(qb, 2) == nblk - 1)(functools.partial(attend, (3 - nblk) * TQ))

    full = lambda shape: pl.BlockSpec(shape, lambda i: (0,) * len(shape))
    return _call(
        body, "attn_fwd", (T // TQ,),
        _attn_specs(nq) + [full((HEADS, NSKEW)), full((TQ, KW))],
        [pl.BlockSpec((TQ, BW), lambda i: (i, 0))], [SDS((T, BW), BF16)],
        [pltpu.VMEM((KW, BW), BF16), pltpu.VMEM((KW, BW), BF16), pltpu.VMEM((HEADS, TQ, KW), F32)],
        (z, z, z, z, z, z, z, f, band), ("arbitrary",), ex)


def _gates(gl_ref, gh_ref):
    gl = _sig(gl_ref[...].astype(F32))
    gh = _sig(gh_ref[...].astype(F32))
    return (gl[:, 0:D], jnp.concatenate([gl[:, D:1536], gh[:, 0:512]], axis=1), gh[:, 512:1536])


def _out_specs_in(tm):
    row = lambda w: pl.BlockSpec((tm, w), lambda i: (i, 0))
    full = lambda shape: pl.BlockSpec(shape, lambda i: (0,) * len(shape))
    return [row(BW), row(BW), row(BW),
            pl.BlockSpec((tm, BW), lambda i: (i, CB_AG)),
            pl.BlockSpec((tm, 1536), lambda i: (i, 3)),
            pl.BlockSpec((tm, 1536), lambda i: (i, 4)),
            full((BW, D)), full((BW, D)), full((BW, D)), full((D, D)), full((1, D))]


def _out_fwd(x, ac, o, ap, z, wco, wao, wpo, wout, postg, ex=None, tgt=None):
    T = x.shape[0]
    tm = 512
    last = tgt is not None

    def body(ac_ref, o_ref, ap_ref, ag_ref, gl_ref, gh_ref, wco_ref, wao_ref, wpo_ref, wout_ref, pg_ref,
             x_ref, *rest):
        ag = ag_ref[...].astype(F32)
        aat = (o_ref[...].astype(F32) * ag * _sig(ag)).astype(BF16)
        g0, g1, g2 = _gates(gl_ref, gh_ref)
        merged = g0 * _nn(ac_ref[...], wco_ref[...])
        merged = merged + g1 * _nn(aat, wao_ref[...])
        merged = merged + g2 * _nn(ap_ref[...], wpo_ref[...])
        y = _nn(merged.astype(BF16), wout_ref[...])
        ry = lax.rsqrt(jnp.mean(y * y, axis=-1, keepdims=True) + EPS)
        out = x_ref[...] + y * ry * pg_ref[...]
        if not last:
            rest[0][...] = out
            return
        t_ref, d_ref, l_ref = rest

        @pl.when(pl.program_id(0) == 0)
        def _():
            l_ref[...] = jnp.zeros_like(l_ref)
        d = out - t_ref[...]
        d_ref[...] = d * (1.0 / D)
        l_ref[...] += jnp.sum(jnp.sum(d * d, axis=0, keepdims=True), axis=1, keepdims=True)

    row = pl.BlockSpec((tm, D), lambda i: (i, 0))
    if not last:
        return _call(body, "out_fwd", (T // tm,), _out_specs_in(tm) + [row], [row], [SDS((T, D), F32)], [],
                     (ac, o, ap, z, z, z, wco, wao, wpo, wout, postg, x), ("parallel",), ex)
    return _call(body, "out_fwd_loss", (T // tm,), _out_specs_in(tm) + [row, row],
                 [row, pl.BlockSpec((1, 128), lambda i: (0, 0))], [SDS((T, D), F32), SDS((1, 128), F32)], [],
                 (ac, o, ap, z, z, z, wco, wao, wpo, wout, postg, x, tgt), ("arbitrary",), ex)


def _out_bwd(dout, ac, o, ap, z, wco, wao, wpo, wout, postg, ex=None):
    T = dout.shape[0]
    tm = 256

    def body(ac_ref, o_ref, ap_ref, ag_ref, gl_ref, gh_ref, wco_ref, wao_ref, wpo_ref, wout_ref, pg_ref, do_ref,
             dac_ref, dao_ref, dag_ref, dap_ref, dgm_ref, dwco_ref, dwao_ref, dwpo_ref, dwout_ref, dpg_ref):
        @pl.when(pl.program_id(0) == 0)
        def _():
            for r in (dwco_ref, dwao_ref, dwpo_ref, dwout_ref, dpg_ref):
                r[...] = jnp.zeros_like(r)

        ag = ag_ref[...].astype(F32)
        sag = _sig(ag)
        ov = o_ref[...].astype(F32)
        acts = (ac_ref[...], (ov * ag * sag).astype(BF16), ap_ref[...])
        ws = (wco_ref, wao_ref, wpo_ref)
        gates = _gates(gl_ref, gh_ref)
        ys = [_nn(acts[b], ws[b][...]) for b in range(3)]
        merged = (gates[0] * ys[0] + gates[1] * ys[1] + gates[2] * ys[2]).astype(BF16)
        y = _nn(merged, wout_ref[...])
        ry = lax.rsqrt(jnp.mean(y * y, axis=-1, keepdims=True) + EPS)
        yn = y * ry
        dout_v = do_ref[...]
        dpg_ref[...] += jnp.sum(dout_v * yn, axis=0, keepdims=True)
        dyn = dout_v * pg_ref[...]
        dy = (ry * (dyn - yn * jnp.mean(dyn * yn, axis=-1, keepdims=True))).astype(BF16)
        dmerged = _nt(dy, wout_ref[...])
        dwout_ref[...] += _tn(merged, dy)
        dws = (dwco_ref, dwao_ref, dwpo_ref)
        das = []
        for b in range(3):
            gb = gates[b]
            dgm_ref[:, b * D:(b + 1) * D] = (dmerged * ys[b] * gb * (1.0 - gb)).astype(BF16)
            dyb = (dmerged * gb).astype(BF16)
            dws[b][...] += _tn(acts[b], dyb)
            das.append(_nt(dyb, ws[b][...]))
        dac_ref[...] = das[0].astype(BF16)
        dap_ref[...] = das[2].astype(BF16)
        dao_ref[...] = (das[1] * ag * sag).astype(BF16)
        dag_ref[...] = (das[1] * ov * _dsilu(ag, sag)).astype(BF16)

    row = lambda w: pl.BlockSpec((tm, w), lambda i: (i, 0))
    full = lambda shape: pl.BlockSpec(shape, lambda i: (0,) * len(shape))
    return _call(
        body, "out_bwd", (T // tm,), _out_specs_in(tm) + [row(D)],
        [row(BW), row(BW), row(BW), row(BW), row(3 * D),
         full((BW, D)), full((BW, D)), full((BW, D)), full((D, D)), full((1, D))],
        [SDS((T, BW), BF16)] * 4 + [SDS((T, 3 * D), BF16)]
        + [SDS((BW, D), F32)] * 3 + [SDS((D, D), F32), SDS((1, D), F32)], [],
        (ac, o, ap, z, z, z, wco, wao, wpo, wout, postg, dout), ("arbitrary",), ex)


def _attn_bwd(z, dao, f, band, S, ex=None):
    T = z.shape[0]
    nq = S // TQ
    nsteps = T // TQ

    def body(q_ref, k2_ref, k1_ref, k0_ref, v2_ref, v1_ref, v0_ref, do_ref, f_ref, band_ref,
             dq_ref, dk_ref, dv_ref, df_ref, kbuf, vbuf, dkacc, dvacc, b_scr, db_scr):
        i = pl.program_id(0)
        qb = i % nq

        @pl.when(i == 0)
        def _():
            _bias_from_skew(f_ref, band_ref, b_scr)
            db_scr[...] = jnp.zeros_like(db_scr)

        @pl.when(qb == 0)
        def _():
            dkacc[...] = jnp.zeros_like(dkacc)
            dvacc[...] = jnp.zeros_like(dvacc)

        kbuf[0:TQ, :] = k2_ref[...]
        kbuf[TQ:2 * TQ, :] = k1_ref[...]
        kbuf[2 * TQ:KW, :] = k0_ref[...]
        vbuf[0:TQ, :] = v2_ref[...]
        vbuf[TQ:2 * TQ, :] = v1_ref[...]
        vbuf[2 * TQ:KW, :] = v0_ref[...]
        lane = lax.broadcasted_iota(jnp.int32, (1, 128), 1)
        row0 = pl.multiple_of(qb * TQ, TQ)

        def attend(lo):
            def first_matmuls(h):
                sl = slice((h // 2) * 128, (h // 2 + 1) * 128)
                msk = (lane < HD) if h % 2 == 0 else (lane >= HD)
                qp = q_ref[:, sl] * 0.125
                dop = do_ref[:, sl]
                qm = jnp.where(msk, qp, jnp.zeros_like(qp))
                dom = jnp.where(msk, dop, jnp.zeros_like(dop))
                s = _nt(qm, kbuf[lo:KW, sl]) + b_scr[h, :, lo:KW]
                return s, _nt(dom, vbuf[lo:KW, sl]), qm, dom

            cur = first_matmuls(0)
            dq_acc = dk_acc = dv_acc = None
            for h in range(HEADS):
                nxt = first_matmuls(h + 1) if h + 1 < HEADS else None
                s, dp, qm, dom = cur
                sl = slice((h // 2) * 128, (h // 2 + 1) * 128)
                e = jnp.exp(s - jnp.max(s, axis=-1, keepdims=True))
                p = e * (1.0 / jnp.sum(e, axis=-1, keepdims=True))
                ds = p * (dp - jnp.sum(p * dp, axis=-1, keepdims=True))
                db_scr[h, :, lo:KW] += ds
                dsb = ds.astype(BF16)
                kp = kbuf[lo:KW, sl]
                km = jnp.where((lane < HD) if h % 2 == 0 else (lane >= HD), kp, jnp.zeros_like(kp))
                dq_h = _nn(dsb, km) * 0.125
                dk_h = _tn(dsb, qm)
                dv_h = _tn(p.astype(BF16), dom)
                if h % 2 == 0:
                    dq_acc, dk_acc, dv_acc = dq_h, dk_h, dv_h
                else:
                    dq_ref[:, sl] = (dq_acc + dq_h).astype(BF16)
                    dkacc[pl.ds(row0 + lo, KW - lo), sl] += dk_acc + dk_h
                    dvacc[pl.ds(row0 + lo, KW - lo), sl] += dv_acc + dv_h
                cur = nxt

        for nblk in (1, 2, 3):
            pl.when(jnp.minimum(qb, 2) == nblk - 1)(functools.partial(attend, (3 - nblk) * TQ))

        @pl.when(qb == nq - 1)
        def _():
            dk_ref[...] = dkacc[2 * TQ:2 * TQ + S, :].astype(BF16)
            dv_ref[...] = dvacc[2 * TQ:2 * TQ + S, :].astype(BF16)

        @pl.when(i == nsteps - 1)
        def _():
            for h in range(HEADS):
                df_ref[h:h + 1, :] = _skew_from_bias(db_scr[h])

    full = lambda shape: pl.BlockSpec(shape, lambda i: (0,) * len(shape))
    return _call(
        body, "attn_bwd", (nsteps,),
        _attn_specs(nq) + [pl.BlockSpec((TQ, BW), lambda i: (i, 0)), full((HEADS, NSKEW)), full((TQ, KW))],
        [pl.BlockSpec((TQ, BW), lambda i: (i, 0)), pl.BlockSpec((S, BW), lambda i: (i // nq, 0)),
         pl.BlockSpec((S, BW), lambda i: (i // nq, 0)), full((HEADS, NSKEW))],
        [SDS((T, BW), BF16)] * 3 + [SDS((HEADS, NSKEW), F32)],
        [pltpu.VMEM((KW, BW), BF16), pltpu.VMEM((KW, BW), BF16),
         pltpu.VMEM((S + 2 * TQ, BW), F32), pltpu.VMEM((S + 2 * TQ, BW), F32),
         pltpu.VMEM((HEADS, TQ, KW), F32), pltpu.VMEM((HEADS, TQ, KW), F32)],
        (z, z, z, z, z, z, z, dao, f, band), ("arbitrary",), ex)


def _mix_bwd(z, u1, dac, dap, dw, dwb, lng, lnb, pw, pb, ps, S):
    T = z.shape[0]
    tm = 512
    ts = S // tm
    nh = tm // HALO
    nsteps = T // tm
    nblk32 = T // HALO

    def body(ca_ref, cb_ref, cg_ref, pi_ref, pg_ref, u1_ref, dac_ref, dap_ref,
             cah_ref, cbh_ref, pih_ref,
             cgn_ref, pgn_ref, u1n_ref, dacn_ref, dapn_ref,
             dw_ref, dwb_ref, lng_ref, lnb_ref, pw_ref, pb_ref, ps_ref,
             dzc_ref, dzp_ref, ddw_ref, ddwb_ref, dlng_ref, dlnb_ref, dpw_ref, dpb_ref, dps_ref,
             ubuf, gbuf, pbuf, qbuf, pooled, *accs):
        tap_acc, (lng_acc, lnb_acc, dwb_acc) = accs[:CONV_K], accs[CONV_K:]
        i = pl.program_id(0)
        keep_prev = jnp.where((i % ts) == 0, 0.0, 1.0)
        keep_next = jnp.where((i % ts) == ts - 1, 0.0, 1.0)
        t0 = (i % ts) * tm

        @pl.when(i == 0)
        def _():
            for a in accs:
                a[...] = jnp.zeros_like(a)
            dpw_ref[...] = jnp.zeros_like(dpw_ref)
            dpb_ref[...] = jnp.zeros_like(dpb_ref)
            dps_ref[...] = jnp.zeros_like(dps_ref)

        ubuf[0:HALO, :] = cah_ref[...].astype(F32) * _sig(cbh_ref[...].astype(F32)) * keep_prev
        ubuf[HALO:HALO + tm, :] = ca_ref[...].astype(F32) * _sig(cb_ref[...].astype(F32))
        pbuf[0:HALO, :] = pih_ref[...].astype(F32) * keep_prev
        pbuf[HALO:HALO + tm, :] = pi_ref[...].astype(F32)

        def norm_back(u1v, cg, dacv):
            n, rstd = _layer_norm_fwd(u1v)
            u2 = n * lng_ref[...] + lnb_ref[...]
            s2 = _sig(u2)
            scg = _sig(cg)
            du2 = dacv * cg * scg * _dsilu(u2, s2)
            dn = du2 * lng_ref[...]
            du1 = rstd * (dn - jnp.mean(dn, axis=-1, keepdims=True)
                          - n * jnp.mean(dn * n, axis=-1, keepdims=True))
            return du1, du2, n, dacv * u2 * s2 * _dsilu(cg, scg)

        def chunk_a(c, carry):
            base = pl.multiple_of(c * RC, RC)
            du1, du2, n, dcg = norm_back(u1_ref[pl.ds(base, RC), :], cg_ref[pl.ds(base, RC), :].astype(F32),
                                         dac_ref[pl.ds(base, RC), :].astype(F32))
            gbuf[pl.ds(base, RC), :] = du1
            dzc_ref[pl.ds(base, RC), 2 * BW:3 * BW] = dcg.astype(BF16)
            lng_acc[...] += _rows8(du2 * n)
            lnb_acc[...] += _rows8(du2)
            dwb_acc[...] += _rows8(du1)
            padded = jnp.concatenate([du1, jnp.zeros((8, BW), F32)], axis=0)
            for r in range(8):
                nrow = RC if r == 0 else RC + 8
                g = du1 if r == 0 else pltpu.roll(padded, r, axis=0)
                for q in range((CONV_K + 1 - r) // 8 + 1):
                    o = 8 * q + r
                    if o < 2:
                        continue
                    prod = g * ubuf[pl.ds(base + 8 * q, nrow), :]
                    red = prod[0:8]
                    for k in range(1, nrow // 8):
                        red = red + prod[8 * k:8 * k + 8]
                    tap_acc[o - 2][...] += red
            return carry

        lax.fori_loop(0, tm // RC, chunk_a, 0, unroll=4)
        du1n, _, _, _ = norm_back(u1n_ref[...], cgn_ref[...].astype(F32), dacn_ref[...].astype(F32))
        gbuf[tm:tm + HALO, :] = du1n * keep_next

        def chunk_p(c, carry):
            base = pl.multiple_of(c * RC, RC)
            outs = _pool_chunk(pbuf[pl.ds(base, RC + HALO), :], t0 + base)
            for g in range(4):
                pooled[pl.ds(base, RC), g * 128:(g + 1) * 128] = outs[g].astype(BF16)
            return carry

        lax.fori_loop(0, tm // RC, chunk_p, 0)

        def cnt_of(t_first, rows, w):
            t = t_first + lax.broadcasted_iota(jnp.int32, (rows, 128), 0)
            return jnp.minimum(t + 1, w).astype(F32)

        pg = pg_ref[...].astype(F32)
        spg_s = _sig(pg)
        dapv = dap_ref[...].astype(F32)
        pgn = pgn_ref[...].astype(F32)
        dmixn = dapn_ref[...].astype(F32) * pgn * _sig(pgn) * ps_ref[...] * keep_next
        for g, w in enumerate(WINDOWS):
            sl = slice(g * 128, (g + 1) * 128)
            mixed_u = _nn(pooled[:, sl], pw_ref[g]) + pb_ref[:, sl]
            dap_g = dapv[:, sl]
            pg_g = pg[:, sl]
            s_g = spg_s[:, sl]
            silu_g = pg_g * s_g
            dps_ref[:, sl] += jnp.sum(dap_g * silu_g * mixed_u, axis=0, keepdims=True)
            dzp_ref[:, BW + g * 128:BW + (g + 1) * 128] = (
                dap_g * mixed_u * ps_ref[:, sl] * _dsilu(pg_g, s_g)).astype(BF16)
            dmix = dap_g * silu_g * ps_ref[:, sl]
            dpb_ref[:, sl] += jnp.sum(dmix, axis=0, keepdims=True)
            dmixb = dmix.astype(BF16)
            dpw_ref[g] += _tn(pooled[:, sl], dmixb)
            qbuf[0:tm, sl] = _nt(dmixb, pw_ref[g]) / cnt_of(t0, tm, w)
            qbuf[tm:tm + HALO, sl] = _nt(dmixn[:, sl].astype(BF16), pw_ref[g]) / cnt_of(t0 + tm, HALO, w)

        def chunk_b(c, carry):
            base = pl.multiple_of(c * RC, RC)
            load = lambda q, n: gbuf[pl.ds(base + 8 * q, n), :]
            du0 = _stencil(load, dw_ref, 0, CONV_K - 1, lambda o: CONV_K - 1 - o)
            ca = ca_ref[pl.ds(base, RC), :].astype(F32)
            sb = _sig(cb_ref[pl.ds(base, RC), :].astype(F32))
            dzc_ref[pl.ds(base, RC), 0:BW] = (du0 * sb).astype(BF16)
            dzc_ref[pl.ds(base, RC), BW:2 * BW] = (du0 * ca * sb * (1.0 - sb)).astype(BF16)
            qwin = qbuf[pl.ds(base, RC + HALO), :]
            t = t0 + base + lax.broadcasted_iota(jnp.int32, (RC, 128), 0)
            for g, w in enumerate(WINDOWS):
                x = qwin[:, g * 128:(g + 1) * 128]
                s = _window_sums(x, w, False)
                cnt = jnp.minimum(t + 1, w).astype(F32)
                dzp_ref[pl.ds(base, RC), g * 128:(g + 1) * 128] = (s[0:RC] - cnt * x[0:RC]).astype(BF16)
            return carry

        lax.fori_loop(0, tm // RC, chunk_b, 0)

        @pl.when(i == nsteps - 1)
        def _():
            dlng_ref[...] = jnp.sum(lng_acc[...], axis=0, keepdims=True)
            dlnb_ref[...] = jnp.sum(lnb_acc[...], axis=0, keepdims=True)
            ddwb_ref[...] = jnp.sum(dwb_acc[...], axis=0, keepdims=True)
            for j in range(CONV_K):
                ddw_ref[j:j + 1, :] = jnp.sum(tap_acc[j][...], axis=0, keepdims=True)

    def zmain(cb):
        return pl.BlockSpec((tm, BW), lambda i: (i, cb))

    def zprev(cb):
        return pl.BlockSpec((HALO, BW), lambda i: (jnp.maximum(i * nh - 1, 0), cb))

    def znext(cb):
        return pl.BlockSpec((HALO, BW), lambda i: (jnp.minimum((i + 1) * nh, nblk32 - 1), cb))

    row = lambda w: pl.BlockSpec((tm, w), lambda i: (i, 0))
    full = lambda shape: pl.BlockSpec(shape, lambda i: (0,) * len(shape))
    return pl.pallas_call(
        body, name="mix_bwd", grid=(nsteps,),
        in_specs=[zmain(CB_CA), zmain(CB_CB), zmain(CB_CG), zmain(CB_PI), zmain(CB_PG), row(BW), row(BW), row(BW),
                  zprev(CB_CA), zprev(CB_CB), zprev(CB_PI),
                  znext(CB_CG), znext(CB_PG), znext(0), znext(0), znext(0),
                  full((CONV_K, BW)), full((1, BW)), full((1, BW)), full((1, BW)),
                  full((4, 128, 128)), full((1, BW)), full((1, BW))],
        out_specs=[row(3 * BW), row(2 * BW), full((CONV_K, BW)), full((1, BW)), full((1, BW)), full((1, BW)),
                   full((4, 128, 128)), full((1, BW)), full((1, BW))],
        out_shape=[SDS((T, 3 * BW), BF16), SDS((T, 2 * BW), BF16), SDS((CONV_K, BW), F32),
                   SDS((1, BW), F32), SDS((1, BW), F32), SDS((1, BW), F32),
                   SDS((4, 128, 128), F32), SDS((1, BW), F32), SDS((1, BW), F32)],
        scratch_shapes=[pltpu.VMEM((HALO + tm, BW), F32), pltpu.VMEM((tm + HALO, BW), F32),
                        pltpu.VMEM((HALO + tm, BW), F32), pltpu.VMEM((tm + HALO, BW), F32),
                        pltpu.VMEM((tm, BW), BF16)] + [pltpu.VMEM((8, BW), F32)] * (CONV_K + 3),
        compiler_params=_cparams(("arbitrary",)),
    )(z, z, z, z, z, u1, dac, dap, z, z, z, z, z, u1, dac, dap, dw, dwb, lng, lnb, pw, pb, ps)


def _in_bwd_x(pieces, wt, x, g, dout, ex=None):
    T = x.shape[0]
    tm = 256
    widths = [p.shape[1] for p in pieces]
    offs = np.cumsum([0] + widths)
    npc = len(pieces)

    def body(*refs):
        p_refs = refs[:npc]
        w_ref, x_ref, g_ref, do_ref, dx_ref, dg_ref = refs[npc:]

        @pl.when(pl.program_id(0) == 0)
        def _():
            dg_ref[...] = jnp.zeros_like(dg_ref)

        dh = None
        for k in range(npc):
            t = _nn(p_refs[k][...], w_ref[int(offs[k]):int(offs[k + 1]), :])
            dh = t if dh is None else dh + t
        xv = x_ref[...]
        r = lax.rsqrt(jnp.mean(xv * xv, axis=-1, keepdims=True) + EPS)
        xn = xv * r
        dg_ref[...] += jnp.sum(dh * xn, axis=0, keepdims=True)
        dxn = dh * g_ref[...]
        dx_ref[...] = do_ref[...] + r * (dxn - xn * jnp.mean(dxn * xn, axis=-1, keepdims=True))

    row = lambda wd: pl.BlockSpec((tm, wd), lambda i: (i, 0))
    return _call(
        body, "in_bwd_x", (T // tm,),
        [row(wd) for wd in widths] + [pl.BlockSpec((NCOL, D), lambda i: (0, 0)),
                                      row(D), pl.BlockSpec((1, D), lambda i: (0, 0)), row(D)],
        [row(D), pl.BlockSpec((1, D), lambda i: (0, 0))], [SDS((T, D), F32), SDS((1, D), F32)], [],
        (*pieces, wt, x, g, dout), ("arbitrary",), ex)


def _in_bwd_w(ht, piece, row0, buf=None, ex=None):
    T = ht.shape[1]
    wd = piece.shape[1]
    tn = next(t for t in (1024, 768, 512) if wd % t == 0 and row0 % t == 0)
    tk = min(T, 2048)
    nk = T // tk
    j0 = row0 // tn

    def body(ht_ref, p_ref, *rest):
        o_ref, acc = rest[-2:]
        k = pl.program_id(1)

        @pl.when(k == 0)
        def _():
            acc[...] = jnp.zeros_like(acc)
        acc[...] += _nn(ht_ref[...], p_ref[...])

        @pl.when(k == nk - 1)
        def _():
            o_ref[...] = acc[...].T.astype(BF16)

    in_specs = [pl.BlockSpec((D, tk), lambda j, k: (0, k)), pl.BlockSpec((tk, tn), lambda j, k: (k, j))]
    args = (ht, piece)
    if buf is not None:
        in_specs.append(pl.BlockSpec(memory_space=pl.ANY))
        args += (buf,)
    return _call(
        body, "in_bwd_w", (wd // tn, nk), in_specs,
        [pl.BlockSpec((tn, D), lambda j, k: (j + j0, 0))], [SDS((NCOL, D), BF16)], [pltpu.VMEM((D, tn), F32)],
        args, ("parallel", "arbitrary"), ex, None if buf is None else {2: 0})


def _band_mask():
    qc = np.arange(TQ)[:, None] // CHUNK
    kc = (np.arange(KW)[None, :] - 2 * TQ) // CHUNK
    band = (kc <= qc) & (kc >= qc - LEFT)
    return np.where(band, 0.0, NEG).astype(np.float32)


def _my_id():
    return 4 * lax.axis_index("x") + 2 * lax.axis_index("y") + lax.axis_index("c")


def _peers():
    x, y, c = lax.axis_index("x"), lax.axis_index("y"), lax.axis_index("c")
    out = []
    for k in range(1, N_DEV):
        fx, fy, fc = (k >> 2) & 1, (k >> 1) & 1, k & 1
        px, py, pc = x ^ fx, y ^ fy, c ^ fc
        out.append(((px, py, pc), 4 * px + 2 * py + pc))
    return out


class _Exchange:
    def __init__(self, arrays, scatter):
        self.arrays = list(arrays)
        self.scatter = list(scatter)
        self.n = n = len(arrays)
        hbm = pl.BlockSpec(memory_space=pltpu.HBM)
        self.in_specs = [hbm] * n
        self.out_specs = [hbm] * n
        self.out_shape = [SDS((N_DEV,) + tuple(a.shape[1:] if s else a.shape), a.dtype)
                          for a, s in zip(arrays, scatter)]
        self.scratch = [pltpu.SemaphoreType.DMA((N_DEV - 1, n)), pltpu.SemaphoreType.DMA((N_DEV - 1, n)),
                        pltpu.SemaphoreType.DMA((n,))]

    def split(self, refs, n_in, n_out):
        n = self.n
        own_in = refs[:n_in]
        ex_in = refs[n_in:n_in + n]
        own_out = refs[n_in + n:n_in + n + n_out]
        ex_out = refs[n_in + n + n_out:n_in + 2 * n + n_out]
        rest = refs[n_in + 2 * n + n_out:]
        return own_in, own_out, rest[:-3], (ex_in, ex_out, rest[-3:])

    def _copy(self, ex, k, p, landing):
        in_refs, out_refs, (send_sems, recv_sems, _) = ex
        pos, pid = _peers()[p]
        return pltpu.make_async_remote_copy(
            src_ref=in_refs[k].at[pid] if self.scatter[k] else in_refs[k],
            dst_ref=out_refs[k].at[pid if landing else _my_id()],
            send_sem=send_sems.at[p, k], recv_sem=recv_sems.at[p, k],
            device_id=pos, device_id_type=pl.DeviceIdType.MESH)

    def _own(self, ex, k):
        in_refs, out_refs, (_, _, local_sems) = ex
        me = _my_id()
        return pltpu.make_async_copy(in_refs[k].at[me] if self.scatter[k] else in_refs[k], out_refs[k].at[me],
                                     local_sems.at[k])

    def start(self, ex):
        for k in range(self.n):
            self._own(ex, k).start()
        for p in range(N_DEV - 1):
            for k in range(self.n):
                self._copy(ex, k, p, False).start()

    def finish(self, ex):
        for p in range(N_DEV - 1):
            for k in range(self.n):
                self._copy(ex, k, p, True).wait_recv()
        for p in range(N_DEV - 1):
            for k in range(self.n):
                self._copy(ex, k, p, False).wait_send()
        for k in range(self.n):
            self._own(ex, k).wait()


def _exchange(arrays, scatter, name):
    ex = _Exchange(arrays, scatter)

    def body(*refs):
        _, _, _, exr = ex.split(refs, 0, 0)
        ex.start(exr)
        ex.finish(exr)

    return pl.pallas_call(body, name=name, in_specs=ex.in_specs, out_specs=ex.out_specs,
                          out_shape=ex.out_shape, scratch_shapes=ex.scratch)(*ex.arrays)


def _gather_two_level(shard, name):
    def body(x_ref, out_ref, send_sems, recv_sems, local_sem):
        x, y, c = lax.axis_index("x"), lax.axis_index("y"), lax.axis_index("c")
        me, sibling = (x, y, c), (x, y, 1 - c)
        chips = [(1 - x, y), (x, 1 - y), (1 - x, 1 - y)]
        slab = lambda px, py, pc: out_ref.at[4 * px + 2 * py + pc]

        def copy(k, block, to, src=None):
            return pltpu.make_async_remote_copy(
                src_ref=slab(*block) if src is None else src, dst_ref=slab(*block),
                send_sem=send_sems.at[k], recv_sem=recv_sems.at[k],
                device_id=to, device_id_type=pl.DeviceIdType.MESH)

        mine = pltpu.make_async_copy(x_ref, slab(*me), local_sem)
        mine.start()
        first = [copy(0, me, sibling, src=x_ref)] + [copy(1 + j, me, (*chip, c), src=x_ref)
                                                     for j, chip in enumerate(chips)]
        for cp in first:
            cp.start()
        passed = [copy(4 + j, (*chip, c), sibling) for j, chip in enumerate(chips)]
        for j, chip in enumerate(chips):
            copy(1 + j, (*chip, c), me).wait_recv()
            passed[j].start()
        copy(0, sibling, me).wait_recv()
        for j, chip in enumerate(chips):
            copy(4 + j, (*chip, 1 - c), me).wait_recv()
        for cp in first + passed:
            cp.wait_send()
        mine.wait()

    hbm = pl.BlockSpec(memory_space=pltpu.HBM)
    return pl.pallas_call(
        body, name=name, in_specs=[hbm], out_specs=hbm,
        out_shape=SDS((N_DEV,) + shard.shape, shard.dtype),
        scratch_shapes=[pltpu.SemaphoreType.DMA((N_DEV - 1,)), pltpu.SemaphoreType.DMA((N_DEV - 1,)),
                        pltpu.SemaphoreType.DMA],
    )(shard)


def _adamw_sum(parts0, parts1, w, m, v, name):
    _, R, C = w.shape
    tr = R
    while tr * C > 256 * 1024 and tr % 32 == 0:
        tr //= 2
    c1 = 1.0 / (1.0 - ADAM_B1 ** ADAM_STEP)
    c2 = 1.0 / (1.0 - ADAM_B2 ** ADAM_STEP)

    def body(p0_ref, p1_ref, w_ref, m_ref, v_ref, g_ref, d_ref, mo_ref, vo_ref):
        def update(p_ref):
            g = p_ref[0].astype(F32)
            for s in range(1, N_DEV):
                g = g + p_ref[s].astype(F32)
            mn = ADAM_B1 * m_ref[...] + (1.0 - ADAM_B1) * g
            vn = ADAM_B2 * v_ref[...] + (1.0 - ADAM_B2) * (g * g)
            g_ref[...] = g
            mo_ref[...] = mn
            vo_ref[...] = vn
            d_ref[...] = -ADAM_LR * ((mn * c1) / (jnp.sqrt(vn * c2) + ADAM_EPS) + ADAM_WD * w_ref[...])

        @pl.when(pl.program_id(0) == 0)
        def _():
            update(p0_ref)

        @pl.when(pl.program_id(0) == 1)
        def _():
            update(p1_ref)

    blk = pl.BlockSpec((None, tr, C), lambda l, i: (l, i, 0))
    return pl.pallas_call(
        body, name=name, grid=(2, R // tr),
        in_specs=[pl.BlockSpec((N_DEV, tr, C), lambda l, i: (0, i * (1 - l), 0)),
                  pl.BlockSpec((N_DEV, tr, C), lambda l, i: (0, i * l, 0)), blk, blk, blk],
        out_specs=[blk, blk, blk, blk],
        out_shape=[SDS((2, R, C), F32)] * 4,
        compiler_params=_cparams(("arbitrary", "arbitrary")),
    )(parts0, parts1, w, m, v)


def _layer_fwd(x, P, skew, band, S, rest, ex, tgt=None):
    z, ht, *got0 = _in_proj(x, P["pre_g"], P["w_in_t"], ex[0])
    P = {**P, **rest(got0)}
    ac, ap, u1, *got1 = _mix_fwd(z, P["conv_dw"], P["conv_dw_b"], P["conv_ln_g"], P["conv_ln_b"],
                                 P["pool_w"], P["pool_b"], P["pool_scale"], S, ex[1])
    o, *got2 = _attn_fwd(z, skew, band, S, ex[2])
    out, *got3 = _out_fwd(x, ac, o, ap, z, P["w_conv_out"], P["w_attn_out"], P["w_pool_out"], P["w_out"],
                          P["post_g"], ex[3], tgt)
    return out, (x, z, ht, ac, o, ap, u1), P, (got0, got1, got2, got3)


def _layer_bwd(dout, saved, P, skew, band, S, ex, late_ex):
    x, z, ht, ac, o, ap, u1 = saved
    (dac, dao, dag, dap, dgm, dwco, dwao, dwpo, dwout, dpostg, *got0) = _out_bwd(
        dout, ac, o, ap, z, P["w_conv_out"], P["w_attn_out"], P["w_pool_out"], P["w_out"], P["post_g"], ex[0])
    dq, dk, dv, dskew, *got1 = _attn_bwd(z, dao, skew, band, S, ex[1])
    (dzc, dzp, ddw, ddwb, dlng, dlnb, dpw, dpb, dps) = _mix_bwd(
        z, u1, dac, dap, P["conv_dw"], P["conv_dw_b"], P["conv_ln_g"], P["conv_ln_b"],
        P["pool_w"], P["pool_b"], P["pool_scale"], S)
    grads = dict(post_norm_g=dpostg, conv_dw=ddw, conv_dw_b=ddwb, conv_ln_g=dlng, conv_ln_b=dlnb,
                 w_conv_out=dwco, dskew=dskew, w_attn_out=dwao, pool_w=dpw, pool_b=dpb, pool_scale=dps,
                 w_pool_out=dwpo, w_out=dwout)
    pieces = [dzc, dq, dk, dv, dag, dzp, dgm]
    ex_rest, ex_win = late_ex(grads)
    buf, row0 = None, 0
    for p in pieces[:-1]:
        (buf,) = _in_bwd_w(ht, p, row0, buf)
        row0 += p.shape[1]
    grads["w_in_t"], *got2 = _in_bwd_w(ht, pieces[-1], row0, buf, ex_rest)
    dx, dpreg, *got3 = _in_bwd_x(pieces, P["w_in_t"], x, P["pre_g"], dout, ex_win(grads))
    grads["pre_norm_g"] = dpreg
    return dx, grads, (got0, got1, got2, got3)


WEIGHT_NAMES = ("pre_norm_g", "post_norm_g", "w_in", "conv_dw", "conv_dw_b", "conv_ln_g", "conv_ln_b",
                "w_conv_out", "rel_bias", "w_attn_out", "pool_w", "pool_b", "pool_scale", "w_pool_out", "w_out")
SHARDED = ("w_in", "w_conv_out", "w_attn_out", "w_pool_out", "w_out", "conv_dw")
REST = tuple(n for n in WEIGHT_NAMES if n not in ("w_in", "pre_norm_g"))


def _cols_from_slabs(g):
    return g.transpose(1, 0, 2).reshape(g.shape[1], N_DEV * g.shape[2])


def _slabs_from_cols(full):
    r, wd = full.shape
    return full.reshape(r, N_DEV, wd // N_DEV).transpose(1, 0, 2)


def _rest_shards(weights, l):
    return [weights["w_conv_out"][l].astype(BF16), weights["w_attn_out"][l].astype(BF16),
            weights["w_pool_out"][l].astype(BF16), weights["w_out"][l].astype(BF16), weights["conv_dw"][l]]


def _rest_weights(got):
    wco, wao, wpo, wout, cdw = got
    return dict(w_conv_out=_cols_from_slabs(wco), w_attn_out=_cols_from_slabs(wao),
                w_pool_out=_cols_from_slabs(wpo), w_out=wout.reshape(D, D), conv_dw=_cols_from_slabs(cdw))


def _grad_arrays(g, names):
    make = {"w_in": lambda: g["w_in_t"].reshape(N_DEV, NCOL // N_DEV, D),
            "w_conv_out": lambda: _slabs_from_cols(g["w_conv_out"].astype(BF16)),
            "w_attn_out": lambda: _slabs_from_cols(g["w_attn_out"].astype(BF16)),
            "w_pool_out": lambda: _slabs_from_cols(g["w_pool_out"].astype(BF16)),
            "w_out": lambda: g["w_out"].astype(BF16).reshape(N_DEV, D // N_DEV, D),
            "conv_dw": lambda: _slabs_from_cols(g["conv_dw"].astype(BF16)),
            "rel_bias": lambda: jnp.dot(g["dskew"], jnp.asarray(_skew_select().T), precision=lax.Precision.HIGHEST),
            "pool_w": lambda: g["pool_w"].reshape(4 * 128, 128),
            "pool_b": lambda: g["pool_b"].reshape(4, 128)}
    return [make[n]() if n in make else g[n] for n in names]


def _grad_exchange(g, names):
    return _Exchange(_grad_arrays(g, names), [n in SHARDED for n in names])


def kernel(x, pre_norm_g, post_norm_g, w_in, conv_dw, conv_dw_b, conv_ln_g, conv_ln_b, w_conv_out, rel_bias, w_attn_out, pool_w, pool_b, pool_scale, w_pool_out, w_out, loss_target, m_pre_norm_g, m_post_norm_g, m_w_in, m_conv_dw, m_conv_dw_b, m_conv_ln_g, m_conv_ln_b, m_w_conv_out, m_rel_bias, m_w_attn_out, m_pool_w, m_pool_b, m_pool_scale, m_w_pool_out, m_w_out, v_pre_norm_g, v_post_norm_g, v_w_in, v_conv_dw, v_conv_dw_b, v_conv_ln_g, v_conv_ln_b, v_w_conv_out, v_rel_bias, v_w_attn_out, v_pool_w, v_pool_b, v_pool_scale, v_w_pool_out, v_w_out):
    given = dict(locals())
    weights = {n: given[n] for n in WEIGHT_NAMES}
    nb, S, _ = x.shape
    T = nb * S
    L = pre_norm_g.shape[0]
    assert L == 2
    x2 = x.reshape(T, D)
    tgt2 = loss_target.reshape(T, D)
    band = jnp.asarray(_band_mask())
    skews = [_skew_table(rel_bias[l]) for l in range(L)]

    def local_params(l):
        return dict(pre_g=pre_norm_g[l:l + 1], post_g=post_norm_g[l:l + 1], conv_dw_b=conv_dw_b[l:l + 1],
                    conv_ln_g=conv_ln_g[l:l + 1], conv_ln_b=conv_ln_b[l:l + 1], pool_w=pool_w[l].astype(BF16),
                    pool_b=pool_b[l].reshape(1, BW), pool_scale=pool_scale[l:l + 1])

    win0 = w_in[0].T.astype(BF16)
    win1 = w_in[1].T.astype(BF16)
    half = win1.shape[0] // 2
    w_in_t0 = _gather_two_level(win0, "gather_w_in_0")
    gather = lambda arrays: _Exchange(arrays, [False] * len(arrays))
    h, saved0, P0, (got_rest0, got_a, got_b, got_rest1) = _layer_fwd(
        x2, {**local_params(0), "w_in_t": w_in_t0.reshape(NCOL, D)}, skews[0], band, S, _rest_weights,
        (gather(_rest_shards(weights, 0)), gather([win1[:half]]), gather([win1[half:]]),
         gather(_rest_shards(weights, 1))))
    w_in_t1 = jnp.concatenate([got_a[0], got_b[0]], axis=1).reshape(NCOL, D)
    dout, saved1, P1, (_, _, _, (lsum,)) = _layer_fwd(h, {**local_params(1), "w_in_t": w_in_t1}, skews[1], band, S,
                                                      lambda _: _rest_weights(got_rest1), (None,) * 4, tgt2)
    loss = lax.psum(lsum[0, 0], MESH_AXES) * (0.5 / D)

    no_ex = lambda grads: (None, lambda g: None)
    dout, g1, _ = _layer_bwd(dout, saved1, P1, skews[1], band, S, (None, None), no_ex)
    late0 = lambda grads: (_grad_exchange(grads, REST), lambda g: _grad_exchange(g, ("w_in",)))
    dout, g0, (got_win1, got_rest1g, got_rest0g, got_win0) = _layer_bwd(
        dout, saved0, P0, skews[0], band, S,
        (_grad_exchange(g1, ("w_in",)), _grad_exchange(g1, REST + ("pre_norm_g",))), late0)
    (got_pre0,) = _exchange([g0["pre_norm_g"]], [False], "gather_grad_pre_norm_g_0")
    parts = [{"w_in": got_win0[0], "pre_norm_g": got_pre0, **dict(zip(REST, got_rest0g))},
             {"w_in": got_win1[0], **dict(zip(REST + ("pre_norm_g",), got_rest1g))}]
    grad_x = dout.reshape(x.shape)

    outs = {}
    for n in WEIGHT_NAMES:
        view = (lambda a: a.transpose(0, 2, 1)) if n == "w_in" else (lambda a: a)
        w = view(weights[n])
        shape3 = (L,) + parts[0][n].shape[1:]
        res = _adamw_sum(parts[0][n], parts[1][n], w.reshape(shape3), view(given["m_" + n]).reshape(shape3),
                         view(given["v_" + n]).reshape(shape3), "adamw_" + n)
        outs[n] = [view(a.reshape(w.shape)) for a in res]
    return (loss, grad_x, *[outs[n][0] for n in WEIGHT_NAMES], *[outs[n][1] for n in WEIGHT_NAMES],
            *[outs[n][2] for n in WEIGHT_NAMES], *[outs[n][3] for n in WEIGHT_NAMES])
```

```python
import functools

import numpy as np
import jax
import jax.numpy as jnp
from jax import lax
from jax.experimental import pallas as pl
from jax.experimental.pallas import tpu as pltpu

F32 = jnp.float32
BF16 = jnp.bfloat16
SDS = jax.ShapeDtypeStruct

D = 1024
BW = 512
NCOL = 7680
EPS = 1e-6
NEG = -1e30
HEADS = 8
HD = 64
CHUNK = 64
LEFT = 8
MAX_REL = 256
TQ = 256
KW = 768
CONV_K = 31
WINDOWS = (2, 4, 8, 16)
HALO = 32
RC = 32
N_DEV = 8
MESH_AXES = ("x", "y", "c")

ADAM_LR = 0.001
ADAM_B1 = 0.9
ADAM_B2 = 0.999
ADAM_EPS = 1e-08
ADAM_WD = 0.01
ADAM_STEP = 10

VMEM_LIMIT = 56 * 1024 * 1024

CB_CA, CB_CB, CB_CG, CB_Q, CB_K, CB_V, CB_AG, CB_PI, CB_PG = range(9)
DZ_PIECES = (("conv", 1536), ("q", 512), ("k", 512), ("v", 512), ("ag", 512), ("pool", 1024), ("gm", 3072))


def _cparams(sem):
    return pltpu.CompilerParams(dimension_semantics=sem, vmem_limit_bytes=VMEM_LIMIT)


def _sig(x):
    return 0.5 * jnp.tanh(0.5 * x) + 0.5


def _dsilu(x, s):
    return s * (1.0 + x * (1.0 - s))


def _nt(a, b):
    return lax.dot_general(a, b, (((1,), (1,)), ((), ())), preferred_element_type=F32)


def _tn(a, b):
    return lax.dot_general(a, b, (((0,), (0,)), ((), ())), preferred_element_type=F32)


def _nn(a, b):
    return jnp.dot(a, b, preferred_element_type=F32)


def _rows8(x):
    return x[0:8] + x[8:16] + x[16:24] + x[24:32]


def _call(body, name, grid, in_specs, out_specs, out_shape, scratch, args, sem, ex=None, aliases=None):
    aliases = aliases or {}
    if ex is None:
        return pl.pallas_call(body, name=name, grid=grid, in_specs=in_specs, out_specs=out_specs,
                              out_shape=out_shape, scratch_shapes=scratch, input_output_aliases=aliases,
                              compiler_params=_cparams(sem))(*args)
    n_in, n_out = len(in_specs), len(out_specs)
    steps = int(np.prod(grid))

    def carrier(*refs):
        own_in, own_out, own_scr, exr = ex.split(refs, n_in, n_out)
        step = pl.program_id(0)
        for axis in range(1, len(grid)):
            step = step * grid[axis] + pl.program_id(axis)

        @pl.when(step == 0)
        def _():
            ex.start(exr)

        body(*own_in, *own_out, *own_scr)

        @pl.when(step == steps - 1)
        def _():
            ex.finish(exr)

    return pl.pallas_call(
        carrier, name=name + "_carrier", grid=grid, in_specs=in_specs + ex.in_specs,
        out_specs=out_specs + ex.out_specs, out_shape=out_shape + ex.out_shape,
        scratch_shapes=scratch + ex.scratch, input_output_aliases=aliases,
        compiler_params=_cparams(("arbitrary",) * len(grid)),
    )(*args, *ex.arrays)


def _in_proj(x, g, wt, ex=None):
    T = x.shape[0]
    tm = 512
    tn = 1536

    def body(x_ref, g_ref, w_ref, z_ref, ht_ref):
        xv = x_ref[...]
        r = lax.rsqrt(jnp.mean(xv * xv, axis=-1, keepdims=True) + EPS)
        h = xv * r * g_ref[...]
        hb = h.astype(BF16)
        ht_ref[...] = h.T.astype(BF16)
        for c in range(NCOL // tn):
            z_ref[:, c * tn:(c + 1) * tn] = _nt(hb, w_ref[c * tn:(c + 1) * tn, :]).astype(BF16)

    return _call(
        body, "in_proj", (T // tm,),
        [pl.BlockSpec((tm, D), lambda i: (i, 0)), pl.BlockSpec((1, D), lambda i: (0, 0)),
         pl.BlockSpec((NCOL, D), lambda i: (0, 0))],
        [pl.BlockSpec((tm, NCOL), lambda i: (i, 0)), pl.BlockSpec((D, tm), lambda i: (0, i))],
        [SDS((T, NCOL), BF16), SDS((D, T), BF16)], [],
        (x, g, wt), ("parallel",), ex)


def _stencil(load, w_ref, lo, hi, tap_of):
    out = None
    for r in range(8):
        n = RC if r == 0 else RC + 8
        v = None
        for q in range((hi - r) // 8 + 1):
            o = 8 * q + r
            if o < lo:
                continue
            j = tap_of(o)
            term = w_ref[j:j + 1, :] * load(q, n)
            v = term if v is None else v + term
        if v is None:
            continue
        if r:
            v = pltpu.roll(v, n - r, axis=0)[0:RC]
        out = v if out is None else out + v
    return out


def _layer_norm_fwd(u1):
    mu = jnp.mean(u1, axis=-1, keepdims=True)
    xc = u1 - mu
    rstd = lax.rsqrt(jnp.mean(xc * xc, axis=-1, keepdims=True) + EPS)
    return xc * rstd, rstd


def _window_sums(x, w, back):
    n = x.shape[0]
    s = x
    k = 1
    while k < w:
        s = s + pltpu.roll(s, k if back else n - k, axis=0)
        k *= 2
    return s


def _pool_chunk(pwin, t_first):
    t = t_first + lax.broadcasted_iota(jnp.int32, (RC, 128), 0)
    outs = []
    for g, w in enumerate(WINDOWS):
        x = pwin[:, g * 128:(g + 1) * 128]
        s = _window_sums(x, w, True)
        cnt = jnp.minimum(t + 1, w).astype(F32)
        outs.append(s[HALO:HALO + RC] / cnt - x[HALO:HALO + RC])
    return outs


def _mix_fwd(z, dw, dwb, lng, lnb, pw, pb, ps, S, ex=None):
    T = z.shape[0]
    tm = 512
    ts = S // tm
    nh = tm // HALO

    def body(ca_ref, cb_ref, cg_ref, pi_ref, pg_ref, cah_ref, cbh_ref, pih_ref,
             dw_ref, dwb_ref, lng_ref, lnb_ref, pw_ref, pb_ref, ps_ref,
             ac_ref, ap_ref, u1_ref, ubuf, pbuf, pooled):
        i = pl.program_id(0)
        keep = jnp.where((i % ts) == 0, 0.0, 1.0)
        ubuf[0:HALO, :] = cah_ref[...].astype(F32) * _sig(cbh_ref[...].astype(F32)) * keep
        ubuf[HALO:HALO + tm, :] = ca_ref[...].astype(F32) * _sig(cb_ref[...].astype(F32))
        pbuf[0:HALO, :] = pih_ref[...].astype(F32) * keep
        pbuf[HALO:HALO + tm, :] = pi_ref[...].astype(F32)
        t0 = (i % ts) * tm

        def chunk(c, carry):
            base = pl.multiple_of(c * RC, RC)
            load = lambda q, n: ubuf[pl.ds(base + 8 * q, n), :]
            u1 = _stencil(load, dw_ref, 2, CONV_K + 1, lambda o: o - 2) + dwb_ref[...]
            u1_ref[pl.ds(base, RC), :] = u1
            n, _ = _layer_norm_fwd(u1)
            u2 = n * lng_ref[...] + lnb_ref[...]
            u3 = u2 * _sig(u2)
            cg = cg_ref[pl.ds(base, RC), :].astype(F32)
            ac_ref[pl.ds(base, RC), :] = (u3 * cg * _sig(cg)).astype(BF16)
            pwin = pbuf[pl.ds(base, RC + HALO), :]
            outs = _pool_chunk(pwin, t0 + base)
            for g in range(4):
                pooled[pl.ds(base, RC), g * 128:(g + 1) * 128] = outs[g].astype(BF16)
            return carry

        lax.fori_loop(0, tm // RC, chunk, 0, unroll=4)
        pg = pg_ref[...].astype(F32)
        spg = pg * _sig(pg)
        for g in range(4):
            sl = slice(g * 128, (g + 1) * 128)
            mixed = (_nn(pooled[:, sl], pw_ref[g]) + pb_ref[:, sl]) * ps_ref[:, sl]
            ap_ref[:, sl] = (mixed * spg[:, sl]).astype(BF16)

    def zmain(cb):
        return pl.BlockSpec((tm, BW), lambda i: (i, cb))

    def zprev(cb):
        return pl.BlockSpec((HALO, BW), lambda i: (jnp.maximum(i * nh - 1, 0), cb))

    full = lambda shape: pl.BlockSpec(shape, lambda i: (0,) * len(shape))
    row = pl.BlockSpec((tm, BW), lambda i: (i, 0))
    return _call(
        body, "mix_fwd", (T // tm,),
        [zmain(CB_CA), zmain(CB_CB), zmain(CB_CG), zmain(CB_PI), zmain(CB_PG),
         zprev(CB_CA), zprev(CB_CB), zprev(CB_PI),
         full((CONV_K, BW)), full((1, BW)), full((1, BW)), full((1, BW)),
         full((4, 128, 128)), full((1, BW)), full((1, BW))],
        [row, row, row], [SDS((T, BW), BF16), SDS((T, BW), BF16), SDS((T, BW), F32)],
        [pltpu.VMEM((HALO + tm, BW), F32), pltpu.VMEM((HALO + tm, BW), F32), pltpu.VMEM((tm, BW), BF16)],
        (z, z, z, z, z, z, z, z, dw, dwb, lng, lnb, pw, pb, ps), ("parallel",), ex)


def _attn_specs(nq):
    def kv(cb, off):
        return pl.BlockSpec((TQ, BW), lambda i: (i - jnp.minimum(off, i % nq), cb))
    return [pl.BlockSpec((TQ, BW), lambda i: (i, CB_Q)),
            kv(CB_K, 2), kv(CB_K, 1), kv(CB_K, 0), kv(CB_V, 2), kv(CB_V, 1), kv(CB_V, 0)]


def _softmax_rows(s):
    m = jnp.max(s, axis=-1, keepdims=True)
    e = jnp.exp(s - m)
    return e / jnp.sum(e, axis=-1, keepdims=True)


NSKEW = 1024


def _skew_table(table):
    return jnp.dot(table, jnp.asarray(_skew_select()), precision=lax.Precision.HIGHEST)


def _skew_select():
    d = np.arange(TQ + KW - 1)
    idx = np.clip(3 * TQ - 1 - d, -MAX_REL, MAX_REL) + MAX_REL
    sel = np.zeros((2 * MAX_REL + 1, NSKEW), np.float32)
    sel[idx, d] = 1.0
    return sel


def _bias_from_skew(f_ref, bias_scr):
    qi = lax.broadcasted_iota(jnp.int32, (TQ, KW), 0)
    kj = lax.broadcasted_iota(jnp.int32, (TQ, KW), 1)
    lo = (qi // CHUNK) * CHUNK
    band = jnp.where((kj >= lo) & (kj < lo + (LEFT + 1) * CHUNK), 0.0, NEG)
    for h in range(HEADS):
        rows = jnp.broadcast_to(f_ref[h:h + 1, :], (TQ, NSKEW))
        rows = pltpu.roll(rows, NSKEW - (TQ - 1), axis=1, stride=1, stride_axis=0)
        bias_scr[h] = rows[:, 0:KW] + band


def _skew_from_bias(db):
    i = lax.broadcasted_iota(jnp.int32, (TQ, TQ), 0)
    j = lax.broadcasted_iota(jnp.int32, (TQ, TQ), 1)
    flip = jnp.where(i + j == TQ - 1, 1.0, 0.0).astype(BF16)
    hi = db.astype(BF16)
    lo = (db - hi.astype(F32)).astype(BF16)
    rev = _nn(flip, hi) + _nn(flip, lo)
    rev = jnp.concatenate([rev, jnp.zeros((TQ, NSKEW - KW), F32)], axis=1)
    return jnp.sum(pltpu.roll(rev, 0, axis=1, stride=1, stride_axis=0), axis=0, keepdims=True)


def _attn_fwd(z, f, S, ex=None):
    T = z.shape[0]
    nq = S // TQ

    def body(q_ref, k2_ref, k1_ref, k0_ref, v2_ref, v1_ref, v0_ref, f_ref, o_ref, kbuf, vbuf, b_scr):
        @pl.when(pl.program_id(0) == 0)
        def _():
            _bias_from_skew(f_ref, b_scr)

        qb = pl.program_id(0) % nq
        kbuf[0:TQ, :] = k2_ref[...]
        kbuf[TQ:2 * TQ, :] = k1_ref[...]
        kbuf[2 * TQ:KW, :] = k0_ref[...]
        vbuf[0:TQ, :] = v2_ref[...]
        vbuf[TQ:2 * TQ, :] = v1_ref[...]
        vbuf[2 * TQ:KW, :] = v0_ref[...]
        lane = lax.broadcasted_iota(jnp.int32, (1, 128), 1)

        def attend(lo):
            def scores(h):
                sl = slice((h // 2) * 128, (h // 2 + 1) * 128)
                qp = q_ref[:, sl] * 0.125
                qm = jnp.where((lane < HD) if h % 2 == 0 else (lane >= HD), qp, jnp.zeros_like(qp))
                return _nt(qm, kbuf[lo:KW, sl]) + b_scr[h, :, lo:KW]

            s = scores(0)
            acc = None
            for h in range(HEADS):
                s_next = scores(h + 1) if h + 1 < HEADS else None
                sl = slice((h // 2) * 128, (h // 2 + 1) * 128)
                e = jnp.exp(s - jnp.max(s, axis=-1, keepdims=True))
                vp = vbuf[lo:KW, sl]
                vm = jnp.where((lane < HD) if h % 2 == 0 else (lane >= HD), vp, jnp.zeros_like(vp))
                o = _nn(e.astype(BF16), vm) * (1.0 / jnp.sum(e, axis=-1, keepdims=True))
                acc = o if h % 2 == 0 else acc + o
                if h % 2 == 1:
                    o_ref[:, sl] = acc.astype(BF16)
                s = s_next

        for nblk in (1, 2, 3):
            pl.when(jnp.minimum(qb, 2) == nblk - 1)(functools.partial(attend, (3 - nblk) * TQ))

    full = lambda shape: pl.BlockSpec(shape, lambda i: (0,) * len(shape))
    return _call(
        body, "attn_fwd", (T // TQ,),
        _attn_specs(nq) + [full((HEADS, NSKEW))],
        [pl.BlockSpec((TQ, BW), lambda i: (i, 0))], [SDS((T, BW), BF16)],
        [pltpu.VMEM((KW, BW), BF16), pltpu.VMEM((KW, BW), BF16), pltpu.VMEM((HEADS, TQ, KW), F32)],
        (z, z, z, z, z, z, z, f), ("arbitrary",), ex)


def _gates(gl_ref, gh_ref):
    gl = _sig(gl_ref[...].astype(F32))
    gh = _sig(gh_ref[...].astype(F32))
    return (gl[:, 0:D], jnp.concatenate([gl[:, D:1536], gh[:, 0:512]], axis=1), gh[:, 512:1536])


def _out_specs_in(tm):
    row = lambda w: pl.BlockSpec((tm, w), lambda i: (i, 0))
    full = lambda shape: pl.BlockSpec(shape, lambda i: (0,) * len(shape))
    return [row(BW), row(BW), row(BW),
            pl.BlockSpec((tm, BW), lambda i: (i, CB_AG)),
            pl.BlockSpec((tm, 1536), lambda i: (i, 3)),
            pl.BlockSpec((tm, 1536), lambda i: (i, 4)),
            full((BW, D)), full((BW, D)), full((BW, D)), full((D, D)), full((1, D))]


def _out_fwd(x, ac, o, ap, z, wco, wao, wpo, wout, postg, ex=None, tgt=None):
    T = x.shape[0]
    tm = 512
    last = tgt is not None

    def body(ac_ref, o_ref, ap_ref, ag_ref, gl_ref, gh_ref, wco_ref, wao_ref, wpo_ref, wout_ref, pg_ref,
             x_ref, *rest):
        ag = ag_ref[...].astype(F32)
        aat = (o_ref[...].astype(F32) * ag * _sig(ag)).astype(BF16)
        g0, g1, g2 = _gates(gl_ref, gh_ref)
        merged = g0 * _nn(ac_ref[...], wco_ref[...])
        merged = merged + g1 * _nn(aat, wao_ref[...])
        merged = merged + g2 * _nn(ap_ref[...], wpo_ref[...])
        y = _nn(merged.astype(BF16), wout_ref[...])
        ry = lax.rsqrt(jnp.mean(y * y, axis=-1, keepdims=True) + EPS)
        out = x_ref[...] + y * ry * pg_ref[...]
        if not last:
            rest[0][...] = out
            return
        t_ref, d_ref, l_ref = rest

        @pl.when(pl.program_id(0) == 0)
        def _():
            l_ref[...] = jnp.zeros_like(l_ref)
        d = out - t_ref[...]
        d_ref[...] = d * (1.0 / D)
        l_ref[...] += jnp.sum(jnp.sum(d * d, axis=0, keepdims=True), axis=1, keepdims=True)

    row = pl.BlockSpec((tm, D), lambda i: (i, 0))
    if not last:
        return _call(body, "out_fwd", (T // tm,), _out_specs_in(tm) + [row], [row], [SDS((T, D), F32)], [],
                     (ac, o, ap, z, z, z, wco, wao, wpo, wout, postg, x), ("parallel",), ex)
    return _call(body, "out_fwd_loss", (T // tm,), _out_specs_in(tm) + [row, row],
                 [row, pl.BlockSpec((1, 128), lambda i: (0, 0))], [SDS((T, D), F32), SDS((1, 128), F32)], [],
                 (ac, o, ap, z, z, z, wco, wao, wpo, wout, postg, x, tgt), ("arbitrary",), ex)


def _out_bwd(dout, ac, o, ap, z, wco, wao, wpo, wout, postg, ex=None):
    T = dout.shape[0]
    tm = 256

    def body(ac_ref, o_ref, ap_ref, ag_ref, gl_ref, gh_ref, wco_ref, wao_ref, wpo_ref, wout_ref, pg_ref, do_ref,
             dac_ref, dao_ref, dag_ref, dap_ref, dgm_ref, dwco_ref, dwao_ref, dwpo_ref, dwout_ref, dpg_ref):
        @pl.when(pl.program_id(0) == 0)
        def _():
            for r in (dwco_ref, dwao_ref, dwpo_ref, dwout_ref, dpg_ref):
                r[...] = jnp.zeros_like(r)

        ag = ag_ref[...].astype(F32)
        sag = _sig(ag)
        ov = o_ref[...].astype(F32)
        acts = (ac_ref[...], (ov * ag * sag).astype(BF16), ap_ref[...])
        ws = (wco_ref, wao_ref, wpo_ref)
        gates = _gates(gl_ref, gh_ref)
        ys = [_nn(acts[b], ws[b][...]) for b in range(3)]
        merged = (gates[0] * ys[0] + gates[1] * ys[1] + gates[2] * ys[2]).astype(BF16)
        y = _nn(merged, wout_ref[...])
        ry = lax.rsqrt(jnp.mean(y * y, axis=-1, keepdims=True) + EPS)
        yn = y * ry
        dout_v = do_ref[...]
        dpg_ref[...] += jnp.sum(dout_v * yn, axis=0, keepdims=True)
        dyn = dout_v * pg_ref[...]
        dy = (ry * (dyn - yn * jnp.mean(dyn * yn, axis=-1, keepdims=True))).astype(BF16)
        dmerged = _nt(dy, wout_ref[...])
        dwout_ref[...] += _tn(merged, dy)
        dws = (dwco_ref, dwao_ref, dwpo_ref)
        das = []
        for b in range(3):
            gb = gates[b]
            dgm_ref[:, b * D:(b + 1) * D] = (dmerged * ys[b] * gb * (1.0 - gb)).astype(BF16)
            dyb = (dmerged * gb).astype(BF16)
            dws[b][...] += _tn(acts[b], dyb)
            das.append(_nt(dyb, ws[b][...]))
        dac_ref[...] = das[0].astype(BF16)
        dap_ref[...] = das[2].astype(BF16)
        dao_ref[...] = (das[1] * ag * sag).astype(BF16)
        dag_ref[...] = (das[1] * ov * _dsilu(ag, sag)).astype(BF16)

    row = lambda w: pl.BlockSpec((tm, w), lambda i: (i, 0))
    full = lambda shape: pl.BlockSpec(shape, lambda i: (0,) * len(shape))
    return _call(
        body, "out_bwd", (T // tm,), _out_specs_in(tm) + [row(D)],
        [row(BW), row(BW), row(BW), row(BW), row(3 * D),
         full((BW, D)), full((BW, D)), full((BW, D)), full((D, D)), full((1, D))],
        [SDS((T, BW), BF16)] * 4 + [SDS((T, 3 * D), BF16)]
        + [SDS((BW, D), F32)] * 3 + [SDS((D, D), F32), SDS((1, D), F32)], [],
        (ac, o, ap, z, z, z, wco, wao, wpo, wout, postg, dout), ("arbitrary",), ex)


def _attn_bwd(z, dao, f, S, ex=None):
    T = z.shape[0]
    nq = S // TQ
    nsteps = T // TQ

    def body(q_ref, k2_ref, k1_ref, k0_ref, v2_ref, v1_ref, v0_ref, do_ref, f_ref,
             dq_ref, dk_ref, dv_ref, df_ref, kbuf, vbuf, dkacc, dvacc, b_scr, db_scr):
        i = pl.program_id(0)
        qb = i % nq

        @pl.when(i == 0)
        def _():
            _bias_from_skew(f_ref, b_scr)
            db_scr[...] = jnp.zeros_like(db_scr)

        @pl.when(qb == 0)
        def _():
            dkacc[...] = jnp.zeros_like(dkacc)
            dvacc[...] = jnp.zeros_like(dvacc)

        kbuf[0:TQ, :] = k2_ref[...]
        kbuf[TQ:2 * TQ, :] = k1_ref[...]
        kbuf[2 * TQ:KW, :] = k0_ref[...]
        vbuf[0:TQ, :] = v2_ref[...]
        vbuf[TQ:2 * TQ, :] = v1_ref[...]
        vbuf[2 * TQ:KW, :] = v0_ref[...]
        lane = lax.broadcasted_iota(jnp.int32, (1, 128), 1)
        row0 = pl.multiple_of(qb * TQ, TQ)

        def attend(lo):
            def first_matmuls(h):
                sl = slice((h // 2) * 128, (h // 2 + 1) * 128)
                msk = (lane < HD) if h % 2 == 0 else (lane >= HD)
                qp = q_ref[:, sl] * 0.125
                dop = do_ref[:, sl]
                qm = jnp.where(msk, qp, jnp.zeros_like(qp))
                dom = jnp.where(msk, dop, jnp.zeros_like(dop))
                s = _nt(qm, kbuf[lo:KW, sl]) + b_scr[h, :, lo:KW]
                return s, _nt(dom, vbuf[lo:KW, sl]), qm, dom

            cur = first_matmuls(0)
            dq_acc = dk_acc = dv_acc = None
            for h in range(HEADS):
                nxt = first_matmuls(h + 1) if h + 1 < HEADS else None
                s, dp, qm, dom = cur
                sl = slice((h // 2) * 128, (h // 2 + 1) * 128)
                e = jnp.exp(s - jnp.max(s, axis=-1, keepdims=True))
                p = e * (1.0 / jnp.sum(e, axis=-1, keepdims=True))
                ds = p * (dp - jnp.sum(p * dp, axis=-1, keepdims=True))
                db_scr[h, :, lo:KW] += ds
                dsb = ds.astype(BF16)
                kp = kbuf[lo:KW, sl]
                km = jnp.where((lane < HD) if h % 2 == 0 else (lane >= HD), kp, jnp.zeros_like(kp))
                dq_h = _nn(dsb, km) * 0.125
                dk_h = _tn(dsb, qm)
                dv_h = _tn(p.astype(BF16), dom)
                if h % 2 == 0:
                    dq_acc, dk_acc, dv_acc = dq_h, dk_h, dv_h
                else:
                    dq_ref[:, sl] = (dq_acc + dq_h).astype(BF16)
                    dkacc[pl.ds(row0 + lo, KW - lo), sl] += dk_acc + dk_h
                    dvacc[pl.ds(row0 + lo, KW - lo), sl] += dv_acc + dv_h
                cur = nxt

        for nblk in (1, 2, 3):
            pl.when(jnp.minimum(qb, 2) == nblk - 1)(functools.partial(attend, (3 - nblk) * TQ))

        @pl.when(qb == nq - 1)
        def _():
            dk_ref[...] = dkacc[2 * TQ:2 * TQ + S, :].astype(BF16)
            dv_ref[...] = dvacc[2 * TQ:2 * TQ + S, :].astype(BF16)

        @pl.when(i == nsteps - 1)
        def _():
            for h in range(HEADS):
                df_ref[h:h + 1, :] = _skew_from_bias(db_scr[h])

    full = lambda shape: pl.BlockSpec(shape, lambda i: (0,) * len(shape))
    return _call(
        body, "attn_bwd", (nsteps,),
        _attn_specs(nq) + [pl.BlockSpec((TQ, BW), lambda i: (i, 0)), full((HEADS, NSKEW))],
        [pl.BlockSpec((TQ, BW), lambda i: (i, 0)), pl.BlockSpec((S, BW), lambda i: (i // nq, 0)),
         pl.BlockSpec((S, BW), lambda i: (i // nq, 0)), full((HEADS, NSKEW))],
        [SDS((T, BW), BF16)] * 3 + [SDS((HEADS, NSKEW), F32)],
        [pltpu.VMEM((KW, BW), BF16), pltpu.VMEM((KW, BW), BF16),
         pltpu.VMEM((S + 2 * TQ, BW), F32), pltpu.VMEM((S + 2 * TQ, BW), F32),
         pltpu.VMEM((HEADS, TQ, KW), F32), pltpu.VMEM((HEADS, TQ, KW), F32)],
        (z, z, z, z, z, z, z, dao, f), ("arbitrary",), ex)


def _mix_bwd(z, u1, dac, dap, dw, dwb, lng, lnb, pw, pb, ps, S):
    T = z.shape[0]
    tm = 512
    ts = S // tm
    nh = tm // HALO
    nsteps = T // tm
    nblk32 = T // HALO

    def body(ca_ref, cb_ref, cg_ref, pi_ref, pg_ref, u1_ref, dac_ref, dap_ref,
             cah_ref, cbh_ref, pih_ref,
             cgn_ref, pgn_ref, u1n_ref, dacn_ref, dapn_ref,
             dw_ref, dwb_ref, lng_ref, lnb_ref, pw_ref, pb_ref, ps_ref,
             dzc_ref, dzp_ref, ddw_ref, ddwb_ref, dlng_ref, dlnb_ref, dpw_ref, dpb_ref, dps_ref,
             ubuf, gbuf, pbuf, qbuf, pooled, *accs):
        tap_acc, (lng_acc, lnb_acc, dwb_acc) = accs[:CONV_K], accs[CONV_K:]
        i = pl.program_id(0)
        keep_prev = jnp.where((i % ts) == 0, 0.0, 1.0)
        keep_next = jnp.where((i % ts) == ts - 1, 0.0, 1.0)
        t0 = (i % ts) * tm

        @pl.when(i == 0)
        def _():
            for a in accs:
                a[...] = jnp.zeros_like(a)
            dpw_ref[...] = jnp.zeros_like(dpw_ref)
            dpb_ref[...] = jnp.zeros_like(dpb_ref)
            dps_ref[...] = jnp.zeros_like(dps_ref)

        ubuf[0:HALO, :] = cah_ref[...].astype(F32) * _sig(cbh_ref[...].astype(F32)) * keep_prev
        ubuf[HALO:HALO + tm, :] = ca_ref[...].astype(F32) * _sig(cb_ref[...].astype(F32))
        pbuf[0:HALO, :] = pih_ref[...].astype(F32) * keep_prev
        pbuf[HALO:HALO + tm, :] = pi_ref[...].astype(F32)

        def norm_back(u1v, cg, dacv):
            n, rstd = _layer_norm_fwd(u1v)
            u2 = n * lng_ref[...] + lnb_ref[...]
            s2 = _sig(u2)
            scg = _sig(cg)
            du2 = dacv * cg * scg * _dsilu(u2, s2)
            dn = du2 * lng_ref[...]
            du1 = rstd * (dn - jnp.mean(dn, axis=-1, keepdims=True)
                          - n * jnp.mean(dn * n, axis=-1, keepdims=True))
            return du1, du2, n, dacv * u2 * s2 * _dsilu(cg, scg)

        def chunk_a(c, carry):
            base = pl.multiple_of(c * RC, RC)
            du1, du2, n, dcg = norm_back(u1_ref[pl.ds(base, RC), :], cg_ref[pl.ds(base, RC), :].astype(F32),
                                         dac_ref[pl.ds(base, RC), :].astype(F32))
            gbuf[pl.ds(base, RC), :] = du1
            dzc_ref[pl.ds(base, RC), 2 * BW:3 * BW] = dcg.astype(BF16)
            lng_acc[...] += _rows8(du2 * n)
            lnb_acc[...] += _rows8(du2)
            dwb_acc[...] += _rows8(du1)
            padded = jnp.concatenate([du1, jnp.zeros((8, BW), F32)], axis=0)
            for r in range(8):
                nrow = RC if r == 0 else RC + 8
                g = du1 if r == 0 else pltpu.roll(padded, r, axis=0)
                for q in range((CONV_K + 1 - r) // 8 + 1):
                    o = 8 * q + r
                    if o < 2:
                        continue
                    prod = g * ubuf[pl.ds(base + 8 * q, nrow), :]
                    red = prod[0:8]
                    for k in range(1, nrow // 8):
                        red = red + prod[8 * k:8 * k + 8]
                    tap_acc[o - 2][...] += red
            return carry

        lax.fori_loop(0, tm // RC, chunk_a, 0, unroll=4)
        du1n, _, _, _ = norm_back(u1n_ref[...], cgn_ref[...].astype(F32), dacn_ref[...].astype(F32))
        gbuf[tm:tm + HALO, :] = du1n * keep_next

        def chunk_p(c, carry):
            base = pl.multiple_of(c * RC, RC)
            outs = _pool_chunk(pbuf[pl.ds(base, RC + HALO), :], t0 + base)
            for g in range(4):
                pooled[pl.ds(base, RC), g * 128:(g + 1) * 128] = outs[g].astype(BF16)
            return carry

        lax.fori_loop(0, tm // RC, chunk_p, 0)

        def cnt_of(t_first, rows, w):
            t = t_first + lax.broadcasted_iota(jnp.int32, (rows, 128), 0)
            return jnp.minimum(t + 1, w).astype(F32)

        pg = pg_ref[...].astype(F32)
        spg_s = _sig(pg)
        dapv = dap_ref[...].astype(F32)
        pgn = pgn_ref[...].astype(F32)
        dmixn = dapn_ref[...].astype(F32) * pgn * _sig(pgn) * ps_ref[...] * keep_next
        for g, w in enumerate(WINDOWS):
            sl = slice(g * 128, (g + 1) * 128)
            mixed_u = _nn(pooled[:, sl], pw_ref[g]) + pb_ref[:, sl]
            dap_g = dapv[:, sl]
            pg_g = pg[:, sl]
            s_g = spg_s[:, sl]
            silu_g = pg_g * s_g
            dps_ref[:, sl] += jnp.sum(dap_g * silu_g * mixed_u, axis=0, keepdims=True)
            dzp_ref[:, BW + g * 128:BW + (g + 1) * 128] = (
                dap_g * mixed_u * ps_ref[:, sl] * _dsilu(pg_g, s_g)).astype(BF16)
            dmix = dap_g * silu_g * ps_ref[:, sl]
            dpb_ref[:, sl] += jnp.sum(dmix, axis=0, keepdims=True)
            dmixb = dmix.astype(BF16)
            dpw_ref[g] += _tn(pooled[:, sl], dmixb)
            qbuf[0:tm, sl] = _nt(dmixb, pw_ref[g]) / cnt_of(t0, tm, w)
            qbuf[tm:tm + HALO, sl] = _nt(dmixn[:, sl].astype(BF16), pw_ref[g]) / cnt_of(t0 + tm, HALO, w)

        def chunk_b(c, carry):
            base = pl.multiple_of(c * RC, RC)
            load = lambda q, n: gbuf[pl.ds(base + 8 * q, n), :]
            du0 = _stencil(load, dw_ref, 0, CONV_K - 1, lambda o: CONV_K - 1 - o)
            ca = ca_ref[pl.ds(base, RC), :].astype(F32)
            sb = _sig(cb_ref[pl.ds(base, RC), :].astype(F32))
            dzc_ref[pl.ds(base, RC), 0:BW] = (du0 * sb).astype(BF16)
            dzc_ref[pl.ds(base, RC), BW:2 * BW] = (du0 * ca * sb * (1.0 - sb)).astype(BF16)
            qwin = qbuf[pl.ds(base, RC + HALO), :]
            t = t0 + base + lax.broadcasted_iota(jnp.int32, (RC, 128), 0)
            for g, w in enumerate(WINDOWS):
                x = qwin[:, g * 128:(g + 1) * 128]
                s = _window_sums(x, w, False)
                cnt = jnp.minimum(t + 1, w).astype(F32)
                dzp_ref[pl.ds(base, RC), g * 128:(g + 1) * 128] = (s[0:RC] - cnt * x[0:RC]).astype(BF16)
            return carry

        lax.fori_loop(0, tm // RC, chunk_b, 0)

        @pl.when(i == nsteps - 1)
        def _():
            dlng_ref[...] = jnp.sum(lng_acc[...], axis=0, keepdims=True)
            dlnb_ref[...] = jnp.sum(lnb_acc[...], axis=0, keepdims=True)
            ddwb_ref[...] = jnp.sum(dwb_acc[...], axis=0, keepdims=True)
            for j in range(CONV_K):
                ddw_ref[j:j + 1, :] = jnp.sum(tap_acc[j][...], axis=0, keepdims=True)

    def zmain(cb):
        return pl.BlockSpec((tm, BW), lambda i: (i, cb))

    def zprev(cb):
        return pl.BlockSpec((HALO, BW), lambda i: (jnp.maximum(i * nh - 1, 0), cb))

    def znext(cb):
        return pl.BlockSpec((HALO, BW), lambda i: (jnp.minimum((i + 1) * nh, nblk32 - 1), cb))

    row = lambda w: pl.BlockSpec((tm, w), lambda i: (i, 0))
    full = lambda shape: pl.BlockSpec(shape, lambda i: (0,) * len(shape))
    return pl.pallas_call(
        body, name="mix_bwd", grid=(nsteps,),
        in_specs=[zmain(CB_CA), zmain(CB_CB), zmain(CB_CG), zmain(CB_PI), zmain(CB_PG), row(BW), row(BW), row(BW),
                  zprev(CB_CA), zprev(CB_CB), zprev(CB_PI),
                  znext(CB_CG), znext(CB_PG), znext(0), znext(0), znext(0),
                  full((CONV_K, BW)), full((1, BW)), full((1, BW)), full((1, BW)),
                  full((4, 128, 128)), full((1, BW)), full((1, BW))],
        out_specs=[row(3 * BW), row(2 * BW), full((CONV_K, BW)), full((1, BW)), full((1, BW)), full((1, BW)),
                   full((4, 128, 128)), full((1, BW)), full((1, BW))],
        out_shape=[SDS((T, 3 * BW), BF16), SDS((T, 2 * BW), BF16), SDS((CONV_K, BW), F32),
                   SDS((1, BW), F32), SDS((1, BW), F32), SDS((1, BW), F32),
                   SDS((4, 128, 128), F32), SDS((1, BW), F32), SDS((1, BW), F32)],
        scratch_shapes=[pltpu.VMEM((HALO + tm, BW), F32), pltpu.VMEM((tm + HALO, BW), F32),
                        pltpu.VMEM((HALO + tm, BW), F32), pltpu.VMEM((tm + HALO, BW), F32),
                        pltpu.VMEM((tm, BW), BF16)] + [pltpu.VMEM((8, BW), F32)] * (CONV_K + 3),
        compiler_params=_cparams(("arbitrary",)),
    )(z, z, z, z, z, u1, dac, dap, z, z, z, z, z, u1, dac, dap, dw, dwb, lng, lnb, pw, pb, ps)


def _in_bwd_x(pieces, wt, x, g, dout, ex=None):
    T = x.shape[0]
    tm = 256
    widths = [p.shape[1] for p in pieces]
    offs = np.cumsum([0] + widths)
    npc = len(pieces)

    def body(*refs):
        p_refs = refs[:npc]
        w_ref, x_ref, g_ref, do_ref, dx_ref, dg_ref = refs[npc:]

        @pl.when(pl.program_id(0) == 0)
        def _():
            dg_ref[...] = jnp.zeros_like(dg_ref)

        dh = None
        for k in range(npc):
            t = _nn(p_refs[k][...], w_ref[int(offs[k]):int(offs[k + 1]), :])
            dh = t if dh is None else dh + t
        xv = x_ref[...]
        r = lax.rsqrt(jnp.mean(xv * xv, axis=-1, keepdims=True) + EPS)
        xn = xv * r
        dg_ref[...] += jnp.sum(dh * xn, axis=0, keepdims=True)
        dxn = dh * g_ref[...]
        dx_ref[...] = do_ref[...] + r * (dxn - xn * jnp.mean(dxn * xn, axis=-1, keepdims=True))

    row = lambda wd: pl.BlockSpec((tm, wd), lambda i: (i, 0))
    return _call(
        body, "in_bwd_x", (T // tm,),
        [row(wd) for wd in widths] + [pl.BlockSpec((NCOL, D), lambda i: (0, 0)),
                                      row(D), pl.BlockSpec((1, D), lambda i: (0, 0)), row(D)],
        [row(D), pl.BlockSpec((1, D), lambda i: (0, 0))], [SDS((T, D), F32), SDS((1, D), F32)], [],
        (*pieces, wt, x, g, dout), ("arbitrary",), ex)


def _in_bwd_w(ht, piece, row0, buf=None, ex=None):
    T = ht.shape[1]
    wd = piece.shape[1]
    tn = next(t for t in (1024, 768, 512) if wd % t == 0 and row0 % t == 0)
    tk = min(T, 2048)
    nk = T // tk
    j0 = row0 // tn

    def body(ht_ref, p_ref, *rest):
        o_ref, acc = rest[-2:]
        k = pl.program_id(1)

        @pl.when(k == 0)
        def _():
            acc[...] = jnp.zeros_like(acc)
        acc[...] += _nn(ht_ref[...], p_ref[...])

        @pl.when(k == nk - 1)
        def _():
            o_ref[...] = acc[...].T.astype(BF16)

    in_specs = [pl.BlockSpec((D, tk), lambda j, k: (0, k)), pl.BlockSpec((tk, tn), lambda j, k: (k, j))]
    args = (ht, piece)
    if buf is not None:
        in_specs.append(pl.BlockSpec(memory_space=pl.ANY))
        args += (buf,)
    return _call(
        body, "in_bwd_w", (wd // tn, nk), in_specs,
        [pl.BlockSpec((tn, D), lambda j, k: (j + j0, 0))], [SDS((NCOL, D), BF16)], [pltpu.VMEM((D, tn), F32)],
        args, ("parallel", "arbitrary"), ex, None if buf is None else {2: 0})


def _my_id():
    return 4 * lax.axis_index("x") + 2 * lax.axis_index("y") + lax.axis_index("c")


def _peers():
    x, y, c = lax.axis_index("x"), lax.axis_index("y"), lax.axis_index("c")
    out = []
    for k in range(1, N_DEV):
        fx, fy, fc = (k >> 2) & 1, (k >> 1) & 1, k & 1
        px, py, pc = x ^ fx, y ^ fy, c ^ fc
        out.append(((px, py, pc), 4 * px + 2 * py + pc))
    return out


class _Exchange:
    def __init__(self, arrays, scatter):
        self.arrays = list(arrays)
        self.scatter = list(scatter)
        self.n = n = len(arrays)
        hbm = pl.BlockSpec(memory_space=pltpu.HBM)
        self.in_specs = [hbm] * n
        self.out_specs = [hbm] * n
        self.out_shape = [SDS((N_DEV,) + tuple(a.shape[1:] if s else a.shape), a.dtype)
                          for a, s in zip(arrays, scatter)]
        self.scratch = [pltpu.SemaphoreType.DMA((N_DEV - 1, n)), pltpu.SemaphoreType.DMA((N_DEV - 1, n)),
                        pltpu.SemaphoreType.DMA((n,))]

    def split(self, refs, n_in, n_out):
        n = self.n
        own_in = refs[:n_in]
        ex_in = refs[n_in:n_in + n]
        own_out = refs[n_in + n:n_in + n + n_out]
        ex_out = refs[n_in + n + n_out:n_in + 2 * n + n_out]
        rest = refs[n_in + 2 * n + n_out:]
        return own_in, own_out, rest[:-3], (ex_in, ex_out, rest[-3:])

    def _copy(self, ex, k, p, landing):
        in_refs, out_refs, (send_sems, recv_sems, _) = ex
        pos, pid = _peers()[p]
        return pltpu.make_async_remote_copy(
            src_ref=in_refs[k].at[pid] if self.scatter[k] else in_refs[k],
            dst_ref=out_refs[k].at[pid if landing else _my_id()],
            send_sem=send_sems.at[p, k], recv_sem=recv_sems.at[p, k],
            device_id=pos, device_id_type=pl.DeviceIdType.MESH)

    def _own(self, ex, k):
        in_refs, out_refs, (_, _, local_sems) = ex
        me = _my_id()
        return pltpu.make_async_copy(in_refs[k].at[me] if self.scatter[k] else in_refs[k], out_refs[k].at[me],
                                     local_sems.at[k])

    def start(self, ex):
        for k in range(self.n):
            self._own(ex, k).start()
        for p in range(N_DEV - 1):
            for k in range(self.n):
                self._copy(ex, k, p, False).start()

    def finish(self, ex):
        for p in range(N_DEV - 1):
            for k in range(self.n):
                self._copy(ex, k, p, True).wait_recv()
        for p in range(N_DEV - 1):
            for k in range(self.n):
                self._copy(ex, k, p, False).wait_send()
        for k in range(self.n):
            self._own(ex, k).wait()


def _exchange(arrays, scatter, name):
    ex = _Exchange(arrays, scatter)

    def body(*refs):
        _, _, _, exr = ex.split(refs, 0, 0)
        ex.start(exr)
        ex.finish(exr)

    return pl.pallas_call(body, name=name, in_specs=ex.in_specs, out_specs=ex.out_specs,
                          out_shape=ex.out_shape, scratch_shapes=ex.scratch)(*ex.arrays)


def _gather_two_level(shard, name):
    def body(x_ref, out_ref, send_sems, recv_sems, local_sem):
        x, y, c = lax.axis_index("x"), lax.axis_index("y"), lax.axis_index("c")
        me, sibling = (x, y, c), (x, y, 1 - c)
        chips = [(1 - x, y), (x, 1 - y), (1 - x, 1 - y)]
        slab = lambda px, py, pc: out_ref.at[4 * px + 2 * py + pc]

        def copy(k, block, to, src=None):
            return pltpu.make_async_remote_copy(
                src_ref=slab(*block) if src is None else src, dst_ref=slab(*block),
                send_sem=send_sems.at[k], recv_sem=recv_sems.at[k],
                device_id=to, device_id_type=pl.DeviceIdType.MESH)

        mine = pltpu.make_async_copy(x_ref, slab(*me), local_sem)
        mine.start()
        first = [copy(0, me, sibling, src=x_ref)] + [copy(1 + j, me, (*chip, c), src=x_ref)
                                                     for j, chip in enumerate(chips)]
        for cp in first:
            cp.start()
        passed = [copy(4 + j, (*chip, c), sibling) for j, chip in enumerate(chips)]
        for j, chip in enumerate(chips):
            copy(1 + j, (*chip, c), me).wait_recv()
            passed[j].start()
        copy(0, sibling, me).wait_recv()
        for j, chip in enumerate(chips):
            copy(4 + j, (*chip, 1 - c), me).wait_recv()
        for cp in first + passed:
            cp.wait_send()
        mine.wait()

    hbm = pl.BlockSpec(memory_space=pltpu.HBM)
    return pl.pallas_call(
        body, name=name, in_specs=[hbm], out_specs=hbm,
        out_shape=SDS((N_DEV,) + shard.shape, shard.dtype),
        scratch_shapes=[pltpu.SemaphoreType.DMA((N_DEV - 1,)), pltpu.SemaphoreType.DMA((N_DEV - 1,)),
                        pltpu.SemaphoreType.DMA],
    )(shard)


def _adamw_update(g, w, m, v):
    c1 = 1.0 / (1.0 - ADAM_B1 ** ADAM_STEP)
    c2 = 1.0 / (1.0 - ADAM_B2 ** ADAM_STEP)
    mn = ADAM_B1 * m + (1.0 - ADAM_B1) * g
    vn = ADAM_B2 * v + (1.0 - ADAM_B2) * (g * g)
    return -ADAM_LR * ((mn * c1) / (jnp.sqrt(vn * c2) + ADAM_EPS) + ADAM_WD * w), mn, vn


def _adamw_small(parts, w, m, v):
    n = len(w)

    def body(*refs):
        p_refs = (refs[0:n], refs[n:2 * n])
        w_refs, m_refs, v_refs = refs[2 * n:3 * n], refs[3 * n:4 * n], refs[4 * n:5 * n]
        outs = refs[5 * n:]
        for k in range(n):
            g_ref, d_ref, mo_ref, vo_ref = outs[4 * k:4 * k + 4]
            for l in range(2):
                at = (slice(l, l + 1),) if len(w_refs[k].shape) == 2 else (l,)
                g = p_refs[l][k][0]
                for s in range(1, N_DEV):
                    g = g + p_refs[l][k][s]
                delta, mn, vn = _adamw_update(g, w_refs[k][at], m_refs[k][at], v_refs[k][at])
                g_ref[at] = g
                d_ref[at] = delta
                mo_ref[at] = mn
                vo_ref[at] = vn

    vmem = pl.BlockSpec(memory_space=pltpu.VMEM)
    res = pl.pallas_call(
        body, name="adamw_replicated", in_specs=[vmem] * (5 * n), out_specs=[vmem] * (4 * n),
        out_shape=[SDS(a.shape, F32) for a in w for _ in range(4)],
        compiler_params=pltpu.CompilerParams(vmem_limit_bytes=VMEM_LIMIT),
    )(*parts[0], *parts[1], *w, *m, *v)
    return [res[4 * k:4 * k + 4] for k in range(n)]


def _adamw_sum(parts0, parts1, w, m, v, name):
    _, R, C = w.shape
    tr = R
    while tr * C > 256 * 1024 and tr % 32 == 0:
        tr //= 2

    def body(p0_ref, p1_ref, w_ref, m_ref, v_ref, g_ref, d_ref, mo_ref, vo_ref):
        def update(p_ref):
            g = p_ref[0].astype(F32)
            for s in range(1, N_DEV):
                g = g + p_ref[s].astype(F32)
            g_ref[...] = g
            d_ref[...], mo_ref[...], vo_ref[...] = _adamw_update(g, w_ref[...], m_ref[...], v_ref[...])

        @pl.when(pl.program_id(0) == 0)
        def _():
            update(p0_ref)

        @pl.when(pl.program_id(0) == 1)
        def _():
            update(p1_ref)

    blk = pl.BlockSpec((None, tr, C), lambda l, i: (l, i, 0))
    return pl.pallas_call(
        body, name=name, grid=(2, R // tr),
        in_specs=[pl.BlockSpec((N_DEV, tr, C), lambda l, i: (0, i * (1 - l), 0)),
                  pl.BlockSpec((N_DEV, tr, C), lambda l, i: (0, i * l, 0)), blk, blk, blk],
        out_specs=[blk, blk, blk, blk],
        out_shape=[SDS((2, R, C), F32)] * 4,
        compiler_params=_cparams(("arbitrary", "arbitrary")),
    )(parts0, parts1, w, m, v)


def _layer_fwd(x, P, skew, S, rest, ex, tgt=None):
    z, ht, *got0 = _in_proj(x, P["pre_g"], P["w_in_t"], ex[0])
    P = {**P, **rest(got0)}
    ac, ap, u1, *got1 = _mix_fwd(z, P["conv_dw"], P["conv_dw_b"], P["conv_ln_g"], P["conv_ln_b"],
                                 P["pool_w"], P["pool_b"], P["pool_scale"], S, ex[1])
    o, *got2 = _attn_fwd(z, skew, S, ex[2])
    out, *got3 = _out_fwd(x, ac, o, ap, z, P["w_conv_out"], P["w_attn_out"], P["w_pool_out"], P["w_out"],
                          P["post_g"], ex[3], tgt)
    return out, (x, z, ht, ac, o, ap, u1), P, (got0, got1, got2, got3)


def _layer_bwd(dout, saved, P, skew, S, ex, late_ex):
    x, z, ht, ac, o, ap, u1 = saved
    (dac, dao, dag, dap, dgm, dwco, dwao, dwpo, dwout, dpostg, *got0) = _out_bwd(
        dout, ac, o, ap, z, P["w_conv_out"], P["w_attn_out"], P["w_pool_out"], P["w_out"], P["post_g"], ex[0])
    dq, dk, dv, dskew, *got1 = _attn_bwd(z, dao, skew, S, ex[1])
    (dzc, dzp, ddw, ddwb, dlng, dlnb, dpw, dpb, dps) = _mix_bwd(
        z, u1, dac, dap, P["conv_dw"], P["conv_dw_b"], P["conv_ln_g"], P["conv_ln_b"],
        P["pool_w"], P["pool_b"], P["pool_scale"], S)
    grads = dict(post_norm_g=dpostg, conv_dw=ddw, conv_dw_b=ddwb, conv_ln_g=dlng, conv_ln_b=dlnb,
                 w_conv_out=dwco, dskew=dskew, w_attn_out=dwao, pool_w=dpw, pool_b=dpb, pool_scale=dps,
                 w_pool_out=dwpo, w_out=dwout)
    pieces = [dzc, dq, dk, dv, dag, dzp, dgm]
    ex_rest, ex_win = late_ex(grads)
    buf, row0 = None, 0
    for p in pieces[:-1]:
        (buf,) = _in_bwd_w(ht, p, row0, buf)
        row0 += p.shape[1]
    grads["w_in_t"], *got2 = _in_bwd_w(ht, pieces[-1], row0, buf, ex_rest)
    dx, dpreg, *got3 = _in_bwd_x(pieces, P["w_in_t"], x, P["pre_g"], dout, ex_win(grads))
    grads["pre_norm_g"] = dpreg
    return dx, grads, (got0, got1, got2, got3)


WEIGHT_NAMES = ("pre_norm_g", "post_norm_g", "w_in", "conv_dw", "conv_dw_b", "conv_ln_g", "conv_ln_b",
                "w_conv_out", "rel_bias", "w_attn_out", "pool_w", "pool_b", "pool_scale", "w_pool_out", "w_out")
SHARDED = ("w_in", "w_conv_out", "w_attn_out", "w_pool_out", "w_out", "conv_dw")
REST = tuple(n for n in WEIGHT_NAMES if n not in ("w_in", "pre_norm_g"))


def _cols_from_slabs(g):
    return g.transpose(1, 0, 2).reshape(g.shape[1], N_DEV * g.shape[2])


def _slabs_from_cols(full):
    r, wd = full.shape
    return full.reshape(r, N_DEV, wd // N_DEV).transpose(1, 0, 2)


def _rest_shards(weights, l):
    return [weights["w_conv_out"][l].astype(BF16), weights["w_attn_out"][l].astype(BF16),
            weights["w_pool_out"][l].astype(BF16), weights["w_out"][l].astype(BF16), weights["conv_dw"][l]]


def _rest_weights(got):
    wco, wao, wpo, wout, cdw = got
    return dict(w_conv_out=_cols_from_slabs(wco), w_attn_out=_cols_from_slabs(wao),
                w_pool_out=_cols_from_slabs(wpo), w_out=wout.reshape(D, D), conv_dw=_cols_from_slabs(cdw))


def _grad_arrays(g, names):
    make = {"w_in": lambda: g["w_in_t"].reshape(N_DEV, NCOL // N_DEV, D),
            "w_conv_out": lambda: _slabs_from_cols(g["w_conv_out"].astype(BF16)),
            "w_attn_out": lambda: _slabs_from_cols(g["w_attn_out"].astype(BF16)),
            "w_pool_out": lambda: _slabs_from_cols(g["w_pool_out"].astype(BF16)),
            "w_out": lambda: g["w_out"].astype(BF16).reshape(N_DEV, D // N_DEV, D),
            "conv_dw": lambda: _slabs_from_cols(g["conv_dw"].astype(BF16)),
            "rel_bias": lambda: jnp.dot(g["dskew"], jnp.asarray(_skew_select().T), precision=lax.Precision.HIGHEST),
            "pool_b": lambda: g["pool_b"].reshape(4, 128)}
    return [make[n]() if n in make else g[n] for n in names]


def _grad_exchange(g, names):
    return _Exchange(_grad_arrays(g, names), [n in SHARDED for n in names])


def kernel(x, pre_norm_g, post_norm_g, w_in, conv_dw, conv_dw_b, conv_ln_g, conv_ln_b, w_conv_out, rel_bias, w_attn_out, pool_w, pool_b, pool_scale, w_pool_out, w_out, loss_target, m_pre_norm_g, m_post_norm_g, m_w_in, m_conv_dw, m_conv_dw_b, m_conv_ln_g, m_conv_ln_b, m_w_conv_out, m_rel_bias, m_w_attn_out, m_pool_w, m_pool_b, m_pool_scale, m_w_pool_out, m_w_out, v_pre_norm_g, v_post_norm_g, v_w_in, v_conv_dw, v_conv_dw_b, v_conv_ln_g, v_conv_ln_b, v_w_conv_out, v_rel_bias, v_w_attn_out, v_pool_w, v_pool_b, v_pool_scale, v_w_pool_out, v_w_out):
    given = dict(locals())
    weights = {n: given[n] for n in WEIGHT_NAMES}
    nb, S, _ = x.shape
    T = nb * S
    L = pre_norm_g.shape[0]
    assert L == 2
    x2 = x.reshape(T, D)
    tgt2 = loss_target.reshape(T, D)
    skews = [_skew_table(rel_bias[l]) for l in range(L)]

    def local_params(l):
        return dict(pre_g=pre_norm_g[l:l + 1], post_g=post_norm_g[l:l + 1], conv_dw_b=conv_dw_b[l:l + 1],
                    conv_ln_g=conv_ln_g[l:l + 1], conv_ln_b=conv_ln_b[l:l + 1], pool_w=pool_w[l].astype(BF16),
                    pool_b=pool_b[l].reshape(1, BW), pool_scale=pool_scale[l:l + 1])

    win0 = w_in[0].T.astype(BF16)
    win1 = w_in[1].T.astype(BF16)
    half = win1.shape[0] // 2
    w_in_t0 = _gather_two_level(win0, "gather_w_in_0")
    gather = lambda arrays: _Exchange(arrays, [False] * len(arrays))
    h, saved0, P0, (got_rest0, got_a, got_b, got_rest1) = _layer_fwd(
        x2, {**local_params(0), "w_in_t": w_in_t0.reshape(NCOL, D)}, skews[0], S, _rest_weights,
        (gather(_rest_shards(weights, 0)), gather([win1[:half]]), gather([win1[half:]]),
         gather(_rest_shards(weights, 1))))
    w_in_t1 = jnp.concatenate([got_a[0], got_b[0]], axis=1).reshape(NCOL, D)
    dout, saved1, P1, (_, _, _, (lsum,)) = _layer_fwd(h, {**local_params(1), "w_in_t": w_in_t1}, skews[1], S,
                                                      lambda _: _rest_weights(got_rest1), (None,) * 4, tgt2)
    loss = lax.psum(lsum[0, 0], MESH_AXES) * (0.5 / D)

    no_ex = lambda grads: (None, lambda g: None)
    dout, g1, _ = _layer_bwd(dout, saved1, P1, skews[1], S, (None, None), no_ex)
    late0 = lambda grads: (_grad_exchange(grads, REST), lambda g: _grad_exchange(g, ("w_in",)))
    dout, g0, (got_win1, got_rest1g, got_rest0g, got_win0) = _layer_bwd(
        dout, saved0, P0, skews[0], S,
        (_grad_exchange(g1, ("w_in",)), _grad_exchange(g1, REST + ("pre_norm_g",))), late0)
    (got_pre0,) = _exchange([g0["pre_norm_g"]], [False], "gather_grad_pre_norm_g_0")
    parts = [{"w_in": got_win0[0], "pre_norm_g": got_pre0, **dict(zip(REST, got_rest0g))},
             {"w_in": got_win1[0], **dict(zip(REST + ("pre_norm_g",), got_rest1g))}]
    grad_x = dout.reshape(x.shape)

    outs = {}
    small = [n for n in WEIGHT_NAMES if n not in SHARDED]
    res = _adamw_small([[parts[l][n] for n in small] for l in range(L)], [weights[n] for n in small],
                       [given["m_" + n] for n in small], [given["v_" + n] for n in small])
    outs.update(zip(small, res))
    for n in SHARDED:
        view = (lambda a: a.transpose(0, 2, 1)) if n == "w_in" else (lambda a: a)
        res = _adamw_sum(parts[0][n], parts[1][n], view(weights[n]), view(given["m_" + n]), view(given["v_" + n]),
                         "adamw_" + n)
        outs[n] = [view(a) for a in res]
    return (loss, grad_x, *[outs[n][0] for n in WEIGHT_NAMES], *[outs[n][1] for n in WEIGHT_NAMES],
            *[outs[n][2] for n in WEIGHT_NAMES], *[outs[n][3] for n in WEIGHT_NAMES])
```

```python
import functools

import numpy as np
import jax
import jax.numpy as jnp
from jax import lax
from jax.experimental import pallas as pl
from jax.experimental.pallas import tpu as pltpu

F32 = jnp.float32
BF16 = jnp.bfloat16
SDS = jax.ShapeDtypeStruct

D = 1024
BW = 512
NCOL = 7680
EPS = 1e-6
NEG = -1e30
HEADS = 8
HD = 64
CHUNK = 64
LEFT = 8
MAX_REL = 256
TQ = 256
KW = 768
CONV_K = 31
WINDOWS = (2, 4, 8, 16)
HALO = 32
RC = 32
N_DEV = 8
MESH_AXES = ("x", "y", "c")

ADAM_LR = 0.001
ADAM_B1 = 0.9
ADAM_B2 = 0.999
ADAM_EPS = 1e-08
ADAM_WD = 0.01
ADAM_STEP = 10

VMEM_LIMIT = 56 * 1024 * 1024

CB_CA, CB_CB, CB_CG, CB_Q, CB_K, CB_V, CB_AG, CB_PI, CB_PG = range(9)
DZ_PIECES = (("conv", 1536), ("q", 512), ("k", 512), ("v", 512), ("ag", 512), ("pool", 1024), ("gm", 3072))


def _cparams(sem):
    return pltpu.CompilerParams(dimension_semantics=sem, vmem_limit_bytes=VMEM_LIMIT)


def _sig(x):
    return 0.5 * jnp.tanh(0.5 * x) + 0.5


def _dsilu(x, s):
    return s * (1.0 + x * (1.0 - s))


def _nt(a, b):
    return lax.dot_general(a, b, (((1,), (1,)), ((), ())), preferred_element_type=F32)


def _tn(a, b):
    return lax.dot_general(a, b, (((0,), (0,)), ((), ())), preferred_element_type=F32)


def _nn(a, b):
    return jnp.dot(a, b, preferred_element_type=F32)


def _rows8(x):
    return x[0:8] + x[8:16] + x[16:24] + x[24:32]


def _call(body, name, grid, in_specs, out_specs, out_shape, scratch, args, sem, ex=None, aliases=None):
    aliases = aliases or {}
    if ex is None:
        return pl.pallas_call(body, name=name, grid=grid, in_specs=in_specs, out_specs=out_specs,
                              out_shape=out_shape, scratch_shapes=scratch, input_output_aliases=aliases,
                              compiler_params=_cparams(sem))(*args)
    n_in, n_out = len(in_specs), len(out_specs)
    steps = int(np.prod(grid))

    def carrier(*refs):
        own_in, own_out, own_scr, exr = ex.split(refs, n_in, n_out)
        step = pl.program_id(0)
        for axis in range(1, len(grid)):
            step = step * grid[axis] + pl.program_id(axis)

        @pl.when(step == 0)
        def _():
            ex.start(exr)

        body(*own_in, *own_out, *own_scr)

        @pl.when(step == steps - 1)
        def _():
            ex.finish(exr)

    return pl.pallas_call(
        carrier, name=name + "_carrier", grid=grid, in_specs=in_specs + ex.in_specs,
        out_specs=out_specs + ex.out_specs, out_shape=out_shape + ex.out_shape,
        scratch_shapes=scratch + ex.scratch, input_output_aliases=aliases,
        compiler_params=_cparams(("arbitrary",) * len(grid)),
    )(*args, *ex.arrays)


def _in_proj(x, g, wt, ex=None):
    T = x.shape[0]
    tm = 512
    tn = 1536

    def body(x_ref, g_ref, w_ref, z_ref, ht_ref):
        xv = x_ref[...]
        r = lax.rsqrt(jnp.mean(xv * xv, axis=-1, keepdims=True) + EPS)
        h = xv * r * g_ref[...]
        hb = h.astype(BF16)
        ht_ref[...] = h.T.astype(BF16)
        for c in range(NCOL // tn):
            z_ref[:, c * tn:(c + 1) * tn] = _nt(hb, w_ref[c * tn:(c + 1) * tn, :]).astype(BF16)

    return _call(
        body, "in_proj", (T // tm,),
        [pl.BlockSpec((tm, D), lambda i: (i, 0)), pl.BlockSpec((1, D), lambda i: (0, 0)),
         pl.BlockSpec((NCOL, D), lambda i: (0, 0))],
        [pl.BlockSpec((tm, NCOL), lambda i: (i, 0)), pl.BlockSpec((D, tm), lambda i: (0, i))],
        [SDS((T, NCOL), BF16), SDS((D, T), BF16)], [],
        (x, g, wt), ("parallel",), ex)


def _stencil(load, w_ref, lo, hi, tap_of):
    out = None
    for r in range(8):
        n = RC if r == 0 else RC + 8
        v = None
        for q in range((hi - r) // 8 + 1):
            o = 8 * q + r
            if o < lo:
                continue
            j = tap_of(o)
            term = w_ref[j:j + 1, :] * load(q, n)
            v = term if v is None else v + term
        if v is None:
            continue
        if r:
            v = pltpu.roll(v, n - r, axis=0)[0:RC]
        out = v if out is None else out + v
    return out


def _layer_norm_fwd(u1):
    mu = jnp.mean(u1, axis=-1, keepdims=True)
    xc = u1 - mu
    rstd = lax.rsqrt(jnp.mean(xc * xc, axis=-1, keepdims=True) + EPS)
    return xc * rstd, rstd


def _window_sums(x, w, back):
    n = x.shape[0]
    s = x
    k = 1
    while k < w:
        s = s + pltpu.roll(s, k if back else n - k, axis=0)
        k *= 2
    return s


def _pool_chunk(pwin, t_first):
    t = t_first + lax.broadcasted_iota(jnp.int32, (RC, 128), 0)
    outs = []
    for g, w in enumerate(WINDOWS):
        x = pwin[:, g * 128:(g + 1) * 128]
        s = _window_sums(x, w, True)
        cnt = jnp.minimum(t + 1, w).astype(F32)
        outs.append(s[HALO:HALO + RC] / cnt - x[HALO:HALO + RC])
    return outs


def _mix_fwd(z, dw, dwb, lng, lnb, pw, pb, ps, S, ex=None):
    T = z.shape[0]
    tm = 512
    ts = S // tm
    nh = tm // HALO

    def body(ca_ref, cb_ref, cg_ref, pi_ref, pg_ref, cah_ref, cbh_ref, pih_ref,
             dw_ref, dwb_ref, lng_ref, lnb_ref, pw_ref, pb_ref, ps_ref,
             ac_ref, ap_ref, u1_ref, ubuf, pbuf, pooled):
        i = pl.program_id(0)
        keep = jnp.where((i % ts) == 0, 0.0, 1.0)
        ubuf[0:HALO, :] = cah_ref[...].astype(F32) * _sig(cbh_ref[...].astype(F32)) * keep
        ubuf[HALO:HALO + tm, :] = ca_ref[...].astype(F32) * _sig(cb_ref[...].astype(F32))
        pbuf[0:HALO, :] = pih_ref[...].astype(F32) * keep
        pbuf[HALO:HALO + tm, :] = pi_ref[...].astype(F32)
        t0 = (i % ts) * tm

        def chunk(c, carry):
            base = pl.multiple_of(c * RC, RC)
            load = lambda q, n: ubuf[pl.ds(base + 8 * q, n), :]
            u1 = _stencil(load, dw_ref, 2, CONV_K + 1, lambda o: o - 2) + dwb_ref[...]
            u1_ref[pl.ds(base, RC), :] = u1
            n, _ = _layer_norm_fwd(u1)
            u2 = n * lng_ref[...] + lnb_ref[...]
            u3 = u2 * _sig(u2)
            cg = cg_ref[pl.ds(base, RC), :].astype(F32)
            ac_ref[pl.ds(base, RC), :] = (u3 * cg * _sig(cg)).astype(BF16)
            pwin = pbuf[pl.ds(base, RC + HALO), :]
            outs = _pool_chunk(pwin, t0 + base)
            for g in range(4):
                pooled[pl.ds(base, RC), g * 128:(g + 1) * 128] = outs[g].astype(BF16)
            return carry

        lax.fori_loop(0, tm // RC, chunk, 0, unroll=4)
        pg = pg_ref[...].astype(F32)
        spg = pg * _sig(pg)
        for g in range(4):
            sl = slice(g * 128, (g + 1) * 128)
            mixed = (_nn(pooled[:, sl], pw_ref[g]) + pb_ref[:, sl]) * ps_ref[:, sl]
            ap_ref[:, sl] = (mixed * spg[:, sl]).astype(BF16)

    def zmain(cb):
        return pl.BlockSpec((tm, BW), lambda i: (i, cb))

    def zprev(cb):
        return pl.BlockSpec((HALO, BW), lambda i: (jnp.maximum(i * nh - 1, 0), cb))

    full = lambda shape: pl.BlockSpec(shape, lambda i: (0,) * len(shape))
    row = pl.BlockSpec((tm, BW), lambda i: (i, 0))
    return _call(
        body, "mix_fwd", (T // tm,),
        [zmain(CB_CA), zmain(CB_CB), zmain(CB_CG), zmain(CB_PI), zmain(CB_PG),
         zprev(CB_CA), zprev(CB_CB), zprev(CB_PI),
         full((CONV_K, BW)), full((1, BW)), full((1, BW)), full((1, BW)),
         full((4, 128, 128)), full((1, BW)), full((1, BW))],
        [row, row, row], [SDS((T, BW), BF16), SDS((T, BW), BF16), SDS((T, BW), F32)],
        [pltpu.VMEM((HALO + tm, BW), F32), pltpu.VMEM((HALO + tm, BW), F32), pltpu.VMEM((tm, BW), BF16)],
        (z, z, z, z, z, z, z, z, dw, dwb, lng, lnb, pw, pb, ps), ("parallel",), ex)


def _attn_specs(nq):
    def kv(cb, off):
        return pl.BlockSpec((TQ, BW), lambda i: (i - jnp.minimum(off, i % nq), cb))
    return [pl.BlockSpec((TQ, BW), lambda i: (i, CB_Q)),
            kv(CB_K, 2), kv(CB_K, 1), kv(CB_K, 0), kv(CB_V, 2), kv(CB_V, 1), kv(CB_V, 0)]


def _softmax_rows(s):
    m = jnp.max(s, axis=-1, keepdims=True)
    e = jnp.exp(s - m)
    return e / jnp.sum(e, axis=-1, keepdims=True)


NSKEW = 1024


def _skew_table(table):
    return jnp.dot(table, jnp.asarray(_skew_select()), precision=lax.Precision.HIGHEST)


def _skew_select():
    d = np.arange(TQ + KW - 1)
    idx = np.clip(3 * TQ - 1 - d, -MAX_REL, MAX_REL) + MAX_REL
    sel = np.zeros((2 * MAX_REL + 1, NSKEW), np.float32)
    sel[idx, d] = 1.0
    return sel


def _bias_from_skew(f_ref, bias_scr):
    qi = lax.broadcasted_iota(jnp.int32, (TQ, KW), 0)
    kj = lax.broadcasted_iota(jnp.int32, (TQ, KW), 1)
    lo = (qi // CHUNK) * CHUNK
    band = jnp.where((kj >= lo) & (kj < lo + (LEFT + 1) * CHUNK), 0.0, NEG)
    for h in range(HEADS):
        rows = jnp.broadcast_to(f_ref[h:h + 1, :], (TQ, NSKEW))
        rows = pltpu.roll(rows, NSKEW - (TQ - 1), axis=1, stride=1, stride_axis=0)
        bias_scr[h] = rows[:, 0:KW] + band


def _skew_from_bias(db):
    i = lax.broadcasted_iota(jnp.int32, (TQ, TQ), 0)
    j = lax.broadcasted_iota(jnp.int32, (TQ, TQ), 1)
    flip = jnp.where(i + j == TQ - 1, 1.0, 0.0).astype(BF16)
    hi = db.astype(BF16)
    lo = (db - hi.astype(F32)).astype(BF16)
    rev = _nn(flip, hi) + _nn(flip, lo)
    rev = jnp.concatenate([rev, jnp.zeros((TQ, NSKEW - KW), F32)], axis=1)
    return jnp.sum(pltpu.roll(rev, 0, axis=1, stride=1, stride_axis=0), axis=0, keepdims=True)


def _attn_fwd(z, f, S, ex=None):
    T = z.shape[0]
    nq = S // TQ

    def body(q_ref, k2_ref, k1_ref, k0_ref, v2_ref, v1_ref, v0_ref, f_ref, o_ref, kbuf, vbuf, b_scr):
        @pl.when(pl.program_id(0) == 0)
        def _():
            _bias_from_skew(f_ref, b_scr)

        qb = pl.program_id(0) % nq
        kbuf[0:TQ, :] = k2_ref[...]
        kbuf[TQ:2 * TQ, :] = k1_ref[...]
        kbuf[2 * TQ:KW, :] = k0_ref[...]
        vbuf[0:TQ, :] = v2_ref[...]
        vbuf[TQ:2 * TQ, :] = v1_ref[...]
        vbuf[2 * TQ:KW, :] = v0_ref[...]
        lane = lax.broadcasted_iota(jnp.int32, (1, 128), 1)

        def attend(lo):
            def scores(h):
                sl = slice((h // 2) * 128, (h // 2 + 1) * 128)
                qp = q_ref[:, sl] * 0.125
                qm = jnp.where((lane < HD) if h % 2 == 0 else (lane >= HD), qp, jnp.zeros_like(qp))
                return _nt(qm, kbuf[lo:KW, sl]) + b_scr[h, :, lo:KW]

            s = scores(0)
            acc = None
            for h in range(HEADS):
                s_next = scores(h + 1) if h + 1 < HEADS else None
                sl = slice((h // 2) * 128, (h // 2 + 1) * 128)
                e = jnp.exp(s - jnp.max(s, axis=-1, keepdims=True))
                vp = vbuf[lo:KW, sl]
                vm = jnp.where((lane < HD) if h % 2 == 0 else (lane >= HD), vp, jnp.zeros_like(vp))
                o = _nn(e.astype(BF16), vm) * (1.0 / jnp.sum(e, axis=-1, keepdims=True))
                acc = o if h % 2 == 0 else acc + o
                if h % 2 == 1:
                    o_ref[:, sl] = acc.astype(BF16)
                s = s_next

        for nblk in (1, 2, 3):
            pl.when(jnp.minimum(qb, 2) == nblk - 1)(functools.partial(attend, (3 - nblk) * TQ))

    full = lambda shape: pl.BlockSpec(shape, lambda i: (0,) * len(shape))
    return _call(
        body, "attn_fwd", (T // TQ,),
        _attn_specs(nq) + [full((HEADS, NSKEW))],
        [pl.BlockSpec((TQ, BW), lambda i: (i, 0))], [SDS((T, BW), BF16)],
        [pltpu.VMEM((KW, BW), BF16), pltpu.VMEM((KW, BW), BF16), pltpu.VMEM((HEADS, TQ, KW), F32)],
        (z, z, z, z, z, z, z, f), ("arbitrary",), ex)


def _gates(gl_ref, gh_ref):
    gl = _sig(gl_ref[...].astype(F32))
    gh = _sig(gh_ref[...].astype(F32))
    return (gl[:, 0:D], jnp.concatenate([gl[:, D:1536], gh[:, 0:512]], axis=1), gh[:, 512:1536])


def _out_specs_in(tm):
    row = lambda w: pl.BlockSpec((tm, w), lambda i: (i, 0))
    full = lambda shape: pl.BlockSpec(shape, lambda i: (0,) * len(shape))
    return [row(BW), row(BW), row(BW),
            pl.BlockSpec((tm, BW), lambda i: (i, CB_AG)),
            pl.BlockSpec((tm, 1536), lambda i: (i, 3)),
            pl.BlockSpec((tm, 1536), lambda i: (i, 4)),
            full((BW, D)), full((BW, D)), full((BW, D)), full((D, D)), full((1, D))]


def _out_fwd(x, ac, o, ap, z, wco, wao, wpo, wout, postg, ex=None, tgt=None):
    T = x.shape[0]
    tm = 512
    last = tgt is not None

    def body(ac_ref, o_ref, ap_ref, ag_ref, gl_ref, gh_ref, wco_ref, wao_ref, wpo_ref, wout_ref, pg_ref,
             x_ref, *rest):
        ag = ag_ref[...].astype(F32)
        aat = (o_ref[...].astype(F32) * ag * _sig(ag)).astype(BF16)
        gates = _gates(gl_ref, gh_ref)
        acts = (ac_ref[...], aat, ap_ref[...])
        merged = None
        for b, w_ref in enumerate((wco_ref, wao_ref, wpo_ref)):
            yb = _nn(acts[b], w_ref[...])
            rest[-4 + b][...] = yb.astype(BF16)
            merged = gates[b] * yb if merged is None else merged + gates[b] * yb
        y = _nn(merged.astype(BF16), wout_ref[...])
        rest[-1][...] = y.astype(BF16)
        ry = lax.rsqrt(jnp.mean(y * y, axis=-1, keepdims=True) + EPS)
        out = x_ref[...] + y * ry * pg_ref[...]
        if not last:
            rest[0][...] = out
            return
        t_ref, d_ref, l_ref = rest[:3]

        @pl.when(pl.program_id(0) == 0)
        def _():
            l_ref[...] = jnp.zeros_like(l_ref)
        d = out - t_ref[...]
        d_ref[...] = d * (1.0 / D)
        l_ref[...] += jnp.sum(jnp.sum(d * d, axis=0, keepdims=True), axis=1, keepdims=True)

    row = pl.BlockSpec((tm, D), lambda i: (i, 0))
    kept_specs, kept_shapes = [row] * 4, [SDS((T, D), BF16)] * 4
    if not last:
        res = _call(body, "out_fwd", (T // tm,), _out_specs_in(tm) + [row], [row] + kept_specs,
                    [SDS((T, D), F32)] + kept_shapes, [],
                    (ac, o, ap, z, z, z, wco, wao, wpo, wout, postg, x), ("parallel",), ex)
        return res[:1], res[1:5], res[5:]
    res = _call(body, "out_fwd_loss", (T // tm,), _out_specs_in(tm) + [row, row],
                [row, pl.BlockSpec((1, 128), lambda i: (0, 0))] + kept_specs,
                [SDS((T, D), F32), SDS((1, 128), F32)] + kept_shapes, [],
                (ac, o, ap, z, z, z, wco, wao, wpo, wout, postg, x, tgt), ("arbitrary",), ex)
    return res[:2], res[2:6], res[6:]


def _out_bwd(dout, ac, o, ap, z, kept, wco, wao, wpo, wout, postg, ex=None):
    T = dout.shape[0]
    tm = 256

    def body(ac_ref, o_ref, ap_ref, ag_ref, gl_ref, gh_ref, wco_ref, wao_ref, wpo_ref, wout_ref, pg_ref, do_ref,
             yc_ref, ya_ref, yp_ref, y_ref,
             dac_ref, dao_ref, dag_ref, dap_ref, dgm_ref, dwco_ref, dwao_ref, dwpo_ref, dwout_ref, dpg_ref):
        @pl.when(pl.program_id(0) == 0)
        def _():
            for r in (dwco_ref, dwao_ref, dwpo_ref, dwout_ref, dpg_ref):
                r[...] = jnp.zeros_like(r)

        ag = ag_ref[...].astype(F32)
        sag = _sig(ag)
        ov = o_ref[...].astype(F32)
        acts = (ac_ref[...], (ov * ag * sag).astype(BF16), ap_ref[...])
        ws = (wco_ref, wao_ref, wpo_ref)
        gates = _gates(gl_ref, gh_ref)
        ys = [r[...].astype(F32) for r in (yc_ref, ya_ref, yp_ref)]
        merged = (gates[0] * ys[0] + gates[1] * ys[1] + gates[2] * ys[2]).astype(BF16)
        y = y_ref[...].astype(F32)
        ry = lax.rsqrt(jnp.mean(y * y, axis=-1, keepdims=True) + EPS)
        yn = y * ry
        dout_v = do_ref[...]
        dpg_ref[...] += jnp.sum(dout_v * yn, axis=0, keepdims=True)
        dyn = dout_v * pg_ref[...]
        dy = (ry * (dyn - yn * jnp.mean(dyn * yn, axis=-1, keepdims=True))).astype(BF16)
        dmerged = _nt(dy, wout_ref[...])
        dwout_ref[...] += _tn(merged, dy)
        dws = (dwco_ref, dwao_ref, dwpo_ref)
        das = []
        for b in range(3):
            gb = gates[b]
            dgm_ref[:, b * D:(b + 1) * D] = (dmerged * ys[b] * gb * (1.0 - gb)).astype(BF16)
            dyb = (dmerged * gb).astype(BF16)
            dws[b][...] += _tn(acts[b], dyb)
            das.append(_nt(dyb, ws[b][...]))
        dac_ref[...] = das[0].astype(BF16)
        dap_ref[...] = das[2].astype(BF16)
        dao_ref[...] = (das[1] * ag * sag).astype(BF16)
        dag_ref[...] = (das[1] * ov * _dsilu(ag, sag)).astype(BF16)

    row = lambda w: pl.BlockSpec((tm, w), lambda i: (i, 0))
    full = lambda shape: pl.BlockSpec(shape, lambda i: (0,) * len(shape))
    return _call(
        body, "out_bwd", (T // tm,), _out_specs_in(tm) + [row(D)] * 5,
        [row(BW), row(BW), row(BW), row(BW), row(3 * D),
         full((BW, D)), full((BW, D)), full((BW, D)), full((D, D)), full((1, D))],
        [SDS((T, BW), BF16)] * 4 + [SDS((T, 3 * D), BF16)]
        + [SDS((BW, D), F32)] * 3 + [SDS((D, D), F32), SDS((1, D), F32)], [],
        (ac, o, ap, z, z, z, wco, wao, wpo, wout, postg, dout, *kept), ("arbitrary",), ex)


def _attn_bwd(z, dao, f, S, ex=None):
    T = z.shape[0]
    nq = S // TQ
    nsteps = T // TQ

    def body(q_ref, k2_ref, k1_ref, k0_ref, v2_ref, v1_ref, v0_ref, do_ref, f_ref,
             dq_ref, dk_ref, dv_ref, df_ref, kbuf, vbuf, dkacc, dvacc, b_scr, db_scr):
        i = pl.program_id(0)
        qb = i % nq

        @pl.when(i == 0)
        def _():
            _bias_from_skew(f_ref, b_scr)
            db_scr[...] = jnp.zeros_like(db_scr)

        @pl.when(qb == 0)
        def _():
            dkacc[...] = jnp.zeros_like(dkacc)
            dvacc[...] = jnp.zeros_like(dvacc)

        kbuf[0:TQ, :] = k2_ref[...]
        kbuf[TQ:2 * TQ, :] = k1_ref[...]
        kbuf[2 * TQ:KW, :] = k0_ref[...]
        vbuf[0:TQ, :] = v2_ref[...]
        vbuf[TQ:2 * TQ, :] = v1_ref[...]
        vbuf[2 * TQ:KW, :] = v0_ref[...]
        lane = lax.broadcasted_iota(jnp.int32, (1, 128), 1)
        row0 = pl.multiple_of(qb * TQ, TQ)

        def attend(lo):
            def first_matmuls(h):
                sl = slice((h // 2) * 128, (h // 2 + 1) * 128)
                msk = (lane < HD) if h % 2 == 0 else (lane >= HD)
                qp = q_ref[:, sl] * 0.125
                dop = do_ref[:, sl]
                qm = jnp.where(msk, qp, jnp.zeros_like(qp))
                dom = jnp.where(msk, dop, jnp.zeros_like(dop))
                s = _nt(qm, kbuf[lo:KW, sl]) + b_scr[h, :, lo:KW]
                return s, _nt(dom, vbuf[lo:KW, sl]), qm, dom

            cur = first_matmuls(0)
            dq_acc = dk_acc = dv_acc = None
            for h in range(HEADS):
                nxt = first_matmuls(h + 1) if h + 1 < HEADS else None
                s, dp, qm, dom = cur
                sl = slice((h // 2) * 128, (h // 2 + 1) * 128)
                e = jnp.exp(s - jnp.max(s, axis=-1, keepdims=True))
                p = e * (1.0 / jnp.sum(e, axis=-1, keepdims=True))
                ds = p * (dp - jnp.sum(p * dp, axis=-1, keepdims=True))
                db_scr[h, :, lo:KW] += ds
                dsb = ds.astype(BF16)
                kp = kbuf[lo:KW, sl]
                km = jnp.where((lane < HD) if h % 2 == 0 else (lane >= HD), kp, jnp.zeros_like(kp))
                dq_h = _nn(dsb, km) * 0.125
                dk_h = _tn(dsb, qm)
                dv_h = _tn(p.astype(BF16), dom)
                if h % 2 == 0:
                    dq_acc, dk_acc, dv_acc = dq_h, dk_h, dv_h
                else:
                    dq_ref[:, sl] = (dq_acc + dq_h).astype(BF16)
                    dkacc[pl.ds(row0 + lo, KW - lo), sl] += dk_acc + dk_h
                    dvacc[pl.ds(row0 + lo, KW - lo), sl] += dv_acc + dv_h
                cur = nxt

        for nblk in (1, 2, 3):
            pl.when(jnp.minimum(qb, 2) == nblk - 1)(functools.partial(attend, (3 - nblk) * TQ))

        @pl.when(qb == nq - 1)
        def _():
            dk_ref[...] = dkacc[2 * TQ:2 * TQ + S, :].astype(BF16)
            dv_ref[...] = dvacc[2 * TQ:2 * TQ + S, :].astype(BF16)

        @pl.when(i == nsteps - 1)
        def _():
            for h in range(HEADS):
                df_ref[h:h + 1, :] = _skew_from_bias(db_scr[h])

    full = lambda shape: pl.BlockSpec(shape, lambda i: (0,) * len(shape))
    return _call(
        body, "attn_bwd", (nsteps,),
        _attn_specs(nq) + [pl.BlockSpec((TQ, BW), lambda i: (i, 0)), full((HEADS, NSKEW))],
        [pl.BlockSpec((TQ, BW), lambda i: (i, 0)), pl.BlockSpec((S, BW), lambda i: (i // nq, 0)),
         pl.BlockSpec((S, BW), lambda i: (i // nq, 0)), full((HEADS, NSKEW))],
        [SDS((T, BW), BF16)] * 3 + [SDS((HEADS, NSKEW), F32)],
        [pltpu.VMEM((KW, BW), BF16), pltpu.VMEM((KW, BW), BF16),
         pltpu.VMEM((S + 2 * TQ, BW), F32), pltpu.VMEM((S + 2 * TQ, BW), F32),
         pltpu.VMEM((HEADS, TQ, KW), F32), pltpu.VMEM((HEADS, TQ, KW), F32)],
        (z, z, z, z, z, z, z, dao, f), ("arbitrary",), ex)


def _mix_bwd(z, u1, dac, dap, dw, dwb, lng, lnb, pw, pb, ps, S):
    T = z.shape[0]
    tm = 512
    ts = S // tm
    nh = tm // HALO
    nsteps = T // tm
    nblk32 = T // HALO

    def body(ca_ref, cb_ref, cg_ref, pi_ref, pg_ref, u1_ref, dac_ref, dap_ref,
             cah_ref, cbh_ref, pih_ref,
             cgn_ref, pgn_ref, u1n_ref, dacn_ref, dapn_ref,
             dw_ref, dwb_ref, lng_ref, lnb_ref, pw_ref, pb_ref, ps_ref,
             dzc_ref, dzp_ref, ddw_ref, ddwb_ref, dlng_ref, dlnb_ref, dpw_ref, dpb_ref, dps_ref,
             ubuf, gbuf, pbuf, qbuf, pooled, *accs):
        tap_acc, (lng_acc, lnb_acc, dwb_acc) = accs[:CONV_K], accs[CONV_K:]
        i = pl.program_id(0)
        keep_prev = jnp.where((i % ts) == 0, 0.0, 1.0)
        keep_next = jnp.where((i % ts) == ts - 1, 0.0, 1.0)
        t0 = (i % ts) * tm

        @pl.when(i == 0)
        def _():
            for a in accs:
                a[...] = jnp.zeros_like(a)
            dpw_ref[...] = jnp.zeros_like(dpw_ref)
            dpb_ref[...] = jnp.zeros_like(dpb_ref)
            dps_ref[...] = jnp.zeros_like(dps_ref)

        ubuf[0:HALO, :] = cah_ref[...].astype(F32) * _sig(cbh_ref[...].astype(F32)) * keep_prev
        ubuf[HALO:HALO + tm, :] = ca_ref[...].astype(F32) * _sig(cb_ref[...].astype(F32))
        pbuf[0:HALO, :] = pih_ref[...].astype(F32) * keep_prev
        pbuf[HALO:HALO + tm, :] = pi_ref[...].astype(F32)

        def norm_back(u1v, cg, dacv):
            n, rstd = _layer_norm_fwd(u1v)
            u2 = n * lng_ref[...] + lnb_ref[...]
            s2 = _sig(u2)
            scg = _sig(cg)
            du2 = dacv * cg * scg * _dsilu(u2, s2)
            dn = du2 * lng_ref[...]
            du1 = rstd * (dn - jnp.mean(dn, axis=-1, keepdims=True)
                          - n * jnp.mean(dn * n, axis=-1, keepdims=True))
            return du1, du2, n, dacv * u2 * s2 * _dsilu(cg, scg)

        def chunk_a(c, carry):
            base = pl.multiple_of(c * RC, RC)
            du1, du2, n, dcg = norm_back(u1_ref[pl.ds(base, RC), :], cg_ref[pl.ds(base, RC), :].astype(F32),
                                         dac_ref[pl.ds(base, RC), :].astype(F32))
            gbuf[pl.ds(base, RC), :] = du1
            dzc_ref[pl.ds(base, RC), 2 * BW:3 * BW] = dcg.astype(BF16)
            lng_acc[...] += _rows8(du2 * n)
            lnb_acc[...] += _rows8(du2)
            dwb_acc[...] += _rows8(du1)
            padded = jnp.concatenate([du1, jnp.zeros((8, BW), F32)], axis=0)
            for r in range(8):
                nrow = RC if r == 0 else RC + 8
                g = du1 if r == 0 else pltpu.roll(padded, r, axis=0)
                for q in range((CONV_K + 1 - r) // 8 + 1):
                    o = 8 * q + r
                    if o < 2:
                        continue
                    prod = g * ubuf[pl.ds(base + 8 * q, nrow), :]
                    red = prod[0:8]
                    for k in range(1, nrow // 8):
                        red = red + prod[8 * k:8 * k + 8]
                    tap_acc[o - 2][...] += red
            return carry

        lax.fori_loop(0, tm // RC, chunk_a, 0, unroll=4)
        du1n, _, _, _ = norm_back(u1n_ref[...], cgn_ref[...].astype(F32), dacn_ref[...].astype(F32))
        gbuf[tm:tm + HALO, :] = du1n * keep_next

        def chunk_p(c, carry):
            base = pl.multiple_of(c * RC, RC)
            outs = _pool_chunk(pbuf[pl.ds(base, RC + HALO), :], t0 + base)
            for g in range(4):
                pooled[pl.ds(base, RC), g * 128:(g + 1) * 128] = outs[g].astype(BF16)
            return carry

        lax.fori_loop(0, tm // RC, chunk_p, 0)

        def cnt_of(t_first, rows, w):
            t = t_first + lax.broadcasted_iota(jnp.int32, (rows, 128), 0)
            return jnp.minimum(t + 1, w).astype(F32)

        pg = pg_ref[...].astype(F32)
        spg_s = _sig(pg)
        dapv = dap_ref[...].astype(F32)
        pgn = pgn_ref[...].astype(F32)
        dmixn = dapn_ref[...].astype(F32) * pgn * _sig(pgn) * ps_ref[...] * keep_next
        for g, w in enumerate(WINDOWS):
            sl = slice(g * 128, (g + 1) * 128)
            mixed_u = _nn(pooled[:, sl], pw_ref[g]) + pb_ref[:, sl]
            dap_g = dapv[:, sl]
            pg_g = pg[:, sl]
            s_g = spg_s[:, sl]
            silu_g = pg_g * s_g
            dps_ref[:, sl] += jnp.sum(dap_g * silu_g * mixed_u, axis=0, keepdims=True)
            dzp_ref[:, BW + g * 128:BW + (g + 1) * 128] = (
                dap_g * mixed_u * ps_ref[:, sl] * _dsilu(pg_g, s_g)).astype(BF16)
            dmix = dap_g * silu_g * ps_ref[:, sl]
            dpb_ref[:, sl] += jnp.sum(dmix, axis=0, keepdims=True)
            dmixb = dmix.astype(BF16)
            dpw_ref[g] += _tn(pooled[:, sl], dmixb)
            qbuf[0:tm, sl] = _nt(dmixb, pw_ref[g]) / cnt_of(t0, tm, w)
            qbuf[tm:tm + HALO, sl] = _nt(dmixn[:, sl].astype(BF16), pw_ref[g]) / cnt_of(t0 + tm, HALO, w)

        def chunk_b(c, carry):
            base = pl.multiple_of(c * RC, RC)
            load = lambda q, n: gbuf[pl.ds(base + 8 * q, n), :]
            du0 = _stencil(load, dw_ref, 0, CONV_K - 1, lambda o: CONV_K - 1 - o)
            ca = ca_ref[pl.ds(base, RC), :].astype(F32)
            sb = _sig(cb_ref[pl.ds(base, RC), :].astype(F32))
            dzc_ref[pl.ds(base, RC), 0:BW] = (du0 * sb).astype(BF16)
            dzc_ref[pl.ds(base, RC), BW:2 * BW] = (du0 * ca * sb * (1.0 - sb)).astype(BF16)
            qwin = qbuf[pl.ds(base, RC + HALO), :]
            t = t0 + base + lax.broadcasted_iota(jnp.int32, (RC, 128), 0)
            for g, w in enumerate(WINDOWS):
                x = qwin[:, g * 128:(g + 1) * 128]
                s = _window_sums(x, w, False)
                cnt = jnp.minimum(t + 1, w).astype(F32)
                dzp_ref[pl.ds(base, RC), g * 128:(g + 1) * 128] = (s[0:RC] - cnt * x[0:RC]).astype(BF16)
            return carry

        lax.fori_loop(0, tm // RC, chunk_b, 0)

        @pl.when(i == nsteps - 1)
        def _():
            dlng_ref[...] = jnp.sum(lng_acc[...], axis=0, keepdims=True)
            dlnb_ref[...] = jnp.sum(lnb_acc[...], axis=0, keepdims=True)
            ddwb_ref[...] = jnp.sum(dwb_acc[...], axis=0, keepdims=True)
            for j in range(CONV_K):
                ddw_ref[j:j + 1, :] = jnp.sum(tap_acc[j][...], axis=0, keepdims=True)

    def zmain(cb):
        return pl.BlockSpec((tm, BW), lambda i: (i, cb))

    def zprev(cb):
        return pl.BlockSpec((HALO, BW), lambda i: (jnp.maximum(i * nh - 1, 0), cb))

    def znext(cb):
        return pl.BlockSpec((HALO, BW), lambda i: (jnp.minimum((i + 1) * nh, nblk32 - 1), cb))

    row = lambda w: pl.BlockSpec((tm, w), lambda i: (i, 0))
    full = lambda shape: pl.BlockSpec(shape, lambda i: (0,) * len(shape))
    return pl.pallas_call(
        body, name="mix_bwd", grid=(nsteps,),
        in_specs=[zmain(CB_CA), zmain(CB_CB), zmain(CB_CG), zmain(CB_PI), zmain(CB_PG), row(BW), row(BW), row(BW),
                  zprev(CB_CA), zprev(CB_CB), zprev(CB_PI),
                  znext(CB_CG), znext(CB_PG), znext(0), znext(0), znext(0),
                  full((CONV_K, BW)), full((1, BW)), full((1, BW)), full((1, BW)),
                  full((4, 128, 128)), full((1, BW)), full((1, BW))],
        out_specs=[row(3 * BW), row(2 * BW), full((CONV_K, BW)), full((1, BW)), full((1, BW)), full((1, BW)),
                   full((4, 128, 128)), full((1, BW)), full((1, BW))],
        out_shape=[SDS((T, 3 * BW), BF16), SDS((T, 2 * BW), BF16), SDS((CONV_K, BW), F32),
                   SDS((1, BW), F32), SDS((1, BW), F32), SDS((1, BW), F32),
                   SDS((4, 128, 128), F32), SDS((1, BW), F32), SDS((1, BW), F32)],
        scratch_shapes=[pltpu.VMEM((HALO + tm, BW), F32), pltpu.VMEM((tm + HALO, BW), F32),
                        pltpu.VMEM((HALO + tm, BW), F32), pltpu.VMEM((tm + HALO, BW), F32),
                        pltpu.VMEM((tm, BW), BF16)] + [pltpu.VMEM((8, BW), F32)] * (CONV_K + 3),
        compiler_params=_cparams(("arbitrary",)),
    )(z, z, z, z, z, u1, dac, dap, z, z, z, z, z, u1, dac, dap, dw, dwb, lng, lnb, pw, pb, ps)


def _in_bwd_x(pieces, wt, x, g, dout, ex=None):
    T = x.shape[0]
    tm = 256
    widths = [p.shape[1] for p in pieces]
    offs = np.cumsum([0] + widths)
    npc = len(pieces)

    def body(*refs):
        p_refs = refs[:npc]
        w_ref, x_ref, g_ref, do_ref, dx_ref, dg_ref = refs[npc:]

        @pl.when(pl.program_id(0) == 0)
        def _():
            dg_ref[...] = jnp.zeros_like(dg_ref)

        dh = None
        for k in range(npc):
            t = _nn(p_refs[k][...], w_ref[int(offs[k]):int(offs[k + 1]), :])
            dh = t if dh is None else dh + t
        xv = x_ref[...]
        r = lax.rsqrt(jnp.mean(xv * xv, axis=-1, keepdims=True) + EPS)
        xn = xv * r
        dg_ref[...] += jnp.sum(dh * xn, axis=0, keepdims=True)
        dxn = dh * g_ref[...]
        dx_ref[...] = do_ref[...] + r * (dxn - xn * jnp.mean(dxn * xn, axis=-1, keepdims=True))

    row = lambda wd: pl.BlockSpec((tm, wd), lambda i: (i, 0))
    return _call(
        body, "in_bwd_x", (T // tm,),
        [row(wd) for wd in widths] + [pl.BlockSpec((NCOL, D), lambda i: (0, 0)),
                                      row(D), pl.BlockSpec((1, D), lambda i: (0, 0)), row(D)],
        [row(D), pl.BlockSpec((1, D), lambda i: (0, 0))], [SDS((T, D), F32), SDS((1, D), F32)], [],
        (*pieces, wt, x, g, dout), ("arbitrary",), ex)


def _in_bwd_w(ht, piece, row0, buf=None, ex=None):
    T = ht.shape[1]
    wd = piece.shape[1]
    tn = next(t for t in (1024, 768, 512) if wd % t == 0 and row0 % t == 0)
    tk = min(T, 2048)
    nk = T // tk
    j0 = row0 // tn

    def body(ht_ref, p_ref, *rest):
        o_ref, acc = rest[-2:]
        k = pl.program_id(1)

        @pl.when(k == 0)
        def _():
            acc[...] = jnp.zeros_like(acc)
        acc[...] += _nn(ht_ref[...], p_ref[...])

        @pl.when(k == nk - 1)
        def _():
            o_ref[...] = acc[...].T.astype(BF16)

    in_specs = [pl.BlockSpec((D, tk), lambda j, k: (0, k)), pl.BlockSpec((tk, tn), lambda j, k: (k, j))]
    args = (ht, piece)
    if buf is not None:
        in_specs.append(pl.BlockSpec(memory_space=pl.ANY))
        args += (buf,)
    return _call(
        body, "in_bwd_w", (wd // tn, nk), in_specs,
        [pl.BlockSpec((tn, D), lambda j, k: (j + j0, 0))], [SDS((NCOL, D), BF16)], [pltpu.VMEM((D, tn), F32)],
        args, ("parallel", "arbitrary"), ex, None if buf is None else {2: 0})


def _my_id():
    return 4 * lax.axis_index("x") + 2 * lax.axis_index("y") + lax.axis_index("c")


def _peers():
    x, y, c = lax.axis_index("x"), lax.axis_index("y"), lax.axis_index("c")
    out = []
    for k in range(1, N_DEV):
        fx, fy, fc = (k >> 2) & 1, (k >> 1) & 1, k & 1
        px, py, pc = x ^ fx, y ^ fy, c ^ fc
        out.append(((px, py, pc), 4 * px + 2 * py + pc))
    return out


class _Exchange:
    def __init__(self, arrays, scatter):
        self.arrays = list(arrays)
        self.scatter = list(scatter)
        self.n = n = len(arrays)
        hbm = pl.BlockSpec(memory_space=pltpu.HBM)
        self.in_specs = [hbm] * n
        self.out_specs = [hbm] * n
        self.out_shape = [SDS((N_DEV,) + tuple(a.shape[1:] if s else a.shape), a.dtype)
                          for a, s in zip(arrays, scatter)]
        self.scratch = [pltpu.SemaphoreType.DMA((N_DEV - 1, n)), pltpu.SemaphoreType.DMA((N_DEV - 1, n)),
                        pltpu.SemaphoreType.DMA((n,))]

    def split(self, refs, n_in, n_out):
        n = self.n
        own_in = refs[:n_in]
        ex_in = refs[n_in:n_in + n]
        own_out = refs[n_in + n:n_in + n + n_out]
        ex_out = refs[n_in + n + n_out:n_in + 2 * n + n_out]
        rest = refs[n_in + 2 * n + n_out:]
        return own_in, own_out, rest[:-3], (ex_in, ex_out, rest[-3:])

    def _copy(self, ex, k, p, landing):
        in_refs, out_refs, (send_sems, recv_sems, _) = ex
        pos, pid = _peers()[p]
        return pltpu.make_async_remote_copy(
            src_ref=in_refs[k].at[pid] if self.scatter[k] else in_refs[k],
            dst_ref=out_refs[k].at[pid if landing else _my_id()],
            send_sem=send_sems.at[p, k], recv_sem=recv_sems.at[p, k],
            device_id=pos, device_id_type=pl.DeviceIdType.MESH)

    def _own(self, ex, k):
        in_refs, out_refs, (_, _, local_sems) = ex
        me = _my_id()
        return pltpu.make_async_copy(in_refs[k].at[me] if self.scatter[k] else in_refs[k], out_refs[k].at[me],
                                     local_sems.at[k])

    def start(self, ex):
        for k in range(self.n):
            self._own(ex, k).start()
        for p in range(N_DEV - 1):
            for k in range(self.n):
                self._copy(ex, k, p, False).start()

    def finish(self, ex):
        for p in range(N_DEV - 1):
            for k in range(self.n):
                self._copy(ex, k, p, True).wait_recv()
        for p in range(N_DEV - 1):
            for k in range(self.n):
                self._copy(ex, k, p, False).wait_send()
        for k in range(self.n):
            self._own(ex, k).wait()


def _exchange(arrays, scatter, name):
    ex = _Exchange(arrays, scatter)

    def body(*refs):
        _, _, _, exr = ex.split(refs, 0, 0)
        ex.start(exr)
        ex.finish(exr)

    return pl.pallas_call(body, name=name, in_specs=ex.in_specs, out_specs=ex.out_specs,
                          out_shape=ex.out_shape, scratch_shapes=ex.scratch)(*ex.arrays)


def _gather_two_level(shard, name):
    def body(x_ref, out_ref, send_sems, recv_sems, local_sem):
        x, y, c = lax.axis_index("x"), lax.axis_index("y"), lax.axis_index("c")
        me, sibling = (x, y, c), (x, y, 1 - c)
        chips = [(1 - x, y), (x, 1 - y), (1 - x, 1 - y)]
        slab = lambda px, py, pc: out_ref.at[4 * px + 2 * py + pc]

        def copy(k, block, to, src=None):
            return pltpu.make_async_remote_copy(
                src_ref=slab(*block) if src is None else src, dst_ref=slab(*block),
                send_sem=send_sems.at[k], recv_sem=recv_sems.at[k],
                device_id=to, device_id_type=pl.DeviceIdType.MESH)

        mine = pltpu.make_async_copy(x_ref, slab(*me), local_sem)
        mine.start()
        first = [copy(0, me, sibling, src=x_ref)] + [copy(1 + j, me, (*chip, c), src=x_ref)
                                                     for j, chip in enumerate(chips)]
        for cp in first:
            cp.start()
        passed = [copy(4 + j, (*chip, c), sibling) for j, chip in enumerate(chips)]
        for j, chip in enumerate(chips):
            copy(1 + j, (*chip, c), me).wait_recv()
            passed[j].start()
        copy(0, sibling, me).wait_recv()
        for j, chip in enumerate(chips):
            copy(4 + j, (*chip, 1 - c), me).wait_recv()
        for cp in first + passed:
            cp.wait_send()
        mine.wait()

    hbm = pl.BlockSpec(memory_space=pltpu.HBM)
    return pl.pallas_call(
        body, name=name, in_specs=[hbm], out_specs=hbm,
        out_shape=SDS((N_DEV,) + shard.shape, shard.dtype),
        scratch_shapes=[pltpu.SemaphoreType.DMA((N_DEV - 1,)), pltpu.SemaphoreType.DMA((N_DEV - 1,)),
                        pltpu.SemaphoreType.DMA],
    )(shard)


def _adamw_update(g, w, m, v):
    c1 = 1.0 / (1.0 - ADAM_B1 ** ADAM_STEP)
    c2 = 1.0 / (1.0 - ADAM_B2 ** ADAM_STEP)
    mn = ADAM_B1 * m + (1.0 - ADAM_B1) * g
    vn = ADAM_B2 * v + (1.0 - ADAM_B2) * (g * g)
    return -ADAM_LR * ((mn * c1) / (jnp.sqrt(vn * c2) + ADAM_EPS) + ADAM_WD * w), mn, vn


def _adamw_small(parts, w, m, v):
    n = len(w)

    def body(*refs):
        p_refs = (refs[0:n], refs[n:2 * n])
        w_refs, m_refs, v_refs = refs[2 * n:3 * n], refs[3 * n:4 * n], refs[4 * n:5 * n]
        outs = refs[5 * n:]
        for k in range(n):
            g_ref, d_ref, mo_ref, vo_ref = outs[4 * k:4 * k + 4]
            for l in range(2):
                at = (slice(l, l + 1),) if len(w_refs[k].shape) == 2 else (l,)
                g = p_refs[l][k][0]
                for s in range(1, N_DEV):
                    g = g + p_refs[l][k][s]
                delta, mn, vn = _adamw_update(g, w_refs[k][at], m_refs[k][at], v_refs[k][at])
                g_ref[at] = g
                d_ref[at] = delta
                mo_ref[at] = mn
                vo_ref[at] = vn

    vmem = pl.BlockSpec(memory_space=pltpu.VMEM)
    res = pl.pallas_call(
        body, name="adamw_replicated", in_specs=[vmem] * (5 * n), out_specs=[vmem] * (4 * n),
        out_shape=[SDS(a.shape, F32) for a in w for _ in range(4)],
        compiler_params=pltpu.CompilerParams(vmem_limit_bytes=VMEM_LIMIT),
    )(*parts[0], *parts[1], *w, *m, *v)
    return [res[4 * k:4 * k + 4] for k in range(n)]


def _adamw_sum(parts0, parts1, w, m, v, name):
    _, R, C = w.shape
    tr = R
    while tr * C > 256 * 1024 and tr % 32 == 0:
        tr //= 2

    def body(p0_ref, p1_ref, w_ref, m_ref, v_ref, g_ref, d_ref, mo_ref, vo_ref):
        def update(p_ref):
            g = p_ref[0].astype(F32)
            for s in range(1, N_DEV):
                g = g + p_ref[s].astype(F32)
            g_ref[...] = g
            d_ref[...], mo_ref[...], vo_ref[...] = _adamw_update(g, w_ref[...], m_ref[...], v_ref[...])

        @pl.when(pl.program_id(0) == 0)
        def _():
            update(p0_ref)

        @pl.when(pl.program_id(0) == 1)
        def _():
            update(p1_ref)

    blk = pl.BlockSpec((None, tr, C), lambda l, i: (l, i, 0))
    return pl.pallas_call(
        body, name=name, grid=(2, R // tr),
        in_specs=[pl.BlockSpec((N_DEV, tr, C), lambda l, i: (0, i * (1 - l), 0)),
                  pl.BlockSpec((N_DEV, tr, C), lambda l, i: (0, i * l, 0)), blk, blk, blk],
        out_specs=[blk, blk, blk, blk],
        out_shape=[SDS((2, R, C), F32)] * 4,
        compiler_params=_cparams(("arbitrary", "arbitrary")),
    )(parts0, parts1, w, m, v)


def _layer_fwd(x, P, skew, S, rest, ex, tgt=None):
    z, ht, *got0 = _in_proj(x, P["pre_g"], P["w_in_t"], ex[0])
    P = {**P, **rest(got0)}
    ac, ap, u1, *got1 = _mix_fwd(z, P["conv_dw"], P["conv_dw_b"], P["conv_ln_g"], P["conv_ln_b"],
                                 P["pool_w"], P["pool_b"], P["pool_scale"], S, ex[1])
    o, *got2 = _attn_fwd(z, skew, S, ex[2])
    out, kept, got3 = _out_fwd(x, ac, o, ap, z, P["w_conv_out"], P["w_attn_out"], P["w_pool_out"], P["w_out"],
                               P["post_g"], ex[3], tgt)
    return out, (x, z, ht, ac, o, ap, u1, kept), P, (got0, got1, got2, got3)


def _layer_bwd(dout, saved, P, skew, S, ex, late_ex):
    x, z, ht, ac, o, ap, u1, kept = saved
    (dac, dao, dag, dap, dgm, dwco, dwao, dwpo, dwout, dpostg, *got0) = _out_bwd(
        dout, ac, o, ap, z, kept, P["w_conv_out"], P["w_attn_out"], P["w_pool_out"], P["w_out"], P["post_g"], ex[0])
    dq, dk, dv, dskew, *got1 = _attn_bwd(z, dao, skew, S, ex[1])
    (dzc, dzp, ddw, ddwb, dlng, dlnb, dpw, dpb, dps) = _mix_bwd(
        z, u1, dac, dap, P["conv_dw"], P["conv_dw_b"], P["conv_ln_g"], P["conv_ln_b"],
        P["pool_w"], P["pool_b"], P["pool_scale"], S)
    grads = dict(post_norm_g=dpostg, conv_dw=ddw, conv_dw_b=ddwb, conv_ln_g=dlng, conv_ln_b=dlnb,
                 w_conv_out=dwco, dskew=dskew, w_attn_out=dwao, pool_w=dpw, pool_b=dpb, pool_scale=dps,
                 w_pool_out=dwpo, w_out=dwout)
    pieces = [dzc, dq, dk, dv, dag, dzp, dgm]
    ex_rest, ex_win = late_ex(grads)
    buf, row0 = None, 0
    for p in pieces[:-1]:
        (buf,) = _in_bwd_w(ht, p, row0, buf)
        row0 += p.shape[1]
    grads["w_in_t"], *got2 = _in_bwd_w(ht, pieces[-1], row0, buf, ex_rest)
    dx, dpreg, *got3 = _in_bwd_x(pieces, P["w_in_t"], x, P["pre_g"], dout, ex_win(grads))
    grads["pre_norm_g"] = dpreg
    return dx, grads, (got0, got1, got2, got3)


WEIGHT_NAMES = ("pre_norm_g", "post_norm_g", "w_in", "conv_dw", "conv_dw_b", "conv_ln_g", "conv_ln_b",
                "w_conv_out", "rel_bias", "w_attn_out", "pool_w", "pool_b", "pool_scale", "w_pool_out", "w_out")
SHARDED = ("w_in", "w_conv_out", "w_attn_out", "w_pool_out", "w_out", "conv_dw")
REST = tuple(n for n in WEIGHT_NAMES if n not in ("w_in", "pre_norm_g"))


def _cols_from_slabs(g):
    return g.transpose(1, 0, 2).reshape(g.shape[1], N_DEV * g.shape[2])


def _slabs_from_cols(full):
    r, wd = full.shape
    return full.reshape(r, N_DEV, wd // N_DEV).transpose(1, 0, 2)


def _rest_shards(weights, l):
    return [weights["w_conv_out"][l].astype(BF16), weights["w_attn_out"][l].astype(BF16),
            weights["w_pool_out"][l].astype(BF16), weights["w_out"][l].astype(BF16), weights["conv_dw"][l]]


def _rest_weights(got):
    wco, wao, wpo, wout, cdw = got
    return dict(w_conv_out=_cols_from_slabs(wco), w_attn_out=_cols_from_slabs(wao),
                w_pool_out=_cols_from_slabs(wpo), w_out=wout.reshape(D, D), conv_dw=_cols_from_slabs(cdw))


def _grad_arrays(g, names):
    make = {"w_in": lambda: g["w_in_t"].reshape(N_DEV, NCOL // N_DEV, D),
            "w_conv_out": lambda: _slabs_from_cols(g["w_conv_out"].astype(BF16)),
            "w_attn_out": lambda: _slabs_from_cols(g["w_attn_out"].astype(BF16)),
            "w_pool_out": lambda: _slabs_from_cols(g["w_pool_out"].astype(BF16)),
            "w_out": lambda: g["w_out"].astype(BF16).reshape(N_DEV, D // N_DEV, D),
            "conv_dw": lambda: _slabs_from_cols(g["conv_dw"].astype(BF16)),
            "rel_bias": lambda: jnp.dot(g["dskew"], jnp.asarray(_skew_select().T), precision=lax.Precision.HIGHEST),
            "pool_b": lambda: g["pool_b"].reshape(4, 128)}
    return [make[n]() if n in make else g[n] for n in names]


def _grad_exchange(g, names):
    return _Exchange(_grad_arrays(g, names), [n in SHARDED for n in names])


def kernel(x, pre_norm_g, post_norm_g, w_in, conv_dw, conv_dw_b, conv_ln_g, conv_ln_b, w_conv_out, rel_bias, w_attn_out, pool_w, pool_b, pool_scale, w_pool_out, w_out, loss_target, m_pre_norm_g, m_post_norm_g, m_w_in, m_conv_dw, m_conv_dw_b, m_conv_ln_g, m_conv_ln_b, m_w_conv_out, m_rel_bias, m_w_attn_out, m_pool_w, m_pool_b, m_pool_scale, m_w_pool_out, m_w_out, v_pre_norm_g, v_post_norm_g, v_w_in, v_conv_dw, v_conv_dw_b, v_conv_ln_g, v_conv_ln_b, v_w_conv_out, v_rel_bias, v_w_attn_out, v_pool_w, v_pool_b, v_pool_scale, v_w_pool_out, v_w_out):
    given = dict(locals())
    weights = {n: given[n] for n in WEIGHT_NAMES}
    nb, S, _ = x.shape
    T = nb * S
    L = pre_norm_g.shape[0]
    assert L == 2
    x2 = x.reshape(T, D)
    tgt2 = loss_target.reshape(T, D)
    skews = [_skew_table(rel_bias[l]) for l in range(L)]

    def local_params(l):
        return dict(pre_g=pre_norm_g[l:l + 1], post_g=post_norm_g[l:l + 1], conv_dw_b=conv_dw_b[l:l + 1],
                    conv_ln_g=conv_ln_g[l:l + 1], conv_ln_b=conv_ln_b[l:l + 1], pool_w=pool_w[l].astype(BF16),
                    pool_b=pool_b[l].reshape(1, BW), pool_scale=pool_scale[l:l + 1])

    win0 = w_in[0].T.astype(BF16)
    win1 = w_in[1].T.astype(BF16)
    half = win1.shape[0] // 2
    w_in_t0 = _gather_two_level(win0, "gather_w_in_0")
    gather = lambda arrays: _Exchange(arrays, [False] * len(arrays))
    (h,), saved0, P0, (got_rest0, got_a, got_b, got_rest1) = _layer_fwd(
        x2, {**local_params(0), "w_in_t": w_in_t0.reshape(NCOL, D)}, skews[0], S, _rest_weights,
        (gather(_rest_shards(weights, 0)), gather([win1[:half]]), gather([win1[half:]]),
         gather(_rest_shards(weights, 1))))
    w_in_t1 = jnp.concatenate([got_a[0], got_b[0]], axis=1).reshape(NCOL, D)
    (dout, lsum), saved1, P1, _ = _layer_fwd(h, {**local_params(1), "w_in_t": w_in_t1}, skews[1], S,
                                             lambda _: _rest_weights(got_rest1), (None,) * 4, tgt2)
    loss = lax.psum(lsum[0, 0], MESH_AXES) * (0.5 / D)

    no_ex = lambda grads: (None, lambda g: None)
    dout, g1, _ = _layer_bwd(dout, saved1, P1, skews[1], S, (None, None), no_ex)
    late0 = lambda grads: (_grad_exchange(grads, REST), lambda g: _grad_exchange(g, ("w_in",)))
    dout, g0, (got_win1, got_rest1g, got_rest0g, got_win0) = _layer_bwd(
        dout, saved0, P0, skews[0], S,
        (_grad_exchange(g1, ("w_in",)), _grad_exchange(g1, REST + ("pre_norm_g",))), late0)
    (got_pre0,) = _exchange([g0["pre_norm_g"]], [False], "gather_grad_pre_norm_g_0")
    parts = [{"w_in": got_win0[0], "pre_norm_g": got_pre0, **dict(zip(REST, got_rest0g))},
             {"w_in": got_win1[0], **dict(zip(REST + ("pre_norm_g",), got_rest1g))}]
    grad_x = dout.reshape(x.shape)

    outs = {}
    small = [n for n in WEIGHT_NAMES if n not in SHARDED]
    res = _adamw_small([[parts[l][n] for n in small] for l in range(L)], [weights[n] for n in small],
                       [given["m_" + n] for n in small], [given["v_" + n] for n in small])
    outs.update(zip(small, res))
    for n in SHARDED:
        view = (lambda a: a.transpose(0, 2, 1)) if n == "w_in" else (lambda a: a)
        res = _adamw_sum(parts[0][n], parts[1][n], view(weights[n]), view(given["m_" + n]), view(given["v_" + n]),
                         "adamw_" + n)
        outs[n] = [view(a) for a in res]
    return (loss, grad_x, *[outs[n][0] for n in WEIGHT_NAMES], *[outs[n][1] for n in WEIGHT_NAMES],
            *[outs[n][2] for n in WEIGHT_NAMES], *[outs[n][3] for n in WEIGHT_NAMES])
```

```python
import functools

import numpy as np
import jax
import jax.numpy as jnp
from jax import lax
from jax.experimental import pallas as pl
from jax.experimental.pallas import tpu as pltpu

F32 = jnp.float32
BF16 = jnp.bfloat16
SDS = jax.ShapeDtypeStruct

D = 1024
BW = 512
NCOL = 7680
EPS = 1e-6
NEG = -1e30
HEADS = 8
HD = 64
CHUNK = 64
LEFT = 8
MAX_REL = 256
TQ = 256
KW = 768
CONV_K = 31
WINDOWS = (2, 4, 8, 16)
HALO = 32
RC = 32
N_DEV = 8
MESH_AXES = ("x", "y", "c")

ADAM_LR = 0.001
ADAM_B1 = 0.9
ADAM_B2 = 0.999
ADAM_EPS = 1e-08
ADAM_WD = 0.01
ADAM_STEP = 10

VMEM_LIMIT = 56 * 1024 * 1024

CB_CA, CB_CB, CB_CG, CB_Q, CB_K, CB_V, CB_AG, CB_PI, CB_PG = range(9)
DZ_PIECES = (("conv", 1536), ("q", 512), ("k", 512), ("v", 512), ("ag", 512), ("pool", 1024), ("gm", 3072))


def _cparams(sem):
    return pltpu.CompilerParams(dimension_semantics=sem, vmem_limit_bytes=VMEM_LIMIT)


def _sig(x):
    return 0.5 * jnp.tanh(0.5 * x) + 0.5


def _dsilu(x, s):
    return s * (1.0 + x * (1.0 - s))


def _nt(a, b):
    return lax.dot_general(a, b, (((1,), (1,)), ((), ())), preferred_element_type=F32)


def _tn(a, b):
    return lax.dot_general(a, b, (((0,), (0,)), ((), ())), preferred_element_type=F32)


def _nn(a, b):
    return jnp.dot(a, b, preferred_element_type=F32)


def _rows8(x):
    return x[0:8] + x[8:16] + x[16:24] + x[24:32]


def _call(body, name, grid, in_specs, out_specs, out_shape, scratch, args, sem, ex=None, aliases=None):
    aliases = aliases or {}
    if ex is None:
        return pl.pallas_call(body, name=name, grid=grid, in_specs=in_specs, out_specs=out_specs,
                              out_shape=out_shape, scratch_shapes=scratch, input_output_aliases=aliases,
                              compiler_params=_cparams(sem))(*args)
    n_in, n_out = len(in_specs), len(out_specs)
    steps = int(np.prod(grid))

    def carrier(*refs):
        own_in, own_out, own_scr, exr = ex.split(refs, n_in, n_out)
        step = pl.program_id(0)
        for axis in range(1, len(grid)):
            step = step * grid[axis] + pl.program_id(axis)

        @pl.when(step == 0)
        def _():
            ex.start(exr)

        body(*own_in, *own_out, *own_scr)

        @pl.when(step == steps - 1)
        def _():
            ex.finish(exr)

    return pl.pallas_call(
        carrier, name=name + "_carrier", grid=grid, in_specs=in_specs + ex.in_specs,
        out_specs=out_specs + ex.out_specs, out_shape=out_shape + ex.out_shape,
        scratch_shapes=scratch + ex.scratch, input_output_aliases=aliases,
        compiler_params=_cparams(("arbitrary",) * len(grid)),
    )(*args, *ex.arrays)


def _in_proj(x, g, wt, ex=None):
    T = x.shape[0]
    tm = 512
    tn = 1536

    def body(x_ref, g_ref, w_ref, z_ref, ht_ref):
        xv = x_ref[...]
        r = lax.rsqrt(jnp.mean(xv * xv, axis=-1, keepdims=True) + EPS)
        h = xv * r * g_ref[...]
        hb = h.astype(BF16)
        ht_ref[...] = h.T.astype(BF16)
        for c in range(NCOL // tn):
            z_ref[:, c * tn:(c + 1) * tn] = _nt(hb, w_ref[c * tn:(c + 1) * tn, :]).astype(BF16)

    return _call(
        body, "in_proj", (T // tm,),
        [pl.BlockSpec((tm, D), lambda i: (i, 0)), pl.BlockSpec((1, D), lambda i: (0, 0)),
         pl.BlockSpec((NCOL, D), lambda i: (0, 0))],
        [pl.BlockSpec((tm, NCOL), lambda i: (i, 0)), pl.BlockSpec((D, tm), lambda i: (0, i))],
        [SDS((T, NCOL), BF16), SDS((D, T), BF16)], [],
        (x, g, wt), ("parallel",), ex)


def _stencil(load, w_ref, lo, hi, tap_of):
    out = None
    for r in range(8):
        n = RC if r == 0 else RC + 8
        v = None
        for q in range((hi - r) // 8 + 1):
            o = 8 * q + r
            if o < lo:
                continue
            j = tap_of(o)
            term = w_ref[j:j + 1, :] * load(q, n)
            v = term if v is None else v + term
        if v is None:
            continue
        if r:
            v = pltpu.roll(v, n - r, axis=0)[0:RC]
        out = v if out is None else out + v
    return out


def _layer_norm_fwd(u1):
    mu = jnp.mean(u1, axis=-1, keepdims=True)
    xc = u1 - mu
    rstd = lax.rsqrt(jnp.mean(xc * xc, axis=-1, keepdims=True) + EPS)
    return xc * rstd, rstd


def _window_sums(x, w, back):
    n = x.shape[0]
    s = x
    k = 1
    while k < w:
        s = s + pltpu.roll(s, k if back else n - k, axis=0)
        k *= 2
    return s


def _pool_chunk(pwin, t_first):
    t = t_first + lax.broadcasted_iota(jnp.int32, (RC, 128), 0)
    outs = []
    for g, w in enumerate(WINDOWS):
        x = pwin[:, g * 128:(g + 1) * 128]
        s = _window_sums(x, w, True)
        cnt = jnp.minimum(t + 1, w).astype(F32)
        outs.append(s[HALO:HALO + RC] / cnt - x[HALO:HALO + RC])
    return outs


def _mix_fwd(z, dw, dwb, lng, lnb, pw, pb, ps, S, ex=None):
    T = z.shape[0]
    tm = min(S, 1024)
    ts = S // tm
    nh = tm // HALO

    def body(ca_ref, cb_ref, cg_ref, pi_ref, pg_ref, cah_ref, cbh_ref, pih_ref,
             dw_ref, dwb_ref, lng_ref, lnb_ref, pw_ref, pb_ref, ps_ref,
             ac_ref, ap_ref, u1_ref, ubuf, pbuf, pooled):
        i = pl.program_id(0)
        keep = jnp.where((i % ts) == 0, 0.0, 1.0)
        ubuf[0:HALO, :] = cah_ref[...].astype(F32) * _sig(cbh_ref[...].astype(F32)) * keep
        ubuf[HALO:HALO + tm, :] = ca_ref[...].astype(F32) * _sig(cb_ref[...].astype(F32))
        pbuf[0:HALO, :] = pih_ref[...].astype(F32) * keep
        pbuf[HALO:HALO + tm, :] = pi_ref[...].astype(F32)
        t0 = (i % ts) * tm

        def chunk(c, carry):
            base = pl.multiple_of(c * RC, RC)
            load = lambda q, n: ubuf[pl.ds(base + 8 * q, n), :]
            u1 = _stencil(load, dw_ref, 2, CONV_K + 1, lambda o: o - 2) + dwb_ref[...]
            u1_ref[pl.ds(base, RC), :] = u1
            n, _ = _layer_norm_fwd(u1)
            u2 = n * lng_ref[...] + lnb_ref[...]
            u3 = u2 * _sig(u2)
            cg = cg_ref[pl.ds(base, RC), :].astype(F32)
            ac_ref[pl.ds(base, RC), :] = (u3 * cg * _sig(cg)).astype(BF16)
            pwin = pbuf[pl.ds(base, RC + HALO), :]
            outs = _pool_chunk(pwin, t0 + base)
            for g in range(4):
                pooled[pl.ds(base, RC), g * 128:(g + 1) * 128] = outs[g].astype(BF16)
            return carry

        lax.fori_loop(0, tm // RC, chunk, 0, unroll=4)
        pg = pg_ref[...].astype(F32)
        spg = pg * _sig(pg)
        for g in range(4):
            sl = slice(g * 128, (g + 1) * 128)
            mixed = (_nn(pooled[:, sl], pw_ref[g]) + pb_ref[:, sl]) * ps_ref[:, sl]
            ap_ref[:, sl] = (mixed * spg[:, sl]).astype(BF16)

    def zmain(cb):
        return pl.BlockSpec((tm, BW), lambda i: (i, cb))

    def zprev(cb):
        return pl.BlockSpec((HALO, BW), lambda i: (jnp.maximum(i * nh - 1, 0), cb))

    full = lambda shape: pl.BlockSpec(shape, lambda i: (0,) * len(shape))
    row = pl.BlockSpec((tm, BW), lambda i: (i, 0))
    return _call(
        body, "mix_fwd", (T // tm,),
        [zmain(CB_CA), zmain(CB_CB), zmain(CB_CG), zmain(CB_PI), zmain(CB_PG),
         zprev(CB_CA), zprev(CB_CB), zprev(CB_PI),
         full((CONV_K, BW)), full((1, BW)), full((1, BW)), full((1, BW)),
         full((4, 128, 128)), full((1, BW)), full((1, BW))],
        [row, row, row], [SDS((T, BW), BF16), SDS((T, BW), BF16), SDS((T, BW), F32)],
        [pltpu.VMEM((HALO + tm, BW), F32), pltpu.VMEM((HALO + tm, BW), F32), pltpu.VMEM((tm, BW), BF16)],
        (z, z, z, z, z, z, z, z, dw, dwb, lng, lnb, pw, pb, ps), ("parallel",), ex)


def _attn_specs(nq):
    def kv(cb, off):
        return pl.BlockSpec((TQ, BW), lambda i: (i - jnp.minimum(off, i % nq), cb))
    return [pl.BlockSpec((TQ, BW), lambda i: (i, CB_Q)),
            kv(CB_K, 2), kv(CB_K, 1), kv(CB_K, 0), kv(CB_V, 2), kv(CB_V, 1), kv(CB_V, 0)]


def _softmax_rows(s):
    m = jnp.max(s, axis=-1, keepdims=True)
    e = jnp.exp(s - m)
    return e / jnp.sum(e, axis=-1, keepdims=True)


NSKEW = 1024


def _skew_table(table):
    return jnp.dot(table, jnp.asarray(_skew_select()), precision=lax.Precision.HIGHEST)


def _skew_select():
    d = np.arange(TQ + KW - 1)
    idx = np.clip(3 * TQ - 1 - d, -MAX_REL, MAX_REL) + MAX_REL
    sel = np.zeros((2 * MAX_REL + 1, NSKEW), np.float32)
    sel[idx, d] = 1.0
    return sel


def _bias_from_skew(f_ref, bias_scr):
    qi = lax.broadcasted_iota(jnp.int32, (TQ, KW), 0)
    kj = lax.broadcasted_iota(jnp.int32, (TQ, KW), 1)
    lo = (qi // CHUNK) * CHUNK
    band = jnp.where((kj >= lo) & (kj < lo + (LEFT + 1) * CHUNK), 0.0, NEG)
    for h in range(HEADS):
        rows = jnp.broadcast_to(f_ref[h:h + 1, :], (TQ, NSKEW))
        rows = pltpu.roll(rows, NSKEW - (TQ - 1), axis=1, stride=1, stride_axis=0)
        bias_scr[h] = rows[:, 0:KW] + band


def _skew_from_bias(db):
    i = lax.broadcasted_iota(jnp.int32, (TQ, TQ), 0)
    j = lax.broadcasted_iota(jnp.int32, (TQ, TQ), 1)
    flip = jnp.where(i + j == TQ - 1, 1.0, 0.0).astype(BF16)
    hi = db.astype(BF16)
    lo = (db - hi.astype(F32)).astype(BF16)
    rev = _nn(flip, hi) + _nn(flip, lo)
    rev = jnp.concatenate([rev, jnp.zeros((TQ, NSKEW - KW), F32)], axis=1)
    return jnp.sum(pltpu.roll(rev, 0, axis=1, stride=1, stride_axis=0), axis=0, keepdims=True)


def _attn_fwd(z, f, S, ex=None):
    T = z.shape[0]
    nq = S // TQ

    def body(q_ref, k2_ref, k1_ref, k0_ref, v2_ref, v1_ref, v0_ref, f_ref, o_ref, kbuf, vbuf, b_scr):
        @pl.when(pl.program_id(0) == 0)
        def _():
            _bias_from_skew(f_ref, b_scr)

        qb = pl.program_id(0) % nq
        kbuf[0:TQ, :] = k2_ref[...]
        kbuf[TQ:2 * TQ, :] = k1_ref[...]
        kbuf[2 * TQ:KW, :] = k0_ref[...]
        vbuf[0:TQ, :] = v2_ref[...]
        vbuf[TQ:2 * TQ, :] = v1_ref[...]
        vbuf[2 * TQ:KW, :] = v0_ref[...]
        lane = lax.broadcasted_iota(jnp.int32, (1, 128), 1)

        def attend(lo):
            def scores(h):
                sl = slice((h // 2) * 128, (h // 2 + 1) * 128)
                qp = q_ref[:, sl] * 0.125
                qm = jnp.where((lane < HD) if h % 2 == 0 else (lane >= HD), qp, jnp.zeros_like(qp))
                return _nt(qm, kbuf[lo:KW, sl]) + b_scr[h, :, lo:KW]

            s = scores(0)
            acc = None
            for h in range(HEADS):
                s_next = scores(h + 1) if h + 1 < HEADS else None
                sl = slice((h // 2) * 128, (h // 2 + 1) * 128)
                e = jnp.exp(s - jnp.max(s, axis=-1, keepdims=True))
                vp = vbuf[lo:KW, sl]
                vm = jnp.where((lane < HD) if h % 2 == 0 else (lane >= HD), vp, jnp.zeros_like(vp))
                o = _nn(e.astype(BF16), vm) * (1.0 / jnp.sum(e, axis=-1, keepdims=True))
                acc = o if h % 2 == 0 else acc + o
                if h % 2 == 1:
                    o_ref[:, sl] = acc.astype(BF16)
                s = s_next

        for nblk in (1, 2, 3):
            pl.when(jnp.minimum(qb, 2) == nblk - 1)(functools.partial(attend, (3 - nblk) * TQ))

    full = lambda shape: pl.BlockSpec(shape, lambda i: (0,) * len(shape))
    return _call(
        body, "attn_fwd", (T // TQ,),
        _attn_specs(nq) + [full((HEADS, NSKEW))],
        [pl.BlockSpec((TQ, BW), lambda i: (i, 0))], [SDS((T, BW), BF16)],
        [pltpu.VMEM((KW, BW), BF16), pltpu.VMEM((KW, BW), BF16), pltpu.VMEM((HEADS, TQ, KW), F32)],
        (z, z, z, z, z, z, z, f), ("arbitrary",), ex)


def _gates(gl_ref, gh_ref):
    gl = _sig(gl_ref[...].astype(F32))
    gh = _sig(gh_ref[...].astype(F32))
    return (gl[:, 0:D], jnp.concatenate([gl[:, D:1536], gh[:, 0:512]], axis=1), gh[:, 512:1536])


def _out_specs_in(tm):
    row = lambda w: pl.BlockSpec((tm, w), lambda i: (i, 0))
    full = lambda shape: pl.BlockSpec(shape, lambda i: (0,) * len(shape))
    return [row(BW), row(BW), row(BW),
            pl.BlockSpec((tm, BW), lambda i: (i, CB_AG)),
            pl.BlockSpec((tm, 1536), lambda i: (i, 3)),
            pl.BlockSpec((tm, 1536), lambda i: (i, 4)),
            full((BW, D)), full((BW, D)), full((BW, D)), full((D, D)), full((1, D))]


def _out_fwd(x, ac, o, ap, z, wco, wao, wpo, wout, postg, ex=None, tgt=None):
    T = x.shape[0]
    tm = 512
    last = tgt is not None

    def body(ac_ref, o_ref, ap_ref, ag_ref, gl_ref, gh_ref, wco_ref, wao_ref, wpo_ref, wout_ref, pg_ref,
             x_ref, *rest):
        ag = ag_ref[...].astype(F32)
        aat = (o_ref[...].astype(F32) * ag * _sig(ag)).astype(BF16)
        gates = _gates(gl_ref, gh_ref)
        acts = (ac_ref[...], aat, ap_ref[...])
        merged = None
        for b, w_ref in enumerate((wco_ref, wao_ref, wpo_ref)):
            yb = _nn(acts[b], w_ref[...])
            rest[-4 + b][...] = yb.astype(BF16)
            merged = gates[b] * yb if merged is None else merged + gates[b] * yb
        y = _nn(merged.astype(BF16), wout_ref[...])
        rest[-1][...] = y.astype(BF16)
        ry = lax.rsqrt(jnp.mean(y * y, axis=-1, keepdims=True) + EPS)
        out = x_ref[...] + y * ry * pg_ref[...]
        if not last:
            rest[0][...] = out
            return
        t_ref, d_ref, l_ref = rest[:3]

        @pl.when(pl.program_id(0) == 0)
        def _():
            l_ref[...] = jnp.zeros_like(l_ref)
        d = out - t_ref[...]
        d_ref[...] = d * (1.0 / D)
        l_ref[...] += jnp.sum(jnp.sum(d * d, axis=0, keepdims=True), axis=1, keepdims=True)

    row = pl.BlockSpec((tm, D), lambda i: (i, 0))
    kept_specs, kept_shapes = [row] * 4, [SDS((T, D), BF16)] * 4
    if not last:
        res = _call(body, "out_fwd", (T // tm,), _out_specs_in(tm) + [row], [row] + kept_specs,
                    [SDS((T, D), F32)] + kept_shapes, [],
                    (ac, o, ap, z, z, z, wco, wao, wpo, wout, postg, x), ("parallel",), ex)
        return res[:1], res[1:5], res[5:]
    res = _call(body, "out_fwd_loss", (T // tm,), _out_specs_in(tm) + [row, row],
                [row, pl.BlockSpec((1, 128), lambda i: (0, 0))] + kept_specs,
                [SDS((T, D), F32), SDS((1, 128), F32)] + kept_shapes, [],
                (ac, o, ap, z, z, z, wco, wao, wpo, wout, postg, x, tgt), ("arbitrary",), ex)
    return res[:2], res[2:6], res[6:]


def _out_bwd(dout, ac, o, ap, z, kept, wco, wao, wpo, wout, postg, ex=None):
    T = dout.shape[0]
    tm = 256

    def body(ac_ref, o_ref, ap_ref, ag_ref, gl_ref, gh_ref, wco_ref, wao_ref, wpo_ref, wout_ref, pg_ref, do_ref,
             yc_ref, ya_ref, yp_ref, y_ref,
             dac_ref, dao_ref, dag_ref, dap_ref, dgm_ref, dwco_ref, dwao_ref, dwpo_ref, dwout_ref, dpg_ref):
        @pl.when(pl.program_id(0) == 0)
        def _():
            for r in (dwco_ref, dwao_ref, dwpo_ref, dwout_ref, dpg_ref):
                r[...] = jnp.zeros_like(r)

        ag = ag_ref[...].astype(F32)
        sag = _sig(ag)
        ov = o_ref[...].astype(F32)
        acts = (ac_ref[...], (ov * ag * sag).astype(BF16), ap_ref[...])
        ws = (wco_ref, wao_ref, wpo_ref)
        gates = _gates(gl_ref, gh_ref)
        ys = [r[...].astype(F32) for r in (yc_ref, ya_ref, yp_ref)]
        merged = (gates[0] * ys[0] + gates[1] * ys[1] + gates[2] * ys[2]).astype(BF16)
        y = y_ref[...].astype(F32)
        ry = lax.rsqrt(jnp.mean(y * y, axis=-1, keepdims=True) + EPS)
        yn = y * ry
        dout_v = do_ref[...]
        dpg_ref[...] += jnp.sum(dout_v * yn, axis=0, keepdims=True)
        dyn = dout_v * pg_ref[...]
        dy = (ry * (dyn - yn * jnp.mean(dyn * yn, axis=-1, keepdims=True))).astype(BF16)
        dmerged = _nt(dy, wout_ref[...])
        dwout_ref[...] += _tn(merged, dy)
        dws = (dwco_ref, dwao_ref, dwpo_ref)
        das = []
        for b in range(3):
            gb = gates[b]
            dgm_ref[:, b * D:(b + 1) * D] = (dmerged * ys[b] * gb * (1.0 - gb)).astype(BF16)
            dyb = (dmerged * gb).astype(BF16)
            dws[b][...] += _tn(acts[b], dyb)
            das.append(_nt(dyb, ws[b][...]))
        dac_ref[...] = das[0].astype(BF16)
        dap_ref[...] = das[2].astype(BF16)
        dao_ref[...] = (das[1] * ag * sag).astype(BF16)
        dag_ref[...] = (das[1] * ov * _dsilu(ag, sag)).astype(BF16)

    row = lambda w: pl.BlockSpec((tm, w), lambda i: (i, 0))
    full = lambda shape: pl.BlockSpec(shape, lambda i: (0,) * len(shape))
    return _call(
        body, "out_bwd", (T // tm,), _out_specs_in(tm) + [row(D)] * 5,
        [row(BW), row(BW), row(BW), row(BW), row(3 * D),
         full((BW, D)), full((BW, D)), full((BW, D)), full((D, D)), full((1, D))],
        [SDS((T, BW), BF16)] * 4 + [SDS((T, 3 * D), BF16)]
        + [SDS((BW, D), F32)] * 3 + [SDS((D, D), F32), SDS((1, D), F32)], [],
        (ac, o, ap, z, z, z, wco, wao, wpo, wout, postg, dout, *kept), ("arbitrary",), ex)


def _attn_bwd(z, dao, f, S, ex=None):
    T = z.shape[0]
    nq = S // TQ
    nsteps = T // TQ

    def body(q_ref, k2_ref, k1_ref, k0_ref, v2_ref, v1_ref, v0_ref, do_ref, f_ref,
             dq_ref, dk_ref, dv_ref, df_ref, kbuf, vbuf, dkacc, dvacc, b_scr, db_scr):
        i = pl.program_id(0)
        qb = i % nq

        @pl.when(i == 0)
        def _():
            _bias_from_skew(f_ref, b_scr)
            db_scr[...] = jnp.zeros_like(db_scr)

        @pl.when(qb == 0)
        def _():
            dkacc[...] = jnp.zeros_like(dkacc)
            dvacc[...] = jnp.zeros_like(dvacc)

        kbuf[0:TQ, :] = k2_ref[...]
        kbuf[TQ:2 * TQ, :] = k1_ref[...]
        kbuf[2 * TQ:KW, :] = k0_ref[...]
        vbuf[0:TQ, :] = v2_ref[...]
        vbuf[TQ:2 * TQ, :] = v1_ref[...]
        vbuf[2 * TQ:KW, :] = v0_ref[...]
        lane = lax.broadcasted_iota(jnp.int32, (1, 128), 1)
        row0 = pl.multiple_of(qb * TQ, TQ)

        def attend(lo):
            def first_matmuls(h):
                sl = slice((h // 2) * 128, (h // 2 + 1) * 128)
                msk = (lane < HD) if h % 2 == 0 else (lane >= HD)
                qp = q_ref[:, sl] * 0.125
                dop = do_ref[:, sl]
                qm = jnp.where(msk, qp, jnp.zeros_like(qp))
                dom = jnp.where(msk, dop, jnp.zeros_like(dop))
                s = _nt(qm, kbuf[lo:KW, sl]) + b_scr[h, :, lo:KW]
                return s, _nt(dom, vbuf[lo:KW, sl]), qm, dom

            cur = first_matmuls(0)
            dq_acc = dk_acc = dv_acc = None
            for h in range(HEADS):
                nxt = first_matmuls(h + 1) if h + 1 < HEADS else None
                s, dp, qm, dom = cur
                sl = slice((h // 2) * 128, (h // 2 + 1) * 128)
                e = jnp.exp(s - jnp.max(s, axis=-1, keepdims=True))
                p = e * (1.0 / jnp.sum(e, axis=-1, keepdims=True))
                ds = p * (dp - jnp.sum(p * dp, axis=-1, keepdims=True))
                db_scr[h, :, lo:KW] += ds
                dsb = ds.astype(BF16)
                kp = kbuf[lo:KW, sl]
                km = jnp.where((lane < HD) if h % 2 == 0 else (lane >= HD), kp, jnp.zeros_like(kp))
                dq_h = _nn(dsb, km) * 0.125
                dk_h = _tn(dsb, qm)
                dv_h = _tn(p.astype(BF16), dom)
                if h % 2 == 0:
                    dq_acc, dk_acc, dv_acc = dq_h, dk_h, dv_h
                else:
                    dq_ref[:, sl] = (dq_acc + dq_h).astype(BF16)
                    dkacc[pl.ds(row0 + lo, KW - lo), sl] += dk_acc + dk_h
                    dvacc[pl.ds(row0 + lo, KW - lo), sl] += dv_acc + dv_h
                cur = nxt

        for nblk in (1, 2, 3):
            pl.when(jnp.minimum(qb, 2) == nblk - 1)(functools.partial(attend, (3 - nblk) * TQ))

        @pl.when(qb == nq - 1)
        def _():
            dk_ref[...] = dkacc[2 * TQ:2 * TQ + S, :].astype(BF16)
            dv_ref[...] = dvacc[2 * TQ:2 * TQ + S, :].astype(BF16)

        @pl.when(i == nsteps - 1)
        def _():
            for h in range(HEADS):
                df_ref[h:h + 1, :] = _skew_from_bias(db_scr[h])

    full = lambda shape: pl.BlockSpec(shape, lambda i: (0,) * len(shape))
    return _call(
        body, "attn_bwd", (nsteps,),
        _attn_specs(nq) + [pl.BlockSpec((TQ, BW), lambda i: (i, 0)), full((HEADS, NSKEW))],
        [pl.BlockSpec((TQ, BW), lambda i: (i, 0)), pl.BlockSpec((S, BW), lambda i: (i // nq, 0)),
         pl.BlockSpec((S, BW), lambda i: (i // nq, 0)), full((HEADS, NSKEW))],
        [SDS((T, BW), BF16)] * 3 + [SDS((HEADS, NSKEW), F32)],
        [pltpu.VMEM((KW, BW), BF16), pltpu.VMEM((KW, BW), BF16),
         pltpu.VMEM((S + 2 * TQ, BW), F32), pltpu.VMEM((S + 2 * TQ, BW), F32),
         pltpu.VMEM((HEADS, TQ, KW), F32), pltpu.VMEM((HEADS, TQ, KW), F32)],
        (z, z, z, z, z, z, z, dao, f), ("arbitrary",), ex)


def _mix_bwd(z, u1, dac, dap, dw, dwb, lng, lnb, pw, pb, ps, S):
    T = z.shape[0]
    tm = min(S, 1024)
    ts = S // tm
    nh = tm // HALO
    nsteps = T // tm
    nblk32 = T // HALO

    def body(ca_ref, cb_ref, cg_ref, pi_ref, pg_ref, u1_ref, dac_ref, dap_ref,
             cah_ref, cbh_ref, pih_ref,
             cgn_ref, pgn_ref, u1n_ref, dacn_ref, dapn_ref,
             dw_ref, dwb_ref, lng_ref, lnb_ref, pw_ref, pb_ref, ps_ref,
             dzc_ref, dzp_ref, ddw_ref, ddwb_ref, dlng_ref, dlnb_ref, dpw_ref, dpb_ref, dps_ref,
             ubuf, gbuf, pbuf, qbuf, pooled, *accs):
        tap_acc, (lng_acc, lnb_acc, dwb_acc) = accs[:CONV_K], accs[CONV_K:]
        i = pl.program_id(0)
        keep_prev = jnp.where((i % ts) == 0, 0.0, 1.0)
        keep_next = jnp.where((i % ts) == ts - 1, 0.0, 1.0)
        t0 = (i % ts) * tm

        @pl.when(i == 0)
        def _():
            for a in accs:
                a[...] = jnp.zeros_like(a)
            dpw_ref[...] = jnp.zeros_like(dpw_ref)
            dpb_ref[...] = jnp.zeros_like(dpb_ref)
            dps_ref[...] = jnp.zeros_like(dps_ref)

        ubuf[0:HALO, :] = cah_ref[...].astype(F32) * _sig(cbh_ref[...].astype(F32)) * keep_prev
        ubuf[HALO:HALO + tm, :] = ca_ref[...].astype(F32) * _sig(cb_ref[...].astype(F32))
        pbuf[0:HALO, :] = pih_ref[...].astype(F32) * keep_prev
        pbuf[HALO:HALO + tm, :] = pi_ref[...].astype(F32)

        def norm_back(u1v, cg, dacv):
            n, rstd = _layer_norm_fwd(u1v)
            u2 = n * lng_ref[...] + lnb_ref[...]
            s2 = _sig(u2)
            scg = _sig(cg)
            du2 = dacv * cg * scg * _dsilu(u2, s2)
            dn = du2 * lng_ref[...]
            du1 = rstd * (dn - jnp.mean(dn, axis=-1, keepdims=True)
                          - n * jnp.mean(dn * n, axis=-1, keepdims=True))
            return du1, du2, n, dacv * u2 * s2 * _dsilu(cg, scg)

        def chunk_a(c, carry):
            base = pl.multiple_of(c * RC, RC)
            du1, du2, n, dcg = norm_back(u1_ref[pl.ds(base, RC), :], cg_ref[pl.ds(base, RC), :].astype(F32),
                                         dac_ref[pl.ds(base, RC), :].astype(F32))
            gbuf[pl.ds(base, RC), :] = du1
            dzc_ref[pl.ds(base, RC), 2 * BW:3 * BW] = dcg.astype(BF16)
            lng_acc[...] += _rows8(du2 * n)
            lnb_acc[...] += _rows8(du2)
            dwb_acc[...] += _rows8(du1)
            padded = jnp.concatenate([du1, jnp.zeros((8, BW), F32)], axis=0)
            for r in range(8):
                nrow = RC if r == 0 else RC + 8
                g = du1 if r == 0 else pltpu.roll(padded, r, axis=0)
                for q in range((CONV_K + 1 - r) // 8 + 1):
                    o = 8 * q + r
                    if o < 2:
                        continue
                    prod = g * ubuf[pl.ds(base + 8 * q, nrow), :]
                    red = prod[0:8]
                    for k in range(1, nrow // 8):
                        red = red + prod[8 * k:8 * k + 8]
                    tap_acc[o - 2][...] += red
            return carry

        lax.fori_loop(0, tm // RC, chunk_a, 0, unroll=4)
        du1n, _, _, _ = norm_back(u1n_ref[...], cgn_ref[...].astype(F32), dacn_ref[...].astype(F32))
        gbuf[tm:tm + HALO, :] = du1n * keep_next

        def chunk_p(c, carry):
            base = pl.multiple_of(c * RC, RC)
            outs = _pool_chunk(pbuf[pl.ds(base, RC + HALO), :], t0 + base)
            for g in range(4):
                pooled[pl.ds(base, RC), g * 128:(g + 1) * 128] = outs[g].astype(BF16)
            return carry

        lax.fori_loop(0, tm // RC, chunk_p, 0)

        def cnt_of(t_first, rows, w):
            t = t_first + lax.broadcasted_iota(jnp.int32, (rows, 128), 0)
            return jnp.minimum(t + 1, w).astype(F32)

        pg = pg_ref[...].astype(F32)
        spg_s = _sig(pg)
        dapv = dap_ref[...].astype(F32)
        pgn = pgn_ref[...].astype(F32)
        dmixn = dapn_ref[...].astype(F32) * pgn * _sig(pgn) * ps_ref[...] * keep_next
        for g, w in enumerate(WINDOWS):
            sl = slice(g * 128, (g + 1) * 128)
            mixed_u = _nn(pooled[:, sl], pw_ref[g]) + pb_ref[:, sl]
            dap_g = dapv[:, sl]
            pg_g = pg[:, sl]
            s_g = spg_s[:, sl]
            silu_g = pg_g * s_g
            dps_ref[:, sl] += jnp.sum(dap_g * silu_g * mixed_u, axis=0, keepdims=True)
            dzp_ref[:, BW + g * 128:BW + (g + 1) * 128] = (
                dap_g * mixed_u * ps_ref[:, sl] * _dsilu(pg_g, s_g)).astype(BF16)
            dmix = dap_g * silu_g * ps_ref[:, sl]
            dpb_ref[:, sl] += jnp.sum(dmix, axis=0, keepdims=True)
            dmixb = dmix.astype(BF16)
            dpw_ref[g] += _tn(pooled[:, sl], dmixb)
            qbuf[0:tm, sl] = _nt(dmixb, pw_ref[g]) / cnt_of(t0, tm, w)
            qbuf[tm:tm + HALO, sl] = _nt(dmixn[:, sl].astype(BF16), pw_ref[g]) / cnt_of(t0 + tm, HALO, w)

        def chunk_b(c, carry):
            base = pl.multiple_of(c * RC, RC)
            load = lambda q, n: gbuf[pl.ds(base + 8 * q, n), :]
            du0 = _stencil(load, dw_ref, 0, CONV_K - 1, lambda o: CONV_K - 1 - o)
            ca = ca_ref[pl.ds(base, RC), :].astype(F32)
            sb = _sig(cb_ref[pl.ds(base, RC), :].astype(F32))
            dzc_ref[pl.ds(base, RC), 0:BW] = (du0 * sb).astype(BF16)
            dzc_ref[pl.ds(base, RC), BW:2 * BW] = (du0 * ca * sb * (1.0 - sb)).astype(BF16)
            qwin = qbuf[pl.ds(base, RC + HALO), :]
            t = t0 + base + lax.broadcasted_iota(jnp.int32, (RC, 128), 0)
            for g, w in enumerate(WINDOWS):
                x = qwin[:, g * 128:(g + 1) * 128]
                s = _window_sums(x, w, False)
                cnt = jnp.minimum(t + 1, w).astype(F32)
                dzp_ref[pl.ds(base, RC), g * 128:(g + 1) * 128] = (s[0:RC] - cnt * x[0:RC]).astype(BF16)
            return carry

        lax.fori_loop(0, tm // RC, chunk_b, 0)

        @pl.when(i == nsteps - 1)
        def _():
            dlng_ref[...] = jnp.sum(lng_acc[...], axis=0, keepdims=True)
            dlnb_ref[...] = jnp.sum(lnb_acc[...], axis=0, keepdims=True)
            ddwb_ref[...] = jnp.sum(dwb_acc[...], axis=0, keepdims=True)
            for j in range(CONV_K):
                ddw_ref[j:j + 1, :] = jnp.sum(tap_acc[j][...], axis=0, keepdims=True)

    def zmain(cb):
        return pl.BlockSpec((tm, BW), lambda i: (i, cb))

    def zprev(cb):
        return pl.BlockSpec((HALO, BW), lambda i: (jnp.maximum(i * nh - 1, 0), cb))

    def znext(cb):
        return pl.BlockSpec((HALO, BW), lambda i: (jnp.minimum((i + 1) * nh, nblk32 - 1), cb))

    row = lambda w: pl.BlockSpec((tm, w), lambda i: (i, 0))
    full = lambda shape: pl.BlockSpec(shape, lambda i: (0,) * len(shape))
    return pl.pallas_call(
        body, name="mix_bwd", grid=(nsteps,),
        in_specs=[zmain(CB_CA), zmain(CB_CB), zmain(CB_CG), zmain(CB_PI), zmain(CB_PG), row(BW), row(BW), row(BW),
                  zprev(CB_CA), zprev(CB_CB), zprev(CB_PI),
                  znext(CB_CG), znext(CB_PG), znext(0), znext(0), znext(0),
                  full((CONV_K, BW)), full((1, BW)), full((1, BW)), full((1, BW)),
                  full((4, 128, 128)), full((1, BW)), full((1, BW))],
        out_specs=[row(3 * BW), row(2 * BW), full((CONV_K, BW)), full((1, BW)), full((1, BW)), full((1, BW)),
                   full((4, 128, 128)), full((1, BW)), full((1, BW))],
        out_shape=[SDS((T, 3 * BW), BF16), SDS((T, 2 * BW), BF16), SDS((CONV_K, BW), F32),
                   SDS((1, BW), F32), SDS((1, BW), F32), SDS((1, BW), F32),
                   SDS((4, 128, 128), F32), SDS((1, BW), F32), SDS((1, BW), F32)],
        scratch_shapes=[pltpu.VMEM((HALO + tm, BW), F32), pltpu.VMEM((tm + HALO, BW), F32),
                        pltpu.VMEM((HALO + tm, BW), F32), pltpu.VMEM((tm + HALO, BW), F32),
                        pltpu.VMEM((tm, BW), BF16)] + [pltpu.VMEM((8, BW), F32)] * (CONV_K + 3),
        compiler_params=_cparams(("arbitrary",)),
    )(z, z, z, z, z, u1, dac, dap, z, z, z, z, z, u1, dac, dap, dw, dwb, lng, lnb, pw, pb, ps)


def _in_bwd_x(pieces, wt, x, g, dout, ex=None):
    T = x.shape[0]
    tm = 512
    widths = [p.shape[1] for p in pieces]
    offs = np.cumsum([0] + widths)
    npc = len(pieces)

    def body(*refs):
        p_refs = refs[:npc]
        w_ref, x_ref, g_ref, do_ref, dx_ref, dg_ref = refs[npc:]

        @pl.when(pl.program_id(0) == 0)
        def _():
            dg_ref[...] = jnp.zeros_like(dg_ref)

        dh = None
        for k in range(npc):
            t = _nn(p_refs[k][...], w_ref[int(offs[k]):int(offs[k + 1]), :])
            dh = t if dh is None else dh + t
        xv = x_ref[...]
        r = lax.rsqrt(jnp.mean(xv * xv, axis=-1, keepdims=True) + EPS)
        xn = xv * r
        dg_ref[...] += jnp.sum(dh * xn, axis=0, keepdims=True)
        dxn = dh * g_ref[...]
        dx_ref[...] = do_ref[...] + r * (dxn - xn * jnp.mean(dxn * xn, axis=-1, keepdims=True))

    row = lambda wd: pl.BlockSpec((tm, wd), lambda i: (i, 0))
    return _call(
        body, "in_bwd_x", (T // tm,),
        [row(wd) for wd in widths] + [pl.BlockSpec((NCOL, D), lambda i: (0, 0)),
                                      row(D), pl.BlockSpec((1, D), lambda i: (0, 0)), row(D)],
        [row(D), pl.BlockSpec((1, D), lambda i: (0, 0))], [SDS((T, D), F32), SDS((1, D), F32)], [],
        (*pieces, wt, x, g, dout), ("arbitrary",), ex)


def _in_bwd_w(ht, piece, row0, buf=None, ex=None):
    T = ht.shape[1]
    wd = piece.shape[1]
    tn = next(t for t in (1536, 1024, 768, 512) if wd % t == 0 and row0 % t == 0)
    tk = min(T, 2048)
    nk = T // tk
    j0 = row0 // tn

    def body(ht_ref, p_ref, *rest):
        o_ref, acc = rest[-2:]
        k = pl.program_id(1)

        @pl.when(k == 0)
        def _():
            acc[...] = jnp.zeros_like(acc)
        acc[...] += _nn(ht_ref[...], p_ref[...])

        @pl.when(k == nk - 1)
        def _():
            o_ref[...] = acc[...].T.astype(BF16)

    in_specs = [pl.BlockSpec((D, tk), lambda j, k: (0, k)), pl.BlockSpec((tk, tn), lambda j, k: (k, j))]
    args = (ht, piece)
    if buf is not None:
        in_specs.append(pl.BlockSpec(memory_space=pl.ANY))
        args += (buf,)
    return _call(
        body, "in_bwd_w", (wd // tn, nk), in_specs,
        [pl.BlockSpec((tn, D), lambda j, k: (j + j0, 0))], [SDS((NCOL, D), BF16)], [pltpu.VMEM((D, tn), F32)],
        args, ("parallel", "arbitrary"), ex, None if buf is None else {2: 0})


def _my_id():
    return 4 * lax.axis_index("x") + 2 * lax.axis_index("y") + lax.axis_index("c")


def _peers():
    x, y, c = lax.axis_index("x"), lax.axis_index("y"), lax.axis_index("c")
    out = []
    for k in range(1, N_DEV):
        fx, fy, fc = (k >> 2) & 1, (k >> 1) & 1, k & 1
        px, py, pc = x ^ fx, y ^ fy, c ^ fc
        out.append(((px, py, pc), 4 * px + 2 * py + pc))
    return out


class _Exchange:
    def __init__(self, arrays, scatter):
        self.arrays = list(arrays)
        self.scatter = list(scatter)
        self.n = n = len(arrays)
        hbm = pl.BlockSpec(memory_space=pltpu.HBM)
        self.in_specs = [hbm] * n
        self.out_specs = [hbm] * n
        self.out_shape = [SDS((N_DEV,) + tuple(a.shape[1:] if s else a.shape), a.dtype)
                          for a, s in zip(arrays, scatter)]
        self.scratch = [pltpu.SemaphoreType.DMA((N_DEV - 1, n)), pltpu.SemaphoreType.DMA((N_DEV - 1, n)),
                        pltpu.SemaphoreType.DMA((n,))]

    def split(self, refs, n_in, n_out):
        n = self.n
        own_in = refs[:n_in]
        ex_in = refs[n_in:n_in + n]
        own_out = refs[n_in + n:n_in + n + n_out]
        ex_out = refs[n_in + n + n_out:n_in + 2 * n + n_out]
        rest = refs[n_in + 2 * n + n_out:]
        return own_in, own_out, rest[:-3], (ex_in, ex_out, rest[-3:])

    def _copy(self, ex, k, p, landing):
        in_refs, out_refs, (send_sems, recv_sems, _) = ex
        pos, pid = _peers()[p]
        return pltpu.make_async_remote_copy(
            src_ref=in_refs[k].at[pid] if self.scatter[k] else in_refs[k],
            dst_ref=out_refs[k].at[pid if landing else _my_id()],
            send_sem=send_sems.at[p, k], recv_sem=recv_sems.at[p, k],
            device_id=pos, device_id_type=pl.DeviceIdType.MESH)

    def _own(self, ex, k):
        in_refs, out_refs, (_, _, local_sems) = ex
        me = _my_id()
        return pltpu.make_async_copy(in_refs[k].at[me] if self.scatter[k] else in_refs[k], out_refs[k].at[me],
                                     local_sems.at[k])

    def start(self, ex):
        for k in range(self.n):
            self._own(ex, k).start()
        for p in range(N_DEV - 1):
            for k in range(self.n):
                self._copy(ex, k, p, False).start()

    def finish(self, ex):
        for p in range(N_DEV - 1):
            for k in range(self.n):
                self._copy(ex, k, p, True).wait_recv()
        for p in range(N_DEV - 1):
            for k in range(self.n):
                self._copy(ex, k, p, False).wait_send()
        for k in range(self.n):
            self._own(ex, k).wait()


def _exchange(arrays, scatter, name):
    ex = _Exchange(arrays, scatter)

    def body(*refs):
        _, _, _, exr = ex.split(refs, 0, 0)
        ex.start(exr)
        ex.finish(exr)

    return pl.pallas_call(body, name=name, in_specs=ex.in_specs, out_specs=ex.out_specs,
                          out_shape=ex.out_shape, scratch_shapes=ex.scratch)(*ex.arrays)


def _gather_two_level(shard, name):
    def body(x_ref, out_ref, send_sems, recv_sems, local_sem):
        x, y, c = lax.axis_index("x"), lax.axis_index("y"), lax.axis_index("c")
        me, sibling = (x, y, c), (x, y, 1 - c)
        chips = [(1 - x, y), (x, 1 - y), (1 - x, 1 - y)]
        slab = lambda px, py, pc: out_ref.at[4 * px + 2 * py + pc]

        def copy(k, block, to, src=None):
            return pltpu.make_async_remote_copy(
                src_ref=slab(*block) if src is None else src, dst_ref=slab(*block),
                send_sem=send_sems.at[k], recv_sem=recv_sems.at[k],
                device_id=to, device_id_type=pl.DeviceIdType.MESH)

        mine = pltpu.make_async_copy(x_ref, slab(*me), local_sem)
        mine.start()
        first = [copy(0, me, sibling, src=x_ref)] + [copy(1 + j, me, (*chip, c), src=x_ref)
                                                     for j, chip in enumerate(chips)]
        for cp in first:
            cp.start()
        passed = [copy(4 + j, (*chip, c), sibling) for j, chip in enumerate(chips)]
        for j, chip in enumerate(chips):
            copy(1 + j, (*chip, c), me).wait_recv()
            passed[j].start()
        copy(0, sibling, me).wait_recv()
        for j, chip in enumerate(chips):
            copy(4 + j, (*chip, 1 - c), me).wait_recv()
        for cp in first + passed:
            cp.wait_send()
        mine.wait()

    hbm = pl.BlockSpec(memory_space=pltpu.HBM)
    return pl.pallas_call(
        body, name=name, in_specs=[hbm], out_specs=hbm,
        out_shape=SDS((N_DEV,) + shard.shape, shard.dtype),
        scratch_shapes=[pltpu.SemaphoreType.DMA((N_DEV - 1,)), pltpu.SemaphoreType.DMA((N_DEV - 1,)),
                        pltpu.SemaphoreType.DMA],
    )(shard)


def _adamw_update(g, w, m, v):
    c1 = 1.0 / (1.0 - ADAM_B1 ** ADAM_STEP)
    c2 = 1.0 / (1.0 - ADAM_B2 ** ADAM_STEP)
    mn = ADAM_B1 * m + (1.0 - ADAM_B1) * g
    vn = ADAM_B2 * v + (1.0 - ADAM_B2) * (g * g)
    return -ADAM_LR * ((mn * c1) / (jnp.sqrt(vn * c2) + ADAM_EPS) + ADAM_WD * w), mn, vn


def _adamw_small(parts, w, m, v):
    n = len(w)

    def body(*refs):
        p_refs = (refs[0:n], refs[n:2 * n])
        w_refs, m_refs, v_refs = refs[2 * n:3 * n], refs[3 * n:4 * n], refs[4 * n:5 * n]
        outs = refs[5 * n:]
        for k in range(n):
            g_ref, d_ref, mo_ref, vo_ref = outs[4 * k:4 * k + 4]
            for l in range(2):
                at = (slice(l, l + 1),) if len(w_refs[k].shape) == 2 else (l,)
                g = p_refs[l][k][0]
                for s in range(1, N_DEV):
                    g = g + p_refs[l][k][s]
                delta, mn, vn = _adamw_update(g, w_refs[k][at], m_refs[k][at], v_refs[k][at])
                g_ref[at] = g
                d_ref[at] = delta
                mo_ref[at] = mn
                vo_ref[at] = vn

    vmem = pl.BlockSpec(memory_space=pltpu.VMEM)
    res = pl.pallas_call(
        body, name="adamw_replicated", in_specs=[vmem] * (5 * n), out_specs=[vmem] * (4 * n),
        out_shape=[SDS(a.shape, F32) for a in w for _ in range(4)],
        compiler_params=pltpu.CompilerParams(vmem_limit_bytes=VMEM_LIMIT),
    )(*parts[0], *parts[1], *w, *m, *v)
    return [res[4 * k:4 * k + 4] for k in range(n)]


def _adamw_sum(parts0, parts1, w, m, v, name):
    _, R, C = w.shape
    tr = R
    while tr * C > 256 * 1024 and tr % 32 == 0:
        tr //= 2

    def body(p0_ref, p1_ref, w_ref, m_ref, v_ref, g_ref, d_ref, mo_ref, vo_ref):
        def update(p_ref):
            g = p_ref[0].astype(F32)
            for s in range(1, N_DEV):
                g = g + p_ref[s].astype(F32)
            g_ref[...] = g
            d_ref[...], mo_ref[...], vo_ref[...] = _adamw_update(g, w_ref[...], m_ref[...], v_ref[...])

        @pl.when(pl.program_id(0) == 0)
        def _():
            update(p0_ref)

        @pl.when(pl.program_id(0) == 1)
        def _():
            update(p1_ref)

    blk = pl.BlockSpec((None, tr, C), lambda l, i: (l, i, 0))
    return pl.pallas_call(
        body, name=name, grid=(2, R // tr),
        in_specs=[pl.BlockSpec((N_DEV, tr, C), lambda l, i: (0, i * (1 - l), 0)),
                  pl.BlockSpec((N_DEV, tr, C), lambda l, i: (0, i * l, 0)), blk, blk, blk],
        out_specs=[blk, blk, blk, blk],
        out_shape=[SDS((2, R, C), F32)] * 4,
        compiler_params=_cparams(("arbitrary", "arbitrary")),
    )(parts0, parts1, w, m, v)


def _layer_fwd(x, P, skew, S, rest, ex, tgt=None):
    z, ht, *got0 = _in_proj(x, P["pre_g"], P["w_in_t"], ex[0])
    P = {**P, **rest(got0)}
    ac, ap, u1, *got1 = _mix_fwd(z, P["conv_dw"], P["conv_dw_b"], P["conv_ln_g"], P["conv_ln_b"],
                                 P["pool_w"], P["pool_b"], P["pool_scale"], S, ex[1])
    o, *got2 = _attn_fwd(z, skew, S, ex[2])
    out, kept, got3 = _out_fwd(x, ac, o, ap, z, P["w_conv_out"], P["w_attn_out"], P["w_pool_out"], P["w_out"],
                               P["post_g"], ex[3], tgt)
    return out, (x, z, ht, ac, o, ap, u1, kept), P, (got0, got1, got2, got3)


def _layer_bwd(dout, saved, P, skew, S, ex, late_ex):
    x, z, ht, ac, o, ap, u1, kept = saved
    (dac, dao, dag, dap, dgm, dwco, dwao, dwpo, dwout, dpostg, *got0) = _out_bwd(
        dout, ac, o, ap, z, kept, P["w_conv_out"], P["w_attn_out"], P["w_pool_out"], P["w_out"], P["post_g"], ex[0])
    dq, dk, dv, dskew, *got1 = _attn_bwd(z, dao, skew, S, ex[1])
    (dzc, dzp, ddw, ddwb, dlng, dlnb, dpw, dpb, dps) = _mix_bwd(
        z, u1, dac, dap, P["conv_dw"], P["conv_dw_b"], P["conv_ln_g"], P["conv_ln_b"],
        P["pool_w"], P["pool_b"], P["pool_scale"], S)
    grads = dict(post_norm_g=dpostg, conv_dw=ddw, conv_dw_b=ddwb, conv_ln_g=dlng, conv_ln_b=dlnb,
                 w_conv_out=dwco, dskew=dskew, w_attn_out=dwao, pool_w=dpw, pool_b=dpb, pool_scale=dps,
                 w_pool_out=dwpo, w_out=dwout)
    pieces = [dzc, dq, dk, dv, dag, dzp, dgm]
    ex_rest, ex_win = late_ex(grads)
    buf, row0 = None, 0
    for p in pieces[:-1]:
        (buf,) = _in_bwd_w(ht, p, row0, buf)
        row0 += p.shape[1]
    grads["w_in_t"], *got2 = _in_bwd_w(ht, pieces[-1], row0, buf, ex_rest)
    dx, dpreg, *got3 = _in_bwd_x(pieces, P["w_in_t"], x, P["pre_g"], dout, ex_win(grads))
    grads["pre_norm_g"] = dpreg
    return dx, grads, (got0, got1, got2, got3)


WEIGHT_NAMES = ("pre_norm_g", "post_norm_g", "w_in", "conv_dw", "conv_dw_b", "conv_ln_g", "conv_ln_b",
                "w_conv_out", "rel_bias", "w_attn_out", "pool_w", "pool_b", "pool_scale", "w_pool_out", "w_out")
SHARDED = ("w_in", "w_conv_out", "w_attn_out", "w_pool_out", "w_out", "conv_dw")
REST = tuple(n for n in WEIGHT_NAMES if n not in ("w_in", "pre_norm_g"))


def _cols_from_slabs(g):
    return g.transpose(1, 0, 2).reshape(g.shape[1], N_DEV * g.shape[2])


def _slabs_from_cols(full):
    r, wd = full.shape
    return full.reshape(r, N_DEV, wd // N_DEV).transpose(1, 0, 2)


def _rest_shards(weights, l):
    return [weights["w_conv_out"][l].astype(BF16), weights["w_attn_out"][l].astype(BF16),
            weights["w_pool_out"][l].astype(BF16), weights["w_out"][l].astype(BF16), weights["conv_dw"][l]]


def _rest_weights(got):
    wco, wao, wpo, wout, cdw = got
    return dict(w_conv_out=_cols_from_slabs(wco), w_attn_out=_cols_from_slabs(wao),
                w_pool_out=_cols_from_slabs(wpo), w_out=wout.reshape(D, D), conv_dw=_cols_from_slabs(cdw))


def _grad_arrays(g, names):
    make = {"w_in": lambda: g["w_in_t"].reshape(N_DEV, NCOL // N_DEV, D),
            "w_conv_out": lambda: _slabs_from_cols(g["w_conv_out"].astype(BF16)),
            "w_attn_out": lambda: _slabs_from_cols(g["w_attn_out"].astype(BF16)),
            "w_pool_out": lambda: _slabs_from_cols(g["w_pool_out"].astype(BF16)),
            "w_out": lambda: g["w_out"].astype(BF16).reshape(N_DEV, D // N_DEV, D),
            "conv_dw": lambda: _slabs_from_cols(g["conv_dw"].astype(BF16)),
            "rel_bias": lambda: jnp.dot(g["dskew"], jnp.asarray(_skew_select().T), precision=lax.Precision.HIGHEST),
            "pool_b": lambda: g["pool_b"].reshape(4, 128)}
    return [make[n]() if n in make else g[n] for n in names]


def _grad_exchange(g, names):
    return _Exchange(_grad_arrays(g, names), [n in SHARDED for n in names])


def kernel(x, pre_norm_g, post_norm_g, w_in, conv_dw, conv_dw_b, conv_ln_g, conv_ln_b, w_conv_out, rel_bias, w_attn_out, pool_w, pool_b, pool_scale, w_pool_out, w_out, loss_target, m_pre_norm_g, m_post_norm_g, m_w_in, m_conv_dw, m_conv_dw_b, m_conv_ln_g, m_conv_ln_b, m_w_conv_out, m_rel_bias, m_w_attn_out, m_pool_w, m_pool_b, m_pool_scale, m_w_pool_out, m_w_out, v_pre_norm_g, v_post_norm_g, v_w_in, v_conv_dw, v_conv_dw_b, v_conv_ln_g, v_conv_ln_b, v_w_conv_out, v_rel_bias, v_w_attn_out, v_pool_w, v_pool_b, v_pool_scale, v_w_pool_out, v_w_out):
    given = dict(locals())
    weights = {n: given[n] for n in WEIGHT_NAMES}
    nb, S, _ = x.shape
    T = nb * S
    L = pre_norm_g.shape[0]
    assert L == 2
    x2 = x.reshape(T, D)
    tgt2 = loss_target.reshape(T, D)
    skews = [_skew_table(rel_bias[l]) for l in range(L)]

    def local_params(l):
        return dict(pre_g=pre_norm_g[l:l + 1], post_g=post_norm_g[l:l + 1], conv_dw_b=conv_dw_b[l:l + 1],
                    conv_ln_g=conv_ln_g[l:l + 1], conv_ln_b=conv_ln_b[l:l + 1], pool_w=pool_w[l].astype(BF16),
                    pool_b=pool_b[l].reshape(1, BW), pool_scale=pool_scale[l:l + 1])

    win0 = w_in[0].T.astype(BF16)
    win1 = w_in[1].T.astype(BF16)
    half = win1.shape[0] // 2
    w_in_t0 = _gather_two_level(win0, "gather_w_in_0")
    gather = lambda arrays: _Exchange(arrays, [False] * len(arrays))
    (h,), saved0, P0, (got_rest0, got_a, got_b, got_rest1) = _layer_fwd(
        x2, {**local_params(0), "w_in_t": w_in_t0.reshape(NCOL, D)}, skews[0], S, _rest_weights,
        (gather(_rest_shards(weights, 0)), gather([win1[:half]]), gather([win1[half:]]),
         gather(_rest_shards(weights, 1))))
    w_in_t1 = jnp.concatenate([got_a[0], got_b[0]], axis=1).reshape(NCOL, D)
    (dout, lsum), saved1, P1, _ = _layer_fwd(h, {**local_params(1), "w_in_t": w_in_t1}, skews[1], S,
                                             lambda _: _rest_weights(got_rest1), (None,) * 4, tgt2)
    loss = lax.psum(lsum[0, 0], MESH_AXES) * (0.5 / D)

    no_ex = lambda grads: (None, lambda g: None)
    dout, g1, _ = _layer_bwd(dout, saved1, P1, skews[1], S, (None, None), no_ex)
    late0 = lambda grads: (_grad_exchange(grads, REST), lambda g: _grad_exchange(g, ("w_in",)))
    dout, g0, (got_rest1g, got_win1, got_rest0g, got_win0) = _layer_bwd(
        dout, saved0, P0, skews[0], S,
        (_grad_exchange(g1, REST + ("pre_norm_g",)), _grad_exchange(g1, ("w_in",))), late0)
    (got_pre0,) = _exchange([g0["pre_norm_g"]], [False], "gather_grad_pre_norm_g_0")
    parts = [{"w_in": got_win0[0], "pre_norm_g": got_pre0, **dict(zip(REST, got_rest0g))},
             {"w_in": got_win1[0], **dict(zip(REST + ("pre_norm_g",), got_rest1g))}]
    grad_x = dout.reshape(x.shape)

    outs = {}
    small = [n for n in WEIGHT_NAMES if n not in SHARDED]
    res = _adamw_small([[parts[l][n] for n in small] for l in range(L)], [weights[n] for n in small],
                       [given["m_" + n] for n in small], [given["v_" + n] for n in small])
    outs.update(zip(small, res))
    for n in SHARDED:
        view = (lambda a: a.transpose(0, 2, 1)) if n == "w_in" else (lambda a: a)
        res = _adamw_sum(parts[0][n], parts[1][n], view(weights[n]), view(given["m_" + n]), view(given["v_" + n]),
                         "adamw_" + n)
        outs[n] = [view(a) for a in res]
    return (loss, grad_x, *[outs[n][0] for n in WEIGHT_NAMES], *[outs[n][1] for n in WEIGHT_NAMES],
            *[outs[n][2] for n in WEIGHT_NAMES], *[outs[n][3] for n in WEIGHT_NAMES])
```

```python
import functools

import numpy as np
import jax
import jax.numpy as jnp
from jax import lax
from jax.experimental import pallas as pl
from jax.experimental.pallas import tpu as pltpu

F32 = jnp.float32
BF16 = jnp.bfloat16
SDS = jax.ShapeDtypeStruct

D = 1024
BW = 512
NCOL = 7680
EPS = 1e-6
NEG = -1e30
HEADS = 8
HD = 64
CHUNK = 64
LEFT = 8
MAX_REL = 256
TQ = 256
KW = 768
CONV_K = 31
WINDOWS = (2, 4, 8, 16)
HALO = 32
RC = 32
N_DEV = 8
MESH_AXES = ("x", "y", "c")

ADAM_LR = 0.001
ADAM_B1 = 0.9
ADAM_B2 = 0.999
ADAM_EPS = 1e-08
ADAM_WD = 0.01
ADAM_STEP = 10

VMEM_LIMIT = 56 * 1024 * 1024

CB_CA, CB_CB, CB_CG, CB_Q, CB_K, CB_V, CB_AG, CB_PI, CB_PG = range(9)
DZ_PIECES = (("conv", 1536), ("q", 512), ("k", 512), ("v", 512), ("ag", 512), ("pool", 1024), ("gm", 3072))


def _cparams(sem):
    return pltpu.CompilerParams(dimension_semantics=sem, vmem_limit_bytes=VMEM_LIMIT)


def _sig(x):
    return 0.5 * jnp.tanh(0.5 * x) + 0.5


def _dsilu(x, s):
    return s * (1.0 + x * (1.0 - s))


def _nt(a, b):
    return lax.dot_general(a, b, (((1,), (1,)), ((), ())), preferred_element_type=F32)


def _tn(a, b):
    return lax.dot_general(a, b, (((0,), (0,)), ((), ())), preferred_element_type=F32)


def _nn(a, b):
    return jnp.dot(a, b, preferred_element_type=F32)


def _rows8(x):
    return x[0:8] + x[8:16] + x[16:24] + x[24:32]


def _call(body, name, grid, in_specs, out_specs, out_shape, scratch, args, sem, ex=None, aliases=None):
    aliases = aliases or {}
    if ex is None:
        return pl.pallas_call(body, name=name, grid=grid, in_specs=in_specs, out_specs=out_specs,
                              out_shape=out_shape, scratch_shapes=scratch, input_output_aliases=aliases,
                              compiler_params=_cparams(sem))(*args)
    n_in, n_out = len(in_specs), len(out_specs)
    steps = int(np.prod(grid))

    def carrier(*refs):
        own_in, own_out, own_scr, exr = ex.split(refs, n_in, n_out)
        step = pl.program_id(0)
        for axis in range(1, len(grid)):
            step = step * grid[axis] + pl.program_id(axis)

        @pl.when(step == 0)
        def _():
            ex.start(exr)

        body(*own_in, *own_out, *own_scr)

        @pl.when(step == steps - 1)
        def _():
            ex.finish(exr)

    return pl.pallas_call(
        carrier, name=name + "_carrier", grid=grid, in_specs=in_specs + ex.in_specs,
        out_specs=out_specs + ex.out_specs, out_shape=out_shape + ex.out_shape,
        scratch_shapes=scratch + ex.scratch, input_output_aliases=aliases,
        compiler_params=_cparams(("arbitrary",) * len(grid)),
    )(*args, *ex.arrays)


def _in_proj(x, g, wt, ex=None):
    T = x.shape[0]
    tm = 512
    tn = 1536

    def body(x_ref, g_ref, w_ref, z_ref, ht_ref):
        xv = x_ref[...]
        r = lax.rsqrt(jnp.mean(xv * xv, axis=-1, keepdims=True) + EPS)
        h = xv * r * g_ref[...]
        hb = h.astype(BF16)
        ht_ref[...] = h.T.astype(BF16)
        for c in range(NCOL // tn):
            z_ref[:, c * tn:(c + 1) * tn] = _nt(hb, w_ref[c * tn:(c + 1) * tn, :]).astype(BF16)

    return _call(
        body, "in_proj", (T // tm,),
        [pl.BlockSpec((tm, D), lambda i: (i, 0)), pl.BlockSpec((1, D), lambda i: (0, 0)),
         pl.BlockSpec((NCOL, D), lambda i: (0, 0))],
        [pl.BlockSpec((tm, NCOL), lambda i: (i, 0)), pl.BlockSpec((D, tm), lambda i: (0, i))],
        [SDS((T, NCOL), BF16), SDS((D, T), BF16)], [],
        (x, g, wt), ("parallel",), ex)


def _stencil(load, w_ref, lo, hi, tap_of):
    out = None
    for r in range(8):
        n = RC if r == 0 else RC + 8
        v = None
        for q in range((hi - r) // 8 + 1):
            o = 8 * q + r
            if o < lo:
                continue
            j = tap_of(o)
            term = w_ref[j:j + 1, :] * load(q, n)
            v = term if v is None else v + term
        if v is None:
            continue
        if r:
            v = pltpu.roll(v, n - r, axis=0)[0:RC]
        out = v if out is None else out + v
    return out


def _layer_norm_fwd(u1):
    mu = jnp.mean(u1, axis=-1, keepdims=True)
    xc = u1 - mu
    rstd = lax.rsqrt(jnp.mean(xc * xc, axis=-1, keepdims=True) + EPS)
    return xc * rstd, rstd


def _window_sums(x, w, back):
    n = x.shape[0]
    s = x
    k = 1
    while k < w:
        s = s + pltpu.roll(s, k if back else n - k, axis=0)
        k *= 2
    return s


def _pool_chunk(pwin, t_first):
    t = t_first + lax.broadcasted_iota(jnp.int32, (RC, 128), 0)
    outs = []
    for g, w in enumerate(WINDOWS):
        x = pwin[:, g * 128:(g + 1) * 128]
        s = _window_sums(x, w, True)
        cnt = jnp.minimum(t + 1, w).astype(F32)
        outs.append(s[HALO:HALO + RC] / cnt - x[HALO:HALO + RC])
    return outs


def _mix_fwd(z, dw, dwb, lng, lnb, pw, pb, ps, S, ex=None):
    T = z.shape[0]
    tm = min(S, 1024)
    ts = S // tm
    nh = tm // HALO

    def body(ca_ref, cb_ref, cg_ref, pi_ref, pg_ref, cah_ref, cbh_ref, pih_ref,
             dw_ref, dwb_ref, lng_ref, lnb_ref, pw_ref, pb_ref, ps_ref,
             ac_ref, ap_ref, u1_ref, ubuf, pbuf, pooled):
        i = pl.program_id(0)
        keep = jnp.where((i % ts) == 0, 0.0, 1.0)
        ubuf[0:HALO, :] = cah_ref[...].astype(F32) * _sig(cbh_ref[...].astype(F32)) * keep
        ubuf[HALO:HALO + tm, :] = ca_ref[...].astype(F32) * _sig(cb_ref[...].astype(F32))
        pbuf[0:HALO, :] = pih_ref[...].astype(F32) * keep
        pbuf[HALO:HALO + tm, :] = pi_ref[...].astype(F32)
        t0 = (i % ts) * tm

        def chunk(c, carry):
            base = pl.multiple_of(c * RC, RC)
            load = lambda q, n: ubuf[pl.ds(base + 8 * q, n), :]
            u1 = _stencil(load, dw_ref, 2, CONV_K + 1, lambda o: o - 2) + dwb_ref[...]
            u1_ref[pl.ds(base, RC), :] = u1
            n, _ = _layer_norm_fwd(u1)
            u2 = n * lng_ref[...] + lnb_ref[...]
            u3 = u2 * _sig(u2)
            cg = cg_ref[pl.ds(base, RC), :].astype(F32)
            ac_ref[pl.ds(base, RC), :] = (u3 * cg * _sig(cg)).astype(BF16)
            pwin = pbuf[pl.ds(base, RC + HALO), :]
            outs = _pool_chunk(pwin, t0 + base)
            for g in range(4):
                pooled[pl.ds(base, RC), g * 128:(g + 1) * 128] = outs[g].astype(BF16)
            return carry

        lax.fori_loop(0, tm // RC, chunk, 0, unroll=4)
        pg = pg_ref[...].astype(F32)
        spg = pg * _sig(pg)
        for g in range(4):
            sl = slice(g * 128, (g + 1) * 128)
            mixed = (_nn(pooled[:, sl], pw_ref[g]) + pb_ref[:, sl]) * ps_ref[:, sl]
            ap_ref[:, sl] = (mixed * spg[:, sl]).astype(BF16)

    def zmain(cb):
        return pl.BlockSpec((tm, BW), lambda i: (i, cb))

    def zprev(cb):
        return pl.BlockSpec((HALO, BW), lambda i: (jnp.maximum(i * nh - 1, 0), cb))

    full = lambda shape: pl.BlockSpec(shape, lambda i: (0,) * len(shape))
    row = pl.BlockSpec((tm, BW), lambda i: (i, 0))
    return _call(
        body, "mix_fwd", (T // tm,),
        [zmain(CB_CA), zmain(CB_CB), zmain(CB_CG), zmain(CB_PI), zmain(CB_PG),
         zprev(CB_CA), zprev(CB_CB), zprev(CB_PI),
         full((CONV_K, BW)), full((1, BW)), full((1, BW)), full((1, BW)),
         full((4, 128, 128)), full((1, BW)), full((1, BW))],
        [row, row, row], [SDS((T, BW), BF16), SDS((T, BW), BF16), SDS((T, BW), F32)],
        [pltpu.VMEM((HALO + tm, BW), F32), pltpu.VMEM((HALO + tm, BW), F32), pltpu.VMEM((tm, BW), BF16)],
        (z, z, z, z, z, z, z, z, dw, dwb, lng, lnb, pw, pb, ps), ("parallel",), ex)


def _attn_specs(nq):
    def kv(cb, off):
        return pl.BlockSpec((TQ, BW), lambda i: (i - jnp.minimum(off, i % nq), cb))
    return [pl.BlockSpec((TQ, BW), lambda i: (i, CB_Q)),
            kv(CB_K, 2), kv(CB_K, 1), kv(CB_K, 0), kv(CB_V, 2), kv(CB_V, 1), kv(CB_V, 0)]


def _softmax_rows(s):
    m = jnp.max(s, axis=-1, keepdims=True)
    e = jnp.exp(s - m)
    return e / jnp.sum(e, axis=-1, keepdims=True)


NSKEW = 1024


def _skew_table(table):
    return jnp.dot(table, jnp.asarray(_skew_select()), precision=lax.Precision.HIGHEST)


def _skew_select():
    d = np.arange(TQ + KW - 1)
    idx = np.clip(3 * TQ - 1 - d, -MAX_REL, MAX_REL) + MAX_REL
    sel = np.zeros((2 * MAX_REL + 1, NSKEW), np.float32)
    sel[idx, d] = 1.0
    return sel


def _bias_from_skew(f_ref, bias_scr):
    qi = lax.broadcasted_iota(jnp.int32, (TQ, KW), 0)
    kj = lax.broadcasted_iota(jnp.int32, (TQ, KW), 1)
    lo = (qi // CHUNK) * CHUNK
    band = jnp.where((kj >= lo) & (kj < lo + (LEFT + 1) * CHUNK), 0.0, NEG)
    for h in range(HEADS):
        rows = jnp.broadcast_to(f_ref[h:h + 1, :], (TQ, NSKEW))
        rows = pltpu.roll(rows, NSKEW - (TQ - 1), axis=1, stride=1, stride_axis=0)
        bias_scr[h] = rows[:, 0:KW] + band


def _skew_from_bias(db):
    i = lax.broadcasted_iota(jnp.int32, (TQ, TQ), 0)
    j = lax.broadcasted_iota(jnp.int32, (TQ, TQ), 1)
    flip = jnp.where(i + j == TQ - 1, 1.0, 0.0).astype(BF16)
    hi = db.astype(BF16)
    lo = (db - hi.astype(F32)).astype(BF16)
    rev = _nn(flip, hi) + _nn(flip, lo)
    rev = jnp.concatenate([rev, jnp.zeros((TQ, NSKEW - KW), F32)], axis=1)
    return jnp.sum(pltpu.roll(rev, 0, axis=1, stride=1, stride_axis=0), axis=0, keepdims=True)


def _attn_fwd(z, f, S, ex=None):
    T = z.shape[0]
    nq = S // TQ

    def body(q_ref, k2_ref, k1_ref, k0_ref, v2_ref, v1_ref, v0_ref, f_ref, o_ref, kbuf, vbuf, b_scr):
        @pl.when(pl.program_id(0) == 0)
        def _():
            _bias_from_skew(f_ref, b_scr)

        qb = pl.program_id(0) % nq
        kbuf[0:TQ, :] = k2_ref[...]
        kbuf[TQ:2 * TQ, :] = k1_ref[...]
        kbuf[2 * TQ:KW, :] = k0_ref[...]
        vbuf[0:TQ, :] = v2_ref[...]
        vbuf[TQ:2 * TQ, :] = v1_ref[...]
        vbuf[2 * TQ:KW, :] = v0_ref[...]
        lane = lax.broadcasted_iota(jnp.int32, (1, 128), 1)

        def attend(lo):
            def scores(h):
                sl = slice((h // 2) * 128, (h // 2 + 1) * 128)
                qp = q_ref[:, sl] * 0.125
                qm = jnp.where((lane < HD) if h % 2 == 0 else (lane >= HD), qp, jnp.zeros_like(qp))
                return _nt(qm, kbuf[lo:KW, sl]) + b_scr[h, :, lo:KW]

            s = scores(0)
            acc = None
            for h in range(HEADS):
                s_next = scores(h + 1) if h + 1 < HEADS else None
                sl = slice((h // 2) * 128, (h // 2 + 1) * 128)
                e = jnp.exp(s - jnp.max(s, axis=-1, keepdims=True))
                vp = vbuf[lo:KW, sl]
                vm = jnp.where((lane < HD) if h % 2 == 0 else (lane >= HD), vp, jnp.zeros_like(vp))
                o = _nn(e.astype(BF16), vm) * (1.0 / jnp.sum(e, axis=-1, keepdims=True))
                acc = o if h % 2 == 0 else acc + o
                if h % 2 == 1:
                    o_ref[:, sl] = acc.astype(BF16)
                s = s_next

        for nblk in (1, 2, 3):
            pl.when(jnp.minimum(qb, 2) == nblk - 1)(functools.partial(attend, (3 - nblk) * TQ))

    full = lambda shape: pl.BlockSpec(shape, lambda i: (0,) * len(shape))
    return _call(
        body, "attn_fwd", (T // TQ,),
        _attn_specs(nq) + [full((HEADS, NSKEW))],
        [pl.BlockSpec((TQ, BW), lambda i: (i, 0))], [SDS((T, BW), BF16)],
        [pltpu.VMEM((KW, BW), BF16), pltpu.VMEM((KW, BW), BF16), pltpu.VMEM((HEADS, TQ, KW), F32)],
        (z, z, z, z, z, z, z, f), ("arbitrary",), ex)


def _gates(gl_ref, gh_ref):
    gl = _sig(gl_ref[...].astype(F32))
    gh = _sig(gh_ref[...].astype(F32))
    return (gl[:, 0:D], jnp.concatenate([gl[:, D:1536], gh[:, 0:512]], axis=1), gh[:, 512:1536])


def _out_specs_in(tm):
    row = lambda w: pl.BlockSpec((tm, w), lambda i: (i, 0))
    full = lambda shape: pl.BlockSpec(shape, lambda i: (0,) * len(shape))
    return [row(BW), row(BW), row(BW),
            pl.BlockSpec((tm, BW), lambda i: (i, CB_AG)),
            pl.BlockSpec((tm, 1536), lambda i: (i, 3)),
            pl.BlockSpec((tm, 1536), lambda i: (i, 4)),
            full((BW, D)), full((BW, D)), full((BW, D)), full((D, D)), full((1, D))]


def _out_fwd(x, ac, o, ap, z, wco, wao, wpo, wout, postg, ex=None, tgt=None):
    T = x.shape[0]
    tm = 512
    last = tgt is not None

    def body(ac_ref, o_ref, ap_ref, ag_ref, gl_ref, gh_ref, wco_ref, wao_ref, wpo_ref, wout_ref, pg_ref,
             x_ref, *rest):
        ag = ag_ref[...].astype(F32)
        aat = (o_ref[...].astype(F32) * ag * _sig(ag)).astype(BF16)
        gates = _gates(gl_ref, gh_ref)
        acts = (ac_ref[...], aat, ap_ref[...])
        merged = None
        for b, w_ref in enumerate((wco_ref, wao_ref, wpo_ref)):
            yb = _nn(acts[b], w_ref[...])
            rest[-4 + b][...] = yb.astype(BF16)
            merged = gates[b] * yb if merged is None else merged + gates[b] * yb
        y = _nn(merged.astype(BF16), wout_ref[...])
        rest[-1][...] = y.astype(BF16)
        ry = lax.rsqrt(jnp.mean(y * y, axis=-1, keepdims=True) + EPS)
        out = x_ref[...] + y * ry * pg_ref[...]
        if not last:
            rest[0][...] = out
            return
        t_ref, d_ref, l_ref = rest[:3]

        @pl.when(pl.program_id(0) == 0)
        def _():
            l_ref[...] = jnp.zeros_like(l_ref)
        d = out - t_ref[...]
        d_ref[...] = d * (1.0 / D)
        l_ref[...] += jnp.sum(jnp.sum(d * d, axis=0, keepdims=True), axis=1, keepdims=True)

    row = pl.BlockSpec((tm, D), lambda i: (i, 0))
    kept_specs, kept_shapes = [row] * 4, [SDS((T, D), BF16)] * 4
    if not last:
        res = _call(body, "out_fwd", (T // tm,), _out_specs_in(tm) + [row], [row] + kept_specs,
                    [SDS((T, D), F32)] + kept_shapes, [],
                    (ac, o, ap, z, z, z, wco, wao, wpo, wout, postg, x), ("parallel",), ex)
        return res[:1], res[1:5], res[5:]
    res = _call(body, "out_fwd_loss", (T // tm,), _out_specs_in(tm) + [row, row],
                [row, pl.BlockSpec((1, 128), lambda i: (0, 0))] + kept_specs,
                [SDS((T, D), F32), SDS((1, 128), F32)] + kept_shapes, [],
                (ac, o, ap, z, z, z, wco, wao, wpo, wout, postg, x, tgt), ("arbitrary",), ex)
    return res[:2], res[2:6], res[6:]


def _out_bwd(dout, ac, o, ap, z, kept, wco, wao, wpo, wout, postg, ex=None):
    T = dout.shape[0]
    tm = 256

    def body(ac_ref, o_ref, ap_ref, ag_ref, gl_ref, gh_ref, wco_ref, wao_ref, wpo_ref, wout_ref, pg_ref, do_ref,
             yc_ref, ya_ref, yp_ref, y_ref,
             dac_ref, dao_ref, dag_ref, dap_ref, dgm_ref, dwco_ref, dwao_ref, dwpo_ref, dwout_ref, dpg_ref):
        @pl.when(pl.program_id(0) == 0)
        def _():
            for r in (dwco_ref, dwao_ref, dwpo_ref, dwout_ref, dpg_ref):
                r[...] = jnp.zeros_like(r)

        ag = ag_ref[...].astype(F32)
        sag = _sig(ag)
        ov = o_ref[...].astype(F32)
        acts = (ac_ref[...], (ov * ag * sag).astype(BF16), ap_ref[...])
        ws = (wco_ref, wao_ref, wpo_ref)
        gates = _gates(gl_ref, gh_ref)
        ys = [r[...].astype(F32) for r in (yc_ref, ya_ref, yp_ref)]
        merged = (gates[0] * ys[0] + gates[1] * ys[1] + gates[2] * ys[2]).astype(BF16)
        y = y_ref[...].astype(F32)
        ry = lax.rsqrt(jnp.mean(y * y, axis=-1, keepdims=True) + EPS)
        yn = y * ry
        dout_v = do_ref[...]
        dpg_ref[...] += jnp.sum(dout_v * yn, axis=0, keepdims=True)
        dyn = dout_v * pg_ref[...]
        dy = (ry * (dyn - yn * jnp.mean(dyn * yn, axis=-1, keepdims=True))).astype(BF16)
        dmerged = _nt(dy, wout_ref[...])
        dwout_ref[...] += _tn(merged, dy)
        dws = (dwco_ref, dwao_ref, dwpo_ref)
        das = []
        for b in range(3):
            gb = gates[b]
            dgm_ref[:, b * D:(b + 1) * D] = (dmerged * ys[b] * gb * (1.0 - gb)).astype(BF16)
            dyb = (dmerged * gb).astype(BF16)
            dws[b][...] += _tn(acts[b], dyb)
            das.append(_nt(dyb, ws[b][...]))
        dac_ref[...] = das[0].astype(BF16)
        dap_ref[...] = das[2].astype(BF16)
        dao_ref[...] = (das[1] * ag * sag).astype(BF16)
        dag_ref[...] = (das[1] * ov * _dsilu(ag, sag)).astype(BF16)

    row = lambda w: pl.BlockSpec((tm, w), lambda i: (i, 0))
    full = lambda shape: pl.BlockSpec(shape, lambda i: (0,) * len(shape))
    return _call(
        body, "out_bwd", (T // tm,), _out_specs_in(tm) + [row(D)] * 5,
        [row(BW), row(BW), row(BW), row(BW), row(3 * D),
         full((BW, D)), full((BW, D)), full((BW, D)), full((D, D)), full((1, D))],
        [SDS((T, BW), BF16)] * 4 + [SDS((T, 3 * D), BF16)]
        + [SDS((BW, D), F32)] * 3 + [SDS((D, D), F32), SDS((1, D), F32)], [],
        (ac, o, ap, z, z, z, wco, wao, wpo, wout, postg, dout, *kept), ("arbitrary",), ex)


def _attn_bwd(z, dao, f, S, ex=None):
    T = z.shape[0]
    nq = S // TQ
    nsteps = T // TQ

    def body(q_ref, k2_ref, k1_ref, k0_ref, v2_ref, v1_ref, v0_ref, do_ref, f_ref,
             dq_ref, dk_ref, dv_ref, df_ref, kbuf, vbuf, dkacc, dvacc, b_scr, db_scr):
        i = pl.program_id(0)
        qb = i % nq

        @pl.when(i == 0)
        def _():
            _bias_from_skew(f_ref, b_scr)
            db_scr[...] = jnp.zeros_like(db_scr)

        @pl.when(qb == 0)
        def _():
            dkacc[...] = jnp.zeros_like(dkacc)
            dvacc[...] = jnp.zeros_like(dvacc)

        kbuf[0:TQ, :] = k2_ref[...]
        kbuf[TQ:2 * TQ, :] = k1_ref[...]
        kbuf[2 * TQ:KW, :] = k0_ref[...]
        vbuf[0:TQ, :] = v2_ref[...]
        vbuf[TQ:2 * TQ, :] = v1_ref[...]
        vbuf[2 * TQ:KW, :] = v0_ref[...]
        lane = lax.broadcasted_iota(jnp.int32, (1, 128), 1)
        row0 = pl.multiple_of(qb * TQ, TQ)

        def attend(lo):
            def first_matmuls(h):
                sl = slice((h // 2) * 128, (h // 2 + 1) * 128)
                msk = (lane < HD) if h % 2 == 0 else (lane >= HD)
                qp = q_ref[:, sl] * 0.125
                dop = do_ref[:, sl]
                qm = jnp.where(msk, qp, jnp.zeros_like(qp))
                dom = jnp.where(msk, dop, jnp.zeros_like(dop))
                s = _nt(qm, kbuf[lo:KW, sl]) + b_scr[h, :, lo:KW]
                return s, _nt(dom, vbuf[lo:KW, sl]), qm, dom

            cur = first_matmuls(0)
            dq_acc = dk_acc = dv_acc = None
            for h in range(HEADS):
                nxt = first_matmuls(h + 1) if h + 1 < HEADS else None
                s, dp, qm, dom = cur
                sl = slice((h // 2) * 128, (h // 2 + 1) * 128)
                e = jnp.exp(s - jnp.max(s, axis=-1, keepdims=True))
                p = e * (1.0 / jnp.sum(e, axis=-1, keepdims=True))
                ds = p * (dp - jnp.sum(p * dp, axis=-1, keepdims=True))
                db_scr[h, :, lo:KW] += ds
                dsb = ds.astype(BF16)
                kp = kbuf[lo:KW, sl]
                km = jnp.where((lane < HD) if h % 2 == 0 else (lane >= HD), kp, jnp.zeros_like(kp))
                dq_h = _nn(dsb, km) * 0.125
                dk_h = _tn(dsb, qm)
                dv_h = _tn(p.astype(BF16), dom)
                if h % 2 == 0:
                    dq_acc, dk_acc, dv_acc = dq_h, dk_h, dv_h
                else:
                    dq_ref[:, sl] = (dq_acc + dq_h).astype(BF16)
                    dkacc[pl.ds(row0 + lo, KW - lo), sl] += dk_acc + dk_h
                    dvacc[pl.ds(row0 + lo, KW - lo), sl] += dv_acc + dv_h
                cur = nxt

        for nblk in (1, 2, 3):
            pl.when(jnp.minimum(qb, 2) == nblk - 1)(functools.partial(attend, (3 - nblk) * TQ))

        @pl.when(qb == nq - 1)
        def _():
            dk_ref[...] = dkacc[2 * TQ:2 * TQ + S, :].astype(BF16)
            dv_ref[...] = dvacc[2 * TQ:2 * TQ + S, :].astype(BF16)

        @pl.when(i == nsteps - 1)
        def _():
            for h in range(HEADS):
                df_ref[h:h + 1, :] = _skew_from_bias(db_scr[h])

    full = lambda shape: pl.BlockSpec(shape, lambda i: (0,) * len(shape))
    return _call(
        body, "attn_bwd", (nsteps,),
        _attn_specs(nq) + [pl.BlockSpec((TQ, BW), lambda i: (i, 0)), full((HEADS, NSKEW))],
        [pl.BlockSpec((TQ, BW), lambda i: (i, 0)), pl.BlockSpec((S, BW), lambda i: (i // nq, 0)),
         pl.BlockSpec((S, BW), lambda i: (i // nq, 0)), full((HEADS, NSKEW))],
        [SDS((T, BW), BF16)] * 3 + [SDS((HEADS, NSKEW), F32)],
        [pltpu.VMEM((KW, BW), BF16), pltpu.VMEM((KW, BW), BF16),
         pltpu.VMEM((S + 2 * TQ, BW), F32), pltpu.VMEM((S + 2 * TQ, BW), F32),
         pltpu.VMEM((HEADS, TQ, KW), F32), pltpu.VMEM((HEADS, TQ, KW), F32)],
        (z, z, z, z, z, z, z, dao, f), ("arbitrary",), ex)


def _mix_bwd(z, u1, dac, dap, dw, dwb, lng, lnb, pw, pb, ps, S, ex=None):
    T = z.shape[0]
    tm = min(S, 1024)
    ts = S // tm
    nh = tm // HALO
    nsteps = T // tm
    nblk32 = T // HALO

    def body(ca_ref, cb_ref, cg_ref, pi_ref, pg_ref, u1_ref, dac_ref, dap_ref,
             cah_ref, cbh_ref, pih_ref,
             cgn_ref, pgn_ref, u1n_ref, dacn_ref, dapn_ref,
             dw_ref, dwb_ref, lng_ref, lnb_ref, pw_ref, pb_ref, ps_ref,
             dzc_ref, dzp_ref, ddw_ref, ddwb_ref, dlng_ref, dlnb_ref, dpw_ref, dpb_ref, dps_ref,
             ubuf, gbuf, pbuf, qbuf, pooled, *accs):
        tap_acc, (lng_acc, lnb_acc, dwb_acc) = accs[:CONV_K], accs[CONV_K:]
        i = pl.program_id(0)
        keep_prev = jnp.where((i % ts) == 0, 0.0, 1.0)
        keep_next = jnp.where((i % ts) == ts - 1, 0.0, 1.0)
        t0 = (i % ts) * tm

        @pl.when(i == 0)
        def _():
            for a in accs:
                a[...] = jnp.zeros_like(a)
            dpw_ref[...] = jnp.zeros_like(dpw_ref)
            dpb_ref[...] = jnp.zeros_like(dpb_ref)
            dps_ref[...] = jnp.zeros_like(dps_ref)

        ubuf[0:HALO, :] = cah_ref[...].astype(F32) * _sig(cbh_ref[...].astype(F32)) * keep_prev
        ubuf[HALO:HALO + tm, :] = ca_ref[...].astype(F32) * _sig(cb_ref[...].astype(F32))
        pbuf[0:HALO, :] = pih_ref[...].astype(F32) * keep_prev
        pbuf[HALO:HALO + tm, :] = pi_ref[...].astype(F32)

        def norm_back(u1v, cg, dacv):
            n, rstd = _layer_norm_fwd(u1v)
            u2 = n * lng_ref[...] + lnb_ref[...]
            s2 = _sig(u2)
            scg = _sig(cg)
            du2 = dacv * cg * scg * _dsilu(u2, s2)
            dn = du2 * lng_ref[...]
            du1 = rstd * (dn - jnp.mean(dn, axis=-1, keepdims=True)
                          - n * jnp.mean(dn * n, axis=-1, keepdims=True))
            return du1, du2, n, dacv * u2 * s2 * _dsilu(cg, scg)

        def chunk_a(c, carry):
            base = pl.multiple_of(c * RC, RC)
            du1, du2, n, dcg = norm_back(u1_ref[pl.ds(base, RC), :], cg_ref[pl.ds(base, RC), :].astype(F32),
                                         dac_ref[pl.ds(base, RC), :].astype(F32))
            gbuf[pl.ds(base, RC), :] = du1
            dzc_ref[pl.ds(base, RC), 2 * BW:3 * BW] = dcg.astype(BF16)
            lng_acc[...] += _rows8(du2 * n)
            lnb_acc[...] += _rows8(du2)
            dwb_acc[...] += _rows8(du1)
            padded = jnp.concatenate([du1, jnp.zeros((8, BW), F32)], axis=0)
            for r in range(8):
                nrow = RC if r == 0 else RC + 8
                g = du1 if r == 0 else pltpu.roll(padded, r, axis=0)
                for q in range((CONV_K + 1 - r) // 8 + 1):
                    o = 8 * q + r
                    if o < 2:
                        continue
                    prod = g * ubuf[pl.ds(base + 8 * q, nrow), :]
                    red = prod[0:8]
                    for k in range(1, nrow // 8):
                        red = red + prod[8 * k:8 * k + 8]
                    tap_acc[o - 2][...] += red
            return carry

        lax.fori_loop(0, tm // RC, chunk_a, 0, unroll=4)
        du1n, _, _, _ = norm_back(u1n_ref[...], cgn_ref[...].astype(F32), dacn_ref[...].astype(F32))
        gbuf[tm:tm + HALO, :] = du1n * keep_next

        def chunk_p(c, carry):
            base = pl.multiple_of(c * RC, RC)
            outs = _pool_chunk(pbuf[pl.ds(base, RC + HALO), :], t0 + base)
            for g in range(4):
                pooled[pl.ds(base, RC), g * 128:(g + 1) * 128] = outs[g].astype(BF16)
            return carry

        lax.fori_loop(0, tm // RC, chunk_p, 0)

        def cnt_of(t_first, rows, w):
            t = t_first + lax.broadcasted_iota(jnp.int32, (rows, 128), 0)
            return jnp.minimum(t + 1, w).astype(F32)

        pg = pg_ref[...].astype(F32)
        spg_s = _sig(pg)
        dapv = dap_ref[...].astype(F32)
        pgn = pgn_ref[...].astype(F32)
        dmixn = dapn_ref[...].astype(F32) * pgn * _sig(pgn) * ps_ref[...] * keep_next
        for g, w in enumerate(WINDOWS):
            sl = slice(g * 128, (g + 1) * 128)
            mixed_u = _nn(pooled[:, sl], pw_ref[g]) + pb_ref[:, sl]
            dap_g = dapv[:, sl]
            pg_g = pg[:, sl]
            s_g = spg_s[:, sl]
            silu_g = pg_g * s_g
            dps_ref[:, sl] += jnp.sum(dap_g * silu_g * mixed_u, axis=0, keepdims=True)
            dzp_ref[:, BW + g * 128:BW + (g + 1) * 128] = (
                dap_g * mixed_u * ps_ref[:, sl] * _dsilu(pg_g, s_g)).astype(BF16)
            dmix = dap_g * silu_g * ps_ref[:, sl]
            dpb_ref[:, sl] += jnp.sum(dmix, axis=0, keepdims=True)
            dmixb = dmix.astype(BF16)
            dpw_ref[g] += _tn(pooled[:, sl], dmixb)
            qbuf[0:tm, sl] = _nt(dmixb, pw_ref[g]) / cnt_of(t0, tm, w)
            qbuf[tm:tm + HALO, sl] = _nt(dmixn[:, sl].astype(BF16), pw_ref[g]) / cnt_of(t0 + tm, HALO, w)

        def chunk_b(c, carry):
            base = pl.multiple_of(c * RC, RC)
            load = lambda q, n: gbuf[pl.ds(base + 8 * q, n), :]
            du0 = _stencil(load, dw_ref, 0, CONV_K - 1, lambda o: CONV_K - 1 - o)
            ca = ca_ref[pl.ds(base, RC), :].astype(F32)
            sb = _sig(cb_ref[pl.ds(base, RC), :].astype(F32))
            dzc_ref[pl.ds(base, RC), 0:BW] = (du0 * sb).astype(BF16)
            dzc_ref[pl.ds(base, RC), BW:2 * BW] = (du0 * ca * sb * (1.0 - sb)).astype(BF16)
            qwin = qbuf[pl.ds(base, RC + HALO), :]
            t = t0 + base + lax.broadcasted_iota(jnp.int32, (RC, 128), 0)
            for g, w in enumerate(WINDOWS):
                x = qwin[:, g * 128:(g + 1) * 128]
                s = _window_sums(x, w, False)
                cnt = jnp.minimum(t + 1, w).astype(F32)
                dzp_ref[pl.ds(base, RC), g * 128:(g + 1) * 128] = (s[0:RC] - cnt * x[0:RC]).astype(BF16)
            return carry

        lax.fori_loop(0, tm // RC, chunk_b, 0)

        @pl.when(i == nsteps - 1)
        def _():
            dlng_ref[...] = jnp.sum(lng_acc[...], axis=0, keepdims=True)
            dlnb_ref[...] = jnp.sum(lnb_acc[...], axis=0, keepdims=True)
            ddwb_ref[...] = jnp.sum(dwb_acc[...], axis=0, keepdims=True)
            for j in range(CONV_K):
                ddw_ref[j:j + 1, :] = jnp.sum(tap_acc[j][...], axis=0, keepdims=True)

    def zmain(cb):
        return pl.BlockSpec((tm, BW), lambda i: (i, cb))

    def zprev(cb):
        return pl.BlockSpec((HALO, BW), lambda i: (jnp.maximum(i * nh - 1, 0), cb))

    def znext(cb):
        return pl.BlockSpec((HALO, BW), lambda i: (jnp.minimum((i + 1) * nh, nblk32 - 1), cb))

    row = lambda w: pl.BlockSpec((tm, w), lambda i: (i, 0))
    full = lambda shape: pl.BlockSpec(shape, lambda i: (0,) * len(shape))
    return _call(
        body, "mix_bwd", (nsteps,),
        [zmain(CB_CA), zmain(CB_CB), zmain(CB_CG), zmain(CB_PI), zmain(CB_PG), row(BW), row(BW), row(BW),
         zprev(CB_CA), zprev(CB_CB), zprev(CB_PI),
         znext(CB_CG), znext(CB_PG), znext(0), znext(0), znext(0),
         full((CONV_K, BW)), full((1, BW)), full((1, BW)), full((1, BW)),
         full((4, 128, 128)), full((1, BW)), full((1, BW))],
        [row(3 * BW), row(2 * BW), full((CONV_K, BW)), full((1, BW)), full((1, BW)), full((1, BW)),
         full((4, 128, 128)), full((1, BW)), full((1, BW))],
        [SDS((T, 3 * BW), BF16), SDS((T, 2 * BW), BF16), SDS((CONV_K, BW), F32),
         SDS((1, BW), F32), SDS((1, BW), F32), SDS((1, BW), F32),
         SDS((4, 128, 128), F32), SDS((1, BW), F32), SDS((1, BW), F32)],
        [pltpu.VMEM((HALO + tm, BW), F32), pltpu.VMEM((tm + HALO, BW), F32),
         pltpu.VMEM((HALO + tm, BW), F32), pltpu.VMEM((tm + HALO, BW), F32),
         pltpu.VMEM((tm, BW), BF16)] + [pltpu.VMEM((8, BW), F32)] * (CONV_K + 3),
        (z, z, z, z, z, u1, dac, dap, z, z, z, z, z, u1, dac, dap, dw, dwb, lng, lnb, pw, pb, ps),
        ("arbitrary",), ex)


def _in_bwd_x(pieces, wt, x, g, dout, ex=None):
    T = x.shape[0]
    tm = 512
    widths = [p.shape[1] for p in pieces]
    offs = np.cumsum([0] + widths)
    npc = len(pieces)

    def body(*refs):
        p_refs = refs[:npc]
        w_ref, x_ref, g_ref, do_ref, dx_ref, dg_ref = refs[npc:]

        @pl.when(pl.program_id(0) == 0)
        def _():
            dg_ref[...] = jnp.zeros_like(dg_ref)

        dh = None
        for k in range(npc):
            t = _nn(p_refs[k][...], w_ref[int(offs[k]):int(offs[k + 1]), :])
            dh = t if dh is None else dh + t
        xv = x_ref[...]
        r = lax.rsqrt(jnp.mean(xv * xv, axis=-1, keepdims=True) + EPS)
        xn = xv * r
        dg_ref[...] += jnp.sum(dh * xn, axis=0, keepdims=True)
        dxn = dh * g_ref[...]
        dx_ref[...] = do_ref[...] + r * (dxn - xn * jnp.mean(dxn * xn, axis=-1, keepdims=True))

    row = lambda wd: pl.BlockSpec((tm, wd), lambda i: (i, 0))
    return _call(
        body, "in_bwd_x", (T // tm,),
        [row(wd) for wd in widths] + [pl.BlockSpec((NCOL, D), lambda i: (0, 0)),
                                      row(D), pl.BlockSpec((1, D), lambda i: (0, 0)), row(D)],
        [row(D), pl.BlockSpec((1, D), lambda i: (0, 0))], [SDS((T, D), F32), SDS((1, D), F32)], [],
        (*pieces, wt, x, g, dout), ("arbitrary",), ex)


def _in_bwd_w(ht, piece, row0, buf=None, ex=None):
    T = ht.shape[1]
    wd = piece.shape[1]
    tn = next(t for t in (1536, 1024, 768, 512) if wd % t == 0 and row0 % t == 0)
    tk = min(T, 2048)
    nk = T // tk
    j0 = row0 // tn

    def body(ht_ref, p_ref, *rest):
        o_ref, acc = rest[-2:]
        k = pl.program_id(1)

        @pl.when(k == 0)
        def _():
            acc[...] = jnp.zeros_like(acc)
        acc[...] += _nn(ht_ref[...], p_ref[...])

        @pl.when(k == nk - 1)
        def _():
            o_ref[...] = acc[...].T.astype(BF16)

    in_specs = [pl.BlockSpec((D, tk), lambda j, k: (0, k)), pl.BlockSpec((tk, tn), lambda j, k: (k, j))]
    args = (ht, piece)
    if buf is not None:
        in_specs.append(pl.BlockSpec(memory_space=pl.ANY))
        args += (buf,)
    return _call(
        body, "in_bwd_w", (wd // tn, nk), in_specs,
        [pl.BlockSpec((tn, D), lambda j, k: (j + j0, 0))], [SDS((NCOL, D), BF16)], [pltpu.VMEM((D, tn), F32)],
        args, ("parallel", "arbitrary"), ex, None if buf is None else {2: 0})


def _my_id():
    return 4 * lax.axis_index("x") + 2 * lax.axis_index("y") + lax.axis_index("c")


def _peers():
    x, y, c = lax.axis_index("x"), lax.axis_index("y"), lax.axis_index("c")
    out = []
    for k in range(1, N_DEV):
        fx, fy, fc = (k >> 2) & 1, (k >> 1) & 1, k & 1
        px, py, pc = x ^ fx, y ^ fy, c ^ fc
        out.append(((px, py, pc), 4 * px + 2 * py + pc))
    return out


class _Exchange:
    def __init__(self, arrays, scatter):
        self.arrays = list(arrays)
        self.scatter = list(scatter)
        self.n = n = len(arrays)
        hbm = pl.BlockSpec(memory_space=pltpu.HBM)
        self.in_specs = [hbm] * n
        self.out_specs = [hbm] * n
        self.out_shape = [SDS((N_DEV,) + tuple(a.shape[1:] if s else a.shape), a.dtype)
                          for a, s in zip(arrays, scatter)]
        self.scratch = [pltpu.SemaphoreType.DMA((N_DEV - 1, n)), pltpu.SemaphoreType.DMA((N_DEV - 1, n)),
                        pltpu.SemaphoreType.DMA((n,))]

    def split(self, refs, n_in, n_out):
        n = self.n
        own_in = refs[:n_in]
        ex_in = refs[n_in:n_in + n]
        own_out = refs[n_in + n:n_in + n + n_out]
        ex_out = refs[n_in + n + n_out:n_in + 2 * n + n_out]
        rest = refs[n_in + 2 * n + n_out:]
        return own_in, own_out, rest[:-3], (ex_in, ex_out, rest[-3:])

    def _copy(self, ex, k, p, landing):
        in_refs, out_refs, (send_sems, recv_sems, _) = ex
        pos, pid = _peers()[p]
        return pltpu.make_async_remote_copy(
            src_ref=in_refs[k].at[pid] if self.scatter[k] else in_refs[k],
            dst_ref=out_refs[k].at[pid if landing else _my_id()],
            send_sem=send_sems.at[p, k], recv_sem=recv_sems.at[p, k],
            device_id=pos, device_id_type=pl.DeviceIdType.MESH)

    def _own(self, ex, k):
        in_refs, out_refs, (_, _, local_sems) = ex
        me = _my_id()
        return pltpu.make_async_copy(in_refs[k].at[me] if self.scatter[k] else in_refs[k], out_refs[k].at[me],
                                     local_sems.at[k])

    def start(self, ex):
        for k in range(self.n):
            self._own(ex, k).start()
        for p in range(N_DEV - 1):
            for k in range(self.n):
                self._copy(ex, k, p, False).start()

    def finish(self, ex):
        for p in range(N_DEV - 1):
            for k in range(self.n):
                self._copy(ex, k, p, True).wait_recv()
        for p in range(N_DEV - 1):
            for k in range(self.n):
                self._copy(ex, k, p, False).wait_send()
        for k in range(self.n):
            self._own(ex, k).wait()


def _exchange(arrays, scatter, name):
    ex = _Exchange(arrays, scatter)

    def body(*refs):
        _, _, _, exr = ex.split(refs, 0, 0)
        ex.start(exr)
        ex.finish(exr)

    return pl.pallas_call(body, name=name, in_specs=ex.in_specs, out_specs=ex.out_specs,
                          out_shape=ex.out_shape, scratch_shapes=ex.scratch)(*ex.arrays)


def _gather_two_level(shard, name):
    def body(x_ref, out_ref, send_sems, recv_sems, local_sem):
        x, y, c = lax.axis_index("x"), lax.axis_index("y"), lax.axis_index("c")
        me, sibling = (x, y, c), (x, y, 1 - c)
        chips = [(1 - x, y), (x, 1 - y), (1 - x, 1 - y)]
        slab = lambda px, py, pc: out_ref.at[4 * px + 2 * py + pc]

        def copy(k, block, to, src=None):
            return pltpu.make_async_remote_copy(
                src_ref=slab(*block) if src is None else src, dst_ref=slab(*block),
                send_sem=send_sems.at[k], recv_sem=recv_sems.at[k],
                device_id=to, device_id_type=pl.DeviceIdType.MESH)

        mine = pltpu.make_async_copy(x_ref, slab(*me), local_sem)
        mine.start()
        first = [copy(0, me, sibling, src=x_ref)] + [copy(1 + j, me, (*chip, c), src=x_ref)
                                                     for j, chip in enumerate(chips)]
        for cp in first:
            cp.start()
        passed = [copy(4 + j, (*chip, c), sibling) for j, chip in enumerate(chips)]
        for j, chip in enumerate(chips):
            copy(1 + j, (*chip, c), me).wait_recv()
            passed[j].start()
        copy(0, sibling, me).wait_recv()
        for j, chip in enumerate(chips):
            copy(4 + j, (*chip, 1 - c), me).wait_recv()
        for cp in first + passed:
            cp.wait_send()
        mine.wait()

    hbm = pl.BlockSpec(memory_space=pltpu.HBM)
    return pl.pallas_call(
        body, name=name, in_specs=[hbm], out_specs=hbm,
        out_shape=SDS((N_DEV,) + shard.shape, shard.dtype),
        scratch_shapes=[pltpu.SemaphoreType.DMA((N_DEV - 1,)), pltpu.SemaphoreType.DMA((N_DEV - 1,)),
                        pltpu.SemaphoreType.DMA],
    )(shard)


def _adamw_update(g, w, m, v):
    c1 = 1.0 / (1.0 - ADAM_B1 ** ADAM_STEP)
    c2 = 1.0 / (1.0 - ADAM_B2 ** ADAM_STEP)
    mn = ADAM_B1 * m + (1.0 - ADAM_B1) * g
    vn = ADAM_B2 * v + (1.0 - ADAM_B2) * (g * g)
    return -ADAM_LR * ((mn * c1) / (jnp.sqrt(vn * c2) + ADAM_EPS) + ADAM_WD * w), mn, vn


def _adamw_small(parts, w, m, v):
    n = len(w)

    def body(*refs):
        p_refs = (refs[0:n], refs[n:2 * n])
        w_refs, m_refs, v_refs = refs[2 * n:3 * n], refs[3 * n:4 * n], refs[4 * n:5 * n]
        outs = refs[5 * n:]
        for k in range(n):
            g_ref, d_ref, mo_ref, vo_ref = outs[4 * k:4 * k + 4]
            for l in range(2):
                at = (slice(l, l + 1),) if len(w_refs[k].shape) == 2 else (l,)
                g = p_refs[l][k][0]
                for s in range(1, N_DEV):
                    g = g + p_refs[l][k][s]
                delta, mn, vn = _adamw_update(g, w_refs[k][at], m_refs[k][at], v_refs[k][at])
                g_ref[at] = g
                d_ref[at] = delta
                mo_ref[at] = mn
                vo_ref[at] = vn

    vmem = pl.BlockSpec(memory_space=pltpu.VMEM)
    res = pl.pallas_call(
        body, name="adamw_replicated", in_specs=[vmem] * (5 * n), out_specs=[vmem] * (4 * n),
        out_shape=[SDS(a.shape, F32) for a in w for _ in range(4)],
        compiler_params=pltpu.CompilerParams(vmem_limit_bytes=VMEM_LIMIT),
    )(*parts[0], *parts[1], *w, *m, *v)
    return [res[4 * k:4 * k + 4] for k in range(n)]


def _adamw_sum(parts0, parts1, w, m, v, name):
    _, R, C = w.shape
    tr = R
    while tr * C > 256 * 1024 and tr % 32 == 0:
        tr //= 2

    def body(p0_ref, p1_ref, w_ref, m_ref, v_ref, g_ref, d_ref, mo_ref, vo_ref):
        def update(p_ref):
            g = p_ref[0].astype(F32)
            for s in range(1, N_DEV):
                g = g + p_ref[s].astype(F32)
            g_ref[...] = g
            d_ref[...], mo_ref[...], vo_ref[...] = _adamw_update(g, w_ref[...], m_ref[...], v_ref[...])

        @pl.when(pl.program_id(0) == 0)
        def _():
            update(p0_ref)

        @pl.when(pl.program_id(0) == 1)
        def _():
            update(p1_ref)

    blk = pl.BlockSpec((None, tr, C), lambda l, i: (l, i, 0))
    return pl.pallas_call(
        body, name=name, grid=(2, R // tr),
        in_specs=[pl.BlockSpec((N_DEV, tr, C), lambda l, i: (0, i * (1 - l), 0)),
                  pl.BlockSpec((N_DEV, tr, C), lambda l, i: (0, i * l, 0)), blk, blk, blk],
        out_specs=[blk, blk, blk, blk],
        out_shape=[SDS((2, R, C), F32)] * 4,
        compiler_params=_cparams(("arbitrary", "arbitrary")),
    )(parts0, parts1, w, m, v)


def _layer_fwd(x, P, skew, S, rest, ex, tgt=None):
    z, ht, *got0 = _in_proj(x, P["pre_g"], P["w_in_t"], ex[0])
    P = {**P, **rest(got0)}
    ac, ap, u1, *got1 = _mix_fwd(z, P["conv_dw"], P["conv_dw_b"], P["conv_ln_g"], P["conv_ln_b"],
                                 P["pool_w"], P["pool_b"], P["pool_scale"], S, ex[1])
    o, *got2 = _attn_fwd(z, skew, S, ex[2])
    out, kept, got3 = _out_fwd(x, ac, o, ap, z, P["w_conv_out"], P["w_attn_out"], P["w_pool_out"], P["w_out"],
                               P["post_g"], ex[3], tgt)
    return out, (x, z, ht, ac, o, ap, u1, kept), P, (got0, got1, got2, got3)


def _layer_bwd(dout, saved, P, skew, S, ex=(None, None), mix_ex=None, win_ex=None):
    x, z, ht, ac, o, ap, u1, kept = saved
    (dac, dao, dag, dap, dgm, dwco, dwao, dwpo, dwout, dpostg, *got0) = _out_bwd(
        dout, ac, o, ap, z, kept, P["w_conv_out"], P["w_attn_out"], P["w_pool_out"], P["w_out"], P["post_g"], ex[0])
    grads = dict(post_norm_g=dpostg, w_conv_out=dwco, w_attn_out=dwao, w_pool_out=dwpo, w_out=dwout)
    dq, dk, dv, grads["dskew"], *got1 = _attn_bwd(z, dao, skew, S, ex[1])
    (dzc, dzp, grads["conv_dw"], grads["conv_dw_b"], grads["conv_ln_g"], grads["conv_ln_b"], grads["pool_w"],
     grads["pool_b"], grads["pool_scale"], *got2) = _mix_bwd(
        z, u1, dac, dap, P["conv_dw"], P["conv_dw_b"], P["conv_ln_g"], P["conv_ln_b"],
        P["pool_w"], P["pool_b"], P["pool_scale"], S, mix_ex(grads) if mix_ex else None)
    pieces = [dzc, dq, dk, dv, dag, dzp, dgm]
    buf, row0 = None, 0
    for p in pieces:
        (buf,) = _in_bwd_w(ht, p, row0, buf)
        row0 += p.shape[1]
    grads["w_in_t"] = buf
    dx, grads["pre_norm_g"], *got3 = _in_bwd_x(pieces, P["w_in_t"], x, P["pre_g"], dout,
                                               win_ex(grads) if win_ex else None)
    return dx, grads, (got0, got1, got2, got3)


WEIGHT_NAMES = ("pre_norm_g", "post_norm_g", "w_in", "conv_dw", "conv_dw_b", "conv_ln_g", "conv_ln_b",
                "w_conv_out", "rel_bias", "w_attn_out", "pool_w", "pool_b", "pool_scale", "w_pool_out", "w_out")
SHARDED = ("w_in", "w_conv_out", "w_attn_out", "w_pool_out", "w_out", "conv_dw")
OUT_PROJ = ("w_conv_out", "w_attn_out", "w_pool_out", "w_out")
REST = tuple(n for n in WEIGHT_NAMES if n not in ("w_in", "pre_norm_g"))


def _cols_from_slabs(g):
    return g.transpose(1, 0, 2).reshape(g.shape[1], N_DEV * g.shape[2])


def _slabs_from_cols(full):
    r, wd = full.shape
    return full.reshape(r, N_DEV, wd // N_DEV).transpose(1, 0, 2)


def _rest_shards(weights, l):
    return [weights["w_conv_out"][l].astype(BF16), weights["w_attn_out"][l].astype(BF16),
            weights["w_pool_out"][l].astype(BF16), weights["w_out"][l].astype(BF16), weights["conv_dw"][l]]


def _rest_weights(got):
    wco, wao, wpo, wout, cdw = got
    return dict(w_conv_out=_cols_from_slabs(wco), w_attn_out=_cols_from_slabs(wao),
                w_pool_out=_cols_from_slabs(wpo), w_out=wout.reshape(D, D), conv_dw=_cols_from_slabs(cdw))


def _grad_arrays(g, names):
    make = {"w_in": lambda: g["w_in_t"].reshape(N_DEV, NCOL // N_DEV, D),
            "w_conv_out": lambda: _slabs_from_cols(g["w_conv_out"].astype(BF16)),
            "w_attn_out": lambda: _slabs_from_cols(g["w_attn_out"].astype(BF16)),
            "w_pool_out": lambda: _slabs_from_cols(g["w_pool_out"].astype(BF16)),
            "w_out": lambda: g["w_out"].astype(BF16).reshape(N_DEV, D // N_DEV, D),
            "conv_dw": lambda: _slabs_from_cols(g["conv_dw"].astype(BF16)),
            "rel_bias": lambda: jnp.dot(g["dskew"], jnp.asarray(_skew_select().T), precision=lax.Precision.HIGHEST),
            "pool_b": lambda: g["pool_b"].reshape(4, 128)}
    return [make[n]() if n in make else g[n] for n in names]


def _grad_exchange(g, names):
    return _Exchange(_grad_arrays(g, names), [n in SHARDED for n in names])


def kernel(x, pre_norm_g, post_norm_g, w_in, conv_dw, conv_dw_b, conv_ln_g, conv_ln_b, w_conv_out, rel_bias, w_attn_out, pool_w, pool_b, pool_scale, w_pool_out, w_out, loss_target, m_pre_norm_g, m_post_norm_g, m_w_in, m_conv_dw, m_conv_dw_b, m_conv_ln_g, m_conv_ln_b, m_w_conv_out, m_rel_bias, m_w_attn_out, m_pool_w, m_pool_b, m_pool_scale, m_w_pool_out, m_w_out, v_pre_norm_g, v_post_norm_g, v_w_in, v_conv_dw, v_conv_dw_b, v_conv_ln_g, v_conv_ln_b, v_w_conv_out, v_rel_bias, v_w_attn_out, v_pool_w, v_pool_b, v_pool_scale, v_w_pool_out, v_w_out):
    given = dict(locals())
    weights = {n: given[n] for n in WEIGHT_NAMES}
    nb, S, _ = x.shape
    T = nb * S
    L = pre_norm_g.shape[0]
    assert L == 2
    x2 = x.reshape(T, D)
    tgt2 = loss_target.reshape(T, D)
    skews = [_skew_table(rel_bias[l]) for l in range(L)]

    def local_params(l):
        return dict(pre_g=pre_norm_g[l:l + 1], post_g=post_norm_g[l:l + 1], conv_dw_b=conv_dw_b[l:l + 1],
                    conv_ln_g=conv_ln_g[l:l + 1], conv_ln_b=conv_ln_b[l:l + 1], pool_w=pool_w[l].astype(BF16),
                    pool_b=pool_b[l].reshape(1, BW), pool_scale=pool_scale[l:l + 1])

    win0 = w_in[0].T.astype(BF16)
    win1 = w_in[1].T.astype(BF16)
    half = win1.shape[0] // 2
    w_in_t0 = _gather_two_level(win0, "gather_w_in_0")
    gather = lambda arrays: _Exchange(arrays, [False] * len(arrays))
    (h,), saved0, P0, (got_rest0, got_a, got_b, got_rest1) = _layer_fwd(
        x2, {**local_params(0), "w_in_t": w_in_t0.reshape(NCOL, D)}, skews[0], S, _rest_weights,
        (gather(_rest_shards(weights, 0)), gather([win1[:half]]), gather([win1[half:]]),
         gather(_rest_shards(weights, 1))))
    w_in_t1 = jnp.concatenate([got_a[0], got_b[0]], axis=1).reshape(NCOL, D)
    (dout, lsum), saved1, P1, _ = _layer_fwd(h, {**local_params(1), "w_in_t": w_in_t1}, skews[1], S,
                                             lambda _: _rest_weights(got_rest1), (None,) * 4, tgt2)
    loss = lax.psum(lsum[0, 0], MESH_AXES) * (0.5 / D)

    dout, g1, _ = _layer_bwd(dout, saved1, P1, skews[1], S)
    others = REST + ("pre_norm_g",)
    dout, g0, (got_others1, got_win1, got_outp0, got_win0) = _layer_bwd(
        dout, saved0, P0, skews[0], S, (_grad_exchange(g1, others), _grad_exchange(g1, ("w_in",))),
        lambda g: _grad_exchange(g, OUT_PROJ), lambda g: _grad_exchange(g, ("w_in",)))
    late = tuple(n for n in others if n not in OUT_PROJ)
    got_late0 = _exchange(_grad_arrays(g0, late), [n in SHARDED for n in late], "exchange_small_grads_0")
    parts = [{"w_in": got_win0[0], **dict(zip(OUT_PROJ, got_outp0)), **dict(zip(late, got_late0))},
             {"w_in": got_win1[0], **dict(zip(others, got_others1))}]
    grad_x = dout.reshape(x.shape)

    outs = {}
    small = [n for n in WEIGHT_NAMES if n not in SHARDED]
    res = _adamw_small([[parts[l][n] for n in small] for l in range(L)], [weights[n] for n in small],
                       [given["m_" + n] for n in small], [given["v_" + n] for n in small])
    outs.update(zip(small, res))
    for n in SHARDED:
        view = (lambda a: a.transpose(0, 2, 1)) if n == "w_in" else (lambda a: a)
        res = _adamw_sum(parts[0][n], parts[1][n], view(weights[n]), view(given["m_" + n]), view(given["v_" + n]),
                         "adamw_" + n)
        outs[n] = [view(a) for a in res]
    return (loss, grad_x, *[outs[n][0] for n in WEIGHT_NAMES], *[outs[n][1] for n in WEIGHT_NAMES],
            *[outs[n][2] for n in WEIGHT_NAMES], *[outs[n][3] for n in WEIGHT_NAMES])
```

```python
import functools

import numpy as np
import jax
import jax.numpy as jnp
from jax import lax
from jax.experimental import pallas as pl
from jax.experimental.pallas import tpu as pltpu

F32 = jnp.float32
BF16 = jnp.bfloat16
SDS = jax.ShapeDtypeStruct

D = 1024
BW = 512
NCOL = 7680
EPS = 1e-6
NEG = -1e30
HEADS = 8
HD = 64
CHUNK = 64
LEFT = 8
MAX_REL = 256
TQ = 256
KW = 768
CONV_K = 31
WINDOWS = (2, 4, 8, 16)
HALO = 32
RC = 32
N_DEV = 8
MESH_AXES = ("x", "y", "c")

ADAM_LR = 0.001
ADAM_B1 = 0.9
ADAM_B2 = 0.999
ADAM_EPS = 1e-08
ADAM_WD = 0.01
ADAM_STEP = 10

VMEM_LIMIT = 56 * 1024 * 1024

CB_CA, CB_CB, CB_CG, CB_Q, CB_K, CB_V, CB_AG, CB_PI, CB_PG = range(9)
DZ_PIECES = (("conv", 1536), ("q", 512), ("k", 512), ("v", 512), ("ag", 512), ("pool", 1024), ("gm", 3072))


def _cparams(sem):
    return pltpu.CompilerParams(dimension_semantics=sem, vmem_limit_bytes=VMEM_LIMIT)


def _sig(x):
    return 0.5 * jnp.tanh(0.5 * x) + 0.5


def _dsilu(x, s):
    return s * (1.0 + x * (1.0 - s))


def _nt(a, b):
    return lax.dot_general(a, b, (((1,), (1,)), ((), ())), preferred_element_type=F32)


def _tn(a, b):
    return lax.dot_general(a, b, (((0,), (0,)), ((), ())), preferred_element_type=F32)


def _nn(a, b):
    return jnp.dot(a, b, preferred_element_type=F32)


def _rows8(x):
    return x[0:8] + x[8:16] + x[16:24] + x[24:32]


def _call(body, name, grid, in_specs, out_specs, out_shape, scratch, args, sem, ex=None, aliases=None):
    aliases = aliases or {}
    if ex is None:
        return pl.pallas_call(body, name=name, grid=grid, in_specs=in_specs, out_specs=out_specs,
                              out_shape=out_shape, scratch_shapes=scratch, input_output_aliases=aliases,
                              compiler_params=_cparams(sem))(*args)
    n_in, n_out = len(in_specs), len(out_specs)
    steps = int(np.prod(grid))

    def carrier(*refs):
        own_in, own_out, own_scr, exr = ex.split(refs, n_in, n_out)
        step = pl.program_id(0)
        for axis in range(1, len(grid)):
            step = step * grid[axis] + pl.program_id(axis)

        @pl.when(step == 0)
        def _():
            ex.start(exr)

        body(*own_in, *own_out, *own_scr)

        @pl.when(step == steps - 1)
        def _():
            ex.finish(exr)

    return pl.pallas_call(
        carrier, name=name + "_carrier", grid=grid, in_specs=in_specs + ex.in_specs,
        out_specs=out_specs + ex.out_specs, out_shape=out_shape + ex.out_shape,
        scratch_shapes=scratch + ex.scratch, input_output_aliases=aliases,
        compiler_params=_cparams(("arbitrary",) * len(grid)),
    )(*args, *ex.arrays)


def _in_proj(x, g, wt, ex=None):
    T = x.shape[0]
    tm = 512
    tn = 1536

    def body(x_ref, g_ref, w_ref, z_ref, ht_ref):
        xv = x_ref[...]
        r = lax.rsqrt(jnp.mean(xv * xv, axis=-1, keepdims=True) + EPS)
        h = xv * r * g_ref[...]
        hb = h.astype(BF16)
        ht_ref[...] = h.T.astype(BF16)
        for c in range(NCOL // tn):
            z_ref[:, c * tn:(c + 1) * tn] = _nt(hb, w_ref[c * tn:(c + 1) * tn, :]).astype(BF16)

    return _call(
        body, "in_proj", (T // tm,),
        [pl.BlockSpec((tm, D), lambda i: (i, 0)), pl.BlockSpec((1, D), lambda i: (0, 0)),
         pl.BlockSpec((NCOL, D), lambda i: (0, 0))],
        [pl.BlockSpec((tm, NCOL), lambda i: (i, 0)), pl.BlockSpec((D, tm), lambda i: (0, i))],
        [SDS((T, NCOL), BF16), SDS((D, T), BF16)], [],
        (x, g, wt), ("parallel",), ex)


def _stencil(load, w_ref, lo, hi, tap_of):
    out = None
    for r in range(8):
        n = RC if r == 0 else RC + 8
        v = None
        for q in range((hi - r) // 8 + 1):
            o = 8 * q + r
            if o < lo:
                continue
            j = tap_of(o)
            term = w_ref[j:j + 1, :] * load(q, n)
            v = term if v is None else v + term
        if v is None:
            continue
        if r:
            v = pltpu.roll(v, n - r, axis=0)[0:RC]
        out = v if out is None else out + v
    return out


def _layer_norm_fwd(u1):
    mu = jnp.mean(u1, axis=-1, keepdims=True)
    xc = u1 - mu
    rstd = lax.rsqrt(jnp.mean(xc * xc, axis=-1, keepdims=True) + EPS)
    return xc * rstd, rstd


def _window_sums(x, w, back):
    n = x.shape[0]
    s = x
    k = 1
    while k < w:
        s = s + pltpu.roll(s, k if back else n - k, axis=0)
        k *= 2
    return s


def _pool_chunk(pwin, t_first):
    t = t_first + lax.broadcasted_iota(jnp.int32, (RC, 128), 0)
    outs = []
    for g, w in enumerate(WINDOWS):
        x = pwin[:, g * 128:(g + 1) * 128]
        s = _window_sums(x, w, True)
        cnt = jnp.minimum(t + 1, w).astype(F32)
        outs.append(s[HALO:HALO + RC] / cnt - x[HALO:HALO + RC])
    return outs


def _mix_fwd(z, dw, dwb, lng, lnb, pw, pb, ps, S, ex=None):
    T = z.shape[0]
    tm = min(S, 1024)
    ts = S // tm
    nh = tm // HALO

    def body(ca_ref, cb_ref, cg_ref, pi_ref, pg_ref, cah_ref, cbh_ref, pih_ref,
             dw_ref, dwb_ref, lng_ref, lnb_ref, pw_ref, pb_ref, ps_ref,
             ac_ref, ap_ref, u1_ref, ubuf, pbuf, pooled):
        i = pl.program_id(0)
        keep = jnp.where((i % ts) == 0, 0.0, 1.0)
        ubuf[0:HALO, :] = cah_ref[...].astype(F32) * _sig(cbh_ref[...].astype(F32)) * keep
        ubuf[HALO:HALO + tm, :] = ca_ref[...].astype(F32) * _sig(cb_ref[...].astype(F32))
        pbuf[0:HALO, :] = pih_ref[...].astype(F32) * keep
        pbuf[HALO:HALO + tm, :] = pi_ref[...].astype(F32)
        t0 = (i % ts) * tm

        def chunk(c, carry):
            base = pl.multiple_of(c * RC, RC)
            load = lambda q, n: ubuf[pl.ds(base + 8 * q, n), :]
            u1 = _stencil(load, dw_ref, 2, CONV_K + 1, lambda o: o - 2) + dwb_ref[...]
            u1_ref[pl.ds(base, RC), :] = u1
            n, _ = _layer_norm_fwd(u1)
            u2 = n * lng_ref[...] + lnb_ref[...]
            u3 = u2 * _sig(u2)
            cg = cg_ref[pl.ds(base, RC), :].astype(F32)
            ac_ref[pl.ds(base, RC), :] = (u3 * cg * _sig(cg)).astype(BF16)
            pwin = pbuf[pl.ds(base, RC + HALO), :]
            outs = _pool_chunk(pwin, t0 + base)
            for g in range(4):
                pooled[pl.ds(base, RC), g * 128:(g + 1) * 128] = outs[g].astype(BF16)
            return carry

        lax.fori_loop(0, tm // RC, chunk, 0, unroll=4)
        pg = pg_ref[...].astype(F32)
        spg = pg * _sig(pg)
        for g in range(4):
            sl = slice(g * 128, (g + 1) * 128)
            mixed = (_nn(pooled[:, sl], pw_ref[g]) + pb_ref[:, sl]) * ps_ref[:, sl]
            ap_ref[:, sl] = (mixed * spg[:, sl]).astype(BF16)

    def zmain(cb):
        return pl.BlockSpec((tm, BW), lambda i: (i, cb))

    def zprev(cb):
        return pl.BlockSpec((HALO, BW), lambda i: (jnp.maximum(i * nh - 1, 0), cb))

    full = lambda shape: pl.BlockSpec(shape, lambda i: (0,) * len(shape))
    row = pl.BlockSpec((tm, BW), lambda i: (i, 0))
    return _call(
        body, "mix_fwd", (T // tm,),
        [zmain(CB_CA), zmain(CB_CB), zmain(CB_CG), zmain(CB_PI), zmain(CB_PG),
         zprev(CB_CA), zprev(CB_CB), zprev(CB_PI),
         full((CONV_K, BW)), full((1, BW)), full((1, BW)), full((1, BW)),
         full((4, 128, 128)), full((1, BW)), full((1, BW))],
        [row, row, row], [SDS((T, BW), BF16), SDS((T, BW), BF16), SDS((T, BW), F32)],
        [pltpu.VMEM((HALO + tm, BW), F32), pltpu.VMEM((HALO + tm, BW), F32), pltpu.VMEM((tm, BW), BF16)],
        (z, z, z, z, z, z, z, z, dw, dwb, lng, lnb, pw, pb, ps), ("parallel",), ex)


def _attn_specs(nq):
    def kv(cb, off):
        return pl.BlockSpec((TQ, BW), lambda i: (i - jnp.minimum(off, i % nq), cb))
    return [pl.BlockSpec((TQ, BW), lambda i: (i, CB_Q)),
            kv(CB_K, 2), kv(CB_K, 1), kv(CB_K, 0), kv(CB_V, 2), kv(CB_V, 1), kv(CB_V, 0)]


def _softmax_rows(s):
    m = jnp.max(s, axis=-1, keepdims=True)
    e = jnp.exp(s - m)
    return e / jnp.sum(e, axis=-1, keepdims=True)


NSKEW = 1024


def _skew_table(table):
    return jnp.dot(table, jnp.asarray(_skew_select()), precision=lax.Precision.HIGHEST)


def _skew_select():
    d = np.arange(TQ + KW - 1)
    idx = np.clip(3 * TQ - 1 - d, -MAX_REL, MAX_REL) + MAX_REL
    sel = np.zeros((2 * MAX_REL + 1, NSKEW), np.float32)
    sel[idx, d] = 1.0
    return sel


def _bias_from_skew(f_ref, bias_scr):
    qi = lax.broadcasted_iota(jnp.int32, (TQ, KW), 0)
    kj = lax.broadcasted_iota(jnp.int32, (TQ, KW), 1)
    lo = (qi // CHUNK) * CHUNK
    band = jnp.where((kj >= lo) & (kj < lo + (LEFT + 1) * CHUNK), 0.0, NEG)
    for h in range(HEADS):
        rows = jnp.broadcast_to(f_ref[h:h + 1, :], (TQ, NSKEW))
        rows = pltpu.roll(rows, NSKEW - (TQ - 1), axis=1, stride=1, stride_axis=0)
        bias_scr[h] = rows[:, 0:KW] + band


def _skew_from_bias(db):
    i = lax.broadcasted_iota(jnp.int32, (TQ, TQ), 0)
    j = lax.broadcasted_iota(jnp.int32, (TQ, TQ), 1)
    flip = jnp.where(i + j == TQ - 1, 1.0, 0.0).astype(BF16)
    hi = db.astype(BF16)
    lo = (db - hi.astype(F32)).astype(BF16)
    rev = _nn(flip, hi) + _nn(flip, lo)
    rev = jnp.concatenate([rev, jnp.zeros((TQ, NSKEW - KW), F32)], axis=1)
    return jnp.sum(pltpu.roll(rev, 0, axis=1, stride=1, stride_axis=0), axis=0, keepdims=True)


def _attn_fwd(z, f, S, ex=None):
    T = z.shape[0]
    nq = S // TQ

    def body(q_ref, k2_ref, k1_ref, k0_ref, v2_ref, v1_ref, v0_ref, f_ref, o_ref, kbuf, vbuf, b_scr):
        @pl.when(pl.program_id(0) == 0)
        def _():
            _bias_from_skew(f_ref, b_scr)

        qb = pl.program_id(0) % nq
        kbuf[0:TQ, :] = k2_ref[...]
        kbuf[TQ:2 * TQ, :] = k1_ref[...]
        kbuf[2 * TQ:KW, :] = k0_ref[...]
        vbuf[0:TQ, :] = v2_ref[...]
        vbuf[TQ:2 * TQ, :] = v1_ref[...]
        vbuf[2 * TQ:KW, :] = v0_ref[...]
        lane = lax.broadcasted_iota(jnp.int32, (1, 128), 1)

        def attend(lo):
            def scores(h):
                sl = slice((h // 2) * 128, (h // 2 + 1) * 128)
                qp = q_ref[:, sl] * 0.125
                qm = jnp.where((lane < HD) if h % 2 == 0 else (lane >= HD), qp, jnp.zeros_like(qp))
                return _nt(qm, kbuf[lo:KW, sl]) + b_scr[h, :, lo:KW]

            s = scores(0)
            acc = None
            for h in range(HEADS):
                s_next = scores(h + 1) if h + 1 < HEADS else None
                sl = slice((h // 2) * 128, (h // 2 + 1) * 128)
                e = jnp.exp(s - jnp.max(s, axis=-1, keepdims=True))
                vp = vbuf[lo:KW, sl]
                vm = jnp.where((lane < HD) if h % 2 == 0 else (lane >= HD), vp, jnp.zeros_like(vp))
                o = _nn(e.astype(BF16), vm) * (1.0 / jnp.sum(e, axis=-1, keepdims=True))
                acc = o if h % 2 == 0 else acc + o
                if h % 2 == 1:
                    o_ref[:, sl] = acc.astype(BF16)
                s = s_next

        for nblk in (1, 2, 3):
            pl.when(jnp.minimum(qb, 2) == nblk - 1)(functools.partial(attend, (3 - nblk) * TQ))

    full = lambda shape: pl.BlockSpec(shape, lambda i: (0,) * len(shape))
    return _call(
        body, "attn_fwd", (T // TQ,),
        _attn_specs(nq) + [full((HEADS, NSKEW))],
        [pl.BlockSpec((TQ, BW), lambda i: (i, 0))], [SDS((T, BW), BF16)],
        [pltpu.VMEM((KW, BW), BF16), pltpu.VMEM((KW, BW), BF16), pltpu.VMEM((HEADS, TQ, KW), F32)],
        (z, z, z, z, z, z, z, f), ("arbitrary",), ex)


def _gates(gl_ref, gh_ref):
    gl = _sig(gl_ref[...].astype(F32))
    gh = _sig(gh_ref[...].astype(F32))
    return (gl[:, 0:D], jnp.concatenate([gl[:, D:1536], gh[:, 0:512]], axis=1), gh[:, 512:1536])


def _out_specs_in(tm):
    row = lambda w: pl.BlockSpec((tm, w), lambda i: (i, 0))
    full = lambda shape: pl.BlockSpec(shape, lambda i: (0,) * len(shape))
    return [row(BW), row(BW), row(BW),
            pl.BlockSpec((tm, BW), lambda i: (i, CB_AG)),
            pl.BlockSpec((tm, 1536), lambda i: (i, 3)),
            pl.BlockSpec((tm, 1536), lambda i: (i, 4)),
            full((BW, D)), full((BW, D)), full((BW, D)), full((D, D)), full((1, D))]


def _out_fwd(x, ac, o, ap, z, wco, wao, wpo, wout, postg, ex=None, tgt=None):
    T = x.shape[0]
    tm = 512
    last = tgt is not None

    def body(ac_ref, o_ref, ap_ref, ag_ref, gl_ref, gh_ref, wco_ref, wao_ref, wpo_ref, wout_ref, pg_ref,
             x_ref, *rest):
        ag = ag_ref[...].astype(F32)
        aat = (o_ref[...].astype(F32) * ag * _sig(ag)).astype(BF16)
        gates = _gates(gl_ref, gh_ref)
        acts = (ac_ref[...], aat, ap_ref[...])
        merged = None
        for b, w_ref in enumerate((wco_ref, wao_ref, wpo_ref)):
            yb = _nn(acts[b], w_ref[...])
            rest[-4 + b][...] = yb.astype(BF16)
            merged = gates[b] * yb if merged is None else merged + gates[b] * yb
        y = _nn(merged.astype(BF16), wout_ref[...])
        rest[-1][...] = y.astype(BF16)
        ry = lax.rsqrt(jnp.mean(y * y, axis=-1, keepdims=True) + EPS)
        out = x_ref[...] + y * ry * pg_ref[...]
        if not last:
            rest[0][...] = out
            return
        t_ref, d_ref, l_ref = rest[:3]

        @pl.when(pl.program_id(0) == 0)
        def _():
            l_ref[...] = jnp.zeros_like(l_ref)
        d = out - t_ref[...]
        d_ref[...] = d * (1.0 / D)
        l_ref[...] += jnp.sum(jnp.sum(d * d, axis=0, keepdims=True), axis=1, keepdims=True)

    row = pl.BlockSpec((tm, D), lambda i: (i, 0))
    kept_specs, kept_shapes = [row] * 4, [SDS((T, D), BF16)] * 4
    if not last:
        res = _call(body, "out_fwd", (T // tm,), _out_specs_in(tm) + [row], [row] + kept_specs,
                    [SDS((T, D), F32)] + kept_shapes, [],
                    (ac, o, ap, z, z, z, wco, wao, wpo, wout, postg, x), ("parallel",), ex)
        return res[:1], res[1:5], res[5:]
    res = _call(body, "out_fwd_loss", (T // tm,), _out_specs_in(tm) + [row, row],
                [row, pl.BlockSpec((1, 128), lambda i: (0, 0))] + kept_specs,
                [SDS((T, D), F32), SDS((1, 128), F32)] + kept_shapes, [],
                (ac, o, ap, z, z, z, wco, wao, wpo, wout, postg, x, tgt), ("arbitrary",), ex)
    return res[:2], res[2:6], res[6:]


def _out_bwd(dout, ac, o, ap, z, kept, wco, wao, wpo, wout, postg, ex=None):
    T = dout.shape[0]
    tm = 256

    def body(ac_ref, o_ref, ap_ref, ag_ref, gl_ref, gh_ref, wco_ref, wao_ref, wpo_ref, wout_ref, pg_ref, do_ref,
             yc_ref, ya_ref, yp_ref, y_ref,
             dac_ref, dao_ref, dag_ref, dap_ref, dgm_ref, dwco_ref, dwao_ref, dwpo_ref, dwout_ref, dpg_ref):
        @pl.when(pl.program_id(0) == 0)
        def _():
            for r in (dwco_ref, dwao_ref, dwpo_ref, dwout_ref, dpg_ref):
                r[...] = jnp.zeros_like(r)

        ag = ag_ref[...].astype(F32)
        sag = _sig(ag)
        ov = o_ref[...].astype(F32)
        acts = (ac_ref[...], (ov * ag * sag).astype(BF16), ap_ref[...])
        ws = (wco_ref, wao_ref, wpo_ref)
        gates = _gates(gl_ref, gh_ref)
        ys = [r[...].astype(F32) for r in (yc_ref, ya_ref, yp_ref)]
        merged = (gates[0] * ys[0] + gates[1] * ys[1] + gates[2] * ys[2]).astype(BF16)
        y = y_ref[...].astype(F32)
        ry = lax.rsqrt(jnp.mean(y * y, axis=-1, keepdims=True) + EPS)
        yn = y * ry
        dout_v = do_ref[...]
        dpg_ref[...] += jnp.sum(dout_v * yn, axis=0, keepdims=True)
        dyn = dout_v * pg_ref[...]
        dy = (ry * (dyn - yn * jnp.mean(dyn * yn, axis=-1, keepdims=True))).astype(BF16)
        dmerged = _nt(dy, wout_ref[...])
        dwout_ref[...] += _tn(merged, dy)
        dws = (dwco_ref, dwao_ref, dwpo_ref)
        das = []
        for b in range(3):
            gb = gates[b]
            dgm_ref[:, b * D:(b + 1) * D] = (dmerged * ys[b] * gb * (1.0 - gb)).astype(BF16)
            dyb = (dmerged * gb).astype(BF16)
            dws[b][...] += _tn(acts[b], dyb)
            das.append(_nt(dyb, ws[b][...]))
        dac_ref[...] = das[0].astype(BF16)
        dap_ref[...] = das[2].astype(BF16)
        dao_ref[...] = (das[1] * ag * sag).astype(BF16)
        dag_ref[...] = (das[1] * ov * _dsilu(ag, sag)).astype(BF16)

    row = lambda w: pl.BlockSpec((tm, w), lambda i: (i, 0))
    full = lambda shape: pl.BlockSpec(shape, lambda i: (0,) * len(shape))
    return _call(
        body, "out_bwd", (T // tm,), _out_specs_in(tm) + [row(D)] * 5,
        [row(BW), row(BW), row(BW), row(BW), row(3 * D),
         full((BW, D)), full((BW, D)), full((BW, D)), full((D, D)), full((1, D))],
        [SDS((T, BW), BF16)] * 4 + [SDS((T, 3 * D), BF16)]
        + [SDS((BW, D), F32)] * 3 + [SDS((D, D), F32), SDS((1, D), F32)], [],
        (ac, o, ap, z, z, z, wco, wao, wpo, wout, postg, dout, *kept), ("arbitrary",), ex)


def _attn_bwd(z, dao, f, S, ex=None):
    T = z.shape[0]
    nq = S // TQ
    nsteps = T // TQ

    def body(q_ref, k2_ref, k1_ref, k0_ref, v2_ref, v1_ref, v0_ref, do_ref, f_ref,
             dq_ref, dk_ref, dv_ref, df_ref, kbuf, vbuf, dkacc, dvacc, b_scr, db_scr):
        i = pl.program_id(0)
        qb = i % nq

        @pl.when(i == 0)
        def _():
            _bias_from_skew(f_ref, b_scr)
            db_scr[...] = jnp.zeros_like(db_scr)

        @pl.when(qb == 0)
        def _():
            dkacc[...] = jnp.zeros_like(dkacc)
            dvacc[...] = jnp.zeros_like(dvacc)

        kbuf[0:TQ, :] = k2_ref[...]
        kbuf[TQ:2 * TQ, :] = k1_ref[...]
        kbuf[2 * TQ:KW, :] = k0_ref[...]
        vbuf[0:TQ, :] = v2_ref[...]
        vbuf[TQ:2 * TQ, :] = v1_ref[...]
        vbuf[2 * TQ:KW, :] = v0_ref[...]
        lane = lax.broadcasted_iota(jnp.int32, (1, 128), 1)
        row0 = pl.multiple_of(qb * TQ, TQ)

        def attend(lo):
            def first_matmuls(h):
                sl = slice((h // 2) * 128, (h // 2 + 1) * 128)
                msk = (lane < HD) if h % 2 == 0 else (lane >= HD)
                qp = q_ref[:, sl] * 0.125
                dop = do_ref[:, sl]
                qm = jnp.where(msk, qp, jnp.zeros_like(qp))
                dom = jnp.where(msk, dop, jnp.zeros_like(dop))
                s = _nt(qm, kbuf[lo:KW, sl]) + b_scr[h, :, lo:KW]
                return s, _nt(dom, vbuf[lo:KW, sl]), qm, dom

            cur = first_matmuls(0)
            dq_acc = dk_acc = dv_acc = None
            for h in range(HEADS):
                nxt = first_matmuls(h + 1) if h + 1 < HEADS else None
                s, dp, qm, dom = cur
                sl = slice((h // 2) * 128, (h // 2 + 1) * 128)
                e = jnp.exp(s - jnp.max(s, axis=-1, keepdims=True))
                p = e * (1.0 / jnp.sum(e, axis=-1, keepdims=True))
                ds = p * (dp - jnp.sum(p * dp, axis=-1, keepdims=True))
                db_scr[h, :, lo:KW] += ds
                dsb = ds.astype(BF16)
                kp = kbuf[lo:KW, sl]
                km = jnp.where((lane < HD) if h % 2 == 0 else (lane >= HD), kp, jnp.zeros_like(kp))
                dq_h = _nn(dsb, km) * 0.125
                dk_h = _tn(dsb, qm)
                dv_h = _tn(p.astype(BF16), dom)
                if h % 2 == 0:
                    dq_acc, dk_acc, dv_acc = dq_h, dk_h, dv_h
                else:
                    dq_ref[:, sl] = (dq_acc + dq_h).astype(BF16)
                    dkacc[pl.ds(row0 + lo, KW - lo), sl] += dk_acc + dk_h
                    dvacc[pl.ds(row0 + lo, KW - lo), sl] += dv_acc + dv_h
                cur = nxt

        for nblk in (1, 2, 3):
            pl.when(jnp.minimum(qb, 2) == nblk - 1)(functools.partial(attend, (3 - nblk) * TQ))

        @pl.when(qb == nq - 1)
        def _():
            dk_ref[...] = dkacc[2 * TQ:2 * TQ + S, :].astype(BF16)
            dv_ref[...] = dvacc[2 * TQ:2 * TQ + S, :].astype(BF16)

        @pl.when(i == nsteps - 1)
        def _():
            for h in range(HEADS):
                df_ref[h:h + 1, :] = _skew_from_bias(db_scr[h])

    full = lambda shape: pl.BlockSpec(shape, lambda i: (0,) * len(shape))
    return _call(
        body, "attn_bwd", (nsteps,),
        _attn_specs(nq) + [pl.BlockSpec((TQ, BW), lambda i: (i, 0)), full((HEADS, NSKEW))],
        [pl.BlockSpec((TQ, BW), lambda i: (i, 0)), pl.BlockSpec((S, BW), lambda i: (i // nq, 0)),
         pl.BlockSpec((S, BW), lambda i: (i // nq, 0)), full((HEADS, NSKEW))],
        [SDS((T, BW), BF16)] * 3 + [SDS((HEADS, NSKEW), F32)],
        [pltpu.VMEM((KW, BW), BF16), pltpu.VMEM((KW, BW), BF16),
         pltpu.VMEM((S + 2 * TQ, BW), F32), pltpu.VMEM((S + 2 * TQ, BW), F32),
         pltpu.VMEM((HEADS, TQ, KW), F32), pltpu.VMEM((HEADS, TQ, KW), F32)],
        (z, z, z, z, z, z, z, dao, f), ("arbitrary",), ex)


def _mix_bwd(z, u1, dac, dap, dw, dwb, lng, lnb, pw, pb, ps, S, ex=None):
    T = z.shape[0]
    tm = min(S, 1024)
    ts = S // tm
    nh = tm // HALO
    nsteps = T // tm
    nblk32 = T // HALO

    def body(ca_ref, cb_ref, cg_ref, pi_ref, pg_ref, u1_ref, dac_ref, dap_ref,
             cah_ref, cbh_ref, pih_ref,
             cgn_ref, pgn_ref, u1n_ref, dacn_ref, dapn_ref,
             dw_ref, dwb_ref, lng_ref, lnb_ref, pw_ref, pb_ref, ps_ref,
             dzc_ref, dzp_ref, ddw_ref, ddwb_ref, dlng_ref, dlnb_ref, dpw_ref, dpb_ref, dps_ref,
             ubuf, gbuf, pbuf, qbuf, pooled, *accs):
        tap_acc, (lng_acc, lnb_acc, dwb_acc) = accs[:CONV_K], accs[CONV_K:]
        i = pl.program_id(0)
        keep_prev = jnp.where((i % ts) == 0, 0.0, 1.0)
        keep_next = jnp.where((i % ts) == ts - 1, 0.0, 1.0)
        t0 = (i % ts) * tm

        @pl.when(i == 0)
        def _():
            for a in accs:
                a[...] = jnp.zeros_like(a)
            dpw_ref[...] = jnp.zeros_like(dpw_ref)
            dpb_ref[...] = jnp.zeros_like(dpb_ref)
            dps_ref[...] = jnp.zeros_like(dps_ref)

        ubuf[0:HALO, :] = cah_ref[...].astype(F32) * _sig(cbh_ref[...].astype(F32)) * keep_prev
        ubuf[HALO:HALO + tm, :] = ca_ref[...].astype(F32) * _sig(cb_ref[...].astype(F32))
        pbuf[0:HALO, :] = pih_ref[...].astype(F32) * keep_prev
        pbuf[HALO:HALO + tm, :] = pi_ref[...].astype(F32)

        def norm_back(u1v, cg, dacv):
            n, rstd = _layer_norm_fwd(u1v)
            u2 = n * lng_ref[...] + lnb_ref[...]
            s2 = _sig(u2)
            scg = _sig(cg)
            du2 = dacv * cg * scg * _dsilu(u2, s2)
            dn = du2 * lng_ref[...]
            du1 = rstd * (dn - jnp.mean(dn, axis=-1, keepdims=True)
                          - n * jnp.mean(dn * n, axis=-1, keepdims=True))
            return du1, du2, n, dacv * u2 * s2 * _dsilu(cg, scg)

        def chunk_a(c, carry):
            base = pl.multiple_of(c * RC, RC)
            du1, du2, n, dcg = norm_back(u1_ref[pl.ds(base, RC), :], cg_ref[pl.ds(base, RC), :].astype(F32),
                                         dac_ref[pl.ds(base, RC), :].astype(F32))
            gbuf[pl.ds(base, RC), :] = du1
            dzc_ref[pl.ds(base, RC), 2 * BW:3 * BW] = dcg.astype(BF16)
            lng_acc[...] += _rows8(du2 * n)
            lnb_acc[...] += _rows8(du2)
            dwb_acc[...] += _rows8(du1)
            padded = jnp.concatenate([du1, jnp.zeros((8, BW), F32)], axis=0)
            for r in range(8):
                nrow = RC if r == 0 else RC + 8
                g = du1 if r == 0 else pltpu.roll(padded, r, axis=0)
                for q in range((CONV_K + 1 - r) // 8 + 1):
                    o = 8 * q + r
                    if o < 2:
                        continue
                    prod = g * ubuf[pl.ds(base + 8 * q, nrow), :]
                    red = prod[0:8]
                    for k in range(1, nrow // 8):
                        red = red + prod[8 * k:8 * k + 8]
                    tap_acc[o - 2][...] += red
            return carry

        lax.fori_loop(0, tm // RC, chunk_a, 0, unroll=4)
        du1n, _, _, _ = norm_back(u1n_ref[...], cgn_ref[...].astype(F32), dacn_ref[...].astype(F32))
        gbuf[tm:tm + HALO, :] = du1n * keep_next

        def chunk_p(c, carry):
            base = pl.multiple_of(c * RC, RC)
            outs = _pool_chunk(pbuf[pl.ds(base, RC + HALO), :], t0 + base)
            for g in range(4):
                pooled[pl.ds(base, RC), g * 128:(g + 1) * 128] = outs[g].astype(BF16)
            return carry

        lax.fori_loop(0, tm // RC, chunk_p, 0)

        def cnt_of(t_first, rows, w):
            t = t_first + lax.broadcasted_iota(jnp.int32, (rows, 128), 0)
            return jnp.minimum(t + 1, w).astype(F32)

        pg = pg_ref[...].astype(F32)
        spg_s = _sig(pg)
        dapv = dap_ref[...].astype(F32)
        pgn = pgn_ref[...].astype(F32)
        dmixn = dapn_ref[...].astype(F32) * pgn * _sig(pgn) * ps_ref[...] * keep_next
        for g, w in enumerate(WINDOWS):
            sl = slice(g * 128, (g + 1) * 128)
            mixed_u = _nn(pooled[:, sl], pw_ref[g]) + pb_ref[:, sl]
            dap_g = dapv[:, sl]
            pg_g = pg[:, sl]
            s_g = spg_s[:, sl]
            silu_g = pg_g * s_g
            dps_ref[:, sl] += jnp.sum(dap_g * silu_g * mixed_u, axis=0, keepdims=True)
            dzp_ref[:, BW + g * 128:BW + (g + 1) * 128] = (
                dap_g * mixed_u * ps_ref[:, sl] * _dsilu(pg_g, s_g)).astype(BF16)
            dmix = dap_g * silu_g * ps_ref[:, sl]
            dpb_ref[:, sl] += jnp.sum(dmix, axis=0, keepdims=True)
            dmixb = dmix.astype(BF16)
            dpw_ref[g] += _tn(pooled[:, sl], dmixb)
            qbuf[0:tm, sl] = _nt(dmixb, pw_ref[g]) / cnt_of(t0, tm, w)
            qbuf[tm:tm + HALO, sl] = _nt(dmixn[:, sl].astype(BF16), pw_ref[g]) / cnt_of(t0 + tm, HALO, w)

        def chunk_b(c, carry):
            base = pl.multiple_of(c * RC, RC)
            load = lambda q, n: gbuf[pl.ds(base + 8 * q, n), :]
            du0 = _stencil(load, dw_ref, 0, CONV_K - 1, lambda o: CONV_K - 1 - o)
            ca = ca_ref[pl.ds(base, RC), :].astype(F32)
            sb = _sig(cb_ref[pl.ds(base, RC), :].astype(F32))
            dzc_ref[pl.ds(base, RC), 0:BW] = (du0 * sb).astype(BF16)
            dzc_ref[pl.ds(base, RC), BW:2 * BW] = (du0 * ca * sb * (1.0 - sb)).astype(BF16)
            qwin = qbuf[pl.ds(base, RC + HALO), :]
            t = t0 + base + lax.broadcasted_iota(jnp.int32, (RC, 128), 0)
            for g, w in enumerate(WINDOWS):
                x = qwin[:, g * 128:(g + 1) * 128]
                s = _window_sums(x, w, False)
                cnt = jnp.minimum(t + 1, w).astype(F32)
                dzp_ref[pl.ds(base, RC), g * 128:(g + 1) * 128] = (s[0:RC] - cnt * x[0:RC]).astype(BF16)
            return carry

        lax.fori_loop(0, tm // RC, chunk_b, 0)

        @pl.when(i == nsteps - 1)
        def _():
            dlng_ref[...] = jnp.sum(lng_acc[...], axis=0, keepdims=True)
            dlnb_ref[...] = jnp.sum(lnb_acc[...], axis=0, keepdims=True)
            ddwb_ref[...] = jnp.sum(dwb_acc[...], axis=0, keepdims=True)
            for j in range(CONV_K):
                ddw_ref[j:j + 1, :] = jnp.sum(tap_acc[j][...], axis=0, keepdims=True)

    def zmain(cb):
        return pl.BlockSpec((tm, BW), lambda i: (i, cb))

    def zprev(cb):
        return pl.BlockSpec((HALO, BW), lambda i: (jnp.maximum(i * nh - 1, 0), cb))

    def znext(cb):
        return pl.BlockSpec((HALO, BW), lambda i: (jnp.minimum((i + 1) * nh, nblk32 - 1), cb))

    row = lambda w: pl.BlockSpec((tm, w), lambda i: (i, 0))
    full = lambda shape: pl.BlockSpec(shape, lambda i: (0,) * len(shape))
    return _call(
        body, "mix_bwd", (nsteps,),
        [zmain(CB_CA), zmain(CB_CB), zmain(CB_CG), zmain(CB_PI), zmain(CB_PG), row(BW), row(BW), row(BW),
         zprev(CB_CA), zprev(CB_CB), zprev(CB_PI),
         znext(CB_CG), znext(CB_PG), znext(0), znext(0), znext(0),
         full((CONV_K, BW)), full((1, BW)), full((1, BW)), full((1, BW)),
         full((4, 128, 128)), full((1, BW)), full((1, BW))],
        [row(3 * BW), row(2 * BW), full((CONV_K, BW)), full((1, BW)), full((1, BW)), full((1, BW)),
         full((4, 128, 128)), full((1, BW)), full((1, BW))],
        [SDS((T, 3 * BW), BF16), SDS((T, 2 * BW), BF16), SDS((CONV_K, BW), F32),
         SDS((1, BW), F32), SDS((1, BW), F32), SDS((1, BW), F32),
         SDS((4, 128, 128), F32), SDS((1, BW), F32), SDS((1, BW), F32)],
        [pltpu.VMEM((HALO + tm, BW), F32), pltpu.VMEM((tm + HALO, BW), F32),
         pltpu.VMEM((HALO + tm, BW), F32), pltpu.VMEM((tm + HALO, BW), F32),
         pltpu.VMEM((tm, BW), BF16)] + [pltpu.VMEM((8, BW), F32)] * (CONV_K + 3),
        (z, z, z, z, z, u1, dac, dap, z, z, z, z, z, u1, dac, dap, dw, dwb, lng, lnb, pw, pb, ps),
        ("arbitrary",), ex)


def _in_bwd_x(pieces, wt, x, g, dout, ex=None):
    T = x.shape[0]
    tm = 512
    widths = [p.shape[1] for p in pieces]
    offs = np.cumsum([0] + widths)
    npc = len(pieces)

    def body(*refs):
        p_refs = refs[:npc]
        w_ref, x_ref, g_ref, do_ref, dx_ref, dg_ref = refs[npc:]

        @pl.when(pl.program_id(0) == 0)
        def _():
            dg_ref[...] = jnp.zeros_like(dg_ref)

        dh = None
        for k in range(npc):
            t = _nn(p_refs[k][...], w_ref[int(offs[k]):int(offs[k + 1]), :])
            dh = t if dh is None else dh + t
        xv = x_ref[...]
        r = lax.rsqrt(jnp.mean(xv * xv, axis=-1, keepdims=True) + EPS)
        xn = xv * r
        dg_ref[...] += jnp.sum(dh * xn, axis=0, keepdims=True)
        dxn = dh * g_ref[...]
        dx_ref[...] = do_ref[...] + r * (dxn - xn * jnp.mean(dxn * xn, axis=-1, keepdims=True))

    row = lambda wd: pl.BlockSpec((tm, wd), lambda i: (i, 0))
    return _call(
        body, "in_bwd_x", (T // tm,),
        [row(wd) for wd in widths] + [pl.BlockSpec((NCOL, D), lambda i: (0, 0)),
                                      row(D), pl.BlockSpec((1, D), lambda i: (0, 0)), row(D)],
        [row(D), pl.BlockSpec((1, D), lambda i: (0, 0))], [SDS((T, D), F32), SDS((1, D), F32)], [],
        (*pieces, wt, x, g, dout), ("arbitrary",), ex)


def _in_bwd_w(ht, pieces, row0, buf=None, ex=None):
    T = ht.shape[1]
    n = len(pieces)
    wd = pieces[0].shape[1]
    tn = next(t for t in (1536, 1024, 768, 512) if wd % t == 0 and row0 % t == 0)
    per = wd // tn
    tk = min(T, 2048)
    nk = T // tk
    j0 = row0 // tn

    def body(ht_ref, *rest):
        p_refs = rest[:n]
        o_ref, acc = rest[-2:]
        j, k = pl.program_id(0), pl.program_id(1)

        @pl.when(k == 0)
        def _():
            acc[...] = jnp.zeros_like(acc)

        for p in range(n):
            @pl.when(j // per == p)
            def _(p=p):
                acc[...] += _nn(ht_ref[...], p_refs[p][...])

        @pl.when(k == nk - 1)
        def _():
            o_ref[...] = acc[...].T.astype(BF16)

    def piece_spec(p):
        return pl.BlockSpec((tk, tn), lambda j, k: (jnp.where(j // per == p, k, 0), jnp.where(j // per == p, j % per, 0)))

    in_specs = [pl.BlockSpec((D, tk), lambda j, k: (0, k))] + [piece_spec(p) for p in range(n)]
    args = (ht, *pieces)
    if buf is not None:
        in_specs.append(pl.BlockSpec(memory_space=pl.ANY))
        args += (buf,)
    return _call(
        body, "in_bwd_w", (n * per, nk), in_specs,
        [pl.BlockSpec((tn, D), lambda j, k: (j + j0, 0))], [SDS((NCOL, D), BF16)], [pltpu.VMEM((D, tn), F32)],
        args, ("arbitrary", "arbitrary"), ex, None if buf is None else {n + 1: 0})


def _my_id():
    return 4 * lax.axis_index("x") + 2 * lax.axis_index("y") + lax.axis_index("c")


def _peers():
    x, y, c = lax.axis_index("x"), lax.axis_index("y"), lax.axis_index("c")
    out = []
    for k in range(1, N_DEV):
        fx, fy, fc = (k >> 2) & 1, (k >> 1) & 1, k & 1
        px, py, pc = x ^ fx, y ^ fy, c ^ fc
        out.append(((px, py, pc), 4 * px + 2 * py + pc))
    return out


class _Exchange:
    def __init__(self, arrays, scatter):
        self.arrays = list(arrays)
        self.scatter = list(scatter)
        self.n = n = len(arrays)
        hbm = pl.BlockSpec(memory_space=pltpu.HBM)
        self.in_specs = [hbm] * n
        self.out_specs = [hbm] * n
        self.out_shape = [SDS((N_DEV,) + tuple(a.shape[1:] if s else a.shape), a.dtype)
                          for a, s in zip(arrays, scatter)]
        self.scratch = [pltpu.SemaphoreType.DMA((N_DEV - 1, n)), pltpu.SemaphoreType.DMA((N_DEV - 1, n)),
                        pltpu.SemaphoreType.DMA((n,))]

    def split(self, refs, n_in, n_out):
        n = self.n
        own_in = refs[:n_in]
        ex_in = refs[n_in:n_in + n]
        own_out = refs[n_in + n:n_in + n + n_out]
        ex_out = refs[n_in + n + n_out:n_in + 2 * n + n_out]
        rest = refs[n_in + 2 * n + n_out:]
        return own_in, own_out, rest[:-3], (ex_in, ex_out, rest[-3:])

    def _copy(self, ex, k, p, landing):
        in_refs, out_refs, (send_sems, recv_sems, _) = ex
        pos, pid = _peers()[p]
        return pltpu.make_async_remote_copy(
            src_ref=in_refs[k].at[pid] if self.scatter[k] else in_refs[k],
            dst_ref=out_refs[k].at[pid if landing else _my_id()],
            send_sem=send_sems.at[p, k], recv_sem=recv_sems.at[p, k],
            device_id=pos, device_id_type=pl.DeviceIdType.MESH)

    def _own(self, ex, k):
        in_refs, out_refs, (_, _, local_sems) = ex
        me = _my_id()
        return pltpu.make_async_copy(in_refs[k].at[me] if self.scatter[k] else in_refs[k], out_refs[k].at[me],
                                     local_sems.at[k])

    def start(self, ex):
        for k in range(self.n):
            self._own(ex, k).start()
        for p in range(N_DEV - 1):
            for k in range(self.n):
                self._copy(ex, k, p, False).start()

    def finish(self, ex):
        for p in range(N_DEV - 1):
            for k in range(self.n):
                self._copy(ex, k, p, True).wait_recv()
        for p in range(N_DEV - 1):
            for k in range(self.n):
                self._copy(ex, k, p, False).wait_send()
        for k in range(self.n):
            self._own(ex, k).wait()


def _exchange(arrays, scatter, name):
    ex = _Exchange(arrays, scatter)

    def body(*refs):
        _, _, _, exr = ex.split(refs, 0, 0)
        ex.start(exr)
        ex.finish(exr)

    return pl.pallas_call(body, name=name, in_specs=ex.in_specs, out_specs=ex.out_specs,
                          out_shape=ex.out_shape, scratch_shapes=ex.scratch)(*ex.arrays)


def _gather_two_level(shard, name):
    def body(x_ref, out_ref, send_sems, recv_sems, local_sem):
        x, y, c = lax.axis_index("x"), lax.axis_index("y"), lax.axis_index("c")
        me, sibling = (x, y, c), (x, y, 1 - c)
        chips = [(1 - x, y), (x, 1 - y), (1 - x, 1 - y)]
        slab = lambda px, py, pc: out_ref.at[4 * px + 2 * py + pc]

        def copy(k, block, to, src=None):
            return pltpu.make_async_remote_copy(
                src_ref=slab(*block) if src is None else src, dst_ref=slab(*block),
                send_sem=send_sems.at[k], recv_sem=recv_sems.at[k],
                device_id=to, device_id_type=pl.DeviceIdType.MESH)

        mine = pltpu.make_async_copy(x_ref, slab(*me), local_sem)
        mine.start()
        first = [copy(0, me, sibling, src=x_ref)] + [copy(1 + j, me, (*chip, c), src=x_ref)
                                                     for j, chip in enumerate(chips)]
        for cp in first:
            cp.start()
        passed = [copy(4 + j, (*chip, c), sibling) for j, chip in enumerate(chips)]
        for j, chip in enumerate(chips):
            copy(1 + j, (*chip, c), me).wait_recv()
            passed[j].start()
        copy(0, sibling, me).wait_recv()
        for j, chip in enumerate(chips):
            copy(4 + j, (*chip, 1 - c), me).wait_recv()
        for cp in first + passed:
            cp.wait_send()
        mine.wait()

    hbm = pl.BlockSpec(memory_space=pltpu.HBM)
    return pl.pallas_call(
        body, name=name, in_specs=[hbm], out_specs=hbm,
        out_shape=SDS((N_DEV,) + shard.shape, shard.dtype),
        scratch_shapes=[pltpu.SemaphoreType.DMA((N_DEV - 1,)), pltpu.SemaphoreType.DMA((N_DEV - 1,)),
                        pltpu.SemaphoreType.DMA],
    )(shard)


def _adamw_update(g, w, m, v):
    c1 = 1.0 / (1.0 - ADAM_B1 ** ADAM_STEP)
    c2 = 1.0 / (1.0 - ADAM_B2 ** ADAM_STEP)
    mn = ADAM_B1 * m + (1.0 - ADAM_B1) * g
    vn = ADAM_B2 * v + (1.0 - ADAM_B2) * (g * g)
    return -ADAM_LR * ((mn * c1) / (jnp.sqrt(vn * c2) + ADAM_EPS) + ADAM_WD * w), mn, vn


def _adamw_small(parts, w, m, v):
    n = len(w)

    def body(*refs):
        p_refs = (refs[0:n], refs[n:2 * n])
        w_refs, m_refs, v_refs = refs[2 * n:3 * n], refs[3 * n:4 * n], refs[4 * n:5 * n]
        outs = refs[5 * n:]
        for k in range(n):
            g_ref, d_ref, mo_ref, vo_ref = outs[4 * k:4 * k + 4]
            for l in range(2):
                at = (slice(l, l + 1),) if len(w_refs[k].shape) == 2 else (l,)
                g = p_refs[l][k][0]
                for s in range(1, N_DEV):
                    g = g + p_refs[l][k][s]
                delta, mn, vn = _adamw_update(g, w_refs[k][at], m_refs[k][at], v_refs[k][at])
                g_ref[at] = g
                d_ref[at] = delta
                mo_ref[at] = mn
                vo_ref[at] = vn

    vmem = pl.BlockSpec(memory_space=pltpu.VMEM)
    res = pl.pallas_call(
        body, name="adamw_replicated", in_specs=[vmem] * (5 * n), out_specs=[vmem] * (4 * n),
        out_shape=[SDS(a.shape, F32) for a in w for _ in range(4)],
        compiler_params=pltpu.CompilerParams(vmem_limit_bytes=VMEM_LIMIT),
    )(*parts[0], *parts[1], *w, *m, *v)
    return [res[4 * k:4 * k + 4] for k in range(n)]


def _adamw_sum(parts0, parts1, w, m, v, name):
    _, R, C = w.shape
    tr = R
    while tr * C > 256 * 1024 and tr % 32 == 0:
        tr //= 2

    def body(p0_ref, p1_ref, w_ref, m_ref, v_ref, g_ref, d_ref, mo_ref, vo_ref):
        def update(p_ref):
            g = p_ref[0].astype(F32)
            for s in range(1, N_DEV):
                g = g + p_ref[s].astype(F32)
            g_ref[...] = g
            d_ref[...], mo_ref[...], vo_ref[...] = _adamw_update(g, w_ref[...], m_ref[...], v_ref[...])

        @pl.when(pl.program_id(0) == 0)
        def _():
            update(p0_ref)

        @pl.when(pl.program_id(0) == 1)
        def _():
            update(p1_ref)

    blk = pl.BlockSpec((None, tr, C), lambda l, i: (l, i, 0))
    return pl.pallas_call(
        body, name=name, grid=(2, R // tr),
        in_specs=[pl.BlockSpec((N_DEV, tr, C), lambda l, i: (0, i * (1 - l), 0)),
                  pl.BlockSpec((N_DEV, tr, C), lambda l, i: (0, i * l, 0)), blk, blk, blk],
        out_specs=[blk, blk, blk, blk],
        out_shape=[SDS((2, R, C), F32)] * 4,
        compiler_params=_cparams(("arbitrary", "arbitrary")),
    )(parts0, parts1, w, m, v)


def _layer_fwd(x, P, skew, S, rest, ex, tgt=None):
    z, ht, *got0 = _in_proj(x, P["pre_g"], P["w_in_t"], ex[0])
    P = {**P, **rest(got0)}
    ac, ap, u1, *got1 = _mix_fwd(z, P["conv_dw"], P["conv_dw_b"], P["conv_ln_g"], P["conv_ln_b"],
                                 P["pool_w"], P["pool_b"], P["pool_scale"], S, ex[1])
    o, *got2 = _attn_fwd(z, skew, S, ex[2])
    out, kept, got3 = _out_fwd(x, ac, o, ap, z, P["w_conv_out"], P["w_attn_out"], P["w_pool_out"], P["w_out"],
                               P["post_g"], ex[3], tgt)
    return out, (x, z, ht, ac, o, ap, u1, kept), P, (got0, got1, got2, got3)


def _layer_bwd(dout, saved, P, skew, S, ex=(None, None), mix_ex=None, win_ex=None):
    x, z, ht, ac, o, ap, u1, kept = saved
    (dac, dao, dag, dap, dgm, dwco, dwao, dwpo, dwout, dpostg, *got0) = _out_bwd(
        dout, ac, o, ap, z, kept, P["w_conv_out"], P["w_attn_out"], P["w_pool_out"], P["w_out"], P["post_g"], ex[0])
    grads = dict(post_norm_g=dpostg, w_conv_out=dwco, w_attn_out=dwao, w_pool_out=dwpo, w_out=dwout)
    dq, dk, dv, grads["dskew"], *got1 = _attn_bwd(z, dao, skew, S, ex[1])
    (dzc, dzp, grads["conv_dw"], grads["conv_dw_b"], grads["conv_ln_g"], grads["conv_ln_b"], grads["pool_w"],
     grads["pool_b"], grads["pool_scale"], *got2) = _mix_bwd(
        z, u1, dac, dap, P["conv_dw"], P["conv_dw_b"], P["conv_ln_g"], P["conv_ln_b"],
        P["pool_w"], P["pool_b"], P["pool_scale"], S, mix_ex(grads) if mix_ex else None)
    pieces = [dzc, dq, dk, dv, dag, dzp, dgm]
    buf, row0 = None, 0
    for group in ([dzc], [dq, dk, dv, dag], [dzp], [dgm]):
        (buf,) = _in_bwd_w(ht, group, row0, buf)
        row0 += sum(p.shape[1] for p in group)
    grads["w_in_t"] = buf
    dx, grads["pre_norm_g"], *got3 = _in_bwd_x(pieces, P["w_in_t"], x, P["pre_g"], dout,
                                               win_ex(grads) if win_ex else None)
    return dx, grads, (got0, got1, got2, got3)


WEIGHT_NAMES = ("pre_norm_g", "post_norm_g", "w_in", "conv_dw", "conv_dw_b", "conv_ln_g", "conv_ln_b",
                "w_conv_out", "rel_bias", "w_attn_out", "pool_w", "pool_b", "pool_scale", "w_pool_out", "w_out")
SHARDED = ("w_in", "w_conv_out", "w_attn_out", "w_pool_out", "w_out", "conv_dw")
OUT_PROJ = ("w_conv_out", "w_attn_out", "w_pool_out", "w_out")
REST = tuple(n for n in WEIGHT_NAMES if n not in ("w_in", "pre_norm_g"))


def _cols_from_slabs(g):
    return g.transpose(1, 0, 2).reshape(g.shape[1], N_DEV * g.shape[2])


def _slabs_from_cols(full):
    r, wd = full.shape
    return full.reshape(r, N_DEV, wd // N_DEV).transpose(1, 0, 2)


def _rest_shards(weights, l):
    return [weights["w_conv_out"][l].astype(BF16), weights["w_attn_out"][l].astype(BF16),
            weights["w_pool_out"][l].astype(BF16), weights["w_out"][l].astype(BF16), weights["conv_dw"][l]]


def _rest_weights(got):
    wco, wao, wpo, wout, cdw = got
    return dict(w_conv_out=_cols_from_slabs(wco), w_attn_out=_cols_from_slabs(wao),
                w_pool_out=_cols_from_slabs(wpo), w_out=wout.reshape(D, D), conv_dw=_cols_from_slabs(cdw))


def _grad_arrays(g, names):
    make = {"w_in": lambda: g["w_in_t"].reshape(N_DEV, NCOL // N_DEV, D),
            "w_conv_out": lambda: _slabs_from_cols(g["w_conv_out"].astype(BF16)),
            "w_attn_out": lambda: _slabs_from_cols(g["w_attn_out"].astype(BF16)),
            "w_pool_out": lambda: _slabs_from_cols(g["w_pool_out"].astype(BF16)),
            "w_out": lambda: g["w_out"].astype(BF16).reshape(N_DEV, D // N_DEV, D),
            "conv_dw": lambda: _slabs_from_cols(g["conv_dw"].astype(BF16)),
            "rel_bias": lambda: jnp.dot(g["dskew"], jnp.asarray(_skew_select().T), precision=lax.Precision.HIGHEST),
            "pool_b": lambda: g["pool_b"].reshape(4, 128)}
    return [make[n]() if n in make else g[n] for n in names]


def _grad_exchange(g, names):
    return _Exchange(_grad_arrays(g, names), [n in SHARDED for n in names])


def kernel(x, pre_norm_g, post_norm_g, w_in, conv_dw, conv_dw_b, conv_ln_g, conv_ln_b, w_conv_out, rel_bias, w_attn_out, pool_w, pool_b, pool_scale, w_pool_out, w_out, loss_target, m_pre_norm_g, m_post_norm_g, m_w_in, m_conv_dw, m_conv_dw_b, m_conv_ln_g, m_conv_ln_b, m_w_conv_out, m_rel_bias, m_w_attn_out, m_pool_w, m_pool_b, m_pool_scale, m_w_pool_out, m_w_out, v_pre_norm_g, v_post_norm_g, v_w_in, v_conv_dw, v_conv_dw_b, v_conv_ln_g, v_conv_ln_b, v_w_conv_out, v_rel_bias, v_w_attn_out, v_pool_w, v_pool_b, v_pool_scale, v_w_pool_out, v_w_out):
    given = dict(locals())
    weights = {n: given[n] for n in WEIGHT_NAMES}
    nb, S, _ = x.shape
    T = nb * S
    L = pre_norm_g.shape[0]
    assert L == 2
    x2 = x.reshape(T, D)
    tgt2 = loss_target.reshape(T, D)
    skews = [_skew_table(rel_bias[l]) for l in range(L)]

    def local_params(l):
        return dict(pre_g=pre_norm_g[l:l + 1], post_g=post_norm_g[l:l + 1], conv_dw_b=conv_dw_b[l:l + 1],
                    conv_ln_g=conv_ln_g[l:l + 1], conv_ln_b=conv_ln_b[l:l + 1], pool_w=pool_w[l].astype(BF16),
                    pool_b=pool_b[l].reshape(1, BW), pool_scale=pool_scale[l:l + 1])

    win0 = w_in[0].T.astype(BF16)
    win1 = w_in[1].T.astype(BF16)
    half = win1.shape[0] // 2
    w_in_t0 = _gather_two_level(win0, "gather_w_in_0")
    gather = lambda arrays: _Exchange(arrays, [False] * len(arrays))
    (h,), saved0, P0, (got_rest0, got_a, got_b, got_rest1) = _layer_fwd(
        x2, {**local_params(0), "w_in_t": w_in_t0.reshape(NCOL, D)}, skews[0], S, _rest_weights,
        (gather(_rest_shards(weights, 0)), gather([win1[:half]]), gather([win1[half:]]),
         gather(_rest_shards(weights, 1))))
    w_in_t1 = jnp.concatenate([got_a[0], got_b[0]], axis=1).reshape(NCOL, D)
    (dout, lsum), saved1, P1, _ = _layer_fwd(h, {**local_params(1), "w_in_t": w_in_t1}, skews[1], S,
                                             lambda _: _rest_weights(got_rest1), (None,) * 4, tgt2)

    dout, g1, _ = _layer_bwd(dout, saved1, P1, skews[1], S)
    g1["loss"] = lsum
    others = REST + ("pre_norm_g",)
    late = ("w_in",) + tuple(n for n in REST if n not in OUT_PROJ)
    dout, g0, (got_others1, got_win1, got_outp0, got_late0) = _layer_bwd(
        dout, saved0, P0, skews[0], S, (_grad_exchange(g1, others + ("loss",)), _grad_exchange(g1, ("w_in",))),
        lambda g: _grad_exchange(g, OUT_PROJ), lambda g: _grad_exchange(g, late))
    (got_pre0,) = _exchange([g0["pre_norm_g"]], [False], "gather_grad_pre_norm_g_0")
    parts = [{"pre_norm_g": got_pre0, **dict(zip(OUT_PROJ, got_outp0)), **dict(zip(late, got_late0))},
             {"w_in": got_win1[0], **dict(zip(others + ("loss",), got_others1))}]
    loss = jnp.sum(parts[1].pop("loss")[:, 0, 0]) * (0.5 / D)
    grad_x = dout.reshape(x.shape)

    outs = {}
    small = [n for n in WEIGHT_NAMES if n not in SHARDED]
    res = _adamw_small([[parts[l][n] for n in small] for l in range(L)], [weights[n] for n in small],
                       [given["m_" + n] for n in small], [given["v_" + n] for n in small])
    outs.update(zip(small, res))
    for n in SHARDED:
        view = (lambda a: a.transpose(0, 2, 1)) if n == "w_in" else (lambda a: a)
        res = _adamw_sum(parts[0][n], parts[1][n], view(weights[n]), view(given["m_" + n]), view(given["v_" + n]),
                         "adamw_" + n)
        outs[n] = [view(a) for a in res]
    return (loss, grad_x, *[outs[n][0] for n in WEIGHT_NAMES], *[outs[n][1] for n in WEIGHT_NAMES],
            *[outs[n][2] for n in WEIGHT_NAMES], *[outs[n][3] for n in WEIGHT_NAMES])
```

```python
import functools

import numpy as np
import jax
import jax.numpy as jnp
from jax import lax
from jax.experimental import pallas as pl
from jax.experimental.pallas import tpu as pltpu

F32 = jnp.float32
BF16 = jnp.bfloat16
SDS = jax.ShapeDtypeStruct

D = 1024
BW = 512
NCOL = 7680
EPS = 1e-6
NEG = -1e30
HEADS = 8
HD = 64
CHUNK = 64
LEFT = 8
MAX_REL = 256
TQ = 256
KW = 768
CONV_K = 31
WINDOWS = (2, 4, 8, 16)
HALO = 32
RC = 32
N_DEV = 8

ADAM_LR = 0.001
ADAM_B1 = 0.9
ADAM_B2 = 0.999
ADAM_EPS = 1e-08
ADAM_WD = 0.01
ADAM_STEP = 10

VMEM_LIMIT = 56 * 1024 * 1024

CB_CA, CB_CB, CB_CG, CB_Q, CB_K, CB_V, CB_AG, CB_PI, CB_PG = range(9)


def _cparams(sem):
    return pltpu.CompilerParams(dimension_semantics=sem, vmem_limit_bytes=VMEM_LIMIT)


def _sig(x):
    return 0.5 * jnp.tanh(0.5 * x) + 0.5


def _dsilu(x, s):
    return s * (1.0 + x * (1.0 - s))


def _nt(a, b):
    return lax.dot_general(a, b, (((1,), (1,)), ((), ())), preferred_element_type=F32)


def _tn(a, b):
    return lax.dot_general(a, b, (((0,), (0,)), ((), ())), preferred_element_type=F32)


def _nn(a, b):
    return jnp.dot(a, b, preferred_element_type=F32)


def _rows8(x):
    return x[0:8] + x[8:16] + x[16:24] + x[24:32]


def _call(body, name, grid, in_specs, out_specs, out_shape, scratch, args, sem, ex=None, aliases=None):
    aliases = aliases or {}
    if ex is None:
        return pl.pallas_call(body, name=name, grid=grid, in_specs=in_specs, out_specs=out_specs,
                              out_shape=out_shape, scratch_shapes=scratch, input_output_aliases=aliases,
                              compiler_params=_cparams(sem))(*args)
    n_in, n_out = len(in_specs), len(out_specs)
    steps = int(np.prod(grid))

    def carrier(*refs):
        own_in, own_out, own_scr, exr = ex.split(refs, n_in, n_out)
        step = pl.program_id(0)
        for axis in range(1, len(grid)):
            step = step * grid[axis] + pl.program_id(axis)

        @pl.when(step == 0)
        def _():
            ex.start(exr)

        body(*own_in, *own_out, *own_scr)

        @pl.when(step == steps - 1)
        def _():
            ex.finish(exr)

    return pl.pallas_call(
        carrier, name=name + "_carrier", grid=grid, in_specs=in_specs + ex.in_specs,
        out_specs=out_specs + ex.out_specs, out_shape=out_shape + ex.out_shape,
        scratch_shapes=scratch + ex.scratch, input_output_aliases=aliases,
        compiler_params=_cparams(("arbitrary",) * len(grid)),
    )(*args, *ex.arrays)


def _in_proj(x, g, wt, ex=None):
    T = x.shape[0]
    tm = 512
    tn = 1536

    def body(x_ref, g_ref, w_ref, z_ref, ht_ref):
        xv = x_ref[...]
        r = lax.rsqrt(jnp.mean(xv * xv, axis=-1, keepdims=True) + EPS)
        h = xv * r * g_ref[...]
        hb = h.astype(BF16)
        ht_ref[...] = h.T.astype(BF16)
        for c in range(NCOL // tn):
            z_ref[:, c * tn:(c + 1) * tn] = _nt(hb, w_ref[c * tn:(c + 1) * tn, :]).astype(BF16)

    return _call(
        body, "in_proj", (T // tm,),
        [pl.BlockSpec((tm, D), lambda i: (i, 0)), pl.BlockSpec((1, D), lambda i: (0, 0)),
         pl.BlockSpec((NCOL, D), lambda i: (0, 0))],
        [pl.BlockSpec((tm, NCOL), lambda i: (i, 0)), pl.BlockSpec((D, tm), lambda i: (0, i))],
        [SDS((T, NCOL), BF16), SDS((D, T), BF16)], [],
        (x, g, wt), ("parallel",), ex)


def _stencil(load, w_ref, lo, hi, tap_of):
    out = None
    for r in range(8):
        n = RC if r == 0 else RC + 8
        v = None
        for q in range((hi - r) // 8 + 1):
            o = 8 * q + r
            if o < lo:
                continue
            j = tap_of(o)
            term = w_ref[j:j + 1, :] * load(q, n)
            v = term if v is None else v + term
        if v is None:
            continue
        if r:
            v = pltpu.roll(v, n - r, axis=0)[0:RC]
        out = v if out is None else out + v
    return out


def _layer_norm_fwd(u1):
    mu = jnp.mean(u1, axis=-1, keepdims=True)
    xc = u1 - mu
    rstd = lax.rsqrt(jnp.mean(xc * xc, axis=-1, keepdims=True) + EPS)
    return xc * rstd, rstd


def _window_sums(x, w, back):
    n = x.shape[0]
    s = x
    k = 1
    while k < w:
        s = s + pltpu.roll(s, k if back else n - k, axis=0)
        k *= 2
    return s


def _pool_chunk(pwin, t_first):
    t = t_first + lax.broadcasted_iota(jnp.int32, (RC, 128), 0)
    outs = []
    for g, w in enumerate(WINDOWS):
        x = pwin[:, g * 128:(g + 1) * 128]
        s = _window_sums(x, w, True)
        cnt = jnp.minimum(t + 1, w).astype(F32)
        outs.append(s[HALO:HALO + RC] / cnt - x[HALO:HALO + RC])
    return outs


def _mix_fwd(z, dw, dwb, lng, lnb, pw, pb, ps, S, ex=None):
    T = z.shape[0]
    tm = min(S, 1024)
    ts = S // tm
    nh = tm // HALO

    def body(ca_ref, cb_ref, cg_ref, pi_ref, pg_ref, cah_ref, cbh_ref, pih_ref,
             dw_ref, dwb_ref, lng_ref, lnb_ref, pw_ref, pb_ref, ps_ref,
             ac_ref, ap_ref, u1_ref, u0_ref, pooled, ubuf, pbuf):
        i = pl.program_id(0)
        keep = jnp.where((i % ts) == 0, 0.0, 1.0)
        ubuf[0:HALO, :] = cah_ref[...].astype(F32) * _sig(cbh_ref[...].astype(F32)) * keep
        u0 = ca_ref[...].astype(F32) * _sig(cb_ref[...].astype(F32))
        ubuf[HALO:HALO + tm, :] = u0
        u0_ref[...] = u0
        pbuf[0:HALO, :] = pih_ref[...].astype(F32) * keep
        pbuf[HALO:HALO + tm, :] = pi_ref[...].astype(F32)
        t0 = (i % ts) * tm

        def chunk(c, carry):
            base = pl.multiple_of(c * RC, RC)
            load = lambda q, n: ubuf[pl.ds(base + 8 * q, n), :]
            u1 = _stencil(load, dw_ref, 2, CONV_K + 1, lambda o: o - 2) + dwb_ref[...]
            u1_ref[pl.ds(base, RC), :] = u1
            n, _ = _layer_norm_fwd(u1)
            u2 = n * lng_ref[...] + lnb_ref[...]
            u3 = u2 * _sig(u2)
            cg = cg_ref[pl.ds(base, RC), :].astype(F32)
            ac_ref[pl.ds(base, RC), :] = (u3 * cg * _sig(cg)).astype(BF16)
            pwin = pbuf[pl.ds(base, RC + HALO), :]
            outs = _pool_chunk(pwin, t0 + base)
            for g in range(4):
                pooled[pl.ds(base, RC), g * 128:(g + 1) * 128] = outs[g].astype(BF16)
            return carry

        lax.fori_loop(0, tm // RC, chunk, 0, unroll=4)
        pg = pg_ref[...].astype(F32)
        spg = pg * _sig(pg)
        for g in range(4):
            sl = slice(g * 128, (g + 1) * 128)
            mixed = (_nn(pooled[:, sl], pw_ref[g]) + pb_ref[:, sl]) * ps_ref[:, sl]
            ap_ref[:, sl] = (mixed * spg[:, sl]).astype(BF16)

    def zmain(cb):
        return pl.BlockSpec((tm, BW), lambda i: (i, cb))

    def zprev(cb):
        return pl.BlockSpec((HALO, BW), lambda i: (jnp.maximum(i * nh - 1, 0), cb))

    full = lambda shape: pl.BlockSpec(shape, lambda i: (0,) * len(shape))
    row = pl.BlockSpec((tm, BW), lambda i: (i, 0))
    return _call(
        body, "mix_fwd", (T // tm,),
        [zmain(CB_CA), zmain(CB_CB), zmain(CB_CG), zmain(CB_PI), zmain(CB_PG),
         zprev(CB_CA), zprev(CB_CB), zprev(CB_PI),
         full((CONV_K, BW)), full((1, BW)), full((1, BW)), full((1, BW)),
         full((4, 128, 128)), full((1, BW)), full((1, BW))],
        [row] * 5, [SDS((T, BW), BF16), SDS((T, BW), BF16), SDS((T, BW), F32), SDS((T, BW), F32), SDS((T, BW), BF16)],
        [pltpu.VMEM((HALO + tm, BW), F32), pltpu.VMEM((HALO + tm, BW), F32)],
        (z, z, z, z, z, z, z, z, dw, dwb, lng, lnb, pw, pb, ps), ("parallel",), ex)


def _attn_specs(nq):
    def kv(cb, off):
        return pl.BlockSpec((TQ, BW), lambda i: (i - jnp.minimum(off, i % nq), cb))
    return [pl.BlockSpec((TQ, BW), lambda i: (i, CB_Q)),
            kv(CB_K, 2), kv(CB_K, 1), kv(CB_K, 0), kv(CB_V, 2), kv(CB_V, 1), kv(CB_V, 0)]


NSKEW = 1024


def _skew_table(table):
    return jnp.dot(table, jnp.asarray(_skew_select()), precision=lax.Precision.HIGHEST)


def _skew_select():
    d = np.arange(TQ + KW - 1)
    idx = np.clip(3 * TQ - 1 - d, -MAX_REL, MAX_REL) + MAX_REL
    sel = np.zeros((2 * MAX_REL + 1, NSKEW), np.float32)
    sel[idx, d] = 1.0
    return sel


def _bias_from_skew(f_ref, bias_scr):
    qi = lax.broadcasted_iota(jnp.int32, (TQ, KW), 0)
    kj = lax.broadcasted_iota(jnp.int32, (TQ, KW), 1)
    lo = (qi // CHUNK) * CHUNK
    band = jnp.where((kj >= lo) & (kj < lo + (LEFT + 1) * CHUNK), 0.0, NEG)
    for h in range(HEADS):
        rows = jnp.broadcast_to(f_ref[h:h + 1, :], (TQ, NSKEW))
        rows = pltpu.roll(rows, NSKEW - (TQ - 1), axis=1, stride=1, stride_axis=0)
        bias_scr[h] = rows[:, 0:KW] + band


def _skew_from_bias(db):
    i = lax.broadcasted_iota(jnp.int32, (TQ, TQ), 0)
    j = lax.broadcasted_iota(jnp.int32, (TQ, TQ), 1)
    flip = jnp.where(i + j == TQ - 1, 1.0, 0.0).astype(BF16)
    hi = db.astype(BF16)
    lo = (db - hi.astype(F32)).astype(BF16)
    rev = _nn(flip, hi) + _nn(flip, lo)
    rev = jnp.concatenate([rev, jnp.zeros((TQ, NSKEW - KW), F32)], axis=1)
    return jnp.sum(pltpu.roll(rev, 0, axis=1, stride=1, stride_axis=0), axis=0, keepdims=True)


def _attn_fwd(z, f, S, ex=None):
    T = z.shape[0]
    nq = S // TQ

    def body(q_ref, k2_ref, k1_ref, k0_ref, v2_ref, v1_ref, v0_ref, f_ref, o_ref, kbuf, vbuf, b_scr):
        @pl.when(pl.program_id(0) == 0)
        def _():
            _bias_from_skew(f_ref, b_scr)

        qb = pl.program_id(0) % nq
        kbuf[0:TQ, :] = k2_ref[...]
        kbuf[TQ:2 * TQ, :] = k1_ref[...]
        kbuf[2 * TQ:KW, :] = k0_ref[...]
        vbuf[0:TQ, :] = v2_ref[...]
        vbuf[TQ:2 * TQ, :] = v1_ref[...]
        vbuf[2 * TQ:KW, :] = v0_ref[...]
        lane = lax.broadcasted_iota(jnp.int32, (1, 128), 1)

        def attend(lo):
            def scores(h):
                sl = slice((h // 2) * 128, (h // 2 + 1) * 128)
                qp = q_ref[:, sl] * 0.125
                qm = jnp.where((lane < HD) if h % 2 == 0 else (lane >= HD), qp, jnp.zeros_like(qp))
                return _nt(qm, kbuf[lo:KW, sl]) + b_scr[h, :, lo:KW]

            s = scores(0)
            acc = None
            for h in range(HEADS):
                s_next = scores(h + 1) if h + 1 < HEADS else None
                sl = slice((h // 2) * 128, (h // 2 + 1) * 128)
                e = jnp.exp(s - jnp.max(s, axis=-1, keepdims=True))
                vp = vbuf[lo:KW, sl]
                vm = jnp.where((lane < HD) if h % 2 == 0 else (lane >= HD), vp, jnp.zeros_like(vp))
                o = _nn(e.astype(BF16), vm) * (1.0 / jnp.sum(e, axis=-1, keepdims=True))
                acc = o if h % 2 == 0 else acc + o
                if h % 2 == 1:
                    o_ref[:, sl] = acc.astype(BF16)
                s = s_next

        for nblk in (1, 2, 3):
            pl.when(jnp.minimum(qb, 2) == nblk - 1)(functools.partial(attend, (3 - nblk) * TQ))

    full = lambda shape: pl.BlockSpec(shape, lambda i: (0,) * len(shape))
    return _call(
        body, "attn_fwd", (T // TQ,),
        _attn_specs(nq) + [full((HEADS, NSKEW))],
        [pl.BlockSpec((TQ, BW), lambda i: (i, 0))], [SDS((T, BW), BF16)],
        [pltpu.VMEM((KW, BW), BF16), pltpu.VMEM((KW, BW), BF16), pltpu.VMEM((HEADS, TQ, KW), F32)],
        (z, z, z, z, z, z, z, f), ("arbitrary",), ex)


def _gates(gl_ref, gh_ref):
    gl = _sig(gl_ref[...].astype(F32))
    gh = _sig(gh_ref[...].astype(F32))
    return (gl[:, 0:D], jnp.concatenate([gl[:, D:1536], gh[:, 0:512]], axis=1), gh[:, 512:1536])


def _out_specs_in(tm):
    row = lambda w: pl.BlockSpec((tm, w), lambda i: (i, 0))
    full = lambda shape: pl.BlockSpec(shape, lambda i: (0,) * len(shape))
    return [row(BW), row(BW), row(BW),
            pl.BlockSpec((tm, BW), lambda i: (i, CB_AG)),
            pl.BlockSpec((tm, 1536), lambda i: (i, 3)),
            pl.BlockSpec((tm, 1536), lambda i: (i, 4)),
            full((BW, D)), full((BW, D)), full((BW, D)), full((D, D)), full((1, D))]


def _out_fwd(x, ac, o, ap, z, wco, wao, wpo, wout, postg, ex=None, tgt=None):
    T = x.shape[0]
    tm = 512
    last = tgt is not None

    def body(ac_ref, o_ref, ap_ref, ag_ref, gl_ref, gh_ref, wco_ref, wao_ref, wpo_ref, wout_ref, pg_ref,
             x_ref, *rest):
        ag = ag_ref[...].astype(F32)
        aat = (o_ref[...].astype(F32) * ag * _sig(ag)).astype(BF16)
        gates = _gates(gl_ref, gh_ref)
        acts = (ac_ref[...], aat, ap_ref[...])
        merged = None
        for b, w_ref in enumerate((wco_ref, wao_ref, wpo_ref)):
            yb = _nn(acts[b], w_ref[...])
            rest[-4 + b][...] = yb.astype(BF16)
            merged = gates[b] * yb if merged is None else merged + gates[b] * yb
        y = _nn(merged.astype(BF16), wout_ref[...])
        rest[-1][...] = y.astype(BF16)
        ry = lax.rsqrt(jnp.mean(y * y, axis=-1, keepdims=True) + EPS)
        out = x_ref[...] + y * ry * pg_ref[...]
        if not last:
            rest[0][...] = out
            return
        t_ref, d_ref, l_ref = rest[:3]

        @pl.when(pl.program_id(0) == 0)
        def _():
            l_ref[...] = jnp.zeros_like(l_ref)
        d = out - t_ref[...]
        d_ref[...] = d * (1.0 / D)
        l_ref[...] += jnp.sum(jnp.sum(d * d, axis=0, keepdims=True), axis=1, keepdims=True)

    row = pl.BlockSpec((tm, D), lambda i: (i, 0))
    kept_specs, kept_shapes = [row] * 4, [SDS((T, D), BF16)] * 4
    if not last:
        res = _call(body, "out_fwd", (T // tm,), _out_specs_in(tm) + [row], [row] + kept_specs,
                    [SDS((T, D), F32)] + kept_shapes, [],
                    (ac, o, ap, z, z, z, wco, wao, wpo, wout, postg, x), ("parallel",), ex)
        return res[:1], res[1:5], res[5:]
    res = _call(body, "out_fwd_loss", (T // tm,), _out_specs_in(tm) + [row, row],
                [row, pl.BlockSpec((1, 128), lambda i: (0, 0))] + kept_specs,
                [SDS((T, D), F32), SDS((1, 128), F32)] + kept_shapes, [],
                (ac, o, ap, z, z, z, wco, wao, wpo, wout, postg, x, tgt), ("arbitrary",), ex)
    return res[:2], res[2:6], res[6:]


def _out_bwd(dout, ac, o, ap, z, kept, wco, wao, wpo, wout, postg, ex=None):
    T = dout.shape[0]
    tm = 256

    def body(ac_ref, o_ref, ap_ref, ag_ref, gl_ref, gh_ref, wco_ref, wao_ref, wpo_ref, wout_ref, pg_ref, do_ref,
             yc_ref, ya_ref, yp_ref, y_ref,
             dac_ref, dao_ref, dag_ref, dap_ref, dgm_ref, dwco_ref, dwao_ref, dwpo_ref, dwout_ref, dpg_ref):
        @pl.when(pl.program_id(0) == 0)
        def _():
            for r in (dwco_ref, dwao_ref, dwpo_ref, dwout_ref, dpg_ref):
                r[...] = jnp.zeros_like(r)

        ag = ag_ref[...].astype(F32)
        sag = _sig(ag)
        ov = o_ref[...].astype(F32)
        acts = (ac_ref[...], (ov * ag * sag).astype(BF16), ap_ref[...])
        ws = (wco_ref, wao_ref, wpo_ref)
        gates = _gates(gl_ref, gh_ref)
        ys = [r[...].astype(F32) for r in (yc_ref, ya_ref, yp_ref)]
        merged = (gates[0] * ys[0] + gates[1] * ys[1] + gates[2] * ys[2]).astype(BF16)
        y = y_ref[...].astype(F32)
        ry = lax.rsqrt(jnp.mean(y * y, axis=-1, keepdims=True) + EPS)
        yn = y * ry
        dout_v = do_ref[...]
        dpg_ref[...] += jnp.sum(dout_v * yn, axis=0, keepdims=True)
        dyn = dout_v * pg_ref[...]
        dy = (ry * (dyn - yn * jnp.mean(dyn * yn, axis=-1, keepdims=True))).astype(BF16)
        dmerged = _nt(dy, wout_ref[...])
        dwout_ref[...] += _tn(merged, dy)
        dws = (dwco_ref, dwao_ref, dwpo_ref)
        das = []
        for b in range(3):
            gb = gates[b]
            dgm_ref[:, b * D:(b + 1) * D] = (dmerged * ys[b] * gb * (1.0 - gb)).astype(BF16)
            dyb = (dmerged * gb).astype(BF16)
            dws[b][...] += _tn(acts[b], dyb)
            das.append(_nt(dyb, ws[b][...]))
        dac_ref[...] = das[0].astype(BF16)
        dap_ref[...] = das[2].astype(BF16)
        dao_ref[...] = (das[1] * ag * sag).astype(BF16)
        dag_ref[...] = (das[1] * ov * _dsilu(ag, sag)).astype(BF16)

    row = lambda w: pl.BlockSpec((tm, w), lambda i: (i, 0))
    full = lambda shape: pl.BlockSpec(shape, lambda i: (0,) * len(shape))
    return _call(
        body, "out_bwd", (T // tm,), _out_specs_in(tm) + [row(D)] * 5,
        [row(BW), row(BW), row(BW), row(BW), row(3 * D),
         full((BW, D)), full((BW, D)), full((BW, D)), full((D, D)), full((1, D))],
        [SDS((T, BW), BF16)] * 4 + [SDS((T, 3 * D), BF16)]
        + [SDS((BW, D), F32)] * 3 + [SDS((D, D), F32), SDS((1, D), F32)], [],
        (ac, o, ap, z, z, z, wco, wao, wpo, wout, postg, dout, *kept), ("arbitrary",), ex)


def _attn_bwd(z, dao, f, S, ex=None):
    T = z.shape[0]
    nq = S // TQ
    nsteps = T // TQ

    def body(q_ref, k2_ref, k1_ref, k0_ref, v2_ref, v1_ref, v0_ref, do_ref, f_ref,
             dq_ref, dk_ref, dv_ref, df_ref, kbuf, vbuf, dkacc, dvacc, b_scr, db_scr):
        i = pl.program_id(0)
        qb = i % nq

        @pl.when(i == 0)
        def _():
            _bias_from_skew(f_ref, b_scr)
            db_scr[...] = jnp.zeros_like(db_scr)

        @pl.when(qb == 0)
        def _():
            dkacc[...] = jnp.zeros_like(dkacc)
            dvacc[...] = jnp.zeros_like(dvacc)

        kbuf[0:TQ, :] = k2_ref[...]
        kbuf[TQ:2 * TQ, :] = k1_ref[...]
        kbuf[2 * TQ:KW, :] = k0_ref[...]
        vbuf[0:TQ, :] = v2_ref[...]
        vbuf[TQ:2 * TQ, :] = v1_ref[...]
        vbuf[2 * TQ:KW, :] = v0_ref[...]
        lane = lax.broadcasted_iota(jnp.int32, (1, 128), 1)
        row0 = pl.multiple_of(qb * TQ, TQ)

        def attend(lo):
            def first_matmuls(h):
                sl = slice((h // 2) * 128, (h // 2 + 1) * 128)
                msk = (lane < HD) if h % 2 == 0 else (lane >= HD)
                qp = q_ref[:, sl] * 0.125
                dop = do_ref[:, sl]
                qm = jnp.where(msk, qp, jnp.zeros_like(qp))
                dom = jnp.where(msk, dop, jnp.zeros_like(dop))
                s = _nt(qm, kbuf[lo:KW, sl]) + b_scr[h, :, lo:KW]
                return s, _nt(dom, vbuf[lo:KW, sl]), qm, dom

            cur = first_matmuls(0)
            dq_acc = dk_acc = dv_acc = None
            for h in range(HEADS):
                nxt = first_matmuls(h + 1) if h + 1 < HEADS else None
                s, dp, qm, dom = cur
                sl = slice((h // 2) * 128, (h // 2 + 1) * 128)
                e = jnp.exp(s - jnp.max(s, axis=-1, keepdims=True))
                p = e * (1.0 / jnp.sum(e, axis=-1, keepdims=True))
                ds = p * (dp - jnp.sum(p * dp, axis=-1, keepdims=True))
                db_scr[h, :, lo:KW] += ds
                dsb = ds.astype(BF16)
                kp = kbuf[lo:KW, sl]
                km = jnp.where((lane < HD) if h % 2 == 0 else (lane >= HD), kp, jnp.zeros_like(kp))
                dq_h = _nn(dsb, km) * 0.125
                dk_h = _tn(dsb, qm)
                dv_h = _tn(p.astype(BF16), dom)
                if h % 2 == 0:
                    dq_acc, dk_acc, dv_acc = dq_h, dk_h, dv_h
                else:
                    dq_ref[:, sl] = (dq_acc + dq_h).astype(BF16)
                    dkacc[pl.ds(row0 + lo, KW - lo), sl] += dk_acc + dk_h
                    dvacc[pl.ds(row0 + lo, KW - lo), sl] += dv_acc + dv_h
                cur = nxt

        for nblk in (1, 2, 3):
            pl.when(jnp.minimum(qb, 2) == nblk - 1)(functools.partial(attend, (3 - nblk) * TQ))

        @pl.when(qb == nq - 1)
        def _():
            dk_ref[...] = dkacc[2 * TQ:2 * TQ + S, :].astype(BF16)
            dv_ref[...] = dvacc[2 * TQ:2 * TQ + S, :].astype(BF16)

        @pl.when(i == nsteps - 1)
        def _():
            for h in range(HEADS):
                df_ref[h:h + 1, :] = _skew_from_bias(db_scr[h])

    full = lambda shape: pl.BlockSpec(shape, lambda i: (0,) * len(shape))
    return _call(
        body, "attn_bwd", (nsteps,),
        _attn_specs(nq) + [pl.BlockSpec((TQ, BW), lambda i: (i, 0)), full((HEADS, NSKEW))],
        [pl.BlockSpec((TQ, BW), lambda i: (i, 0)), pl.BlockSpec((S, BW), lambda i: (i // nq, 0)),
         pl.BlockSpec((S, BW), lambda i: (i // nq, 0)), full((HEADS, NSKEW))],
        [SDS((T, BW), BF16)] * 3 + [SDS((HEADS, NSKEW), F32)],
        [pltpu.VMEM((KW, BW), BF16), pltpu.VMEM((KW, BW), BF16),
         pltpu.VMEM((S + 2 * TQ, BW), F32), pltpu.VMEM((S + 2 * TQ, BW), F32),
         pltpu.VMEM((HEADS, TQ, KW), F32), pltpu.VMEM((HEADS, TQ, KW), F32)],
        (z, z, z, z, z, z, z, dao, f), ("arbitrary",), ex)


def _mix_bwd(z, u1, u0, pooled_kept, dac, dap, dw, dwb, lng, lnb, pw, pb, ps, S, ex=None):
    T = z.shape[0]
    tm = min(S, 1024)
    ts = S // tm
    nh = tm // HALO
    nsteps = T // tm
    nblk32 = T // HALO

    def body(ca_ref, cb_ref, cg_ref, pg_ref, u1_ref, u0_ref, pooled, dac_ref, dap_ref,
             u0h_ref,
             cgn_ref, pgn_ref, u1n_ref, dacn_ref, dapn_ref,
             dw_ref, dwb_ref, lng_ref, lnb_ref, pw_ref, pb_ref, ps_ref,
             dzc_ref, dzp_ref, ddw_ref, ddwb_ref, dlng_ref, dlnb_ref, dpw_ref, dpb_ref, dps_ref,
             ubuf, gbuf, qbuf, *accs):
        tap_acc, (lng_acc, lnb_acc, dwb_acc) = accs[:CONV_K], accs[CONV_K:]
        i = pl.program_id(0)
        keep_prev = jnp.where((i % ts) == 0, 0.0, 1.0)
        keep_next = jnp.where((i % ts) == ts - 1, 0.0, 1.0)
        t0 = (i % ts) * tm

        @pl.when(i == 0)
        def _():
            for a in accs:
                a[...] = jnp.zeros_like(a)
            dpw_ref[...] = jnp.zeros_like(dpw_ref)
            dpb_ref[...] = jnp.zeros_like(dpb_ref)
            dps_ref[...] = jnp.zeros_like(dps_ref)

        ubuf[0:HALO, :] = u0h_ref[...] * keep_prev
        ubuf[HALO:HALO + tm, :] = u0_ref[...]

        def norm_back(u1v, cg, dacv):
            n, rstd = _layer_norm_fwd(u1v)
            u2 = n * lng_ref[...] + lnb_ref[...]
            s2 = _sig(u2)
            scg = _sig(cg)
            du2 = dacv * cg * scg * _dsilu(u2, s2)
            dn = du2 * lng_ref[...]
            du1 = rstd * (dn - jnp.mean(dn, axis=-1, keepdims=True)
                          - n * jnp.mean(dn * n, axis=-1, keepdims=True))
            return du1, du2, n, dacv * u2 * s2 * _dsilu(cg, scg)

        def chunk_a(c, carry):
            base = pl.multiple_of(c * RC, RC)
            du1, du2, n, dcg = norm_back(u1_ref[pl.ds(base, RC), :], cg_ref[pl.ds(base, RC), :].astype(F32),
                                         dac_ref[pl.ds(base, RC), :].astype(F32))
            gbuf[pl.ds(base, RC), :] = du1
            dzc_ref[pl.ds(base, RC), 2 * BW:3 * BW] = dcg.astype(BF16)
            lng_acc[...] += _rows8(du2 * n)
            lnb_acc[...] += _rows8(du2)
            dwb_acc[...] += _rows8(du1)
            padded = jnp.concatenate([du1, jnp.zeros((8, BW), F32)], axis=0)
            for r in range(8):
                nrow = RC if r == 0 else RC + 8
                g = du1 if r == 0 else pltpu.roll(padded, r, axis=0)
                for q in range((CONV_K + 1 - r) // 8 + 1):
                    o = 8 * q + r
                    if o < 2:
                        continue
                    prod = g * ubuf[pl.ds(base + 8 * q, nrow), :]
                    red = prod[0:8]
                    for k in range(1, nrow // 8):
                        red = red + prod[8 * k:8 * k + 8]
                    tap_acc[o - 2][...] += red
            return carry

        lax.fori_loop(0, tm // RC, chunk_a, 0, unroll=4)
        du1n, _, _, _ = norm_back(u1n_ref[...], cgn_ref[...].astype(F32), dacn_ref[...].astype(F32))
        gbuf[tm:tm + HALO, :] = du1n * keep_next

        def cnt_of(t_first, rows, w):
            t = t_first + lax.broadcasted_iota(jnp.int32, (rows, 128), 0)
            return jnp.minimum(t + 1, w).astype(F32)

        pg = pg_ref[...].astype(F32)
        spg_s = _sig(pg)
        dapv = dap_ref[...].astype(F32)
        pgn = pgn_ref[...].astype(F32)
        dmixn = dapn_ref[...].astype(F32) * pgn * _sig(pgn) * ps_ref[...] * keep_next
        for g, w in enumerate(WINDOWS):
            sl = slice(g * 128, (g + 1) * 128)
            mixed_u = _nn(pooled[:, sl], pw_ref[g]) + pb_ref[:, sl]
            dap_g = dapv[:, sl]
            pg_g = pg[:, sl]
            s_g = spg_s[:, sl]
            silu_g = pg_g * s_g
            dps_ref[:, sl] += jnp.sum(dap_g * silu_g * mixed_u, axis=0, keepdims=True)
            dzp_ref[:, BW + g * 128:BW + (g + 1) * 128] = (
                dap_g * mixed_u * ps_ref[:, sl] * _dsilu(pg_g, s_g)).astype(BF16)
            dmix = dap_g * silu_g * ps_ref[:, sl]
            dpb_ref[:, sl] += jnp.sum(dmix, axis=0, keepdims=True)
            dmixb = dmix.astype(BF16)
            dpw_ref[g] += _tn(pooled[:, sl], dmixb)
            qbuf[0:tm, sl] = _nt(dmixb, pw_ref[g]) / cnt_of(t0, tm, w)
            qbuf[tm:tm + HALO, sl] = _nt(dmixn[:, sl].astype(BF16), pw_ref[g]) / cnt_of(t0 + tm, HALO, w)

        def chunk_b(c, carry):
            base = pl.multiple_of(c * RC, RC)
            load = lambda q, n: gbuf[pl.ds(base + 8 * q, n), :]
            du0 = _stencil(load, dw_ref, 0, CONV_K - 1, lambda o: CONV_K - 1 - o)
            ca = ca_ref[pl.ds(base, RC), :].astype(F32)
            sb = _sig(cb_ref[pl.ds(base, RC), :].astype(F32))
            dzc_ref[pl.ds(base, RC), 0:BW] = (du0 * sb).astype(BF16)
            dzc_ref[pl.ds(base, RC), BW:2 * BW] = (du0 * ca * sb * (1.0 - sb)).astype(BF16)
            qwin = qbuf[pl.ds(base, RC + HALO), :]
            t = t0 + base + lax.broadcasted_iota(jnp.int32, (RC, 128), 0)
            for g, w in enumerate(WINDOWS):
                x = qwin[:, g * 128:(g + 1) * 128]
                s = _window_sums(x, w, False)
                cnt = jnp.minimum(t + 1, w).astype(F32)
                dzp_ref[pl.ds(base, RC), g * 128:(g + 1) * 128] = (s[0:RC] - cnt * x[0:RC]).astype(BF16)
            return carry

        lax.fori_loop(0, tm // RC, chunk_b, 0)

        @pl.when(i == nsteps - 1)
        def _():
            dlng_ref[...] = jnp.sum(lng_acc[...], axis=0, keepdims=True)
            dlnb_ref[...] = jnp.sum(lnb_acc[...], axis=0, keepdims=True)
            ddwb_ref[...] = jnp.sum(dwb_acc[...], axis=0, keepdims=True)
            for j in range(CONV_K):
                ddw_ref[j:j + 1, :] = jnp.sum(tap_acc[j][...], axis=0, keepdims=True)

    def zmain(cb):
        return pl.BlockSpec((tm, BW), lambda i: (i, cb))

    def zprev(cb):
        return pl.BlockSpec((HALO, BW), lambda i: (jnp.maximum(i * nh - 1, 0), cb))

    def znext(cb):
        return pl.BlockSpec((HALO, BW), lambda i: (jnp.minimum((i + 1) * nh, nblk32 - 1), cb))

    row = lambda w: pl.BlockSpec((tm, w), lambda i: (i, 0))
    full = lambda shape: pl.BlockSpec(shape, lambda i: (0,) * len(shape))
    return _call(
        body, "mix_bwd", (nsteps,),
        [zmain(CB_CA), zmain(CB_CB), zmain(CB_CG), zmain(CB_PG), row(BW), row(BW), row(BW), row(BW), row(BW),
         zprev(0),
         znext(CB_CG), znext(CB_PG), znext(0), znext(0), znext(0),
         full((CONV_K, BW)), full((1, BW)), full((1, BW)), full((1, BW)),
         full((4, 128, 128)), full((1, BW)), full((1, BW))],
        [row(3 * BW), row(2 * BW), full((CONV_K, BW)), full((1, BW)), full((1, BW)), full((1, BW)),
         full((4, 128, 128)), full((1, BW)), full((1, BW))],
        [SDS((T, 3 * BW), BF16), SDS((T, 2 * BW), BF16), SDS((CONV_K, BW), F32),
         SDS((1, BW), F32), SDS((1, BW), F32), SDS((1, BW), F32),
         SDS((4, 128, 128), F32), SDS((1, BW), F32), SDS((1, BW), F32)],
        [pltpu.VMEM((HALO + tm, BW), F32), pltpu.VMEM((tm + HALO, BW), F32), pltpu.VMEM((tm + HALO, BW), F32)]
        + [pltpu.VMEM((8, BW), F32)] * (CONV_K + 3),
        (z, z, z, z, u1, u0, pooled_kept, dac, dap, u0, z, z, u1, dac, dap, dw, dwb, lng, lnb, pw, pb, ps),
        ("arbitrary",), ex)


def _in_bwd_x(pieces, wt, x, g, dout, ex=None):
    T = x.shape[0]
    tm = 512
    widths = [p.shape[1] for p in pieces]
    offs = np.cumsum([0] + widths)
    npc = len(pieces)

    def body(*refs):
        p_refs = refs[:npc]
        w_ref, x_ref, g_ref, do_ref, dx_ref, dg_ref = refs[npc:]

        @pl.when(pl.program_id(0) == 0)
        def _():
            dg_ref[...] = jnp.zeros_like(dg_ref)

        dh = None
        for k in range(npc):
            t = _nn(p_refs[k][...], w_ref[int(offs[k]):int(offs[k + 1]), :])
            dh = t if dh is None else dh + t
        xv = x_ref[...]
        r = lax.rsqrt(jnp.mean(xv * xv, axis=-1, keepdims=True) + EPS)
        xn = xv * r
        dg_ref[...] += jnp.sum(dh * xn, axis=0, keepdims=True)
        dxn = dh * g_ref[...]
        dx_ref[...] = do_ref[...] + r * (dxn - xn * jnp.mean(dxn * xn, axis=-1, keepdims=True))

    row = lambda wd: pl.BlockSpec((tm, wd), lambda i: (i, 0))
    return _call(
        body, "in_bwd_x", (T // tm,),
        [row(wd) for wd in widths] + [pl.BlockSpec((NCOL, D), lambda i: (0, 0)),
                                      row(D), pl.BlockSpec((1, D), lambda i: (0, 0)), row(D)],
        [row(D), pl.BlockSpec((1, D), lambda i: (0, 0))], [SDS((T, D), F32), SDS((1, D), F32)], [],
        (*pieces, wt, x, g, dout), ("arbitrary",), ex)


def _in_bwd_w(ht, pieces, row0, buf=None, ex=None):
    T = ht.shape[1]
    n = len(pieces)
    wd = pieces[0].shape[1]
    tn = next(t for t in (1536, 1024, 768, 512) if wd % t == 0 and row0 % t == 0)
    per = wd // tn
    tk = min(T, 2048)
    nk = T // tk
    j0 = row0 // tn

    def body(ht_ref, *rest):
        p_refs = rest[:n]
        o_ref, acc = rest[-2:]
        j, k = pl.program_id(0), pl.program_id(1)

        @pl.when(k == 0)
        def _():
            acc[...] = jnp.zeros_like(acc)

        for p in range(n):
            @pl.when(j // per == p)
            def _(p=p):
                acc[...] += _nn(ht_ref[...], p_refs[p][...])

        @pl.when(k == nk - 1)
        def _():
            o_ref[...] = acc[...].T.astype(BF16)

    def piece_spec(p):
        return pl.BlockSpec((tk, tn), lambda j, k: (jnp.where(j // per == p, k, 0), jnp.where(j // per == p, j % per, 0)))

    in_specs = [pl.BlockSpec((D, tk), lambda j, k: (0, k))] + [piece_spec(p) for p in range(n)]
    args = (ht, *pieces)
    if buf is not None:
        in_specs.append(pl.BlockSpec(memory_space=pl.ANY))
        args += (buf,)
    return _call(
        body, "in_bwd_w", (n * per, nk), in_specs,
        [pl.BlockSpec((tn, D), lambda j, k: (j + j0, 0))], [SDS((NCOL, D), BF16)], [pltpu.VMEM((D, tn), F32)],
        args, ("arbitrary", "arbitrary"), ex, None if buf is None else {n + 1: 0})


def _my_id():
    return 4 * lax.axis_index("x") + 2 * lax.axis_index("y") + lax.axis_index("c")


def _peers():
    x, y, c = lax.axis_index("x"), lax.axis_index("y"), lax.axis_index("c")
    out = []
    for k in range(1, N_DEV):
        fx, fy, fc = (k >> 2) & 1, (k >> 1) & 1, k & 1
        px, py, pc = x ^ fx, y ^ fy, c ^ fc
        out.append(((px, py, pc), 4 * px + 2 * py + pc))
    return out


class _Exchange:
    def __init__(self, arrays, scatter):
        self.arrays = list(arrays)
        self.scatter = list(scatter)
        self.n = n = len(arrays)
        hbm = pl.BlockSpec(memory_space=pltpu.HBM)
        self.in_specs = [hbm] * n
        self.out_specs = [hbm] * n
        self.out_shape = [SDS((N_DEV,) + tuple(a.shape[1:] if s else a.shape), a.dtype)
                          for a, s in zip(arrays, scatter)]
        self.scratch = [pltpu.SemaphoreType.DMA((N_DEV - 1, n)), pltpu.SemaphoreType.DMA((N_DEV - 1, n)),
                        pltpu.SemaphoreType.DMA((n,))]

    def split(self, refs, n_in, n_out):
        n = self.n
        own_in = refs[:n_in]
        ex_in = refs[n_in:n_in + n]
        own_out = refs[n_in + n:n_in + n + n_out]
        ex_out = refs[n_in + n + n_out:n_in + 2 * n + n_out]
        rest = refs[n_in + 2 * n + n_out:]
        return own_in, own_out, rest[:-3], (ex_in, ex_out, rest[-3:])

    def _copy(self, ex, k, p, landing):
        in_refs, out_refs, (send_sems, recv_sems, _) = ex
        pos, pid = _peers()[p]
        return pltpu.make_async_remote_copy(
            src_ref=in_refs[k].at[pid] if self.scatter[k] else in_refs[k],
            dst_ref=out_refs[k].at[pid if landing else _my_id()],
            send_sem=send_sems.at[p, k], recv_sem=recv_sems.at[p, k],
            device_id=pos, device_id_type=pl.DeviceIdType.MESH)

    def _own(self, ex, k):
        in_refs, out_refs, (_, _, local_sems) = ex
        me = _my_id()
        return pltpu.make_async_copy(in_refs[k].at[me] if self.scatter[k] else in_refs[k], out_refs[k].at[me],
                                     local_sems.at[k])

    def start(self, ex):
        for k in range(self.n):
            self._own(ex, k).start()
        for p in range(N_DEV - 1):
            for k in range(self.n):
                self._copy(ex, k, p, False).start()

    def finish(self, ex):
        for p in range(N_DEV - 1):
            for k in range(self.n):
                self._copy(ex, k, p, True).wait_recv()
        for p in range(N_DEV - 1):
            for k in range(self.n):
                self._copy(ex, k, p, False).wait_send()
        for k in range(self.n):
            self._own(ex, k).wait()


def _exchange(arrays, scatter, name):
    ex = _Exchange(arrays, scatter)

    def body(*refs):
        _, _, _, exr = ex.split(refs, 0, 0)
        ex.start(exr)
        ex.finish(exr)

    return pl.pallas_call(body, name=name, in_specs=ex.in_specs, out_specs=ex.out_specs,
                          out_shape=ex.out_shape, scratch_shapes=ex.scratch)(*ex.arrays)


def _gather_two_level(shard, name):
    def body(x_ref, out_ref, send_sems, recv_sems, local_sem):
        x, y, c = lax.axis_index("x"), lax.axis_index("y"), lax.axis_index("c")
        me, sibling = (x, y, c), (x, y, 1 - c)
        chips = [(1 - x, y), (x, 1 - y), (1 - x, 1 - y)]
        slab = lambda px, py, pc: out_ref.at[4 * px + 2 * py + pc]

        def copy(k, block, to, src=None):
            return pltpu.make_async_remote_copy(
                src_ref=slab(*block) if src is None else src, dst_ref=slab(*block),
                send_sem=send_sems.at[k], recv_sem=recv_sems.at[k],
                device_id=to, device_id_type=pl.DeviceIdType.MESH)

        mine = pltpu.make_async_copy(x_ref, slab(*me), local_sem)
        mine.start()
        first = [copy(0, me, sibling, src=x_ref)] + [copy(1 + j, me, (*chip, c), src=x_ref)
                                                     for j, chip in enumerate(chips)]
        for cp in first:
            cp.start()
        passed = [copy(4 + j, (*chip, c), sibling) for j, chip in enumerate(chips)]
        for j, chip in enumerate(chips):
            copy(1 + j, (*chip, c), me).wait_recv()
            passed[j].start()
        copy(0, sibling, me).wait_recv()
        for j, chip in enumerate(chips):
            copy(4 + j, (*chip, 1 - c), me).wait_recv()
        for cp in first + passed:
            cp.wait_send()
        mine.wait()

    hbm = pl.BlockSpec(memory_space=pltpu.HBM)
    return pl.pallas_call(
        body, name=name, in_specs=[hbm], out_specs=hbm,
        out_shape=SDS((N_DEV,) + shard.shape, shard.dtype),
        scratch_shapes=[pltpu.SemaphoreType.DMA((N_DEV - 1,)), pltpu.SemaphoreType.DMA((N_DEV - 1,)),
                        pltpu.SemaphoreType.DMA],
    )(shard)


def _adamw_update(g, w, m, v):
    c1 = 1.0 / (1.0 - ADAM_B1 ** ADAM_STEP)
    c2 = 1.0 / (1.0 - ADAM_B2 ** ADAM_STEP)
    mn = ADAM_B1 * m + (1.0 - ADAM_B1) * g
    vn = ADAM_B2 * v + (1.0 - ADAM_B2) * (g * g)
    return -ADAM_LR * ((mn * c1) / (jnp.sqrt(vn * c2) + ADAM_EPS) + ADAM_WD * w), mn, vn


def _adamw_small(parts, w, m, v):
    n = len(w)

    def body(*refs):
        p_refs = (refs[0:n], refs[n:2 * n])
        w_refs, m_refs, v_refs = refs[2 * n:3 * n], refs[3 * n:4 * n], refs[4 * n:5 * n]
        outs = refs[5 * n:]
        for k in range(n):
            g_ref, d_ref, mo_ref, vo_ref = outs[4 * k:4 * k + 4]
            for l in range(2):
                at = (slice(l, l + 1),) if len(w_refs[k].shape) == 2 else (l,)
                g = p_refs[l][k][0]
                for s in range(1, N_DEV):
                    g = g + p_refs[l][k][s]
                delta, mn, vn = _adamw_update(g, w_refs[k][at], m_refs[k][at], v_refs[k][at])
                g_ref[at] = g
                d_ref[at] = delta
                mo_ref[at] = mn
                vo_ref[at] = vn

    vmem = pl.BlockSpec(memory_space=pltpu.VMEM)
    res = pl.pallas_call(
        body, name="adamw_replicated", in_specs=[vmem] * (5 * n), out_specs=[vmem] * (4 * n),
        out_shape=[SDS(a.shape, F32) for a in w for _ in range(4)],
        compiler_params=pltpu.CompilerParams(vmem_limit_bytes=VMEM_LIMIT),
    )(*parts[0], *parts[1], *w, *m, *v)
    return [res[4 * k:4 * k + 4] for k in range(n)]


def _adamw_sum(parts0, parts1, w, m, v, name):
    _, R, C = w.shape
    tr = R
    while tr * C > 256 * 1024 and tr % 32 == 0:
        tr //= 2

    def body(p0_ref, p1_ref, w_ref, m_ref, v_ref, g_ref, d_ref, mo_ref, vo_ref):
        def update(p_ref):
            g = p_ref[0].astype(F32)
            for s in range(1, N_DEV):
                g = g + p_ref[s].astype(F32)
            g_ref[...] = g
            d_ref[...], mo_ref[...], vo_ref[...] = _adamw_update(g, w_ref[...], m_ref[...], v_ref[...])

        @pl.when(pl.program_id(0) == 0)
        def _():
            update(p0_ref)

        @pl.when(pl.program_id(0) == 1)
        def _():
            update(p1_ref)

    blk = pl.BlockSpec((None, tr, C), lambda l, i: (l, i, 0))
    return pl.pallas_call(
        body, name=name, grid=(2, R // tr),
        in_specs=[pl.BlockSpec((N_DEV, tr, C), lambda l, i: (0, i * (1 - l), 0)),
                  pl.BlockSpec((N_DEV, tr, C), lambda l, i: (0, i * l, 0)), blk, blk, blk],
        out_specs=[blk, blk, blk, blk],
        out_shape=[SDS((2, R, C), F32)] * 4,
        compiler_params=_cparams(("arbitrary", "arbitrary")),
    )(parts0, parts1, w, m, v)


def _layer_fwd(x, P, skew, S, rest, ex, tgt=None):
    z, ht, *got0 = _in_proj(x, P["pre_g"], P["w_in_t"], ex[0])
    P = {**P, **rest(got0)}
    ac, ap, u1, u0, pooled, *got1 = _mix_fwd(z, P["conv_dw"], P["conv_dw_b"], P["conv_ln_g"], P["conv_ln_b"],
                                 P["pool_w"], P["pool_b"], P["pool_scale"], S, ex[1])
    o, *got2 = _attn_fwd(z, skew, S, ex[2])
    out, kept, got3 = _out_fwd(x, ac, o, ap, z, P["w_conv_out"], P["w_attn_out"], P["w_pool_out"], P["w_out"],
                               P["post_g"], ex[3], tgt)
    return out, (x, z, ht, ac, o, ap, u1, u0, pooled, kept), P, (got0, got1, got2, got3)


def _layer_bwd(dout, saved, P, skew, S, ex=(None, None), mix_ex=None, win_ex=None):
    x, z, ht, ac, o, ap, u1, u0, pooled, kept = saved
    (dac, dao, dag, dap, dgm, dwco, dwao, dwpo, dwout, dpostg, *got0) = _out_bwd(
        dout, ac, o, ap, z, kept, P["w_conv_out"], P["w_attn_out"], P["w_pool_out"], P["w_out"], P["post_g"], ex[0])
    grads = dict(post_norm_g=dpostg, w_conv_out=dwco, w_attn_out=dwao, w_pool_out=dwpo, w_out=dwout)
    dq, dk, dv, grads["dskew"], *got1 = _attn_bwd(z, dao, skew, S, ex[1])
    (dzc, dzp, grads["conv_dw"], grads["conv_dw_b"], grads["conv_ln_g"], grads["conv_ln_b"], grads["pool_w"],
     grads["pool_b"], grads["pool_scale"], *got2) = _mix_bwd(
        z, u1, u0, pooled, dac, dap, P["conv_dw"], P["conv_dw_b"], P["conv_ln_g"], P["conv_ln_b"],
        P["pool_w"], P["pool_b"], P["pool_scale"], S, mix_ex(grads) if mix_ex else None)
    pieces = [dzc, dq, dk, dv, dag, dzp, dgm]
    buf, row0 = None, 0
    for group in ([dzc], [dq, dk, dv, dag], [dzp], [dgm]):
        (buf,) = _in_bwd_w(ht, group, row0, buf)
        row0 += sum(p.shape[1] for p in group)
    grads["w_in_t"] = buf
    dx, grads["pre_norm_g"], *got3 = _in_bwd_x(pieces, P["w_in_t"], x, P["pre_g"], dout,
                                               win_ex(grads) if win_ex else None)
    return dx, grads, (got0, got1, got2, got3)


WEIGHT_NAMES = ("pre_norm_g", "post_norm_g", "w_in", "conv_dw", "conv_dw_b", "conv_ln_g", "conv_ln_b",
                "w_conv_out", "rel_bias", "w_attn_out", "pool_w", "pool_b", "pool_scale", "w_pool_out", "w_out")
SHARDED = ("w_in", "w_conv_out", "w_attn_out", "w_pool_out", "w_out", "conv_dw")
OUT_PROJ = ("w_conv_out", "w_attn_out", "w_pool_out", "w_out")
REST = tuple(n for n in WEIGHT_NAMES if n not in ("w_in", "pre_norm_g"))


def _cols_from_slabs(g):
    return g.transpose(1, 0, 2).reshape(g.shape[1], N_DEV * g.shape[2])


def _slabs_from_cols(full):
    r, wd = full.shape
    return full.reshape(r, N_DEV, wd // N_DEV).transpose(1, 0, 2)


def _rest_shards(weights, l):
    return [weights["w_conv_out"][l].astype(BF16), weights["w_attn_out"][l].astype(BF16),
            weights["w_pool_out"][l].astype(BF16), weights["w_out"][l].astype(BF16), weights["conv_dw"][l]]


def _rest_weights(got):
    wco, wao, wpo, wout, cdw = got
    return dict(w_conv_out=_cols_from_slabs(wco), w_attn_out=_cols_from_slabs(wao),
                w_pool_out=_cols_from_slabs(wpo), w_out=wout.reshape(D, D), conv_dw=_cols_from_slabs(cdw))


def _grad_arrays(g, names):
    make = {"w_in": lambda: g["w_in_t"].reshape(N_DEV, NCOL // N_DEV, D),
            "w_conv_out": lambda: _slabs_from_cols(g["w_conv_out"].astype(BF16)),
            "w_attn_out": lambda: _slabs_from_cols(g["w_attn_out"].astype(BF16)),
            "w_pool_out": lambda: _slabs_from_cols(g["w_pool_out"].astype(BF16)),
            "w_out": lambda: g["w_out"].astype(BF16).reshape(N_DEV, D // N_DEV, D),
            "conv_dw": lambda: _slabs_from_cols(g["conv_dw"].astype(BF16)),
            "rel_bias": lambda: jnp.dot(g["dskew"], jnp.asarray(_skew_select().T), precision=lax.Precision.HIGHEST),
            "pool_b": lambda: g["pool_b"].reshape(4, 128)}
    return [make[n]() if n in make else g[n] for n in names]


def _grad_exchange(g, names):
    return _Exchange(_grad_arrays(g, names), [n in SHARDED for n in names])


def kernel(x, pre_norm_g, post_norm_g, w_in, conv_dw, conv_dw_b, conv_ln_g, conv_ln_b, w_conv_out, rel_bias, w_attn_out, pool_w, pool_b, pool_scale, w_pool_out, w_out, loss_target, m_pre_norm_g, m_post_norm_g, m_w_in, m_conv_dw, m_conv_dw_b, m_conv_ln_g, m_conv_ln_b, m_w_conv_out, m_rel_bias, m_w_attn_out, m_pool_w, m_pool_b, m_pool_scale, m_w_pool_out, m_w_out, v_pre_norm_g, v_post_norm_g, v_w_in, v_conv_dw, v_conv_dw_b, v_conv_ln_g, v_conv_ln_b, v_w_conv_out, v_rel_bias, v_w_attn_out, v_pool_w, v_pool_b, v_pool_scale, v_w_pool_out, v_w_out):
    given = dict(locals())
    weights = {n: given[n] for n in WEIGHT_NAMES}
    nb, S, _ = x.shape
    T = nb * S
    L = pre_norm_g.shape[0]
    assert L == 2
    x2 = x.reshape(T, D)
    tgt2 = loss_target.reshape(T, D)
    skews = [_skew_table(rel_bias[l]) for l in range(L)]

    def local_params(l):
        return dict(pre_g=pre_norm_g[l:l + 1], post_g=post_norm_g[l:l + 1], conv_dw_b=conv_dw_b[l:l + 1],
                    conv_ln_g=conv_ln_g[l:l + 1], conv_ln_b=conv_ln_b[l:l + 1], pool_w=pool_w[l].astype(BF16),
                    pool_b=pool_b[l].reshape(1, BW), pool_scale=pool_scale[l:l + 1])

    win0 = w_in[0].T.astype(BF16)
    win1 = w_in[1].T.astype(BF16)
    half = win1.shape[0] // 2
    w_in_t0 = _gather_two_level(win0, "gather_w_in_0")
    gather = lambda arrays: _Exchange(arrays, [False] * len(arrays))
    (h,), saved0, P0, (got_rest0, got_a, got_b, got_rest1) = _layer_fwd(
        x2, {**local_params(0), "w_in_t": w_in_t0.reshape(NCOL, D)}, skews[0], S, _rest_weights,
        (gather(_rest_shards(weights, 0)), gather([win1[:half]]), gather([win1[half:]]),
         gather(_rest_shards(weights, 1))))
    w_in_t1 = jnp.concatenate([got_a[0], got_b[0]], axis=1).reshape(NCOL, D)
    (dout, lsum), saved1, P1, _ = _layer_fwd(h, {**local_params(1), "w_in_t": w_in_t1}, skews[1], S,
                                             lambda _: _rest_weights(got_rest1), (None,) * 4, tgt2)

    dout, g1, _ = _layer_bwd(dout, saved1, P1, skews[1], S)
    g1["loss"] = lsum
    others = REST + ("pre_norm_g",)
    late = ("w_in",) + tuple(n for n in REST if n not in OUT_PROJ)
    dout, g0, (got_others1, got_win1, got_outp0, got_late0) = _layer_bwd(
        dout, saved0, P0, skews[0], S, (_grad_exchange(g1, others + ("loss",)), _grad_exchange(g1, ("w_in",))),
        lambda g: _grad_exchange(g, OUT_PROJ), lambda g: _grad_exchange(g, late))
    (got_pre0,) = _exchange([g0["pre_norm_g"]], [False], "gather_grad_pre_norm_g_0")
    parts = [{"pre_norm_g": got_pre0, **dict(zip(OUT_PROJ, got_outp0)), **dict(zip(late, got_late0))},
             {"w_in": got_win1[0], **dict(zip(others + ("loss",), got_others1))}]
    loss = jnp.sum(parts[1].pop("loss")[:, 0, 0]) * (0.5 / D)
    grad_x = dout.reshape(x.shape)

    outs = {}
    small = [n for n in WEIGHT_NAMES if n not in SHARDED]
    res = _adamw_small([[parts[l][n] for n in small] for l in range(L)], [weights[n] for n in small],
                       [given["m_" + n] for n in small], [given["v_" + n] for n in small])
    outs.update(zip(small, res))
    for n in SHARDED:
        view = (lambda a: a.transpose(0, 2, 1)) if n == "w_in" else (lambda a: a)
        res = _adamw_sum(parts[0][n], parts[1][n], view(weights[n]), view(given["m_" + n]), view(given["v_" + n]),
                         "adamw_" + n)
        outs[n] = [view(a) for a in res]
    return (loss, grad_x, *[outs[n][0] for n in WEIGHT_NAMES], *[outs[n][1] for n in WEIGHT_NAMES],
            *[outs[n][2] for n in WEIGHT_NAMES], *[outs[n][3] for n in WEIGHT_NAMES])
```

```python
import functools

import numpy as np
import jax
import jax.numpy as jnp
from jax import lax
from jax.experimental import pallas as pl
from jax.experimental.pallas import tpu as pltpu

F32 = jnp.float32
BF16 = jnp.bfloat16
SDS = jax.ShapeDtypeStruct

D = 1024
BW = 512
NCOL = 7680
EPS = 1e-6
NEG = -1e30
HEADS = 8
HD = 64
CHUNK = 64
LEFT = 8
MAX_REL = 256
TQ = 256
KW = 768
CONV_K = 31
WINDOWS = (2, 4, 8, 16)
HALO = 32
RC = 32
N_DEV = 8

ADAM_LR = 0.001
ADAM_B1 = 0.9
ADAM_B2 = 0.999
ADAM_EPS = 1e-08
ADAM_WD = 0.01
ADAM_STEP = 10

VMEM_LIMIT = 56 * 1024 * 1024

CB_CA, CB_CB, CB_CG, CB_Q, CB_K, CB_V, CB_AG, CB_PI, CB_PG = range(9)


def _cparams(sem):
    return pltpu.CompilerParams(dimension_semantics=sem, vmem_limit_bytes=VMEM_LIMIT)


def _sig(x):
    return 0.5 * jnp.tanh(0.5 * x) + 0.5


def _dsilu(x, s):
    return s * (1.0 + x * (1.0 - s))


def _nt(a, b):
    return lax.dot_general(a, b, (((1,), (1,)), ((), ())), preferred_element_type=F32)


def _tn(a, b):
    return lax.dot_general(a, b, (((0,), (0,)), ((), ())), preferred_element_type=F32)


def _nn(a, b):
    return jnp.dot(a, b, preferred_element_type=F32)


def _rows8(x):
    return x[0:8] + x[8:16] + x[16:24] + x[24:32]


def _call(body, name, grid, in_specs, out_specs, out_shape, scratch, args, sem, ex=None, aliases=None):
    aliases = aliases or {}
    if ex is None:
        return pl.pallas_call(body, name=name, grid=grid, in_specs=in_specs, out_specs=out_specs,
                              out_shape=out_shape, scratch_shapes=scratch, input_output_aliases=aliases,
                              compiler_params=_cparams(sem))(*args)
    n_in, n_out = len(in_specs), len(out_specs)
    steps = int(np.prod(grid))

    def carrier(*refs):
        own_in, own_out, own_scr, exr = ex.split(refs, n_in, n_out)
        step = pl.program_id(0)
        for axis in range(1, len(grid)):
            step = step * grid[axis] + pl.program_id(axis)

        @pl.when(step == 0)
        def _():
            ex.start(exr)

        body(*own_in, *own_out, *own_scr)

        @pl.when(step == steps - 1)
        def _():
            ex.finish(exr)

    return pl.pallas_call(
        carrier, name=name + "_carrier", grid=grid, in_specs=in_specs + ex.in_specs,
        out_specs=out_specs + ex.out_specs, out_shape=out_shape + ex.out_shape,
        scratch_shapes=scratch + ex.scratch, input_output_aliases=aliases,
        compiler_params=_cparams(("arbitrary",) * len(grid)),
    )(*args, *ex.arrays)


def _in_proj(x, g, wt, ex=None):
    T = x.shape[0]
    tm = 512
    tn = 1536

    def body(x_ref, g_ref, w_ref, z_ref, ht_ref):
        xv = x_ref[...]
        r = lax.rsqrt(jnp.mean(xv * xv, axis=-1, keepdims=True) + EPS)
        h = xv * r * g_ref[...]
        hb = h.astype(BF16)
        ht_ref[...] = h.T.astype(BF16)
        for c in range(NCOL // tn):
            z_ref[:, c * tn:(c + 1) * tn] = _nt(hb, w_ref[c * tn:(c + 1) * tn, :]).astype(BF16)

    return _call(
        body, "in_proj", (T // tm,),
        [pl.BlockSpec((tm, D), lambda i: (i, 0)), pl.BlockSpec((1, D), lambda i: (0, 0)),
         pl.BlockSpec((NCOL, D), lambda i: (0, 0))],
        [pl.BlockSpec((tm, NCOL), lambda i: (i, 0)), pl.BlockSpec((D, tm), lambda i: (0, i))],
        [SDS((T, NCOL), BF16), SDS((D, T), BF16)], [],
        (x, g, wt), ("parallel",), ex)


def _stencil(load, w_ref, lo, hi, tap_of):
    out = None
    for r in range(8):
        n = RC if r == 0 else RC + 8
        v = None
        for q in range((hi - r) // 8 + 1):
            o = 8 * q + r
            if o < lo:
                continue
            j = tap_of(o)
            term = w_ref[j:j + 1, :] * load(q, n)
            v = term if v is None else v + term
        if v is None:
            continue
        if r:
            v = pltpu.roll(v, n - r, axis=0)[0:RC]
        out = v if out is None else out + v
    return out


def _layer_norm_fwd(u1):
    mu = jnp.mean(u1, axis=-1, keepdims=True)
    xc = u1 - mu
    rstd = lax.rsqrt(jnp.mean(xc * xc, axis=-1, keepdims=True) + EPS)
    return xc * rstd, rstd


def _window_sums(x, w, back):
    n = x.shape[0]
    s = x
    k = 1
    while k < w:
        s = s + pltpu.roll(s, k if back else n - k, axis=0)
        k *= 2
    return s


def _pool_chunk(pwin, t_first):
    t = t_first + lax.broadcasted_iota(jnp.int32, (RC, 128), 0)
    outs = []
    for g, w in enumerate(WINDOWS):
        x = pwin[:, g * 128:(g + 1) * 128]
        s = _window_sums(x, w, True)
        cnt = jnp.minimum(t + 1, w).astype(F32)
        outs.append(s[HALO:HALO + RC] / cnt - x[HALO:HALO + RC])
    return outs


def _mix_fwd(z, dw, dwb, lng, lnb, pw, pb, ps, S, ex=None):
    T = z.shape[0]
    tm = min(S, 1024)
    ts = S // tm
    nh = tm // HALO

    def body(ca_ref, cb_ref, cg_ref, pi_ref, pg_ref, cah_ref, cbh_ref, pih_ref,
             dw_ref, dwb_ref, lng_ref, lnb_ref, pw_ref, pb_ref, ps_ref,
             ac_ref, ap_ref, u1_ref, u0_ref, pooled, ubuf, pbuf):
        i = pl.program_id(0)
        keep = jnp.where((i % ts) == 0, 0.0, 1.0)
        ubuf[0:HALO, :] = cah_ref[...].astype(F32) * _sig(cbh_ref[...].astype(F32)) * keep
        u0 = ca_ref[...].astype(F32) * _sig(cb_ref[...].astype(F32))
        ubuf[HALO:HALO + tm, :] = u0
        u0_ref[...] = u0
        pbuf[0:HALO, :] = pih_ref[...].astype(F32) * keep
        pbuf[HALO:HALO + tm, :] = pi_ref[...].astype(F32)
        t0 = (i % ts) * tm

        def chunk(c, carry):
            base = pl.multiple_of(c * RC, RC)
            load = lambda q, n: ubuf[pl.ds(base + 8 * q, n), :]
            u1 = _stencil(load, dw_ref, 2, CONV_K + 1, lambda o: o - 2) + dwb_ref[...]
            u1_ref[pl.ds(base, RC), :] = u1
            n, _ = _layer_norm_fwd(u1)
            u2 = n * lng_ref[...] + lnb_ref[...]
            u3 = u2 * _sig(u2)
            cg = cg_ref[pl.ds(base, RC), :].astype(F32)
            ac_ref[pl.ds(base, RC), :] = (u3 * cg * _sig(cg)).astype(BF16)
            pwin = pbuf[pl.ds(base, RC + HALO), :]
            outs = _pool_chunk(pwin, t0 + base)
            for g in range(4):
                pooled[pl.ds(base, RC), g * 128:(g + 1) * 128] = outs[g].astype(BF16)
            return carry

        lax.fori_loop(0, tm // RC, chunk, 0, unroll=4)
        pg = pg_ref[...].astype(F32)
        spg = pg * _sig(pg)
        for g in range(4):
            sl = slice(g * 128, (g + 1) * 128)
            mixed = (_nn(pooled[:, sl], pw_ref[g]) + pb_ref[:, sl]) * ps_ref[:, sl]
            ap_ref[:, sl] = (mixed * spg[:, sl]).astype(BF16)

    def zmain(cb):
        return pl.BlockSpec((tm, BW), lambda i: (i, cb))

    def zprev(cb):
        return pl.BlockSpec((HALO, BW), lambda i: (jnp.maximum(i * nh - 1, 0), cb))

    full = lambda shape: pl.BlockSpec(shape, lambda i: (0,) * len(shape))
    row = pl.BlockSpec((tm, BW), lambda i: (i, 0))
    return _call(
        body, "mix_fwd", (T // tm,),
        [zmain(CB_CA), zmain(CB_CB), zmain(CB_CG), zmain(CB_PI), zmain(CB_PG),
         zprev(CB_CA), zprev(CB_CB), zprev(CB_PI),
         full((CONV_K, BW)), full((1, BW)), full((1, BW)), full((1, BW)),
         full((4, 128, 128)), full((1, BW)), full((1, BW))],
        [row] * 5, [SDS((T, BW), BF16), SDS((T, BW), BF16), SDS((T, BW), F32), SDS((T, BW), F32), SDS((T, BW), BF16)],
        [pltpu.VMEM((HALO + tm, BW), F32), pltpu.VMEM((HALO + tm, BW), F32)],
        (z, z, z, z, z, z, z, z, dw, dwb, lng, lnb, pw, pb, ps), ("parallel",), ex)


def _attn_specs(nq):
    def kv(cb, off):
        return pl.BlockSpec((TQ, BW), lambda i: (i - jnp.minimum(off, i % nq), cb))
    return [pl.BlockSpec((TQ, BW), lambda i: (i, CB_Q)),
            kv(CB_K, 2), kv(CB_K, 1), kv(CB_K, 0), kv(CB_V, 2), kv(CB_V, 1), kv(CB_V, 0)]


NSKEW = 1024


def _skew_table(table):
    return jnp.dot(table, jnp.asarray(_skew_select()), precision=lax.Precision.HIGHEST)


def _skew_select():
    d = np.arange(TQ + KW - 1)
    idx = np.clip(3 * TQ - 1 - d, -MAX_REL, MAX_REL) + MAX_REL
    sel = np.zeros((2 * MAX_REL + 1, NSKEW), np.float32)
    sel[idx, d] = 1.0
    return sel


def _bias_from_skew(f_ref, bias_scr):
    qi = lax.broadcasted_iota(jnp.int32, (TQ, KW), 0)
    kj = lax.broadcasted_iota(jnp.int32, (TQ, KW), 1)
    lo = (qi // CHUNK) * CHUNK
    band = jnp.where((kj >= lo) & (kj < lo + (LEFT + 1) * CHUNK), 0.0, NEG)
    for h in range(HEADS):
        rows = jnp.broadcast_to(f_ref[h:h + 1, :], (TQ, NSKEW))
        rows = pltpu.roll(rows, NSKEW - (TQ - 1), axis=1, stride=1, stride_axis=0)
        bias_scr[h] = rows[:, 0:KW] + band


def _skew_from_bias(db):
    i = lax.broadcasted_iota(jnp.int32, (TQ, TQ), 0)
    j = lax.broadcasted_iota(jnp.int32, (TQ, TQ), 1)
    flip = jnp.where(i + j == TQ - 1, 1.0, 0.0).astype(BF16)
    hi = db.astype(BF16)
    lo = (db - hi.astype(F32)).astype(BF16)
    rev = _nn(flip, hi) + _nn(flip, lo)
    rev = jnp.concatenate([rev, jnp.zeros((TQ, NSKEW - KW), F32)], axis=1)
    return jnp.sum(pltpu.roll(rev, 0, axis=1, stride=1, stride_axis=0), axis=0, keepdims=True)


def _attn_fwd(z, f, S, ex=None):
    T = z.shape[0]
    nq = S // TQ

    def body(q_ref, k2_ref, k1_ref, k0_ref, v2_ref, v1_ref, v0_ref, f_ref, o_ref, kbuf, vbuf, b_scr):
        @pl.when(pl.program_id(0) == 0)
        def _():
            _bias_from_skew(f_ref, b_scr)

        qb = pl.program_id(0) % nq
        kbuf[0:TQ, :] = k2_ref[...]
        kbuf[TQ:2 * TQ, :] = k1_ref[...]
        kbuf[2 * TQ:KW, :] = k0_ref[...]
        vbuf[0:TQ, :] = v2_ref[...]
        vbuf[TQ:2 * TQ, :] = v1_ref[...]
        vbuf[2 * TQ:KW, :] = v0_ref[...]
        lane = lax.broadcasted_iota(jnp.int32, (1, 128), 1)

        def attend(lo):
            def scores(h):
                sl = slice((h // 2) * 128, (h // 2 + 1) * 128)
                qp = q_ref[:, sl] * 0.125
                qm = jnp.where((lane < HD) if h % 2 == 0 else (lane >= HD), qp, jnp.zeros_like(qp))
                return _nt(qm, kbuf[lo:KW, sl]) + b_scr[h, :, lo:KW]

            s = scores(0)
            acc = None
            for h in range(HEADS):
                s_next = scores(h + 1) if h + 1 < HEADS else None
                sl = slice((h // 2) * 128, (h // 2 + 1) * 128)
                e = jnp.exp(s - jnp.max(s, axis=-1, keepdims=True))
                vp = vbuf[lo:KW, sl]
                vm = jnp.where((lane < HD) if h % 2 == 0 else (lane >= HD), vp, jnp.zeros_like(vp))
                o = _nn(e.astype(BF16), vm) * (1.0 / jnp.sum(e, axis=-1, keepdims=True))
                acc = o if h % 2 == 0 else acc + o
                if h % 2 == 1:
                    o_ref[:, sl] = acc.astype(BF16)
                s = s_next

        for nblk in (1, 2, 3):
            pl.when(jnp.minimum(qb, 2) == nblk - 1)(functools.partial(attend, (3 - nblk) * TQ))

    full = lambda shape: pl.BlockSpec(shape, lambda i: (0,) * len(shape))
    return _call(
        body, "attn_fwd", (T // TQ,),
        _attn_specs(nq) + [full((HEADS, NSKEW))],
        [pl.BlockSpec((TQ, BW), lambda i: (i, 0))], [SDS((T, BW), BF16)],
        [pltpu.VMEM((KW, BW), BF16), pltpu.VMEM((KW, BW), BF16), pltpu.VMEM((HEADS, TQ, KW), F32)],
        (z, z, z, z, z, z, z, f), ("arbitrary",), ex)


def _gates(gl_ref, gh_ref):
    gl = _sig(gl_ref[...].astype(F32))
    gh = _sig(gh_ref[...].astype(F32))
    return (gl[:, 0:D], jnp.concatenate([gl[:, D:1536], gh[:, 0:512]], axis=1), gh[:, 512:1536])


def _out_specs_in(tm):
    row = lambda w: pl.BlockSpec((tm, w), lambda i: (i, 0))
    full = lambda shape: pl.BlockSpec(shape, lambda i: (0,) * len(shape))
    return [row(BW), row(BW), row(BW),
            pl.BlockSpec((tm, BW), lambda i: (i, CB_AG)),
            pl.BlockSpec((tm, 1536), lambda i: (i, 3)),
            pl.BlockSpec((tm, 1536), lambda i: (i, 4)),
            full((BW, D)), full((BW, D)), full((BW, D)), full((D, D)), full((1, D))]


def _out_fwd(x, ac, o, ap, z, wco, wao, wpo, wout, postg, ex=None, tgt=None):
    T = x.shape[0]
    tm = 512
    last = tgt is not None

    def body(ac_ref, o_ref, ap_ref, ag_ref, gl_ref, gh_ref, wco_ref, wao_ref, wpo_ref, wout_ref, pg_ref,
             x_ref, *rest):
        ag = ag_ref[...].astype(F32)
        aat = (o_ref[...].astype(F32) * ag * _sig(ag)).astype(BF16)
        gates = _gates(gl_ref, gh_ref)
        acts = (ac_ref[...], aat, ap_ref[...])
        merged = None
        for b, w_ref in enumerate((wco_ref, wao_ref, wpo_ref)):
            yb = _nn(acts[b], w_ref[...])
            rest[-4 + b][...] = yb.astype(BF16)
            merged = gates[b] * yb if merged is None else merged + gates[b] * yb
        y = _nn(merged.astype(BF16), wout_ref[...])
        rest[-1][...] = y.astype(BF16)
        ry = lax.rsqrt(jnp.mean(y * y, axis=-1, keepdims=True) + EPS)
        out = x_ref[...] + y * ry * pg_ref[...]
        if not last:
            rest[0][...] = out
            return
        t_ref, d_ref, l_ref = rest[:3]

        @pl.when(pl.program_id(0) == 0)
        def _():
            l_ref[...] = jnp.zeros_like(l_ref)
        d = out - t_ref[...]
        d_ref[...] = d * (1.0 / D)
        l_ref[...] += jnp.sum(jnp.sum(d * d, axis=0, keepdims=True), axis=1, keepdims=True)

    row = pl.BlockSpec((tm, D), lambda i: (i, 0))
    kept_specs, kept_shapes = [row] * 4, [SDS((T, D), BF16)] * 4
    if not last:
        res = _call(body, "out_fwd", (T // tm,), _out_specs_in(tm) + [row], [row] + kept_specs,
                    [SDS((T, D), F32)] + kept_shapes, [],
                    (ac, o, ap, z, z, z, wco, wao, wpo, wout, postg, x), ("parallel",), ex)
        return res[:1], res[1:5], res[5:]
    res = _call(body, "out_fwd_loss", (T // tm,), _out_specs_in(tm) + [row, row],
                [row, pl.BlockSpec((1, 128), lambda i: (0, 0))] + kept_specs,
                [SDS((T, D), F32), SDS((1, 128), F32)] + kept_shapes, [],
                (ac, o, ap, z, z, z, wco, wao, wpo, wout, postg, x, tgt), ("arbitrary",), ex)
    return res[:2], res[2:6], res[6:]


def _out_bwd(dout, ac, o, ap, z, kept, wco, wao, wpo, wout, postg, ex=None):
    T = dout.shape[0]
    tm = 256

    def body(ac_ref, o_ref, ap_ref, ag_ref, gl_ref, gh_ref, wco_ref, wao_ref, wpo_ref, wout_ref, pg_ref, do_ref,
             yc_ref, ya_ref, yp_ref, y_ref,
             dac_ref, dao_ref, dag_ref, dap_ref, dgm_ref, dwco_ref, dwao_ref, dwpo_ref, dwout_ref, dpg_ref):
        @pl.when(pl.program_id(0) == 0)
        def _():
            for r in (dwco_ref, dwao_ref, dwpo_ref, dwout_ref, dpg_ref):
                r[...] = jnp.zeros_like(r)

        ag = ag_ref[...].astype(F32)
        sag = _sig(ag)
        ov = o_ref[...].astype(F32)
        acts = (ac_ref[...], (ov * ag * sag).astype(BF16), ap_ref[...])
        ws = (wco_ref, wao_ref, wpo_ref)
        gates = _gates(gl_ref, gh_ref)
        ys = [r[...].astype(F32) for r in (yc_ref, ya_ref, yp_ref)]
        merged = (gates[0] * ys[0] + gates[1] * ys[1] + gates[2] * ys[2]).astype(BF16)
        y = y_ref[...].astype(F32)
        ry = lax.rsqrt(jnp.mean(y * y, axis=-1, keepdims=True) + EPS)
        yn = y * ry
        dout_v = do_ref[...]
        dpg_ref[...] += jnp.sum(dout_v * yn, axis=0, keepdims=True)
        dyn = dout_v * pg_ref[...]
        dy = (ry * (dyn - yn * jnp.mean(dyn * yn, axis=-1, keepdims=True))).astype(BF16)
        dmerged = _nt(dy, wout_ref[...])
        dwout_ref[...] += _tn(merged, dy)
        dws = (dwco_ref, dwao_ref, dwpo_ref)
        das = []
        for b in range(3):
            gb = gates[b]
            dgm_ref[:, b * D:(b + 1) * D] = (dmerged * ys[b] * gb * (1.0 - gb)).astype(BF16)
            dyb = (dmerged * gb).astype(BF16)
            dws[b][...] += _tn(acts[b], dyb)
            das.append(_nt(dyb, ws[b][...]))
        dac_ref[...] = das[0].astype(BF16)
        dap_ref[...] = das[2].astype(BF16)
        dao_ref[...] = (das[1] * ag * sag).astype(BF16)
        dag_ref[...] = (das[1] * ov * _dsilu(ag, sag)).astype(BF16)

    row = lambda w: pl.BlockSpec((tm, w), lambda i: (i, 0))
    full = lambda shape: pl.BlockSpec(shape, lambda i: (0,) * len(shape))
    return _call(
        body, "out_bwd", (T // tm,), _out_specs_in(tm) + [row(D)] * 5,
        [row(BW), row(BW), row(BW), row(BW), row(3 * D),
         full((BW, D)), full((BW, D)), full((BW, D)), full((D, D)), full((1, D))],
        [SDS((T, BW), BF16)] * 4 + [SDS((T, 3 * D), BF16)]
        + [SDS((BW, D), F32)] * 3 + [SDS((D, D), F32), SDS((1, D), F32)], [],
        (ac, o, ap, z, z, z, wco, wao, wpo, wout, postg, dout, *kept), ("arbitrary",), ex)


def _attn_bwd(z, dao, f, S, ex=None):
    T = z.shape[0]
    nq = S // TQ
    nsteps = T // TQ

    def body(q_ref, k2_ref, k1_ref, k0_ref, v2_ref, v1_ref, v0_ref, do_ref, f_ref,
             dq_ref, dk_ref, dv_ref, df_ref, kbuf, vbuf, dkacc, dvacc, b_scr, db_scr):
        i = pl.program_id(0)
        qb = i % nq

        @pl.when(i == 0)
        def _():
            _bias_from_skew(f_ref, b_scr)
            db_scr[...] = jnp.zeros_like(db_scr)

        @pl.when(qb == 0)
        def _():
            dkacc[...] = jnp.zeros_like(dkacc)
            dvacc[...] = jnp.zeros_like(dvacc)

        kbuf[0:TQ, :] = k2_ref[...]
        kbuf[TQ:2 * TQ, :] = k1_ref[...]
        kbuf[2 * TQ:KW, :] = k0_ref[...]
        vbuf[0:TQ, :] = v2_ref[...]
        vbuf[TQ:2 * TQ, :] = v1_ref[...]
        vbuf[2 * TQ:KW, :] = v0_ref[...]
        lane = lax.broadcasted_iota(jnp.int32, (1, 128), 1)
        row0 = pl.multiple_of(qb * TQ, TQ)

        def attend(lo):
            def first_matmuls(h):
                sl = slice((h // 2) * 128, (h // 2 + 1) * 128)
                msk = (lane < HD) if h % 2 == 0 else (lane >= HD)
                qp = q_ref[:, sl] * 0.125
                dop = do_ref[:, sl]
                qm = jnp.where(msk, qp, jnp.zeros_like(qp))
                dom = jnp.where(msk, dop, jnp.zeros_like(dop))
                s = _nt(qm, kbuf[lo:KW, sl]) + b_scr[h, :, lo:KW]
                return s, _nt(dom, vbuf[lo:KW, sl]), qm, dom

            cur = first_matmuls(0)
            dq_acc = dk_acc = dv_acc = None
            for h in range(HEADS):
                nxt = first_matmuls(h + 1) if h + 1 < HEADS else None
                s, dp, qm, dom = cur
                sl = slice((h // 2) * 128, (h // 2 + 1) * 128)
                e = jnp.exp(s - jnp.max(s, axis=-1, keepdims=True))
                p = e * (1.0 / jnp.sum(e, axis=-1, keepdims=True))
                ds = p * (dp - jnp.sum(p * dp, axis=-1, keepdims=True))
                db_scr[h, :, lo:KW] += ds
                dsb = ds.astype(BF16)
                kp = kbuf[lo:KW, sl]
                km = jnp.where((lane < HD) if h % 2 == 0 else (lane >= HD), kp, jnp.zeros_like(kp))
                dq_h = _nn(dsb, km) * 0.125
                dk_h = _tn(dsb, qm)
                dv_h = _tn(p.astype(BF16), dom)
                if h % 2 == 0:
                    dq_acc, dk_acc, dv_acc = dq_h, dk_h, dv_h
                else:
                    dq_ref[:, sl] = (dq_acc + dq_h).astype(BF16)
                    dkacc[pl.ds(row0 + lo, KW - lo), sl] += dk_acc + dk_h
                    dvacc[pl.ds(row0 + lo, KW - lo), sl] += dv_acc + dv_h
                cur = nxt

        for nblk in (1, 2, 3):
            pl.when(jnp.minimum(qb, 2) == nblk - 1)(functools.partial(attend, (3 - nblk) * TQ))

        @pl.when(qb == nq - 1)
        def _():
            dk_ref[...] = dkacc[2 * TQ:2 * TQ + S, :].astype(BF16)
            dv_ref[...] = dvacc[2 * TQ:2 * TQ + S, :].astype(BF16)

        @pl.when(i == nsteps - 1)
        def _():
            for h in range(HEADS):
                df_ref[h:h + 1, :] = _skew_from_bias(db_scr[h])

    full = lambda shape: pl.BlockSpec(shape, lambda i: (0,) * len(shape))
    return _call(
        body, "attn_bwd", (nsteps,),
        _attn_specs(nq) + [pl.BlockSpec((TQ, BW), lambda i: (i, 0)), full((HEADS, NSKEW))],
        [pl.BlockSpec((TQ, BW), lambda i: (i, 0)), pl.BlockSpec((S, BW), lambda i: (i // nq, 0)),
         pl.BlockSpec((S, BW), lambda i: (i // nq, 0)), full((HEADS, NSKEW))],
        [SDS((T, BW), BF16)] * 3 + [SDS((HEADS, NSKEW), F32)],
        [pltpu.VMEM((KW, BW), BF16), pltpu.VMEM((KW, BW), BF16),
         pltpu.VMEM((S + 2 * TQ, BW), F32), pltpu.VMEM((S + 2 * TQ, BW), F32),
         pltpu.VMEM((HEADS, TQ, KW), F32), pltpu.VMEM((HEADS, TQ, KW), F32)],
        (z, z, z, z, z, z, z, dao, f), ("arbitrary",), ex)


def _mix_bwd(z, u1, u0, pooled_kept, dac, dap, dw, dwb, lng, lnb, pw, pb, ps, S, ex=None):
    T = z.shape[0]
    tm = min(S, 1024)
    ts = S // tm
    nh = tm // HALO
    nsteps = T // tm
    nblk32 = T // HALO

    def body(ca_ref, cb_ref, cg_ref, pg_ref, u1_ref, u0_ref, pooled, dac_ref, dap_ref,
             u0h_ref,
             cgn_ref, pgn_ref, u1n_ref, dacn_ref, dapn_ref,
             dw_ref, dwb_ref, lng_ref, lnb_ref, pw_ref, pb_ref, ps_ref,
             dzc_ref, dzp_ref, ddw_ref, ddwb_ref, dlng_ref, dlnb_ref, dpw_ref, dpb_ref, dps_ref,
             ubuf, gbuf, qbuf, *accs):
        tap_acc, (lng_acc, lnb_acc, dwb_acc) = accs[:CONV_K], accs[CONV_K:]
        i = pl.program_id(0)
        keep_prev = jnp.where((i % ts) == 0, 0.0, 1.0)
        keep_next = jnp.where((i % ts) == ts - 1, 0.0, 1.0)
        t0 = (i % ts) * tm

        @pl.when(i == 0)
        def _():
            for a in accs:
                a[...] = jnp.zeros_like(a)
            dpw_ref[...] = jnp.zeros_like(dpw_ref)
            dpb_ref[...] = jnp.zeros_like(dpb_ref)
            dps_ref[...] = jnp.zeros_like(dps_ref)

        ubuf[0:HALO, :] = u0h_ref[...] * keep_prev
        ubuf[HALO:HALO + tm, :] = u0_ref[...]

        def norm_back(u1v, cg, dacv):
            n, rstd = _layer_norm_fwd(u1v)
            u2 = n * lng_ref[...] + lnb_ref[...]
            s2 = _sig(u2)
            scg = _sig(cg)
            du2 = dacv * cg * scg * _dsilu(u2, s2)
            dn = du2 * lng_ref[...]
            du1 = rstd * (dn - jnp.mean(dn, axis=-1, keepdims=True)
                          - n * jnp.mean(dn * n, axis=-1, keepdims=True))
            return du1, du2, n, dacv * u2 * s2 * _dsilu(cg, scg)

        def chunk_a(c, carry):
            base = pl.multiple_of(c * RC, RC)
            du1, du2, n, dcg = norm_back(u1_ref[pl.ds(base, RC), :], cg_ref[pl.ds(base, RC), :].astype(F32),
                                         dac_ref[pl.ds(base, RC), :].astype(F32))
            gbuf[pl.ds(base, RC), :] = du1
            dzc_ref[pl.ds(base, RC), 2 * BW:3 * BW] = dcg.astype(BF16)
            lng_acc[...] += _rows8(du2 * n)
            lnb_acc[...] += _rows8(du2)
            dwb_acc[...] += _rows8(du1)
            padded = jnp.concatenate([du1, jnp.zeros((8, BW), F32)], axis=0)
            for r in range(8):
                nrow = RC if r == 0 else RC + 8
                g = du1 if r == 0 else pltpu.roll(padded, r, axis=0)
                for q in range((CONV_K + 1 - r) // 8 + 1):
                    o = 8 * q + r
                    if o < 2:
                        continue
                    prod = g * ubuf[pl.ds(base + 8 * q, nrow), :]
                    red = prod[0:8]
                    for k in range(1, nrow // 8):
                        red = red + prod[8 * k:8 * k + 8]
                    tap_acc[o - 2][...] += red
            return carry

        lax.fori_loop(0, tm // RC, chunk_a, 0, unroll=4)
        du1n, _, _, _ = norm_back(u1n_ref[...], cgn_ref[...].astype(F32), dacn_ref[...].astype(F32))
        gbuf[tm:tm + HALO, :] = du1n * keep_next

        def cnt_of(t_first, rows, w):
            t = t_first + lax.broadcasted_iota(jnp.int32, (rows, 128), 0)
            return jnp.minimum(t + 1, w).astype(F32)

        pg = pg_ref[...].astype(F32)
        spg_s = _sig(pg)
        dapv = dap_ref[...].astype(F32)
        pgn = pgn_ref[...].astype(F32)
        dmixn = dapn_ref[...].astype(F32) * pgn * _sig(pgn) * ps_ref[...] * keep_next
        for g, w in enumerate(WINDOWS):
            sl = slice(g * 128, (g + 1) * 128)
            mixed_u = _nn(pooled[:, sl], pw_ref[g]) + pb_ref[:, sl]
            dap_g = dapv[:, sl]
            pg_g = pg[:, sl]
            s_g = spg_s[:, sl]
            silu_g = pg_g * s_g
            dps_ref[:, sl] += jnp.sum(dap_g * silu_g * mixed_u, axis=0, keepdims=True)
            dzp_ref[:, BW + g * 128:BW + (g + 1) * 128] = (
                dap_g * mixed_u * ps_ref[:, sl] * _dsilu(pg_g, s_g)).astype(BF16)
            dmix = dap_g * silu_g * ps_ref[:, sl]
            dpb_ref[:, sl] += jnp.sum(dmix, axis=0, keepdims=True)
            dmixb = dmix.astype(BF16)
            dpw_ref[g] += _tn(pooled[:, sl], dmixb)
            qbuf[0:tm, sl] = _nt(dmixb, pw_ref[g]) / cnt_of(t0, tm, w)
            qbuf[tm:tm + HALO, sl] = _nt(dmixn[:, sl].astype(BF16), pw_ref[g]) / cnt_of(t0 + tm, HALO, w)

        def chunk_b(c, carry):
            base = pl.multiple_of(c * RC, RC)
            load = lambda q, n: gbuf[pl.ds(base + 8 * q, n), :]
            du0 = _stencil(load, dw_ref, 0, CONV_K - 1, lambda o: CONV_K - 1 - o)
            ca = ca_ref[pl.ds(base, RC), :].astype(F32)
            sb = _sig(cb_ref[pl.ds(base, RC), :].astype(F32))
            dzc_ref[pl.ds(base, RC), 0:BW] = (du0 * sb).astype(BF16)
            dzc_ref[pl.ds(base, RC), BW:2 * BW] = (du0 * ca * sb * (1.0 - sb)).astype(BF16)
            qwin = qbuf[pl.ds(base, RC + HALO), :]
            t = t0 + base + lax.broadcasted_iota(jnp.int32, (RC, 128), 0)
            for g, w in enumerate(WINDOWS):
                x = qwin[:, g * 128:(g + 1) * 128]
                s = _window_sums(x, w, False)
                cnt = jnp.minimum(t + 1, w).astype(F32)
                dzp_ref[pl.ds(base, RC), g * 128:(g + 1) * 128] = (s[0:RC] - cnt * x[0:RC]).astype(BF16)
            return carry

        lax.fori_loop(0, tm // RC, chunk_b, 0)

        @pl.when(i == nsteps - 1)
        def _():
            dlng_ref[...] = jnp.sum(lng_acc[...], axis=0, keepdims=True)
            dlnb_ref[...] = jnp.sum(lnb_acc[...], axis=0, keepdims=True)
            ddwb_ref[...] = jnp.sum(dwb_acc[...], axis=0, keepdims=True)
            for j in range(CONV_K):
                ddw_ref[j:j + 1, :] = jnp.sum(tap_acc[j][...], axis=0, keepdims=True)

    def zmain(cb):
        return pl.BlockSpec((tm, BW), lambda i: (i, cb))

    def zprev(cb):
        return pl.BlockSpec((HALO, BW), lambda i: (jnp.maximum(i * nh - 1, 0), cb))

    def znext(cb):
        return pl.BlockSpec((HALO, BW), lambda i: (jnp.minimum((i + 1) * nh, nblk32 - 1), cb))

    row = lambda w: pl.BlockSpec((tm, w), lambda i: (i, 0))
    full = lambda shape: pl.BlockSpec(shape, lambda i: (0,) * len(shape))
    return _call(
        body, "mix_bwd", (nsteps,),
        [zmain(CB_CA), zmain(CB_CB), zmain(CB_CG), zmain(CB_PG), row(BW), row(BW), row(BW), row(BW), row(BW),
         zprev(0),
         znext(CB_CG), znext(CB_PG), znext(0), znext(0), znext(0),
         full((CONV_K, BW)), full((1, BW)), full((1, BW)), full((1, BW)),
         full((4, 128, 128)), full((1, BW)), full((1, BW))],
        [row(3 * BW), row(2 * BW), full((CONV_K, BW)), full((1, BW)), full((1, BW)), full((1, BW)),
         full((4, 128, 128)), full((1, BW)), full((1, BW))],
        [SDS((T, 3 * BW), BF16), SDS((T, 2 * BW), BF16), SDS((CONV_K, BW), F32),
         SDS((1, BW), F32), SDS((1, BW), F32), SDS((1, BW), F32),
         SDS((4, 128, 128), F32), SDS((1, BW), F32), SDS((1, BW), F32)],
        [pltpu.VMEM((HALO + tm, BW), F32), pltpu.VMEM((tm + HALO, BW), F32), pltpu.VMEM((tm + HALO, BW), F32)]
        + [pltpu.VMEM((8, BW), F32)] * (CONV_K + 3),
        (z, z, z, z, u1, u0, pooled_kept, dac, dap, u0, z, z, u1, dac, dap, dw, dwb, lng, lnb, pw, pb, ps),
        ("arbitrary",), ex)


def _in_bwd_x(pieces, wt, x, g, dout, ex=None):
    T = x.shape[0]
    tm = 512
    widths = [p.shape[1] for p in pieces]
    offs = np.cumsum([0] + widths)
    npc = len(pieces)

    def body(*refs):
        p_refs = refs[:npc]
        w_ref, x_ref, g_ref, do_ref, dx_ref, dg_ref = refs[npc:]

        @pl.when(pl.program_id(0) == 0)
        def _():
            dg_ref[...] = jnp.zeros_like(dg_ref)

        dh = None
        for k in range(npc):
            t = _nn(p_refs[k][...], w_ref[int(offs[k]):int(offs[k + 1]), :])
            dh = t if dh is None else dh + t
        xv = x_ref[...]
        r = lax.rsqrt(jnp.mean(xv * xv, axis=-1, keepdims=True) + EPS)
        xn = xv * r
        dg_ref[...] += jnp.sum(dh * xn, axis=0, keepdims=True)
        dxn = dh * g_ref[...]
        dx_ref[...] = do_ref[...] + r * (dxn - xn * jnp.mean(dxn * xn, axis=-1, keepdims=True))

    row = lambda wd: pl.BlockSpec((tm, wd), lambda i: (i, 0))
    return _call(
        body, "in_bwd_x", (T // tm,),
        [row(wd) for wd in widths] + [pl.BlockSpec((NCOL, D), lambda i: (0, 0)),
                                      row(D), pl.BlockSpec((1, D), lambda i: (0, 0)), row(D)],
        [row(D), pl.BlockSpec((1, D), lambda i: (0, 0))], [SDS((T, D), F32), SDS((1, D), F32)], [],
        (*pieces, wt, x, g, dout), ("arbitrary",), ex)


def _in_bwd_w(ht, pieces, row0, buf=None, ex=None):
    T = ht.shape[1]
    n = len(pieces)
    wd = pieces[0].shape[1]
    tn = next(t for t in (1536, 1024, 768, 512) if wd % t == 0 and row0 % t == 0)
    per = wd // tn
    tk = min(T, 2048)
    nk = T // tk
    j0 = row0 // tn

    def body(ht_ref, *rest):
        p_refs = rest[:n]
        o_ref, acc = rest[-2:]
        j, k = pl.program_id(0), pl.program_id(1)

        @pl.when(k == 0)
        def _():
            acc[...] = jnp.zeros_like(acc)

        for p in range(n):
            @pl.when(j // per == p)
            def _(p=p):
                acc[...] += _nn(ht_ref[...], p_refs[p][...])

        @pl.when(k == nk - 1)
        def _():
            o_ref[...] = acc[...].T.astype(BF16)

    def piece_spec(p):
        return pl.BlockSpec((tk, tn), lambda j, k: (jnp.where(j // per == p, k, 0), jnp.where(j // per == p, j % per, 0)))

    in_specs = [pl.BlockSpec((D, tk), lambda j, k: (0, k))] + [piece_spec(p) for p in range(n)]
    args = (ht, *pieces)
    if buf is not None:
        in_specs.append(pl.BlockSpec(memory_space=pl.ANY))
        args += (buf,)
    return _call(
        body, "in_bwd_w", (n * per, nk), in_specs,
        [pl.BlockSpec((tn, D), lambda j, k: (j + j0, 0))], [SDS((NCOL, D), BF16)], [pltpu.VMEM((D, tn), F32)],
        args, ("arbitrary", "arbitrary"), ex, None if buf is None else {n + 1: 0})


def _my_id():
    return 4 * lax.axis_index("x") + 2 * lax.axis_index("y") + lax.axis_index("c")


def _peers():
    x, y, c = lax.axis_index("x"), lax.axis_index("y"), lax.axis_index("c")
    out = []
    for k in range(1, N_DEV):
        fx, fy, fc = (k >> 2) & 1, (k >> 1) & 1, k & 1
        px, py, pc = x ^ fx, y ^ fy, c ^ fc
        out.append(((px, py, pc), 4 * px + 2 * py + pc))
    return out


class _Exchange:
    def __init__(self, arrays, scatter, only=None, own_src=None):
        self.scatter = list(scatter)
        self.n = n = len(arrays)
        self.only = list(only) if only else [None] * n
        self.own_src = dict(own_src or {})
        self.arrays = [a for k, a in enumerate(arrays) if k not in self.own_src]
        hbm = pl.BlockSpec(memory_space=pltpu.HBM)
        self.in_specs = [hbm] * len(self.arrays)
        self.out_specs = [hbm] * n
        self.out_shape = [SDS((N_DEV,) + tuple(a.shape[1:] if s else a.shape), a.dtype)
                          for a, s in zip(arrays, scatter)]
        self.scratch = [pltpu.SemaphoreType.DMA((N_DEV - 1, n)), pltpu.SemaphoreType.DMA((N_DEV - 1, n)),
                        pltpu.SemaphoreType.DMA((n,))]

    def split(self, refs, n_in, n_out):
        n, m = self.n, len(self.arrays)
        own_in = refs[:n_in]
        operands = iter(refs[n_in:n_in + m])
        ex_in = [own_in[self.own_src[k]] if k in self.own_src else next(operands) for k in range(n)]
        own_out = refs[n_in + m:n_in + m + n_out]
        ex_out = refs[n_in + m + n_out:n_in + m + n_out + n]
        rest = refs[n_in + m + n_out + n:]
        return own_in, own_out, rest[:-3], (ex_in, ex_out, rest[-3:])

    def _src(self, ex, k, slab):
        ref = ex[0][k]
        if not self.scatter[k]:
            return ref
        if k in self.own_src:
            rows = self.out_shape[k].shape[1]
            return ref.at[pl.ds(slab * rows, rows)]
        return ref.at[slab]

    def _if_part(self, k, dev, fn):
        if self.only[k] is None:
            fn()
        else:
            pl.when(functools.reduce(jnp.logical_or, [dev == s for s in self.only[k]]))(fn)

    def _copy(self, ex, k, p, landing):
        _, out_refs, (send_sems, recv_sems, _) = ex
        pos, pid = _peers()[p]
        return pltpu.make_async_remote_copy(
            src_ref=self._src(ex, k, pid),
            dst_ref=out_refs[k].at[pid if landing else _my_id()],
            send_sem=send_sems.at[p, k], recv_sem=recv_sems.at[p, k],
            device_id=pos, device_id_type=pl.DeviceIdType.MESH)

    def _own(self, ex, k):
        _, out_refs, (_, _, local_sems) = ex
        me = _my_id()
        return pltpu.make_async_copy(self._src(ex, k, me), out_refs[k].at[me], local_sems.at[k])

    def start(self, ex):
        me = _my_id()
        for k in range(self.n):
            self._if_part(k, me, lambda k=k: self._own(ex, k).start())
        for p in range(N_DEV - 1):
            for k in range(self.n):
                self._if_part(k, _peers()[p][1], lambda k=k, p=p: self._copy(ex, k, p, False).start())

    def finish(self, ex):
        me = _my_id()
        for p in range(N_DEV - 1):
            for k in range(self.n):
                self._if_part(k, me, lambda k=k, p=p: self._copy(ex, k, p, True).wait_recv())
        for p in range(N_DEV - 1):
            for k in range(self.n):
                self._if_part(k, _peers()[p][1], lambda k=k, p=p: self._copy(ex, k, p, False).wait_send())
        for k in range(self.n):
            self._if_part(k, me, lambda k=k: self._own(ex, k).wait())


def _exchange(arrays, scatter, name):
    ex = _Exchange(arrays, scatter)

    def body(*refs):
        _, _, _, exr = ex.split(refs, 0, 0)
        ex.start(exr)
        ex.finish(exr)

    return pl.pallas_call(body, name=name, in_specs=ex.in_specs, out_specs=ex.out_specs,
                          out_shape=ex.out_shape, scratch_shapes=ex.scratch)(*ex.arrays)


def _gather_two_level(shard, name):
    def body(x_ref, out_ref, send_sems, recv_sems, local_sem):
        x, y, c = lax.axis_index("x"), lax.axis_index("y"), lax.axis_index("c")
        me, sibling = (x, y, c), (x, y, 1 - c)
        chips = [(1 - x, y), (x, 1 - y), (1 - x, 1 - y)]
        slab = lambda px, py, pc: out_ref.at[4 * px + 2 * py + pc]

        def copy(k, block, to, src=None):
            return pltpu.make_async_remote_copy(
                src_ref=slab(*block) if src is None else src, dst_ref=slab(*block),
                send_sem=send_sems.at[k], recv_sem=recv_sems.at[k],
                device_id=to, device_id_type=pl.DeviceIdType.MESH)

        mine = pltpu.make_async_copy(x_ref, slab(*me), local_sem)
        mine.start()
        first = [copy(0, me, sibling, src=x_ref)] + [copy(1 + j, me, (*chip, c), src=x_ref)
                                                     for j, chip in enumerate(chips)]
        for cp in first:
            cp.start()
        passed = [copy(4 + j, (*chip, c), sibling) for j, chip in enumerate(chips)]
        for j, chip in enumerate(chips):
            copy(1 + j, (*chip, c), me).wait_recv()
            passed[j].start()
        copy(0, sibling, me).wait_recv()
        for j, chip in enumerate(chips):
            copy(4 + j, (*chip, 1 - c), me).wait_recv()
        for cp in first + passed:
            cp.wait_send()
        mine.wait()

    hbm = pl.BlockSpec(memory_space=pltpu.HBM)
    return pl.pallas_call(
        body, name=name, in_specs=[hbm], out_specs=hbm,
        out_shape=SDS((N_DEV,) + shard.shape, shard.dtype),
        scratch_shapes=[pltpu.SemaphoreType.DMA((N_DEV - 1,)), pltpu.SemaphoreType.DMA((N_DEV - 1,)),
                        pltpu.SemaphoreType.DMA],
    )(shard)


def _adamw_update(g, w, m, v):
    c1 = 1.0 / (1.0 - ADAM_B1 ** ADAM_STEP)
    c2 = 1.0 / (1.0 - ADAM_B2 ** ADAM_STEP)
    mn = ADAM_B1 * m + (1.0 - ADAM_B1) * g
    vn = ADAM_B2 * v + (1.0 - ADAM_B2) * (g * g)
    return -ADAM_LR * ((mn * c1) / (jnp.sqrt(vn * c2) + ADAM_EPS) + ADAM_WD * w), mn, vn


def _adamw_small(parts, w, m, v):
    n = len(w)

    def body(*refs):
        p_refs = (refs[0:n], refs[n:2 * n])
        w_refs, m_refs, v_refs = refs[2 * n:3 * n], refs[3 * n:4 * n], refs[4 * n:5 * n]
        outs = refs[5 * n:]
        for k in range(n):
            g_ref, d_ref, mo_ref, vo_ref = outs[4 * k:4 * k + 4]
            for l in range(2):
                at = (slice(l, l + 1),) if len(w_refs[k].shape) == 2 else (l,)
                g = p_refs[l][k][0]
                for s in range(1, N_DEV):
                    g = g + p_refs[l][k][s]
                delta, mn, vn = _adamw_update(g, w_refs[k][at], m_refs[k][at], v_refs[k][at])
                g_ref[at] = g
                d_ref[at] = delta
                mo_ref[at] = mn
                vo_ref[at] = vn

    vmem = pl.BlockSpec(memory_space=pltpu.VMEM)
    res = pl.pallas_call(
        body, name="adamw_replicated", in_specs=[vmem] * (5 * n), out_specs=[vmem] * (4 * n),
        out_shape=[SDS(a.shape, F32) for a in w for _ in range(4)],
        compiler_params=pltpu.CompilerParams(vmem_limit_bytes=VMEM_LIMIT),
    )(*parts[0], *parts[1], *w, *m, *v)
    return [res[4 * k:4 * k + 4] for k in range(n)]


def _adamw_sum(parts0, parts1, w, m, v, name, alt0=None, alt_devices=()):
    _, R, C = w.shape
    tr = R
    while tr * C > 256 * 1024 and tr % 32 == 0:
        tr //= 2

    def body(p0_ref, p1_ref, *rest):
        alt_ref = rest[0] if alt0 is not None else None
        w_ref, m_ref, v_ref, g_ref, d_ref, mo_ref, vo_ref = rest[-7:]

        def update(p_ref):
            g = p_ref[0].astype(F32)
            for s in range(1, N_DEV):
                g = g + p_ref[s].astype(F32)
            g_ref[...] = g
            d_ref[...], mo_ref[...], vo_ref[...] = _adamw_update(g, w_ref[...], m_ref[...], v_ref[...])

        first = pl.program_id(0) == 0
        if alt0 is None:
            pl.when(first)(lambda: update(p0_ref))
        else:
            me = _my_id()
            there = functools.reduce(jnp.logical_or, [me == s for s in alt_devices])
            pl.when(first & there)(lambda: update(alt_ref))
            pl.when(first & jnp.logical_not(there))(lambda: update(p0_ref))

        @pl.when(pl.program_id(0) == 1)
        def _():
            update(p1_ref)

    blk = pl.BlockSpec((None, tr, C), lambda l, i: (l, i, 0))
    first_layer = pl.BlockSpec((N_DEV, tr, C), lambda l, i: (0, i * (1 - l), 0))
    extra = ([first_layer], (alt0,)) if alt0 is not None else ([], ())
    return pl.pallas_call(
        body, name=name, grid=(2, R // tr),
        in_specs=[first_layer, pl.BlockSpec((N_DEV, tr, C), lambda l, i: (0, i * l, 0))] + extra[0] + [blk, blk, blk],
        out_specs=[blk, blk, blk, blk],
        out_shape=[SDS((2, R, C), F32)] * 4,
        compiler_params=_cparams(("arbitrary", "arbitrary")),
    )(parts0, parts1, *extra[1], w, m, v)


def _layer_fwd(x, P, skew, S, rest, ex, tgt=None):
    z, ht, *got0 = _in_proj(x, P["pre_g"], P["w_in_t"], ex[0])
    P = {**P, **rest(got0)}
    ac, ap, u1, u0, pooled, *got1 = _mix_fwd(z, P["conv_dw"], P["conv_dw_b"], P["conv_ln_g"], P["conv_ln_b"],
                                 P["pool_w"], P["pool_b"], P["pool_scale"], S, ex[1])
    o, *got2 = _attn_fwd(z, skew, S, ex[2])
    out, kept, got3 = _out_fwd(x, ac, o, ap, z, P["w_conv_out"], P["w_attn_out"], P["w_pool_out"], P["w_out"],
                               P["post_g"], ex[3], tgt)
    return out, (x, z, ht, ac, o, ap, u1, u0, pooled, kept), P, (got0, got1, got2, got3)


def _layer_bwd(dout, saved, P, skew, S, ex=(None, None), mix_ex=None, win_ex=None, early_ex=None):
    x, z, ht, ac, o, ap, u1, u0, pooled, kept = saved
    (dac, dao, dag, dap, dgm, dwco, dwao, dwpo, dwout, dpostg, *got0) = _out_bwd(
        dout, ac, o, ap, z, kept, P["w_conv_out"], P["w_attn_out"], P["w_pool_out"], P["w_out"], P["post_g"], ex[0])
    grads = dict(post_norm_g=dpostg, w_conv_out=dwco, w_attn_out=dwao, w_pool_out=dwpo, w_out=dwout)
    dq, dk, dv, grads["dskew"], *got1 = _attn_bwd(z, dao, skew, S, ex[1])
    (dzc, dzp, grads["conv_dw"], grads["conv_dw_b"], grads["conv_ln_g"], grads["conv_ln_b"], grads["pool_w"],
     grads["pool_b"], grads["pool_scale"], *got2) = _mix_bwd(
        z, u1, u0, pooled, dac, dap, P["conv_dw"], P["conv_dw_b"], P["conv_ln_g"], P["conv_ln_b"],
        P["pool_w"], P["pool_b"], P["pool_scale"], S, mix_ex(grads) if mix_ex else None)
    pieces = [dzc, dq, dk, dv, dag, dzp, dgm]
    buf, got4 = None, []
    for i, (group, row0) in enumerate((([dgm], 4608), ([dzp], 3584), ([dzc], 0), ([dq, dk, dv, dag], 1536))):
        carried = early_ex(len(group) + 1) if (early_ex and i == 2) else None
        buf, *got = _in_bwd_w(ht, group, row0, buf, carried)
        got4 += got
    grads["w_in_t"] = buf
    dx, grads["pre_norm_g"], *got3 = _in_bwd_x(pieces, P["w_in_t"], x, P["pre_g"], dout,
                                               win_ex(grads) if win_ex else None)
    return dx, grads, (got0, got1, got2, got3, got4)


WEIGHT_NAMES = ("pre_norm_g", "post_norm_g", "w_in", "conv_dw", "conv_dw_b", "conv_ln_g", "conv_ln_b",
                "w_conv_out", "rel_bias", "w_attn_out", "pool_w", "pool_b", "pool_scale", "w_pool_out", "w_out")
SHARDED = ("w_in", "w_conv_out", "w_attn_out", "w_pool_out", "w_out", "conv_dw")
EARLY_SLABS = (5, 6, 7)
OUT_PROJ = ("w_conv_out", "w_attn_out", "w_pool_out", "w_out")
REST = tuple(n for n in WEIGHT_NAMES if n not in ("w_in", "pre_norm_g"))


def _cols_from_slabs(g):
    return g.transpose(1, 0, 2).reshape(g.shape[1], N_DEV * g.shape[2])


def _slabs_from_cols(full):
    r, wd = full.shape
    return full.reshape(r, N_DEV, wd // N_DEV).transpose(1, 0, 2)


def _rest_shards(weights, l):
    return [weights["w_conv_out"][l].astype(BF16), weights["w_attn_out"][l].astype(BF16),
            weights["w_pool_out"][l].astype(BF16), weights["w_out"][l].astype(BF16), weights["conv_dw"][l]]


def _rest_weights(got):
    wco, wao, wpo, wout, cdw = got
    return dict(w_conv_out=_cols_from_slabs(wco), w_attn_out=_cols_from_slabs(wao),
                w_pool_out=_cols_from_slabs(wpo), w_out=wout.reshape(D, D), conv_dw=_cols_from_slabs(cdw))


def _grad_arrays(g, names):
    make = {"w_in": lambda: g["w_in_t"].reshape(N_DEV, NCOL // N_DEV, D),
            "w_conv_out": lambda: _slabs_from_cols(g["w_conv_out"].astype(BF16)),
            "w_attn_out": lambda: _slabs_from_cols(g["w_attn_out"].astype(BF16)),
            "w_pool_out": lambda: _slabs_from_cols(g["w_pool_out"].astype(BF16)),
            "w_out": lambda: g["w_out"].astype(BF16).reshape(N_DEV, D // N_DEV, D),
            "conv_dw": lambda: _slabs_from_cols(g["conv_dw"].astype(BF16)),
            "rel_bias": lambda: jnp.dot(g["dskew"], jnp.asarray(_skew_select().T), precision=lax.Precision.HIGHEST),
            "pool_b": lambda: g["pool_b"].reshape(4, 128)}
    return [make[n]() if n in make else g[n] for n in names]


def _grad_exchange(g, names):
    return _Exchange(_grad_arrays(g, names), [n in SHARDED for n in names])


def kernel(x, pre_norm_g, post_norm_g, w_in, conv_dw, conv_dw_b, conv_ln_g, conv_ln_b, w_conv_out, rel_bias, w_attn_out, pool_w, pool_b, pool_scale, w_pool_out, w_out, loss_target, m_pre_norm_g, m_post_norm_g, m_w_in, m_conv_dw, m_conv_dw_b, m_conv_ln_g, m_conv_ln_b, m_w_conv_out, m_rel_bias, m_w_attn_out, m_pool_w, m_pool_b, m_pool_scale, m_w_pool_out, m_w_out, v_pre_norm_g, v_post_norm_g, v_w_in, v_conv_dw, v_conv_dw_b, v_conv_ln_g, v_conv_ln_b, v_w_conv_out, v_rel_bias, v_w_attn_out, v_pool_w, v_pool_b, v_pool_scale, v_w_pool_out, v_w_out):
    given = dict(locals())
    weights = {n: given[n] for n in WEIGHT_NAMES}
    nb, S, _ = x.shape
    T = nb * S
    L = pre_norm_g.shape[0]
    assert L == 2
    x2 = x.reshape(T, D)
    tgt2 = loss_target.reshape(T, D)
    skews = [_skew_table(rel_bias[l]) for l in range(L)]

    def local_params(l):
        return dict(pre_g=pre_norm_g[l:l + 1], post_g=post_norm_g[l:l + 1], conv_dw_b=conv_dw_b[l:l + 1],
                    conv_ln_g=conv_ln_g[l:l + 1], conv_ln_b=conv_ln_b[l:l + 1], pool_w=pool_w[l].astype(BF16),
                    pool_b=pool_b[l].reshape(1, BW), pool_scale=pool_scale[l:l + 1])

    win0 = w_in[0].T.astype(BF16)
    win1 = w_in[1].T.astype(BF16)
    half = win1.shape[0] // 2
    w_in_t0 = _gather_two_level(win0, "gather_w_in_0")
    gather = lambda arrays: _Exchange(arrays, [False] * len(arrays))
    (h,), saved0, P0, (got_rest0, got_a, got_b, got_rest1) = _layer_fwd(
        x2, {**local_params(0), "w_in_t": w_in_t0.reshape(NCOL, D)}, skews[0], S, _rest_weights,
        (gather(_rest_shards(weights, 0)), gather([win1[:half]]), gather([win1[half:]]),
         gather(_rest_shards(weights, 1))))
    w_in_t1 = jnp.concatenate([got_a[0], got_b[0]], axis=1).reshape(NCOL, D)
    (dout, lsum), saved1, P1, _ = _layer_fwd(h, {**local_params(1), "w_in_t": w_in_t1}, skews[1], S,
                                             lambda _: _rest_weights(got_rest1), (None,) * 4, tgt2)

    dout, g1, _ = _layer_bwd(dout, saved1, P1, skews[1], S)
    g1["loss"] = lsum
    others = REST + ("pre_norm_g",)
    late = ("w_in",) + tuple(n for n in REST if n not in OUT_PROJ)
    slab = SDS((N_DEV, NCOL // N_DEV, D), BF16)
    early = lambda i: _Exchange([slab], [True], [EARLY_SLABS], {0: i})
    rest_slabs = tuple(d for d in range(N_DEV) if d not in EARLY_SLABS)
    late_ex = lambda g: _Exchange(_grad_arrays(g, late), [n in SHARDED for n in late],
                                  [rest_slabs] + [None] * (len(late) - 1))
    dout, g0, (got_others1, got_win1, got_outp0, got_late0, got_early0) = _layer_bwd(
        dout, saved0, P0, skews[0], S, (_grad_exchange(g1, others + ("loss",)), _grad_exchange(g1, ("w_in",))),
        lambda g: _grad_exchange(g, OUT_PROJ), late_ex, early)
    (got_pre0,) = _exchange([g0["pre_norm_g"]], [False], "gather_grad_pre_norm_g_0")
    parts = [{"pre_norm_g": got_pre0, **dict(zip(OUT_PROJ, got_outp0)), **dict(zip(late, got_late0))},
             {"w_in": got_win1[0], **dict(zip(others + ("loss",), got_others1))}]
    loss = jnp.sum(parts[1].pop("loss")[:, 0, 0]) * (0.5 / D)
    grad_x = dout.reshape(x.shape)

    outs = {}
    small = [n for n in WEIGHT_NAMES if n not in SHARDED]
    res = _adamw_small([[parts[l][n] for n in small] for l in range(L)], [weights[n] for n in small],
                       [given["m_" + n] for n in small], [given["v_" + n] for n in small])
    outs.update(zip(small, res))
    for n in SHARDED:
        view = (lambda a: a.transpose(0, 2, 1)) if n == "w_in" else (lambda a: a)
        alt = (got_early0[0], EARLY_SLABS) if n == "w_in" else (None, ())
        res = _adamw_sum(parts[0][n], parts[1][n], view(weights[n]), view(given["m_" + n]), view(given["v_" + n]),
                         "adamw_" + n, *alt)
        outs[n] = [view(a) for a in res]
    return (loss, grad_x, *[outs[n][0] for n in WEIGHT_NAMES], *[outs[n][1] for n in WEIGHT_NAMES],
            *[outs[n][2] for n in WEIGHT_NAMES], *[outs[n][3] for n in WEIGHT_NAMES])
```

```python
import functools

import numpy as np
import jax
import jax.numpy as jnp
from jax import lax
from jax.experimental import pallas as pl
from jax.experimental.pallas import tpu as pltpu

F32 = jnp.float32
BF16 = jnp.bfloat16
SDS = jax.ShapeDtypeStruct

D = 1024
BW = 512
NCOL = 7680
EPS = 1e-6
NEG = -1e30
HEADS = 8
HD = 64
CHUNK = 64
LEFT = 8
MAX_REL = 256
TQ = 256
KW = 768
CONV_K = 31
WINDOWS = (2, 4, 8, 16)
HALO = 32
RC = 32
N_DEV = 8

ADAM_LR = 0.001
ADAM_B1 = 0.9
ADAM_B2 = 0.999
ADAM_EPS = 1e-08
ADAM_WD = 0.01
ADAM_STEP = 10

VMEM_LIMIT = 56 * 1024 * 1024

CB_CA, CB_CB, CB_CG, CB_Q, CB_K, CB_V, CB_AG, CB_PI, CB_PG = range(9)


def _cparams(sem):
    return pltpu.CompilerParams(dimension_semantics=sem, vmem_limit_bytes=VMEM_LIMIT)


def _sig(x):
    return 0.5 * jnp.tanh(0.5 * x) + 0.5


def _dsilu(x, s):
    return s * (1.0 + x * (1.0 - s))


def _nt(a, b):
    return lax.dot_general(a, b, (((1,), (1,)), ((), ())), preferred_element_type=F32)


def _tn(a, b):
    return lax.dot_general(a, b, (((0,), (0,)), ((), ())), preferred_element_type=F32)


def _nn(a, b):
    return jnp.dot(a, b, preferred_element_type=F32)


def _rows8(x):
    return x[0:8] + x[8:16] + x[16:24] + x[24:32]


def _call(body, name, grid, in_specs, out_specs, out_shape, scratch, args, sem, ex=None, aliases=None):
    aliases = aliases or {}
    if ex is None:
        return pl.pallas_call(body, name=name, grid=grid, in_specs=in_specs, out_specs=out_specs,
                              out_shape=out_shape, scratch_shapes=scratch, input_output_aliases=aliases,
                              compiler_params=_cparams(sem))(*args)
    n_in, n_out = len(in_specs), len(out_specs)
    steps = int(np.prod(grid))

    def carrier(*refs):
        own_in, own_out, own_scr, exr = ex.split(refs, n_in, n_out)
        step = pl.program_id(0)
        for axis in range(1, len(grid)):
            step = step * grid[axis] + pl.program_id(axis)

        @pl.when(step == 0)
        def _():
            ex.start(exr)

        body(*own_in, *own_out, *own_scr)

        @pl.when(step == steps - 1)
        def _():
            ex.finish(exr)

    return pl.pallas_call(
        carrier, name=name + "_carrier", grid=grid, in_specs=in_specs + ex.in_specs,
        out_specs=out_specs + ex.out_specs, out_shape=out_shape + ex.out_shape,
        scratch_shapes=scratch + ex.scratch, input_output_aliases=aliases,
        compiler_params=_cparams(("arbitrary",) * len(grid)),
    )(*args, *ex.arrays)


def _in_proj(x, g, wt, ex=None):
    T = x.shape[0]
    tm = 512
    tn = 1536

    def body(x_ref, g_ref, w_ref, z_ref, ht_ref):
        xv = x_ref[...]
        r = lax.rsqrt(jnp.mean(xv * xv, axis=-1, keepdims=True) + EPS)
        h = xv * r * g_ref[...]
        hb = h.astype(BF16)
        ht_ref[...] = h.T.astype(BF16)
        for c in range(NCOL // tn):
            z_ref[:, c * tn:(c + 1) * tn] = _nt(hb, w_ref[c * tn:(c + 1) * tn, :]).astype(BF16)

    return _call(
        body, "in_proj", (T // tm,),
        [pl.BlockSpec((tm, D), lambda i: (i, 0)), pl.BlockSpec((1, D), lambda i: (0, 0)),
         pl.BlockSpec((NCOL, D), lambda i: (0, 0))],
        [pl.BlockSpec((tm, NCOL), lambda i: (i, 0)), pl.BlockSpec((D, tm), lambda i: (0, i))],
        [SDS((T, NCOL), BF16), SDS((D, T), BF16)], [],
        (x, g, wt), ("parallel",), ex)


def _stencil(load, w_ref, lo, hi, tap_of):
    out = None
    for r in range(8):
        n = RC if r == 0 else RC + 8
        v = None
        for q in range((hi - r) // 8 + 1):
            o = 8 * q + r
            if o < lo:
                continue
            j = tap_of(o)
            term = w_ref[j:j + 1, :] * load(q, n)
            v = term if v is None else v + term
        if v is None:
            continue
        if r:
            v = pltpu.roll(v, n - r, axis=0)[0:RC]
        out = v if out is None else out + v
    return out


def _layer_norm_fwd(u1):
    mu = jnp.mean(u1, axis=-1, keepdims=True)
    xc = u1 - mu
    rstd = lax.rsqrt(jnp.mean(xc * xc, axis=-1, keepdims=True) + EPS)
    return xc * rstd, rstd


def _window_sums(x, w, back):
    n = x.shape[0]
    s = x
    k = 1
    while k < w:
        s = s + pltpu.roll(s, k if back else n - k, axis=0)
        k *= 2
    return s


def _pool_chunk(pwin, t_first):
    t = t_first + lax.broadcasted_iota(jnp.int32, (RC, 128), 0)
    outs = []
    for g, w in enumerate(WINDOWS):
        x = pwin[:, g * 128:(g + 1) * 128]
        s = _window_sums(x, w, True)
        cnt = jnp.minimum(t + 1, w).astype(F32)
        outs.append(s[HALO:HALO + RC] / cnt - x[HALO:HALO + RC])
    return outs


def _mix_fwd(z, dw, dwb, lng, lnb, pw, pb, ps, S, ex=None):
    T = z.shape[0]
    tm = min(S, 1024)
    ts = S // tm
    nh = tm // HALO

    def body(ca_ref, cb_ref, cg_ref, pi_ref, pg_ref, cah_ref, cbh_ref, pih_ref,
             dw_ref, dwb_ref, lng_ref, lnb_ref, pw_ref, pb_ref, ps_ref,
             ac_ref, ap_ref, u1_ref, u0_ref, pooled, ubuf, pbuf):
        i = pl.program_id(0)
        keep = jnp.where((i % ts) == 0, 0.0, 1.0)
        ubuf[0:HALO, :] = cah_ref[...].astype(F32) * _sig(cbh_ref[...].astype(F32)) * keep
        u0 = ca_ref[...].astype(F32) * _sig(cb_ref[...].astype(F32))
        ubuf[HALO:HALO + tm, :] = u0
        u0_ref[...] = u0
        pbuf[0:HALO, :] = pih_ref[...].astype(F32) * keep
        pbuf[HALO:HALO + tm, :] = pi_ref[...].astype(F32)
        t0 = (i % ts) * tm

        def chunk(c, carry):
            base = pl.multiple_of(c * RC, RC)
            load = lambda q, n: ubuf[pl.ds(base + 8 * q, n), :]
            u1 = _stencil(load, dw_ref, 2, CONV_K + 1, lambda o: o - 2) + dwb_ref[...]
            u1_ref[pl.ds(base, RC), :] = u1
            n, _ = _layer_norm_fwd(u1)
            u2 = n * lng_ref[...] + lnb_ref[...]
            u3 = u2 * _sig(u2)
            cg = cg_ref[pl.ds(base, RC), :].astype(F32)
            ac_ref[pl.ds(base, RC), :] = (u3 * cg * _sig(cg)).astype(BF16)
            pwin = pbuf[pl.ds(base, RC + HALO), :]
            outs = _pool_chunk(pwin, t0 + base)
            for g in range(4):
                pooled[pl.ds(base, RC), g * 128:(g + 1) * 128] = outs[g].astype(BF16)
            return carry

        lax.fori_loop(0, tm // RC, chunk, 0, unroll=8)
        pg = pg_ref[...].astype(F32)
        spg = pg * _sig(pg)
        for g in range(4):
            sl = slice(g * 128, (g + 1) * 128)
            mixed = (_nn(pooled[:, sl], pw_ref[g]) + pb_ref[:, sl]) * ps_ref[:, sl]
            ap_ref[:, sl] = (mixed * spg[:, sl]).astype(BF16)

    def zmain(cb):
        return pl.BlockSpec((tm, BW), lambda i: (i, cb))

    def zprev(cb):
        return pl.BlockSpec((HALO, BW), lambda i: (jnp.maximum(i * nh - 1, 0), cb))

    full = lambda shape: pl.BlockSpec(shape, lambda i: (0,) * len(shape))
    row = pl.BlockSpec((tm, BW), lambda i: (i, 0))
    return _call(
        body, "mix_fwd", (T // tm,),
        [zmain(CB_CA), zmain(CB_CB), zmain(CB_CG), zmain(CB_PI), zmain(CB_PG),
         zprev(CB_CA), zprev(CB_CB), zprev(CB_PI),
         full((CONV_K, BW)), full((1, BW)), full((1, BW)), full((1, BW)),
         full((4, 128, 128)), full((1, BW)), full((1, BW))],
        [row] * 5, [SDS((T, BW), BF16), SDS((T, BW), BF16), SDS((T, BW), F32), SDS((T, BW), F32), SDS((T, BW), BF16)],
        [pltpu.VMEM((HALO + tm, BW), F32), pltpu.VMEM((HALO + tm, BW), F32)],
        (z, z, z, z, z, z, z, z, dw, dwb, lng, lnb, pw, pb, ps), ("parallel",), ex)


def _attn_specs(nq):
    def kv(cb, off):
        return pl.BlockSpec((TQ, BW), lambda i: (i - jnp.minimum(off, i % nq), cb))
    return [pl.BlockSpec((TQ, BW), lambda i: (i, CB_Q)),
            kv(CB_K, 2), kv(CB_K, 1), kv(CB_K, 0), kv(CB_V, 2), kv(CB_V, 1), kv(CB_V, 0)]


NSKEW = 1024


def _skew_table(table):
    return jnp.dot(table, jnp.asarray(_skew_select()), precision=lax.Precision.HIGHEST)


def _skew_select():
    d = np.arange(TQ + KW - 1)
    idx = np.clip(3 * TQ - 1 - d, -MAX_REL, MAX_REL) + MAX_REL
    sel = np.zeros((2 * MAX_REL + 1, NSKEW), np.float32)
    sel[idx, d] = 1.0
    return sel


def _bias_from_skew(f_ref, bias_scr):
    qi = lax.broadcasted_iota(jnp.int32, (TQ, KW), 0)
    kj = lax.broadcasted_iota(jnp.int32, (TQ, KW), 1)
    lo = (qi // CHUNK) * CHUNK
    band = jnp.where((kj >= lo) & (kj < lo + (LEFT + 1) * CHUNK), 0.0, NEG)
    for h in range(HEADS):
        rows = jnp.broadcast_to(f_ref[h:h + 1, :], (TQ, NSKEW))
        rows = pltpu.roll(rows, NSKEW - (TQ - 1), axis=1, stride=1, stride_axis=0)
        bias_scr[h] = rows[:, 0:KW] + band


def _skew_from_bias(db):
    i = lax.broadcasted_iota(jnp.int32, (TQ, TQ), 0)
    j = lax.broadcasted_iota(jnp.int32, (TQ, TQ), 1)
    flip = jnp.where(i + j == TQ - 1, 1.0, 0.0).astype(BF16)
    hi = db.astype(BF16)
    lo = (db - hi.astype(F32)).astype(BF16)
    rev = _nn(flip, hi) + _nn(flip, lo)
    rev = jnp.concatenate([rev, jnp.zeros((TQ, NSKEW - KW), F32)], axis=1)
    return jnp.sum(pltpu.roll(rev, 0, axis=1, stride=1, stride_axis=0), axis=0, keepdims=True)


def _attn_fwd(z, f, S, ex=None):
    T = z.shape[0]
    nq = S // TQ

    def body(q_ref, k2_ref, k1_ref, k0_ref, v2_ref, v1_ref, v0_ref, f_ref, o_ref, kbuf, vbuf, b_scr):
        @pl.when(pl.program_id(0) == 0)
        def _():
            _bias_from_skew(f_ref, b_scr)

        qb = pl.program_id(0) % nq
        kbuf[0:TQ, :] = k2_ref[...]
        kbuf[TQ:2 * TQ, :] = k1_ref[...]
        kbuf[2 * TQ:KW, :] = k0_ref[...]
        vbuf[0:TQ, :] = v2_ref[...]
        vbuf[TQ:2 * TQ, :] = v1_ref[...]
        vbuf[2 * TQ:KW, :] = v0_ref[...]
        lane = lax.broadcasted_iota(jnp.int32, (1, 128), 1)

        def attend(lo):
            def scores(h):
                sl = slice((h // 2) * 128, (h // 2 + 1) * 128)
                qp = q_ref[:, sl] * 0.125
                qm = jnp.where((lane < HD) if h % 2 == 0 else (lane >= HD), qp, jnp.zeros_like(qp))
                return _nt(qm, kbuf[lo:KW, sl]) + b_scr[h, :, lo:KW]

            s = scores(0)
            acc = None
            for h in range(HEADS):
                s_next = scores(h + 1) if h + 1 < HEADS else None
                sl = slice((h // 2) * 128, (h // 2 + 1) * 128)
                e = jnp.exp(s - jnp.max(s, axis=-1, keepdims=True))
                vp = vbuf[lo:KW, sl]
                vm = jnp.where((lane < HD) if h % 2 == 0 else (lane >= HD), vp, jnp.zeros_like(vp))
                o = _nn(e.astype(BF16), vm) * (1.0 / jnp.sum(e, axis=-1, keepdims=True))
                acc = o if h % 2 == 0 else acc + o
                if h % 2 == 1:
                    o_ref[:, sl] = acc.astype(BF16)
                s = s_next

        for nblk in (1, 2, 3):
            pl.when(jnp.minimum(qb, 2) == nblk - 1)(functools.partial(attend, (3 - nblk) * TQ))

    full = lambda shape: pl.BlockSpec(shape, lambda i: (0,) * len(shape))
    return _call(
        body, "attn_fwd", (T // TQ,),
        _attn_specs(nq) + [full((HEADS, NSKEW))],
        [pl.BlockSpec((TQ, BW), lambda i: (i, 0))], [SDS((T, BW), BF16)],
        [pltpu.VMEM((KW, BW), BF16), pltpu.VMEM((KW, BW), BF16), pltpu.VMEM((HEADS, TQ, KW), F32)],
        (z, z, z, z, z, z, z, f), ("arbitrary",), ex)


def _gates(gl_ref, gh_ref):
    gl = _sig(gl_ref[...].astype(F32))
    gh = _sig(gh_ref[...].astype(F32))
    return (gl[:, 0:D], jnp.concatenate([gl[:, D:1536], gh[:, 0:512]], axis=1), gh[:, 512:1536])


def _out_specs_in(tm):
    row = lambda w: pl.BlockSpec((tm, w), lambda i: (i, 0))
    full = lambda shape: pl.BlockSpec(shape, lambda i: (0,) * len(shape))
    return [row(BW), row(BW), row(BW),
            pl.BlockSpec((tm, BW), lambda i: (i, CB_AG)),
            pl.BlockSpec((tm, 1536), lambda i: (i, 3)),
            pl.BlockSpec((tm, 1536), lambda i: (i, 4)),
            full((BW, D)), full((BW, D)), full((BW, D)), full((D, D)), full((1, D))]


def _out_fwd(x, ac, o, ap, z, wco, wao, wpo, wout, postg, ex=None, tgt=None):
    T = x.shape[0]
    tm = 512
    last = tgt is not None

    def body(ac_ref, o_ref, ap_ref, ag_ref, gl_ref, gh_ref, wco_ref, wao_ref, wpo_ref, wout_ref, pg_ref,
             x_ref, *rest):
        ag = ag_ref[...].astype(F32)
        aat = (o_ref[...].astype(F32) * ag * _sig(ag)).astype(BF16)
        gates = _gates(gl_ref, gh_ref)
        acts = (ac_ref[...], aat, ap_ref[...])
        merged = None
        for b, w_ref in enumerate((wco_ref, wao_ref, wpo_ref)):
            yb = _nn(acts[b], w_ref[...])
            rest[-4 + b][...] = yb.astype(BF16)
            merged = gates[b] * yb if merged is None else merged + gates[b] * yb
        y = _nn(merged.astype(BF16), wout_ref[...])
        rest[-1][...] = y.astype(BF16)
        ry = lax.rsqrt(jnp.mean(y * y, axis=-1, keepdims=True) + EPS)
        out = x_ref[...] + y * ry * pg_ref[...]
        if not last:
            rest[0][...] = out
            return
        t_ref, d_ref, l_ref = rest[:3]

        @pl.when(pl.program_id(0) == 0)
        def _():
            l_ref[...] = jnp.zeros_like(l_ref)
        d = out - t_ref[...]
        d_ref[...] = d * (1.0 / D)
        l_ref[...] += jnp.sum(jnp.sum(d * d, axis=0, keepdims=True), axis=1, keepdims=True)

    row = pl.BlockSpec((tm, D), lambda i: (i, 0))
    kept_specs, kept_shapes = [row] * 4, [SDS((T, D), BF16)] * 4
    if not last:
        res = _call(body, "out_fwd", (T // tm,), _out_specs_in(tm) + [row], [row] + kept_specs,
                    [SDS((T, D), F32)] + kept_shapes, [],
                    (ac, o, ap, z, z, z, wco, wao, wpo, wout, postg, x), ("parallel",), ex)
        return res[:1], res[1:5], res[5:]
    res = _call(body, "out_fwd_loss", (T // tm,), _out_specs_in(tm) + [row, row],
                [row, pl.BlockSpec((1, 128), lambda i: (0, 0))] + kept_specs,
                [SDS((T, D), F32), SDS((1, 128), F32)] + kept_shapes, [],
                (ac, o, ap, z, z, z, wco, wao, wpo, wout, postg, x, tgt), ("arbitrary",), ex)
    return res[:2], res[2:6], res[6:]


def _out_bwd(dout, ac, o, ap, z, kept, wco, wao, wpo, wout, postg, ex=None):
    T = dout.shape[0]
    tm = 256

    def body(ac_ref, o_ref, ap_ref, ag_ref, gl_ref, gh_ref, wco_ref, wao_ref, wpo_ref, wout_ref, pg_ref, do_ref,
             yc_ref, ya_ref, yp_ref, y_ref,
             dac_ref, dao_ref, dag_ref, dap_ref, dgm_ref, dwco_ref, dwao_ref, dwpo_ref, dwout_ref, dpg_ref):
        @pl.when(pl.program_id(0) == 0)
        def _():
            for r in (dwco_ref, dwao_ref, dwpo_ref, dwout_ref, dpg_ref):
                r[...] = jnp.zeros_like(r)

        ag = ag_ref[...].astype(F32)
        sag = _sig(ag)
        ov = o_ref[...].astype(F32)
        acts = (ac_ref[...], (ov * ag * sag).astype(BF16), ap_ref[...])
        ws = (wco_ref, wao_ref, wpo_ref)
        gates = _gates(gl_ref, gh_ref)
        ys = [r[...].astype(F32) for r in (yc_ref, ya_ref, yp_ref)]
        merged = (gates[0] * ys[0] + gates[1] * ys[1] + gates[2] * ys[2]).astype(BF16)
        y = y_ref[...].astype(F32)
        ry = lax.rsqrt(jnp.mean(y * y, axis=-1, keepdims=True) + EPS)
        yn = y * ry
        dout_v = do_ref[...]
        dpg_ref[...] += jnp.sum(dout_v * yn, axis=0, keepdims=True)
        dyn = dout_v * pg_ref[...]
        dy = (ry * (dyn - yn * jnp.mean(dyn * yn, axis=-1, keepdims=True))).astype(BF16)
        dmerged = _nt(dy, wout_ref[...])
        dwout_ref[...] += _tn(merged, dy)
        dws = (dwco_ref, dwao_ref, dwpo_ref)
        das = []
        for b in range(3):
            gb = gates[b]
            dgm_ref[:, b * D:(b + 1) * D] = (dmerged * ys[b] * gb * (1.0 - gb)).astype(BF16)
            dyb = (dmerged * gb).astype(BF16)
            dws[b][...] += _tn(acts[b], dyb)
            das.append(_nt(dyb, ws[b][...]))
        dac_ref[...] = das[0].astype(BF16)
        dap_ref[...] = das[2].astype(BF16)
        dao_ref[...] = (das[1] * ag * sag).astype(BF16)
        dag_ref[...] = (das[1] * ov * _dsilu(ag, sag)).astype(BF16)

    row = lambda w: pl.BlockSpec((tm, w), lambda i: (i, 0))
    full = lambda shape: pl.BlockSpec(shape, lambda i: (0,) * len(shape))
    return _call(
        body, "out_bwd", (T // tm,), _out_specs_in(tm) + [row(D)] * 5,
        [row(BW), row(BW), row(BW), row(BW), row(3 * D),
         full((BW, D)), full((BW, D)), full((BW, D)), full((D, D)), full((1, D))],
        [SDS((T, BW), BF16)] * 4 + [SDS((T, 3 * D), BF16)]
        + [SDS((BW, D), F32)] * 3 + [SDS((D, D), F32), SDS((1, D), F32)], [],
        (ac, o, ap, z, z, z, wco, wao, wpo, wout, postg, dout, *kept), ("arbitrary",), ex)


def _attn_bwd(z, dao, f, S, ex=None):
    T = z.shape[0]
    nq = S // TQ
    nsteps = T // TQ

    def body(q_ref, k2_ref, k1_ref, k0_ref, v2_ref, v1_ref, v0_ref, do_ref, f_ref,
             dq_ref, dk_ref, dv_ref, df_ref, kbuf, vbuf, dkacc, dvacc, b_scr, db_scr):
        i = pl.program_id(0)
        qb = i % nq

        @pl.when(i == 0)
        def _():
            _bias_from_skew(f_ref, b_scr)
            db_scr[...] = jnp.zeros_like(db_scr)

        @pl.when(qb == 0)
        def _():
            dkacc[...] = jnp.zeros_like(dkacc)
            dvacc[...] = jnp.zeros_like(dvacc)

        kbuf[0:TQ, :] = k2_ref[...]
        kbuf[TQ:2 * TQ, :] = k1_ref[...]
        kbuf[2 * TQ:KW, :] = k0_ref[...]
        vbuf[0:TQ, :] = v2_ref[...]
        vbuf[TQ:2 * TQ, :] = v1_ref[...]
        vbuf[2 * TQ:KW, :] = v0_ref[...]
        lane = lax.broadcasted_iota(jnp.int32, (1, 128), 1)
        row0 = pl.multiple_of(qb * TQ, TQ)

        def attend(lo):
            def first_matmuls(h):
                sl = slice((h // 2) * 128, (h // 2 + 1) * 128)
                msk = (lane < HD) if h % 2 == 0 else (lane >= HD)
                qp = q_ref[:, sl] * 0.125
                dop = do_ref[:, sl]
                qm = jnp.where(msk, qp, jnp.zeros_like(qp))
                dom = jnp.where(msk, dop, jnp.zeros_like(dop))
                s = _nt(qm, kbuf[lo:KW, sl]) + b_scr[h, :, lo:KW]
                return s, _nt(dom, vbuf[lo:KW, sl]), qm, dom

            cur = first_matmuls(0)
            dq_acc = dk_acc = dv_acc = None
            for h in range(HEADS):
                nxt = first_matmuls(h + 1) if h + 1 < HEADS else None
                s, dp, qm, dom = cur
                sl = slice((h // 2) * 128, (h // 2 + 1) * 128)
                e = jnp.exp(s - jnp.max(s, axis=-1, keepdims=True))
                p = e * (1.0 / jnp.sum(e, axis=-1, keepdims=True))
                ds = p * (dp - jnp.sum(p * dp, axis=-1, keepdims=True))
                db_scr[h, :, lo:KW] += ds
                dsb = ds.astype(BF16)
                kp = kbuf[lo:KW, sl]
                km = jnp.where((lane < HD) if h % 2 == 0 else (lane >= HD), kp, jnp.zeros_like(kp))
                dq_h = _nn(dsb, km) * 0.125
                dk_h = _tn(dsb, qm)
                dv_h = _tn(p.astype(BF16), dom)
                if h % 2 == 0:
                    dq_acc, dk_acc, dv_acc = dq_h, dk_h, dv_h
                else:
                    dq_ref[:, sl] = (dq_acc + dq_h).astype(BF16)
                    dkacc[pl.ds(row0 + lo, KW - lo), sl] += dk_acc + dk_h
                    dvacc[pl.ds(row0 + lo, KW - lo), sl] += dv_acc + dv_h
                cur = nxt

        for nblk in (1, 2, 3):
            pl.when(jnp.minimum(qb, 2) == nblk - 1)(functools.partial(attend, (3 - nblk) * TQ))

        @pl.when(qb == nq - 1)
        def _():
            dk_ref[...] = dkacc[2 * TQ:2 * TQ + S, :].astype(BF16)
            dv_ref[...] = dvacc[2 * TQ:2 * TQ + S, :].astype(BF16)

        @pl.when(i == nsteps - 1)
        def _():
            for h in range(HEADS):
                df_ref[h:h + 1, :] = _skew_from_bias(db_scr[h])

    full = lambda shape: pl.BlockSpec(shape, lambda i: (0,) * len(shape))
    return _call(
        body, "attn_bwd", (nsteps,),
        _attn_specs(nq) + [pl.BlockSpec((TQ, BW), lambda i: (i, 0)), full((HEADS, NSKEW))],
        [pl.BlockSpec((TQ, BW), lambda i: (i, 0)), pl.BlockSpec((S, BW), lambda i: (i // nq, 0)),
         pl.BlockSpec((S, BW), lambda i: (i // nq, 0)), full((HEADS, NSKEW))],
        [SDS((T, BW), BF16)] * 3 + [SDS((HEADS, NSKEW), F32)],
        [pltpu.VMEM((KW, BW), BF16), pltpu.VMEM((KW, BW), BF16),
         pltpu.VMEM((S + 2 * TQ, BW), F32), pltpu.VMEM((S + 2 * TQ, BW), F32),
         pltpu.VMEM((HEADS, TQ, KW), F32), pltpu.VMEM((HEADS, TQ, KW), F32)],
        (z, z, z, z, z, z, z, dao, f), ("arbitrary",), ex)


def _mix_bwd(z, u1, u0, pooled_kept, dac, dap, dw, dwb, lng, lnb, pw, pb, ps, S, ex=None):
    T = z.shape[0]
    tm = min(S, 1024)
    ts = S // tm
    nh = tm // HALO
    nsteps = T // tm
    nblk32 = T // HALO

    def body(ca_ref, cb_ref, cg_ref, pg_ref, u1_ref, u0_ref, pooled, dac_ref, dap_ref,
             u0h_ref,
             cgn_ref, pgn_ref, u1n_ref, dacn_ref, dapn_ref,
             dw_ref, dwb_ref, lng_ref, lnb_ref, pw_ref, pb_ref, ps_ref,
             dzc_ref, dzp_ref, ddw_ref, ddwb_ref, dlng_ref, dlnb_ref, dpw_ref, dpb_ref, dps_ref,
             ubuf, gbuf, qbuf, *accs):
        tap_acc, (lng_acc, lnb_acc, dwb_acc) = accs[:CONV_K], accs[CONV_K:]
        i = pl.program_id(0)
        keep_prev = jnp.where((i % ts) == 0, 0.0, 1.0)
        keep_next = jnp.where((i % ts) == ts - 1, 0.0, 1.0)
        t0 = (i % ts) * tm

        @pl.when(i == 0)
        def _():
            for a in accs:
                a[...] = jnp.zeros_like(a)
            dpw_ref[...] = jnp.zeros_like(dpw_ref)
            dpb_ref[...] = jnp.zeros_like(dpb_ref)
            dps_ref[...] = jnp.zeros_like(dps_ref)

        ubuf[0:HALO, :] = u0h_ref[...] * keep_prev
        ubuf[HALO:HALO + tm, :] = u0_ref[...]

        def norm_back(u1v, cg, dacv):
            n, rstd = _layer_norm_fwd(u1v)
            u2 = n * lng_ref[...] + lnb_ref[...]
            s2 = _sig(u2)
            scg = _sig(cg)
            du2 = dacv * cg * scg * _dsilu(u2, s2)
            dn = du2 * lng_ref[...]
            du1 = rstd * (dn - jnp.mean(dn, axis=-1, keepdims=True)
                          - n * jnp.mean(dn * n, axis=-1, keepdims=True))
            return du1, du2, n, dacv * u2 * s2 * _dsilu(cg, scg)

        def chunk_a(c, carry):
            base = pl.multiple_of(c * RC, RC)
            du1, du2, n, dcg = norm_back(u1_ref[pl.ds(base, RC), :], cg_ref[pl.ds(base, RC), :].astype(F32),
                                         dac_ref[pl.ds(base, RC), :].astype(F32))
            gbuf[pl.ds(base, RC), :] = du1
            dzc_ref[pl.ds(base, RC), 2 * BW:3 * BW] = dcg.astype(BF16)
            lng_acc[...] += _rows8(du2 * n)
            lnb_acc[...] += _rows8(du2)
            dwb_acc[...] += _rows8(du1)
            padded = jnp.concatenate([du1, jnp.zeros((8, BW), F32)], axis=0)
            for r in range(8):
                nrow = RC if r == 0 else RC + 8
                g = du1 if r == 0 else pltpu.roll(padded, r, axis=0)
                for q in range((CONV_K + 1 - r) // 8 + 1):
                    o = 8 * q + r
                    if o < 2:
                        continue
                    prod = g * ubuf[pl.ds(base + 8 * q, nrow), :]
                    red = prod[0:8]
                    for k in range(1, nrow // 8):
                        red = red + prod[8 * k:8 * k + 8]
                    tap_acc[o - 2][...] += red
            return carry

        lax.fori_loop(0, tm // RC, chunk_a, 0, unroll=8)
        du1n, _, _, _ = norm_back(u1n_ref[...], cgn_ref[...].astype(F32), dacn_ref[...].astype(F32))
        gbuf[tm:tm + HALO, :] = du1n * keep_next

        def cnt_of(t_first, rows, w):
            t = t_first + lax.broadcasted_iota(jnp.int32, (rows, 128), 0)
            return jnp.minimum(t + 1, w).astype(F32)

        pg = pg_ref[...].astype(F32)
        spg_s = _sig(pg)
        dapv = dap_ref[...].astype(F32)
        pgn = pgn_ref[...].astype(F32)
        dmixn = dapn_ref[...].astype(F32) * pgn * _sig(pgn) * ps_ref[...] * keep_next
        for g, w in enumerate(WINDOWS):
            sl = slice(g * 128, (g + 1) * 128)
            mixed_u = _nn(pooled[:, sl], pw_ref[g]) + pb_ref[:, sl]
            dap_g = dapv[:, sl]
            pg_g = pg[:, sl]
            s_g = spg_s[:, sl]
            silu_g = pg_g * s_g
            dps_ref[:, sl] += jnp.sum(dap_g * silu_g * mixed_u, axis=0, keepdims=True)
            dzp_ref[:, BW + g * 128:BW + (g + 1) * 128] = (
                dap_g * mixed_u * ps_ref[:, sl] * _dsilu(pg_g, s_g)).astype(BF16)
            dmix = dap_g * silu_g * ps_ref[:, sl]
            dpb_ref[:, sl] += jnp.sum(dmix, axis=0, keepdims=True)
            dmixb = dmix.astype(BF16)
            dpw_ref[g] += _tn(pooled[:, sl], dmixb)
            qbuf[0:tm, sl] = _nt(dmixb, pw_ref[g]) / cnt_of(t0, tm, w)
            qbuf[tm:tm + HALO, sl] = _nt(dmixn[:, sl].astype(BF16), pw_ref[g]) / cnt_of(t0 + tm, HALO, w)

        def chunk_b(c, carry):
            base = pl.multiple_of(c * RC, RC)
            load = lambda q, n: gbuf[pl.ds(base + 8 * q, n), :]
            du0 = _stencil(load, dw_ref, 0, CONV_K - 1, lambda o: CONV_K - 1 - o)
            ca = ca_ref[pl.ds(base, RC), :].astype(F32)
            sb = _sig(cb_ref[pl.ds(base, RC), :].astype(F32))
            dzc_ref[pl.ds(base, RC), 0:BW] = (du0 * sb).astype(BF16)
            dzc_ref[pl.ds(base, RC), BW:2 * BW] = (du0 * ca * sb * (1.0 - sb)).astype(BF16)
            qwin = qbuf[pl.ds(base, RC + HALO), :]
            t = t0 + base + lax.broadcasted_iota(jnp.int32, (RC, 128), 0)
            for g, w in enumerate(WINDOWS):
                x = qwin[:, g * 128:(g + 1) * 128]
                s = _window_sums(x, w, False)
                cnt = jnp.minimum(t + 1, w).astype(F32)
                dzp_ref[pl.ds(base, RC), g * 128:(g + 1) * 128] = (s[0:RC] - cnt * x[0:RC]).astype(BF16)
            return carry

        lax.fori_loop(0, tm // RC, chunk_b, 0)

        @pl.when(i == nsteps - 1)
        def _():
            dlng_ref[...] = jnp.sum(lng_acc[...], axis=0, keepdims=True)
            dlnb_ref[...] = jnp.sum(lnb_acc[...], axis=0, keepdims=True)
            ddwb_ref[...] = jnp.sum(dwb_acc[...], axis=0, keepdims=True)
            for j in range(CONV_K):
                ddw_ref[j:j + 1, :] = jnp.sum(tap_acc[j][...], axis=0, keepdims=True)

    def zmain(cb):
        return pl.BlockSpec((tm, BW), lambda i: (i, cb))

    def zprev(cb):
        return pl.BlockSpec((HALO, BW), lambda i: (jnp.maximum(i * nh - 1, 0), cb))

    def znext(cb):
        return pl.BlockSpec((HALO, BW), lambda i: (jnp.minimum((i + 1) * nh, nblk32 - 1), cb))

    row = lambda w: pl.BlockSpec((tm, w), lambda i: (i, 0))
    full = lambda shape: pl.BlockSpec(shape, lambda i: (0,) * len(shape))
    return _call(
        body, "mix_bwd", (nsteps,),
        [zmain(CB_CA), zmain(CB_CB), zmain(CB_CG), zmain(CB_PG), row(BW), row(BW), row(BW), row(BW), row(BW),
         zprev(0),
         znext(CB_CG), znext(CB_PG), znext(0), znext(0), znext(0),
         full((CONV_K, BW)), full((1, BW)), full((1, BW)), full((1, BW)),
         full((4, 128, 128)), full((1, BW)), full((1, BW))],
        [row(3 * BW), row(2 * BW), full((CONV_K, BW)), full((1, BW)), full((1, BW)), full((1, BW)),
         full((4, 128, 128)), full((1, BW)), full((1, BW))],
        [SDS((T, 3 * BW), BF16), SDS((T, 2 * BW), BF16), SDS((CONV_K, BW), F32),
         SDS((1, BW), F32), SDS((1, BW), F32), SDS((1, BW), F32),
         SDS((4, 128, 128), F32), SDS((1, BW), F32), SDS((1, BW), F32)],
        [pltpu.VMEM((HALO + tm, BW), F32), pltpu.VMEM((tm + HALO, BW), F32), pltpu.VMEM((tm + HALO, BW), F32)]
        + [pltpu.VMEM((8, BW), F32)] * (CONV_K + 3),
        (z, z, z, z, u1, u0, pooled_kept, dac, dap, u0, z, z, u1, dac, dap, dw, dwb, lng, lnb, pw, pb, ps),
        ("arbitrary",), ex)


def _in_bwd_x(pieces, wt, x, g, dout, ex=None):
    T = x.shape[0]
    tm = 512
    widths = [p.shape[1] for p in pieces]
    offs = np.cumsum([0] + widths)
    npc = len(pieces)

    def body(*refs):
        p_refs = refs[:npc]
        w_ref, x_ref, g_ref, do_ref, dx_ref, dg_ref = refs[npc:]

        @pl.when(pl.program_id(0) == 0)
        def _():
            dg_ref[...] = jnp.zeros_like(dg_ref)

        dh = None
        for k in range(npc):
            t = _nn(p_refs[k][...], w_ref[int(offs[k]):int(offs[k + 1]), :])
            dh = t if dh is None else dh + t
        xv = x_ref[...]
        r = lax.rsqrt(jnp.mean(xv * xv, axis=-1, keepdims=True) + EPS)
        xn = xv * r
        dg_ref[...] += jnp.sum(dh * xn, axis=0, keepdims=True)
        dxn = dh * g_ref[...]
        dx_ref[...] = do_ref[...] + r * (dxn - xn * jnp.mean(dxn * xn, axis=-1, keepdims=True))

    row = lambda wd: pl.BlockSpec((tm, wd), lambda i: (i, 0))
    return _call(
        body, "in_bwd_x", (T // tm,),
        [row(wd) for wd in widths] + [pl.BlockSpec((NCOL, D), lambda i: (0, 0)),
                                      row(D), pl.BlockSpec((1, D), lambda i: (0, 0)), row(D)],
        [row(D), pl.BlockSpec((1, D), lambda i: (0, 0))], [SDS((T, D), F32), SDS((1, D), F32)], [],
        (*pieces, wt, x, g, dout), ("arbitrary",), ex)


def _in_bwd_w(ht, pieces, row0, buf=None, ex=None):
    T = ht.shape[1]
    n = len(pieces)
    wd = pieces[0].shape[1]
    tn = next(t for t in (1536, 1024, 768, 512) if wd % t == 0 and row0 % t == 0)
    per = wd // tn
    tk = min(T, 2048)
    nk = T // tk
    j0 = row0 // tn

    def body(ht_ref, *rest):
        p_refs = rest[:n]
        o_ref, acc = rest[-2:]
        j, k = pl.program_id(0), pl.program_id(1)

        @pl.when(k == 0)
        def _():
            acc[...] = jnp.zeros_like(acc)

        for p in range(n):
            @pl.when(j // per == p)
            def _(p=p):
                acc[...] += _nn(ht_ref[...], p_refs[p][...])

        @pl.when(k == nk - 1)
        def _():
            o_ref[...] = acc[...].T.astype(BF16)

    def piece_spec(p):
        return pl.BlockSpec((tk, tn), lambda j, k: (jnp.where(j // per == p, k, 0), jnp.where(j // per == p, j % per, 0)))

    in_specs = [pl.BlockSpec((D, tk), lambda j, k: (0, k))] + [piece_spec(p) for p in range(n)]
    args = (ht, *pieces)
    if buf is not None:
        in_specs.append(pl.BlockSpec(memory_space=pl.ANY))
        args += (buf,)
    return _call(
        body, "in_bwd_w", (n * per, nk), in_specs,
        [pl.BlockSpec((tn, D), lambda j, k: (j + j0, 0))], [SDS((NCOL, D), BF16)], [pltpu.VMEM((D, tn), F32)],
        args, ("arbitrary", "arbitrary"), ex, None if buf is None else {n + 1: 0})


def _my_id():
    return 4 * lax.axis_index("x") + 2 * lax.axis_index("y") + lax.axis_index("c")


def _peers():
    x, y, c = lax.axis_index("x"), lax.axis_index("y"), lax.axis_index("c")
    out = []
    for k in range(1, N_DEV):
        fx, fy, fc = (k >> 2) & 1, (k >> 1) & 1, k & 1
        px, py, pc = x ^ fx, y ^ fy, c ^ fc
        out.append(((px, py, pc), 4 * px + 2 * py + pc))
    return out


class _Exchange:
    def __init__(self, arrays, scatter):
        self.arrays = list(arrays)
        self.scatter = list(scatter)
        self.n = n = len(arrays)
        hbm = pl.BlockSpec(memory_space=pltpu.HBM)
        self.in_specs = [hbm] * n
        self.out_specs = [hbm] * n
        self.out_shape = [SDS((N_DEV,) + tuple(a.shape[1:] if s else a.shape), a.dtype)
                          for a, s in zip(arrays, scatter)]
        self.scratch = [pltpu.SemaphoreType.DMA((N_DEV - 1, n)), pltpu.SemaphoreType.DMA((N_DEV - 1, n)),
                        pltpu.SemaphoreType.DMA((n,))]

    def split(self, refs, n_in, n_out):
        n = self.n
        own_in = refs[:n_in]
        ex_in = refs[n_in:n_in + n]
        own_out = refs[n_in + n:n_in + n + n_out]
        ex_out = refs[n_in + n + n_out:n_in + 2 * n + n_out]
        rest = refs[n_in + 2 * n + n_out:]
        return own_in, own_out, rest[:-3], (ex_in, ex_out, rest[-3:])

    def _copy(self, ex, k, p, landing):
        in_refs, out_refs, (send_sems, recv_sems, _) = ex
        pos, pid = _peers()[p]
        return pltpu.make_async_remote_copy(
            src_ref=in_refs[k].at[pid] if self.scatter[k] else in_refs[k],
            dst_ref=out_refs[k].at[pid if landing else _my_id()],
            send_sem=send_sems.at[p, k], recv_sem=recv_sems.at[p, k],
            device_id=pos, device_id_type=pl.DeviceIdType.MESH)

    def _own(self, ex, k):
        in_refs, out_refs, (_, _, local_sems) = ex
        me = _my_id()
        return pltpu.make_async_copy(in_refs[k].at[me] if self.scatter[k] else in_refs[k], out_refs[k].at[me],
                                     local_sems.at[k])

    def start(self, ex):
        for k in range(self.n):
            self._own(ex, k).start()
        for p in range(N_DEV - 1):
            for k in range(self.n):
                self._copy(ex, k, p, False).start()

    def finish(self, ex):
        for p in range(N_DEV - 1):
            for k in range(self.n):
                self._copy(ex, k, p, True).wait_recv()
        for p in range(N_DEV - 1):
            for k in range(self.n):
                self._copy(ex, k, p, False).wait_send()
        for k in range(self.n):
            self._own(ex, k).wait()


def _exchange(arrays, scatter, name):
    ex = _Exchange(arrays, scatter)

    def body(*refs):
        _, _, _, exr = ex.split(refs, 0, 0)
        ex.start(exr)
        ex.finish(exr)

    return pl.pallas_call(body, name=name, in_specs=ex.in_specs, out_specs=ex.out_specs,
                          out_shape=ex.out_shape, scratch_shapes=ex.scratch)(*ex.arrays)


def _gather_two_level(shard, name):
    def body(x_ref, out_ref, send_sems, recv_sems, local_sem):
        x, y, c = lax.axis_index("x"), lax.axis_index("y"), lax.axis_index("c")
        me, sibling = (x, y, c), (x, y, 1 - c)
        chips = [(1 - x, y), (x, 1 - y), (1 - x, 1 - y)]
        slab = lambda px, py, pc: out_ref.at[4 * px + 2 * py + pc]

        def copy(k, block, to, src=None):
            return pltpu.make_async_remote_copy(
                src_ref=slab(*block) if src is None else src, dst_ref=slab(*block),
                send_sem=send_sems.at[k], recv_sem=recv_sems.at[k],
                device_id=to, device_id_type=pl.DeviceIdType.MESH)

        mine = pltpu.make_async_copy(x_ref, slab(*me), local_sem)
        mine.start()
        first = [copy(0, me, sibling, src=x_ref)] + [copy(1 + j, me, (*chip, c), src=x_ref)
                                                     for j, chip in enumerate(chips)]
        for cp in first:
            cp.start()
        passed = [copy(4 + j, (*chip, c), sibling) for j, chip in enumerate(chips)]
        for j, chip in enumerate(chips):
            copy(1 + j, (*chip, c), me).wait_recv()
            passed[j].start()
        copy(0, sibling, me).wait_recv()
        for j, chip in enumerate(chips):
            copy(4 + j, (*chip, 1 - c), me).wait_recv()
        for cp in first + passed:
            cp.wait_send()
        mine.wait()

    hbm = pl.BlockSpec(memory_space=pltpu.HBM)
    return pl.pallas_call(
        body, name=name, in_specs=[hbm], out_specs=hbm,
        out_shape=SDS((N_DEV,) + shard.shape, shard.dtype),
        scratch_shapes=[pltpu.SemaphoreType.DMA((N_DEV - 1,)), pltpu.SemaphoreType.DMA((N_DEV - 1,)),
                        pltpu.SemaphoreType.DMA],
    )(shard)


def _adamw_update(g, w, m, v):
    c1 = 1.0 / (1.0 - ADAM_B1 ** ADAM_STEP)
    c2 = 1.0 / (1.0 - ADAM_B2 ** ADAM_STEP)
    mn = ADAM_B1 * m + (1.0 - ADAM_B1) * g
    vn = ADAM_B2 * v + (1.0 - ADAM_B2) * (g * g)
    return -ADAM_LR * ((mn * c1) / (jnp.sqrt(vn * c2) + ADAM_EPS) + ADAM_WD * w), mn, vn


def _adamw_small(parts, w, m, v):
    n = len(w)

    def body(*refs):
        p_refs = (refs[0:n], refs[n:2 * n])
        w_refs, m_refs, v_refs = refs[2 * n:3 * n], refs[3 * n:4 * n], refs[4 * n:5 * n]
        outs = refs[5 * n:]
        for k in range(n):
            g_ref, d_ref, mo_ref, vo_ref = outs[4 * k:4 * k + 4]
            for l in range(2):
                at = (slice(l, l + 1),) if len(w_refs[k].shape) == 2 else (l,)
                g = p_refs[l][k][0]
                for s in range(1, N_DEV):
                    g = g + p_refs[l][k][s]
                delta, mn, vn = _adamw_update(g, w_refs[k][at], m_refs[k][at], v_refs[k][at])
                g_ref[at] = g
                d_ref[at] = delta
                mo_ref[at] = mn
                vo_ref[at] = vn

    vmem = pl.BlockSpec(memory_space=pltpu.VMEM)
    res = pl.pallas_call(
        body, name="adamw_replicated", in_specs=[vmem] * (5 * n), out_specs=[vmem] * (4 * n),
        out_shape=[SDS(a.shape, F32) for a in w for _ in range(4)],
        compiler_params=pltpu.CompilerParams(vmem_limit_bytes=VMEM_LIMIT),
    )(*parts[0], *parts[1], *w, *m, *v)
    return [res[4 * k:4 * k + 4] for k in range(n)]


def _adamw_sum(parts0, parts1, w, m, v, name):
    _, R, C = w.shape
    tr = R
    while tr * C > 256 * 1024 and tr % 32 == 0:
        tr //= 2

    def body(p0_ref, p1_ref, w_ref, m_ref, v_ref, g_ref, d_ref, mo_ref, vo_ref):
        def update(p_ref):
            g = p_ref[0].astype(F32)
            for s in range(1, N_DEV):
                g = g + p_ref[s].astype(F32)
            g_ref[...] = g
            d_ref[...], mo_ref[...], vo_ref[...] = _adamw_update(g, w_ref[...], m_ref[...], v_ref[...])

        @pl.when(pl.program_id(0) == 0)
        def _():
            update(p0_ref)

        @pl.when(pl.program_id(0) == 1)
        def _():
            update(p1_ref)

    blk = pl.BlockSpec((None, tr, C), lambda l, i: (l, i, 0))
    return pl.pallas_call(
        body, name=name, grid=(2, R // tr),
        in_specs=[pl.BlockSpec((N_DEV, tr, C), lambda l, i: (0, i * (1 - l), 0)),
                  pl.BlockSpec((N_DEV, tr, C), lambda l, i: (0, i * l, 0)), blk, blk, blk],
        out_specs=[blk, blk, blk, blk],
        out_shape=[SDS((2, R, C), F32)] * 4,
        compiler_params=_cparams(("arbitrary", "arbitrary")),
    )(parts0, parts1, w, m, v)


def _layer_fwd(x, P, skew, S, rest, ex, tgt=None):
    z, ht, *got0 = _in_proj(x, P["pre_g"], P["w_in_t"], ex[0])
    P = {**P, **rest(got0)}
    ac, ap, u1, u0, pooled, *got1 = _mix_fwd(z, P["conv_dw"], P["conv_dw_b"], P["conv_ln_g"], P["conv_ln_b"],
                                 P["pool_w"], P["pool_b"], P["pool_scale"], S, ex[1])
    o, *got2 = _attn_fwd(z, skew, S, ex[2])
    out, kept, got3 = _out_fwd(x, ac, o, ap, z, P["w_conv_out"], P["w_attn_out"], P["w_pool_out"], P["w_out"],
                               P["post_g"], ex[3], tgt)
    return out, (x, z, ht, ac, o, ap, u1, u0, pooled, kept), P, (got0, got1, got2, got3)


def _layer_bwd(dout, saved, P, skew, S, ex=(None, None), mix_ex=None, win_ex=None):
    x, z, ht, ac, o, ap, u1, u0, pooled, kept = saved
    (dac, dao, dag, dap, dgm, dwco, dwao, dwpo, dwout, dpostg, *got0) = _out_bwd(
        dout, ac, o, ap, z, kept, P["w_conv_out"], P["w_attn_out"], P["w_pool_out"], P["w_out"], P["post_g"], ex[0])
    grads = dict(post_norm_g=dpostg, w_conv_out=dwco, w_attn_out=dwao, w_pool_out=dwpo, w_out=dwout)
    dq, dk, dv, grads["dskew"], *got1 = _attn_bwd(z, dao, skew, S, ex[1])
    (dzc, dzp, grads["conv_dw"], grads["conv_dw_b"], grads["conv_ln_g"], grads["conv_ln_b"], grads["pool_w"],
     grads["pool_b"], grads["pool_scale"], *got2) = _mix_bwd(
        z, u1, u0, pooled, dac, dap, P["conv_dw"], P["conv_dw_b"], P["conv_ln_g"], P["conv_ln_b"],
        P["pool_w"], P["pool_b"], P["pool_scale"], S, mix_ex(grads) if mix_ex else None)
    pieces = [dzc, dq, dk, dv, dag, dzp, dgm]
    buf, row0 = None, 0
    for group in ([dzc], [dq, dk, dv, dag], [dzp], [dgm]):
        (buf,) = _in_bwd_w(ht, group, row0, buf)
        row0 += sum(p.shape[1] for p in group)
    grads["w_in_t"] = buf
    dx, grads["pre_norm_g"], *got3 = _in_bwd_x(pieces, P["w_in_t"], x, P["pre_g"], dout,
                                               win_ex(grads) if win_ex else None)
    return dx, grads, (got0, got1, got2, got3)


WEIGHT_NAMES = ("pre_norm_g", "post_norm_g", "w_in", "conv_dw", "conv_dw_b", "conv_ln_g", "conv_ln_b",
                "w_conv_out", "rel_bias", "w_attn_out", "pool_w", "pool_b", "pool_scale", "w_pool_out", "w_out")
SHARDED = ("w_in", "w_conv_out", "w_attn_out", "w_pool_out", "w_out", "conv_dw")
OUT_PROJ = ("w_conv_out", "w_attn_out", "w_pool_out", "w_out")
REST = tuple(n for n in WEIGHT_NAMES if n not in ("w_in", "pre_norm_g"))


def _cols_from_slabs(g):
    return g.transpose(1, 0, 2).reshape(g.shape[1], N_DEV * g.shape[2])


def _slabs_from_cols(full):
    r, wd = full.shape
    return full.reshape(r, N_DEV, wd // N_DEV).transpose(1, 0, 2)


def _rest_shards(weights, l):
    return [weights["w_conv_out"][l].astype(BF16), weights["w_attn_out"][l].astype(BF16),
            weights["w_pool_out"][l].astype(BF16), weights["w_out"][l].astype(BF16), weights["conv_dw"][l]]


def _rest_weights(got):
    wco, wao, wpo, wout, cdw = got
    return dict(w_conv_out=_cols_from_slabs(wco), w_attn_out=_cols_from_slabs(wao),
                w_pool_out=_cols_from_slabs(wpo), w_out=wout.reshape(D, D), conv_dw=_cols_from_slabs(cdw))


def _grad_arrays(g, names):
    make = {"w_in": lambda: g["w_in_t"].reshape(N_DEV, NCOL // N_DEV, D),
            "w_conv_out": lambda: _slabs_from_cols(g["w_conv_out"].astype(BF16)),
            "w_attn_out": lambda: _slabs_from_cols(g["w_attn_out"].astype(BF16)),
            "w_pool_out": lambda: _slabs_from_cols(g["w_pool_out"].astype(BF16)),
            "w_out": lambda: g["w_out"].astype(BF16).reshape(N_DEV, D // N_DEV, D),
            "conv_dw": lambda: _slabs_from_cols(g["conv_dw"].astype(BF16)),
            "rel_bias": lambda: jnp.dot(g["dskew"], jnp.asarray(_skew_select().T), precision=lax.Precision.HIGHEST),
            "pool_b": lambda: g["pool_b"].reshape(4, 128)}
    return [make[n]() if n in make else g[n] for n in names]


def _grad_exchange(g, names):
    return _Exchange(_grad_arrays(g, names), [n in SHARDED for n in names])


def kernel(x, pre_norm_g, post_norm_g, w_in, conv_dw, conv_dw_b, conv_ln_g, conv_ln_b, w_conv_out, rel_bias, w_attn_out, pool_w, pool_b, pool_scale, w_pool_out, w_out, loss_target, m_pre_norm_g, m_post_norm_g, m_w_in, m_conv_dw, m_conv_dw_b, m_conv_ln_g, m_conv_ln_b, m_w_conv_out, m_rel_bias, m_w_attn_out, m_pool_w, m_pool_b, m_pool_scale, m_w_pool_out, m_w_out, v_pre_norm_g, v_post_norm_g, v_w_in, v_conv_dw, v_conv_dw_b, v_conv_ln_g, v_conv_ln_b, v_w_conv_out, v_rel_bias, v_w_attn_out, v_pool_w, v_pool_b, v_pool_scale, v_w_pool_out, v_w_out):
    given = dict(locals())
    weights = {n: given[n] for n in WEIGHT_NAMES}
    nb, S, _ = x.shape
    T = nb * S
    L = pre_norm_g.shape[0]
    assert L == 2
    x2 = x.reshape(T, D)
    tgt2 = loss_target.reshape(T, D)
    skews = [_skew_table(rel_bias[l]) for l in range(L)]

    def local_params(l):
        return dict(pre_g=pre_norm_g[l:l + 1], post_g=post_norm_g[l:l + 1], conv_dw_b=conv_dw_b[l:l + 1],
                    conv_ln_g=conv_ln_g[l:l + 1], conv_ln_b=conv_ln_b[l:l + 1], pool_w=pool_w[l].astype(BF16),
                    pool_b=pool_b[l].reshape(1, BW), pool_scale=pool_scale[l:l + 1])

    win0 = w_in[0].T.astype(BF16)
    win1 = w_in[1].T.astype(BF16)
    half = win1.shape[0] // 2
    w_in_t0 = _gather_two_level(win0, "gather_w_in_0")
    gather = lambda arrays: _Exchange(arrays, [False] * len(arrays))
    (h,), saved0, P0, (got_rest0, got_a, got_b, got_rest1) = _layer_fwd(
        x2, {**local_params(0), "w_in_t": w_in_t0.reshape(NCOL, D)}, skews[0], S, _rest_weights,
        (gather(_rest_shards(weights, 0)), gather([win1[:half]]), gather([win1[half:]]),
         gather(_rest_shards(weights, 1))))
    w_in_t1 = jnp.concatenate([got_a[0], got_b[0]], axis=1).reshape(NCOL, D)
    (dout, lsum), saved1, P1, _ = _layer_fwd(h, {**local_params(1), "w_in_t": w_in_t1}, skews[1], S,
                                             lambda _: _rest_weights(got_rest1), (None,) * 4, tgt2)

    dout, g1, _ = _layer_bwd(dout, saved1, P1, skews[1], S)
    g1["loss"] = lsum
    others = REST + ("pre_norm_g",)
    late = ("w_in",) + tuple(n for n in REST if n not in OUT_PROJ)
    dout, g0, (got_others1, got_win1, got_outp0, got_late0) = _layer_bwd(
        dout, saved0, P0, skews[0], S, (_grad_exchange(g1, others + ("loss",)), _grad_exchange(g1, ("w_in",))),
        lambda g: _grad_exchange(g, OUT_PROJ), lambda g: _grad_exchange(g, late))
    (got_pre0,) = _exchange([g0["pre_norm_g"]], [False], "gather_grad_pre_norm_g_0")
    parts = [{"pre_norm_g": got_pre0, **dict(zip(OUT_PROJ, got_outp0)), **dict(zip(late, got_late0))},
             {"w_in": got_win1[0], **dict(zip(others + ("loss",), got_others1))}]
    loss = jnp.sum(parts[1].pop("loss")[:, 0, 0]) * (0.5 / D)
    grad_x = dout.reshape(x.shape)

    outs = {}
    small = [n for n in WEIGHT_NAMES if n not in SHARDED]
    res = _adamw_small([[parts[l][n] for n in small] for l in range(L)], [weights[n] for n in small],
                       [given["m_" + n] for n in small], [given["v_" + n] for n in small])
    outs.update(zip(small, res))
    for n in SHARDED:
        view = (lambda a: a.transpose(0, 2, 1)) if n == "w_in" else (lambda a: a)
        res = _adamw_sum(parts[0][n], parts[1][n], view(weights[n]), view(given["m_" + n]), view(given["v_" + n]),
                         "adamw_" + n)
        outs[n] = [view(a) for a in res]
    return (loss, grad_x, *[outs[n][0] for n in WEIGHT_NAMES], *[outs[n][1] for n in WEIGHT_NAMES],
            *[outs[n][2] for n in WEIGHT_NAMES], *[outs[n][3] for n in WEIGHT_NAMES])
```

```python
import functools

import numpy as np
import jax
import jax.numpy as jnp
from jax import lax
from jax.experimental import pallas as pl
from jax.experimental.pallas import tpu as pltpu

F32 = jnp.float32
BF16 = jnp.bfloat16
SDS = jax.ShapeDtypeStruct

D = 1024
BW = 512
NCOL = 7680
EPS = 1e-6
NEG = -1e30
HEADS = 8
HD = 64
CHUNK = 64
LEFT = 8
MAX_REL = 256
TQ = 256
KW = 768
CONV_K = 31
WINDOWS = (2, 4, 8, 16)
HALO = 32
RC = 32
N_DEV = 8

ADAM_LR = 0.001
ADAM_B1 = 0.9
ADAM_B2 = 0.999
ADAM_EPS = 1e-08
ADAM_WD = 0.01
ADAM_STEP = 10

VMEM_LIMIT = 62 * 1024 * 1024

CB_CA, CB_CB, CB_CG, CB_Q, CB_K, CB_V, CB_AG, CB_PI, CB_PG = range(9)


def _cparams(sem):
    return pltpu.CompilerParams(dimension_semantics=sem, vmem_limit_bytes=VMEM_LIMIT)


def _sig(x):
    return 0.5 * jnp.tanh(0.5 * x) + 0.5


def _dsilu(x, s):
    return s * (1.0 + x * (1.0 - s))


def _nt(a, b):
    return lax.dot_general(a, b, (((1,), (1,)), ((), ())), preferred_element_type=F32)


def _tn(a, b):
    return lax.dot_general(a, b, (((0,), (0,)), ((), ())), preferred_element_type=F32)


def _nn(a, b):
    return jnp.dot(a, b, preferred_element_type=F32)


def _rows8(x):
    return x[0:8] + x[8:16] + x[16:24] + x[24:32]


def _call(body, name, grid, in_specs, out_specs, out_shape, scratch, args, sem, ex=None, aliases=None):
    aliases = aliases or {}
    if ex is None:
        return pl.pallas_call(body, name=name, grid=grid, in_specs=in_specs, out_specs=out_specs,
                              out_shape=out_shape, scratch_shapes=scratch, input_output_aliases=aliases,
                              compiler_params=_cparams(sem))(*args)
    n_in, n_out = len(in_specs), len(out_specs)
    steps = int(np.prod(grid))

    def carrier(*refs):
        own_in, own_out, own_scr, exr = ex.split(refs, n_in, n_out)
        step = pl.program_id(0)
        for axis in range(1, len(grid)):
            step = step * grid[axis] + pl.program_id(axis)

        @pl.when(step == 0)
        def _():
            ex.start(exr)

        body(*own_in, *own_out, *own_scr)

        @pl.when(step == steps - 1)
        def _():
            ex.finish(exr)

    return pl.pallas_call(
        carrier, name=name + "_carrier", grid=grid, in_specs=in_specs + ex.in_specs,
        out_specs=out_specs + ex.out_specs, out_shape=out_shape + ex.out_shape,
        scratch_shapes=scratch + ex.scratch, input_output_aliases=aliases,
        compiler_params=_cparams(("arbitrary",) * len(grid)),
    )(*args, *ex.arrays)


def _in_proj(x, g, wt, ex=None):
    T = x.shape[0]
    tm = 512
    tn = 1536

    def body(x_ref, g_ref, w_ref, z_ref, ht_ref):
        xv = x_ref[...]
        r = lax.rsqrt(jnp.mean(xv * xv, axis=-1, keepdims=True) + EPS)
        h = xv * r * g_ref[...]
        hb = h.astype(BF16)
        ht_ref[...] = h.T.astype(BF16)
        for c in range(NCOL // tn):
            z_ref[:, c * tn:(c + 1) * tn] = _nt(hb, w_ref[c * tn:(c + 1) * tn, :]).astype(BF16)

    return _call(
        body, "in_proj", (T // tm,),
        [pl.BlockSpec((tm, D), lambda i: (i, 0)), pl.BlockSpec((1, D), lambda i: (0, 0)),
         pl.BlockSpec((NCOL, D), lambda i: (0, 0))],
        [pl.BlockSpec((tm, NCOL), lambda i: (i, 0)), pl.BlockSpec((D, tm), lambda i: (0, i))],
        [SDS((T, NCOL), BF16), SDS((D, T), BF16)], [],
        (x, g, wt), ("parallel",), ex)


def _stencil(load, w_ref, lo, hi, tap_of):
    out = None
    for r in range(8):
        n = RC if r == 0 else RC + 8
        v = None
        for q in range((hi - r) // 8 + 1):
            o = 8 * q + r
            if o < lo:
                continue
            j = tap_of(o)
            term = w_ref[j:j + 1, :] * load(q, n)
            v = term if v is None else v + term
        if v is None:
            continue
        if r:
            v = pltpu.roll(v, n - r, axis=0)[0:RC]
        out = v if out is None else out + v
    return out


def _layer_norm_fwd(u1):
    mu = jnp.mean(u1, axis=-1, keepdims=True)
    xc = u1 - mu
    rstd = lax.rsqrt(jnp.mean(xc * xc, axis=-1, keepdims=True) + EPS)
    return xc * rstd, rstd


def _window_sums(x, w, back):
    n = x.shape[0]
    s = x
    k = 1
    while k < w:
        s = s + pltpu.roll(s, k if back else n - k, axis=0)
        k *= 2
    return s


def _pool_chunk(pwin, t_first):
    t = t_first + lax.broadcasted_iota(jnp.int32, (RC, 128), 0)
    outs = []
    for g, w in enumerate(WINDOWS):
        x = pwin[:, g * 128:(g + 1) * 128]
        s = _window_sums(x, w, True)
        cnt = jnp.minimum(t + 1, w).astype(F32)
        outs.append(s[HALO:HALO + RC] / cnt - x[HALO:HALO + RC])
    return outs


def _mix_fwd(z, dw, dwb, lng, lnb, pw, pb, ps, S, ex=None):
    T = z.shape[0]
    tm = min(S, 1024)
    ts = S // tm
    nh = tm // HALO

    def body(ca_ref, cb_ref, cg_ref, pi_ref, pg_ref, cah_ref, cbh_ref, pih_ref,
             dw_ref, dwb_ref, lng_ref, lnb_ref, pw_ref, pb_ref, ps_ref,
             ac_ref, ap_ref, u1_ref, u0_ref, pooled, ubuf, pbuf):
        i = pl.program_id(0)
        keep = jnp.where((i % ts) == 0, 0.0, 1.0)
        ubuf[0:HALO, :] = cah_ref[...].astype(F32) * _sig(cbh_ref[...].astype(F32)) * keep
        u0 = ca_ref[...].astype(F32) * _sig(cb_ref[...].astype(F32))
        ubuf[HALO:HALO + tm, :] = u0
        u0_ref[...] = u0
        pbuf[0:HALO, :] = pih_ref[...].astype(F32) * keep
        pbuf[HALO:HALO + tm, :] = pi_ref[...].astype(F32)
        t0 = (i % ts) * tm

        def chunk(c, carry):
            base = pl.multiple_of(c * RC, RC)
            load = lambda q, n: ubuf[pl.ds(base + 8 * q, n), :]
            u1 = _stencil(load, dw_ref, 2, CONV_K + 1, lambda o: o - 2) + dwb_ref[...]
            u1_ref[pl.ds(base, RC), :] = u1
            n, _ = _layer_norm_fwd(u1)
            u2 = n * lng_ref[...] + lnb_ref[...]
            u3 = u2 * _sig(u2)
            cg = cg_ref[pl.ds(base, RC), :].astype(F32)
            ac_ref[pl.ds(base, RC), :] = (u3 * cg * _sig(cg)).astype(BF16)
            pwin = pbuf[pl.ds(base, RC + HALO), :]
            outs = _pool_chunk(pwin, t0 + base)
            for g in range(4):
                pooled[pl.ds(base, RC), g * 128:(g + 1) * 128] = outs[g].astype(BF16)
            return carry

        lax.fori_loop(0, tm // RC, chunk, 0, unroll=8)
        pg = pg_ref[...].astype(F32)
        spg = pg * _sig(pg)
        for g in range(4):
            sl = slice(g * 128, (g + 1) * 128)
            mixed = (_nn(pooled[:, sl], pw_ref[g]) + pb_ref[:, sl]) * ps_ref[:, sl]
            ap_ref[:, sl] = (mixed * spg[:, sl]).astype(BF16)

    def zmain(cb):
        return pl.BlockSpec((tm, BW), lambda i: (i, cb))

    def zprev(cb):
        return pl.BlockSpec((HALO, BW), lambda i: (jnp.maximum(i * nh - 1, 0), cb))

    full = lambda shape: pl.BlockSpec(shape, lambda i: (0,) * len(shape))
    row = pl.BlockSpec((tm, BW), lambda i: (i, 0))
    return _call(
        body, "mix_fwd", (T // tm,),
        [zmain(CB_CA), zmain(CB_CB), zmain(CB_CG), zmain(CB_PI), zmain(CB_PG),
         zprev(CB_CA), zprev(CB_CB), zprev(CB_PI),
         full((CONV_K, BW)), full((1, BW)), full((1, BW)), full((1, BW)),
         full((4, 128, 128)), full((1, BW)), full((1, BW))],
        [row] * 5, [SDS((T, BW), BF16), SDS((T, BW), BF16), SDS((T, BW), F32), SDS((T, BW), F32), SDS((T, BW), BF16)],
        [pltpu.VMEM((HALO + tm, BW), F32), pltpu.VMEM((HALO + tm, BW), F32)],
        (z, z, z, z, z, z, z, z, dw, dwb, lng, lnb, pw, pb, ps), ("parallel",), ex)


def _attn_specs(nq):
    def kv(cb, off):
        return pl.BlockSpec((TQ, BW), lambda i: (i - jnp.minimum(off, i % nq), cb))
    return [pl.BlockSpec((TQ, BW), lambda i: (i, CB_Q)),
            kv(CB_K, 2), kv(CB_K, 1), kv(CB_K, 0), kv(CB_V, 2), kv(CB_V, 1), kv(CB_V, 0)]


NSKEW = 1024


def _skew_table(table):
    return jnp.dot(table, jnp.asarray(_skew_select()), precision=lax.Precision.HIGHEST)


def _skew_select():
    d = np.arange(TQ + KW - 1)
    idx = np.clip(3 * TQ - 1 - d, -MAX_REL, MAX_REL) + MAX_REL
    sel = np.zeros((2 * MAX_REL + 1, NSKEW), np.float32)
    sel[idx, d] = 1.0
    return sel


def _bias_from_skew(f_ref, bias_scr):
    qi = lax.broadcasted_iota(jnp.int32, (TQ, KW), 0)
    kj = lax.broadcasted_iota(jnp.int32, (TQ, KW), 1)
    lo = (qi // CHUNK) * CHUNK
    band = jnp.where((kj >= lo) & (kj < lo + (LEFT + 1) * CHUNK), 0.0, NEG)
    for h in range(HEADS):
        rows = jnp.broadcast_to(f_ref[h:h + 1, :], (TQ, NSKEW))
        rows = pltpu.roll(rows, NSKEW - (TQ - 1), axis=1, stride=1, stride_axis=0)
        bias_scr[h] = rows[:, 0:KW] + band


def _skew_from_bias(db):
    i = lax.broadcasted_iota(jnp.int32, (TQ, TQ), 0)
    j = lax.broadcasted_iota(jnp.int32, (TQ, TQ), 1)
    flip = jnp.where(i + j == TQ - 1, 1.0, 0.0).astype(BF16)
    hi = db.astype(BF16)
    lo = (db - hi.astype(F32)).astype(BF16)
    rev = _nn(flip, hi) + _nn(flip, lo)
    rev = jnp.concatenate([rev, jnp.zeros((TQ, NSKEW - KW), F32)], axis=1)
    return jnp.sum(pltpu.roll(rev, 0, axis=1, stride=1, stride_axis=0), axis=0, keepdims=True)


def _attn_fwd(z, f, S, ex=None):
    T = z.shape[0]
    nq = S // TQ

    def body(q_ref, k2_ref, k1_ref, k0_ref, v2_ref, v1_ref, v0_ref, f_ref, o_ref, kbuf, vbuf, b_scr):
        @pl.when(pl.program_id(0) == 0)
        def _():
            _bias_from_skew(f_ref, b_scr)

        qb = pl.program_id(0) % nq
        kbuf[0:TQ, :] = k2_ref[...]
        kbuf[TQ:2 * TQ, :] = k1_ref[...]
        kbuf[2 * TQ:KW, :] = k0_ref[...]
        vbuf[0:TQ, :] = v2_ref[...]
        vbuf[TQ:2 * TQ, :] = v1_ref[...]
        vbuf[2 * TQ:KW, :] = v0_ref[...]
        lane = lax.broadcasted_iota(jnp.int32, (1, 128), 1)

        def attend(lo):
            def scores(h):
                sl = slice((h // 2) * 128, (h // 2 + 1) * 128)
                qp = q_ref[:, sl] * 0.125
                qm = jnp.where((lane < HD) if h % 2 == 0 else (lane >= HD), qp, jnp.zeros_like(qp))
                return _nt(qm, kbuf[lo:KW, sl]) + b_scr[h, :, lo:KW]

            s = scores(0)
            acc = None
            for h in range(HEADS):
                s_next = scores(h + 1) if h + 1 < HEADS else None
                sl = slice((h // 2) * 128, (h // 2 + 1) * 128)
                e = jnp.exp(s - jnp.max(s, axis=-1, keepdims=True))
                vp = vbuf[lo:KW, sl]
                vm = jnp.where((lane < HD) if h % 2 == 0 else (lane >= HD), vp, jnp.zeros_like(vp))
                o = _nn(e.astype(BF16), vm) * (1.0 / jnp.sum(e, axis=-1, keepdims=True))
                acc = o if h % 2 == 0 else acc + o
                if h % 2 == 1:
                    o_ref[:, sl] = acc.astype(BF16)
                s = s_next

        for nblk in (1, 2, 3):
            pl.when(jnp.minimum(qb, 2) == nblk - 1)(functools.partial(attend, (3 - nblk) * TQ))

    full = lambda shape: pl.BlockSpec(shape, lambda i: (0,) * len(shape))
    return _call(
        body, "attn_fwd", (T // TQ,),
        _attn_specs(nq) + [full((HEADS, NSKEW))],
        [pl.BlockSpec((TQ, BW), lambda i: (i, 0))], [SDS((T, BW), BF16)],
        [pltpu.VMEM((KW, BW), BF16), pltpu.VMEM((KW, BW), BF16), pltpu.VMEM((HEADS, TQ, KW), F32)],
        (z, z, z, z, z, z, z, f), ("arbitrary",), ex)


def _gates(gl_ref, gh_ref):
    gl = _sig(gl_ref[...].astype(F32))
    gh = _sig(gh_ref[...].astype(F32))
    return (gl[:, 0:D], jnp.concatenate([gl[:, D:1536], gh[:, 0:512]], axis=1), gh[:, 512:1536])


def _out_specs_in(tm):
    row = lambda w: pl.BlockSpec((tm, w), lambda i: (i, 0))
    full = lambda shape: pl.BlockSpec(shape, lambda i: (0,) * len(shape))
    return [row(BW), row(BW), row(BW),
            pl.BlockSpec((tm, BW), lambda i: (i, CB_AG)),
            pl.BlockSpec((tm, 1536), lambda i: (i, 3)),
            pl.BlockSpec((tm, 1536), lambda i: (i, 4)),
            full((BW, D)), full((BW, D)), full((BW, D)), full((D, D)), full((1, D))]


def _out_fwd(x, ac, o, ap, z, wco, wao, wpo, wout, postg, ex=None, tgt=None):
    T = x.shape[0]
    tm = 512
    last = tgt is not None

    def body(ac_ref, o_ref, ap_ref, ag_ref, gl_ref, gh_ref, wco_ref, wao_ref, wpo_ref, wout_ref, pg_ref,
             x_ref, *rest):
        ag = ag_ref[...].astype(F32)
        aat = (o_ref[...].astype(F32) * ag * _sig(ag)).astype(BF16)
        gates = _gates(gl_ref, gh_ref)
        acts = (ac_ref[...], aat, ap_ref[...])
        merged = None
        for b, w_ref in enumerate((wco_ref, wao_ref, wpo_ref)):
            yb = _nn(acts[b], w_ref[...])
            rest[-4 + b][...] = yb.astype(BF16)
            merged = gates[b] * yb if merged is None else merged + gates[b] * yb
        y = _nn(merged.astype(BF16), wout_ref[...])
        rest[-1][...] = y.astype(BF16)
        ry = lax.rsqrt(jnp.mean(y * y, axis=-1, keepdims=True) + EPS)
        out = x_ref[...] + y * ry * pg_ref[...]
        if not last:
            rest[0][...] = out
            return
        t_ref, d_ref, l_ref = rest[:3]

        @pl.when(pl.program_id(0) == 0)
        def _():
            l_ref[...] = jnp.zeros_like(l_ref)
        d = out - t_ref[...]
        d_ref[...] = d * (1.0 / D)
        l_ref[...] += jnp.sum(jnp.sum(d * d, axis=0, keepdims=True), axis=1, keepdims=True)

    row = pl.BlockSpec((tm, D), lambda i: (i, 0))
    kept_specs, kept_shapes = [row] * 4, [SDS((T, D), BF16)] * 4
    if not last:
        res = _call(body, "out_fwd", (T // tm,), _out_specs_in(tm) + [row], [row] + kept_specs,
                    [SDS((T, D), F32)] + kept_shapes, [],
                    (ac, o, ap, z, z, z, wco, wao, wpo, wout, postg, x), ("parallel",), ex)
        return res[:1], res[1:5], res[5:]
    res = _call(body, "out_fwd_loss", (T // tm,), _out_specs_in(tm) + [row, row],
                [row, pl.BlockSpec((1, 128), lambda i: (0, 0))] + kept_specs,
                [SDS((T, D), F32), SDS((1, 128), F32)] + kept_shapes, [],
                (ac, o, ap, z, z, z, wco, wao, wpo, wout, postg, x, tgt), ("arbitrary",), ex)
    return res[:2], res[2:6], res[6:]


def _out_bwd(dout, ac, o, ap, z, kept, wco, wao, wpo, wout, postg, ex=None):
    T = dout.shape[0]
    tm = 512

    def body(ac_ref, o_ref, ap_ref, ag_ref, gl_ref, gh_ref, wco_ref, wao_ref, wpo_ref, wout_ref, pg_ref, do_ref,
             yc_ref, ya_ref, yp_ref, y_ref,
             dac_ref, dao_ref, dag_ref, dap_ref, dgm_ref, dwco_ref, dwao_ref, dwpo_ref, dwout_ref, dpg_ref):
        @pl.when(pl.program_id(0) == 0)
        def _():
            for r in (dwco_ref, dwao_ref, dwpo_ref, dwout_ref, dpg_ref):
                r[...] = jnp.zeros_like(r)

        ag = ag_ref[...].astype(F32)
        sag = _sig(ag)
        ov = o_ref[...].astype(F32)
        acts = (ac_ref[...], (ov * ag * sag).astype(BF16), ap_ref[...])
        ws = (wco_ref, wao_ref, wpo_ref)
        gates = _gates(gl_ref, gh_ref)
        ys = [r[...].astype(F32) for r in (yc_ref, ya_ref, yp_ref)]
        merged = (gates[0] * ys[0] + gates[1] * ys[1] + gates[2] * ys[2]).astype(BF16)
        y = y_ref[...].astype(F32)
        ry = lax.rsqrt(jnp.mean(y * y, axis=-1, keepdims=True) + EPS)
        yn = y * ry
        dout_v = do_ref[...]
        dpg_ref[...] += jnp.sum(dout_v * yn, axis=0, keepdims=True)
        dyn = dout_v * pg_ref[...]
        dy = (ry * (dyn - yn * jnp.mean(dyn * yn, axis=-1, keepdims=True))).astype(BF16)
        dmerged = _nt(dy, wout_ref[...])
        dwout_ref[...] += _tn(merged, dy)
        dws = (dwco_ref, dwao_ref, dwpo_ref)
        das = []
        for b in range(3):
            gb = gates[b]
            dgm_ref[:, b * D:(b + 1) * D] = (dmerged * ys[b] * gb * (1.0 - gb)).astype(BF16)
            dyb = (dmerged * gb).astype(BF16)
            dws[b][...] += _tn(acts[b], dyb)
            das.append(_nt(dyb, ws[b][...]))
        dac_ref[...] = das[0].astype(BF16)
        dap_ref[...] = das[2].astype(BF16)
        dao_ref[...] = (das[1] * ag * sag).astype(BF16)
        dag_ref[...] = (das[1] * ov * _dsilu(ag, sag)).astype(BF16)

    row = lambda w: pl.BlockSpec((tm, w), lambda i: (i, 0))
    full = lambda shape: pl.BlockSpec(shape, lambda i: (0,) * len(shape))
    return _call(
        body, "out_bwd", (T // tm,), _out_specs_in(tm) + [row(D)] * 5,
        [row(BW), row(BW), row(BW), row(BW), row(3 * D),
         full((BW, D)), full((BW, D)), full((BW, D)), full((D, D)), full((1, D))],
        [SDS((T, BW), BF16)] * 4 + [SDS((T, 3 * D), BF16)]
        + [SDS((BW, D), F32)] * 3 + [SDS((D, D), F32), SDS((1, D), F32)], [],
        (ac, o, ap, z, z, z, wco, wao, wpo, wout, postg, dout, *kept), ("arbitrary",), ex)


def _attn_bwd(z, dao, f, S, ex=None):
    T = z.shape[0]
    nq = S // TQ
    nsteps = T // TQ

    def body(q_ref, k2_ref, k1_ref, k0_ref, v2_ref, v1_ref, v0_ref, do_ref, f_ref,
             dq_ref, dk_ref, dv_ref, df_ref, kbuf, vbuf, dkacc, dvacc, b_scr, db_scr):
        i = pl.program_id(0)
        qb = i % nq

        @pl.when(i == 0)
        def _():
            _bias_from_skew(f_ref, b_scr)
            db_scr[...] = jnp.zeros_like(db_scr)

        @pl.when(qb == 0)
        def _():
            dkacc[...] = jnp.zeros_like(dkacc)
            dvacc[...] = jnp.zeros_like(dvacc)

        kbuf[0:TQ, :] = k2_ref[...]
        kbuf[TQ:2 * TQ, :] = k1_ref[...]
        kbuf[2 * TQ:KW, :] = k0_ref[...]
        vbuf[0:TQ, :] = v2_ref[...]
        vbuf[TQ:2 * TQ, :] = v1_ref[...]
        vbuf[2 * TQ:KW, :] = v0_ref[...]
        lane = lax.broadcasted_iota(jnp.int32, (1, 128), 1)
        row0 = pl.multiple_of(qb * TQ, TQ)

        def attend(lo):
            def first_matmuls(h):
                sl = slice((h // 2) * 128, (h // 2 + 1) * 128)
                msk = (lane < HD) if h % 2 == 0 else (lane >= HD)
                qp = q_ref[:, sl] * 0.125
                dop = do_ref[:, sl]
                qm = jnp.where(msk, qp, jnp.zeros_like(qp))
                dom = jnp.where(msk, dop, jnp.zeros_like(dop))
                s = _nt(qm, kbuf[lo:KW, sl]) + b_scr[h, :, lo:KW]
                return s, _nt(dom, vbuf[lo:KW, sl]), qm, dom

            cur = first_matmuls(0)
            dq_acc = dk_acc = dv_acc = None
            for h in range(HEADS):
                nxt = first_matmuls(h + 1) if h + 1 < HEADS else None
                s, dp, qm, dom = cur
                sl = slice((h // 2) * 128, (h // 2 + 1) * 128)
                e = jnp.exp(s - jnp.max(s, axis=-1, keepdims=True))
                p = e * (1.0 / jnp.sum(e, axis=-1, keepdims=True))
                ds = p * (dp - jnp.sum(p * dp, axis=-1, keepdims=True))
                db_scr[h, :, lo:KW] += ds
                dsb = ds.astype(BF16)
                kp = kbuf[lo:KW, sl]
                km = jnp.where((lane < HD) if h % 2 == 0 else (lane >= HD), kp, jnp.zeros_like(kp))
                dq_h = _nn(dsb, km) * 0.125
                dk_h = _tn(dsb, qm)
                dv_h = _tn(p.astype(BF16), dom)
                if h % 2 == 0:
                    dq_acc, dk_acc, dv_acc = dq_h, dk_h, dv_h
                else:
                    dq_ref[:, sl] = (dq_acc + dq_h).astype(BF16)
                    dkacc[pl.ds(row0 + lo, KW - lo), sl] += dk_acc + dk_h
                    dvacc[pl.ds(row0 + lo, KW - lo), sl] += dv_acc + dv_h
                cur = nxt

        for nblk in (1, 2, 3):
            pl.when(jnp.minimum(qb, 2) == nblk - 1)(functools.partial(attend, (3 - nblk) * TQ))

        @pl.when(qb == nq - 1)
        def _():
            dk_ref[...] = dkacc[2 * TQ:2 * TQ + S, :].astype(BF16)
            dv_ref[...] = dvacc[2 * TQ:2 * TQ + S, :].astype(BF16)

        @pl.when(i == nsteps - 1)
        def _():
            for h in range(HEADS):
                df_ref[h:h + 1, :] = _skew_from_bias(db_scr[h])

    full = lambda shape: pl.BlockSpec(shape, lambda i: (0,) * len(shape))
    return _call(
        body, "attn_bwd", (nsteps,),
        _attn_specs(nq) + [pl.BlockSpec((TQ, BW), lambda i: (i, 0)), full((HEADS, NSKEW))],
        [pl.BlockSpec((TQ, BW), lambda i: (i, 0)), pl.BlockSpec((S, BW), lambda i: (i // nq, 0)),
         pl.BlockSpec((S, BW), lambda i: (i // nq, 0)), full((HEADS, NSKEW))],
        [SDS((T, BW), BF16)] * 3 + [SDS((HEADS, NSKEW), F32)],
        [pltpu.VMEM((KW, BW), BF16), pltpu.VMEM((KW, BW), BF16),
         pltpu.VMEM((S + 2 * TQ, BW), F32), pltpu.VMEM((S + 2 * TQ, BW), F32),
         pltpu.VMEM((HEADS, TQ, KW), F32), pltpu.VMEM((HEADS, TQ, KW), F32)],
        (z, z, z, z, z, z, z, dao, f), ("arbitrary",), ex)


def _mix_bwd(z, u1, u0, pooled_kept, dac, dap, dw, dwb, lng, lnb, pw, pb, ps, S, ex=None):
    T = z.shape[0]
    tm = min(S, 1024)
    ts = S // tm
    nh = tm // HALO
    nsteps = T // tm
    nblk32 = T // HALO

    def body(ca_ref, cb_ref, cg_ref, pg_ref, u1_ref, u0_ref, pooled, dac_ref, dap_ref,
             u0h_ref,
             cgn_ref, pgn_ref, u1n_ref, dacn_ref, dapn_ref,
             dw_ref, dwb_ref, lng_ref, lnb_ref, pw_ref, pb_ref, ps_ref,
             dzc_ref, dzp_ref, ddw_ref, ddwb_ref, dlng_ref, dlnb_ref, dpw_ref, dpb_ref, dps_ref,
             ubuf, gbuf, qbuf, *accs):
        tap_acc, (lng_acc, lnb_acc, dwb_acc) = accs[:CONV_K], accs[CONV_K:]
        i = pl.program_id(0)
        keep_prev = jnp.where((i % ts) == 0, 0.0, 1.0)
        keep_next = jnp.where((i % ts) == ts - 1, 0.0, 1.0)
        t0 = (i % ts) * tm

        @pl.when(i == 0)
        def _():
            for a in accs:
                a[...] = jnp.zeros_like(a)
            dpw_ref[...] = jnp.zeros_like(dpw_ref)
            dpb_ref[...] = jnp.zeros_like(dpb_ref)
            dps_ref[...] = jnp.zeros_like(dps_ref)

        ubuf[0:HALO, :] = u0h_ref[...] * keep_prev
        ubuf[HALO:HALO + tm, :] = u0_ref[...]

        def norm_back(u1v, cg, dacv):
            n, rstd = _layer_norm_fwd(u1v)
            u2 = n * lng_ref[...] + lnb_ref[...]
            s2 = _sig(u2)
            scg = _sig(cg)
            du2 = dacv * cg * scg * _dsilu(u2, s2)
            dn = du2 * lng_ref[...]
            du1 = rstd * (dn - jnp.mean(dn, axis=-1, keepdims=True)
                          - n * jnp.mean(dn * n, axis=-1, keepdims=True))
            return du1, du2, n, dacv * u2 * s2 * _dsilu(cg, scg)

        def chunk_a(c, carry):
            base = pl.multiple_of(c * RC, RC)
            du1, du2, n, dcg = norm_back(u1_ref[pl.ds(base, RC), :], cg_ref[pl.ds(base, RC), :].astype(F32),
                                         dac_ref[pl.ds(base, RC), :].astype(F32))
            gbuf[pl.ds(base, RC), :] = du1
            dzc_ref[pl.ds(base, RC), 2 * BW:3 * BW] = dcg.astype(BF16)
            lng_acc[...] += _rows8(du2 * n)
            lnb_acc[...] += _rows8(du2)
            dwb_acc[...] += _rows8(du1)
            padded = jnp.concatenate([du1, jnp.zeros((8, BW), F32)], axis=0)
            for r in range(8):
                nrow = RC if r == 0 else RC + 8
                g = du1 if r == 0 else pltpu.roll(padded, r, axis=0)
                for q in range((CONV_K + 1 - r) // 8 + 1):
                    o = 8 * q + r
                    if o < 2:
                        continue
                    prod = g * ubuf[pl.ds(base + 8 * q, nrow), :]
                    red = prod[0:8]
                    for k in range(1, nrow // 8):
                        red = red + prod[8 * k:8 * k + 8]
                    tap_acc[o - 2][...] += red
            return carry

        lax.fori_loop(0, tm // RC, chunk_a, 0, unroll=8)
        du1n, _, _, _ = norm_back(u1n_ref[...], cgn_ref[...].astype(F32), dacn_ref[...].astype(F32))
        gbuf[tm:tm + HALO, :] = du1n * keep_next

        def cnt_of(t_first, rows, w):
            t = t_first + lax.broadcasted_iota(jnp.int32, (rows, 128), 0)
            return jnp.minimum(t + 1, w).astype(F32)

        pg = pg_ref[...].astype(F32)
        spg_s = _sig(pg)
        dapv = dap_ref[...].astype(F32)
        pgn = pgn_ref[...].astype(F32)
        dmixn = dapn_ref[...].astype(F32) * pgn * _sig(pgn) * ps_ref[...] * keep_next
        for g, w in enumerate(WINDOWS):
            sl = slice(g * 128, (g + 1) * 128)
            mixed_u = _nn(pooled[:, sl], pw_ref[g]) + pb_ref[:, sl]
            dap_g = dapv[:, sl]
            pg_g = pg[:, sl]
            s_g = spg_s[:, sl]
            silu_g = pg_g * s_g
            dps_ref[:, sl] += jnp.sum(dap_g * silu_g * mixed_u, axis=0, keepdims=True)
            dzp_ref[:, BW + g * 128:BW + (g + 1) * 128] = (
                dap_g * mixed_u * ps_ref[:, sl] * _dsilu(pg_g, s_g)).astype(BF16)
            dmix = dap_g * silu_g * ps_ref[:, sl]
            dpb_ref[:, sl] += jnp.sum(dmix, axis=0, keepdims=True)
            dmixb = dmix.astype(BF16)
            dpw_ref[g] += _tn(pooled[:, sl], dmixb)
            qbuf[0:tm, sl] = _nt(dmixb, pw_ref[g]) / cnt_of(t0, tm, w)
            qbuf[tm:tm + HALO, sl] = _nt(dmixn[:, sl].astype(BF16), pw_ref[g]) / cnt_of(t0 + tm, HALO, w)

        def chunk_b(c, carry):
            base = pl.multiple_of(c * RC, RC)
            load = lambda q, n: gbuf[pl.ds(base + 8 * q, n), :]
            du0 = _stencil(load, dw_ref, 0, CONV_K - 1, lambda o: CONV_K - 1 - o)
            ca = ca_ref[pl.ds(base, RC), :].astype(F32)
            sb = _sig(cb_ref[pl.ds(base, RC), :].astype(F32))
            dzc_ref[pl.ds(base, RC), 0:BW] = (du0 * sb).astype(BF16)
            dzc_ref[pl.ds(base, RC), BW:2 * BW] = (du0 * ca * sb * (1.0 - sb)).astype(BF16)
            qwin = qbuf[pl.ds(base, RC + HALO), :]
            t = t0 + base + lax.broadcasted_iota(jnp.int32, (RC, 128), 0)
            for g, w in enumerate(WINDOWS):
                x = qwin[:, g * 128:(g + 1) * 128]
                s = _window_sums(x, w, False)
                cnt = jnp.minimum(t + 1, w).astype(F32)
                dzp_ref[pl.ds(base, RC), g * 128:(g + 1) * 128] = (s[0:RC] - cnt * x[0:RC]).astype(BF16)
            return carry

        lax.fori_loop(0, tm // RC, chunk_b, 0)

        @pl.when(i == nsteps - 1)
        def _():
            dlng_ref[...] = jnp.sum(lng_acc[...], axis=0, keepdims=True)
            dlnb_ref[...] = jnp.sum(lnb_acc[...], axis=0, keepdims=True)
            ddwb_ref[...] = jnp.sum(dwb_acc[...], axis=0, keepdims=True)
            for j in range(CONV_K):
                ddw_ref[j:j + 1, :] = jnp.sum(tap_acc[j][...], axis=0, keepdims=True)

    def zmain(cb):
        return pl.BlockSpec((tm, BW), lambda i: (i, cb))

    def zprev(cb):
        return pl.BlockSpec((HALO, BW), lambda i: (jnp.maximum(i * nh - 1, 0), cb))

    def znext(cb):
        return pl.BlockSpec((HALO, BW), lambda i: (jnp.minimum((i + 1) * nh, nblk32 - 1), cb))

    row = lambda w: pl.BlockSpec((tm, w), lambda i: (i, 0))
    full = lambda shape: pl.BlockSpec(shape, lambda i: (0,) * len(shape))
    return _call(
        body, "mix_bwd", (nsteps,),
        [zmain(CB_CA), zmain(CB_CB), zmain(CB_CG), zmain(CB_PG), row(BW), row(BW), row(BW), row(BW), row(BW),
         zprev(0),
         znext(CB_CG), znext(CB_PG), znext(0), znext(0), znext(0),
         full((CONV_K, BW)), full((1, BW)), full((1, BW)), full((1, BW)),
         full((4, 128, 128)), full((1, BW)), full((1, BW))],
        [row(3 * BW), row(2 * BW), full((CONV_K, BW)), full((1, BW)), full((1, BW)), full((1, BW)),
         full((4, 128, 128)), full((1, BW)), full((1, BW))],
        [SDS((T, 3 * BW), BF16), SDS((T, 2 * BW), BF16), SDS((CONV_K, BW), F32),
         SDS((1, BW), F32), SDS((1, BW), F32), SDS((1, BW), F32),
         SDS((4, 128, 128), F32), SDS((1, BW), F32), SDS((1, BW), F32)],
        [pltpu.VMEM((HALO + tm, BW), F32), pltpu.VMEM((tm + HALO, BW), F32), pltpu.VMEM((tm + HALO, BW), F32)]
        + [pltpu.VMEM((8, BW), F32)] * (CONV_K + 3),
        (z, z, z, z, u1, u0, pooled_kept, dac, dap, u0, z, z, u1, dac, dap, dw, dwb, lng, lnb, pw, pb, ps),
        ("arbitrary",), ex)


def _in_bwd_x(pieces, wt, x, g, dout, ex=None):
    T = x.shape[0]
    tm = 512
    widths = [p.shape[1] for p in pieces]
    offs = np.cumsum([0] + widths)
    npc = len(pieces)

    def body(*refs):
        p_refs = refs[:npc]
        w_ref, x_ref, g_ref, do_ref, dx_ref, dg_ref = refs[npc:]

        @pl.when(pl.program_id(0) == 0)
        def _():
            dg_ref[...] = jnp.zeros_like(dg_ref)

        dh = None
        for k in range(npc):
            t = _nn(p_refs[k][...], w_ref[int(offs[k]):int(offs[k + 1]), :])
            dh = t if dh is None else dh + t
        xv = x_ref[...]
        r = lax.rsqrt(jnp.mean(xv * xv, axis=-1, keepdims=True) + EPS)
        xn = xv * r
        dg_ref[...] += jnp.sum(dh * xn, axis=0, keepdims=True)
        dxn = dh * g_ref[...]
        dx_ref[...] = do_ref[...] + r * (dxn - xn * jnp.mean(dxn * xn, axis=-1, keepdims=True))

    row = lambda wd: pl.BlockSpec((tm, wd), lambda i: (i, 0))
    return _call(
        body, "in_bwd_x", (T // tm,),
        [row(wd) for wd in widths] + [pl.BlockSpec((NCOL, D), lambda i: (0, 0)),
                                      row(D), pl.BlockSpec((1, D), lambda i: (0, 0)), row(D)],
        [row(D), pl.BlockSpec((1, D), lambda i: (0, 0))], [SDS((T, D), F32), SDS((1, D), F32)], [],
        (*pieces, wt, x, g, dout), ("arbitrary",), ex)


def _in_bwd_w(ht, pieces, row0, buf=None, ex=None):
    T = ht.shape[1]
    n = len(pieces)
    wd = pieces[0].shape[1]
    tn = next(t for t in (1536, 1024, 768, 512) if wd % t == 0 and row0 % t == 0)
    per = wd // tn
    tk = min(T, 2048)
    nk = T // tk
    j0 = row0 // tn

    def body(ht_ref, *rest):
        p_refs = rest[:n]
        o_ref, acc = rest[-2:]
        j, k = pl.program_id(0), pl.program_id(1)

        @pl.when(k == 0)
        def _():
            acc[...] = jnp.zeros_like(acc)

        for p in range(n):
            @pl.when(j // per == p)
            def _(p=p):
                acc[...] += _nn(ht_ref[...], p_refs[p][...])

        @pl.when(k == nk - 1)
        def _():
            o_ref[...] = acc[...].T.astype(BF16)

    def piece_spec(p):
        return pl.BlockSpec((tk, tn), lambda j, k: (jnp.where(j // per == p, k, 0), jnp.where(j // per == p, j % per, 0)))

    in_specs = [pl.BlockSpec((D, tk), lambda j, k: (0, k))] + [piece_spec(p) for p in range(n)]
    args = (ht, *pieces)
    if buf is not None:
        in_specs.append(pl.BlockSpec(memory_space=pl.ANY))
        args += (buf,)
    return _call(
        body, "in_bwd_w", (n * per, nk), in_specs,
        [pl.BlockSpec((tn, D), lambda j, k: (j + j0, 0))], [SDS((NCOL, D), BF16)], [pltpu.VMEM((D, tn), F32)],
        args, ("arbitrary", "arbitrary"), ex, None if buf is None else {n + 1: 0})


def _my_id():
    return 4 * lax.axis_index("x") + 2 * lax.axis_index("y") + lax.axis_index("c")


def _peers():
    x, y, c = lax.axis_index("x"), lax.axis_index("y"), lax.axis_index("c")
    out = []
    for k in range(1, N_DEV):
        fx, fy, fc = (k >> 2) & 1, (k >> 1) & 1, k & 1
        px, py, pc = x ^ fx, y ^ fy, c ^ fc
        out.append(((px, py, pc), 4 * px + 2 * py + pc))
    return out


class _Exchange:
    def __init__(self, arrays, scatter):
        self.arrays = list(arrays)
        self.scatter = list(scatter)
        self.n = n = len(arrays)
        hbm = pl.BlockSpec(memory_space=pltpu.HBM)
        self.in_specs = [hbm] * n
        self.out_specs = [hbm] * n
        self.out_shape = [SDS((N_DEV,) + tuple(a.shape[1:] if s else a.shape), a.dtype)
                          for a, s in zip(arrays, scatter)]
        self.scratch = [pltpu.SemaphoreType.DMA((N_DEV - 1, n)), pltpu.SemaphoreType.DMA((N_DEV - 1, n)),
                        pltpu.SemaphoreType.DMA((n,))]

    def split(self, refs, n_in, n_out):
        n = self.n
        own_in = refs[:n_in]
        ex_in = refs[n_in:n_in + n]
        own_out = refs[n_in + n:n_in + n + n_out]
        ex_out = refs[n_in + n + n_out:n_in + 2 * n + n_out]
        rest = refs[n_in + 2 * n + n_out:]
        return own_in, own_out, rest[:-3], (ex_in, ex_out, rest[-3:])

    def _copy(self, ex, k, p, landing):
        in_refs, out_refs, (send_sems, recv_sems, _) = ex
        pos, pid = _peers()[p]
        return pltpu.make_async_remote_copy(
            src_ref=in_refs[k].at[pid] if self.scatter[k] else in_refs[k],
            dst_ref=out_refs[k].at[pid if landing else _my_id()],
            send_sem=send_sems.at[p, k], recv_sem=recv_sems.at[p, k],
            device_id=pos, device_id_type=pl.DeviceIdType.MESH)

    def _own(self, ex, k):
        in_refs, out_refs, (_, _, local_sems) = ex
        me = _my_id()
        return pltpu.make_async_copy(in_refs[k].at[me] if self.scatter[k] else in_refs[k], out_refs[k].at[me],
                                     local_sems.at[k])

    def start(self, ex):
        for k in range(self.n):
            self._own(ex, k).start()
        for p in range(N_DEV - 1):
            for k in range(self.n):
                self._copy(ex, k, p, False).start()

    def finish(self, ex):
        for p in range(N_DEV - 1):
            for k in range(self.n):
                self._copy(ex, k, p, True).wait_recv()
        for p in range(N_DEV - 1):
            for k in range(self.n):
                self._copy(ex, k, p, False).wait_send()
        for k in range(self.n):
            self._own(ex, k).wait()


def _exchange(arrays, scatter, name):
    ex = _Exchange(arrays, scatter)

    def body(*refs):
        _, _, _, exr = ex.split(refs, 0, 0)
        ex.start(exr)
        ex.finish(exr)

    return pl.pallas_call(body, name=name, in_specs=ex.in_specs, out_specs=ex.out_specs,
                          out_shape=ex.out_shape, scratch_shapes=ex.scratch)(*ex.arrays)


def _gather_two_level(shard, name):
    def body(x_ref, out_ref, send_sems, recv_sems, local_sem):
        x, y, c = lax.axis_index("x"), lax.axis_index("y"), lax.axis_index("c")
        me, sibling = (x, y, c), (x, y, 1 - c)
        chips = [(1 - x, y), (x, 1 - y), (1 - x, 1 - y)]
        slab = lambda px, py, pc: out_ref.at[4 * px + 2 * py + pc]

        def copy(k, block, to, src=None):
            return pltpu.make_async_remote_copy(
                src_ref=slab(*block) if src is None else src, dst_ref=slab(*block),
                send_sem=send_sems.at[k], recv_sem=recv_sems.at[k],
                device_id=to, device_id_type=pl.DeviceIdType.MESH)

        mine = pltpu.make_async_copy(x_ref, slab(*me), local_sem)
        mine.start()
        first = [copy(0, me, sibling, src=x_ref)] + [copy(1 + j, me, (*chip, c), src=x_ref)
                                                     for j, chip in enumerate(chips)]
        for cp in first:
            cp.start()
        passed = [copy(4 + j, (*chip, c), sibling) for j, chip in enumerate(chips)]
        for j, chip in enumerate(chips):
            copy(1 + j, (*chip, c), me).wait_recv()
            passed[j].start()
        copy(0, sibling, me).wait_recv()
        for j, chip in enumerate(chips):
            copy(4 + j, (*chip, 1 - c), me).wait_recv()
        for cp in first + passed:
            cp.wait_send()
        mine.wait()

    hbm = pl.BlockSpec(memory_space=pltpu.HBM)
    return pl.pallas_call(
        body, name=name, in_specs=[hbm], out_specs=hbm,
        out_shape=SDS((N_DEV,) + shard.shape, shard.dtype),
        scratch_shapes=[pltpu.SemaphoreType.DMA((N_DEV - 1,)), pltpu.SemaphoreType.DMA((N_DEV - 1,)),
                        pltpu.SemaphoreType.DMA],
    )(shard)


def _adamw_update(g, w, m, v):
    c1 = 1.0 / (1.0 - ADAM_B1 ** ADAM_STEP)
    c2 = 1.0 / (1.0 - ADAM_B2 ** ADAM_STEP)
    mn = ADAM_B1 * m + (1.0 - ADAM_B1) * g
    vn = ADAM_B2 * v + (1.0 - ADAM_B2) * (g * g)
    return -ADAM_LR * ((mn * c1) / (jnp.sqrt(vn * c2) + ADAM_EPS) + ADAM_WD * w), mn, vn


def _adamw_small(parts, w, m, v):
    n = len(w)

    def body(*refs):
        p_refs = (refs[0:n], refs[n:2 * n])
        w_refs, m_refs, v_refs = refs[2 * n:3 * n], refs[3 * n:4 * n], refs[4 * n:5 * n]
        outs = refs[5 * n:]
        for k in range(n):
            g_ref, d_ref, mo_ref, vo_ref = outs[4 * k:4 * k + 4]
            for l in range(2):
                at = (slice(l, l + 1),) if len(w_refs[k].shape) == 2 else (l,)
                g = p_refs[l][k][0]
                for s in range(1, N_DEV):
                    g = g + p_refs[l][k][s]
                delta, mn, vn = _adamw_update(g, w_refs[k][at], m_refs[k][at], v_refs[k][at])
                g_ref[at] = g
                d_ref[at] = delta
                mo_ref[at] = mn
                vo_ref[at] = vn

    vmem = pl.BlockSpec(memory_space=pltpu.VMEM)
    res = pl.pallas_call(
        body, name="adamw_replicated", in_specs=[vmem] * (5 * n), out_specs=[vmem] * (4 * n),
        out_shape=[SDS(a.shape, F32) for a in w for _ in range(4)],
        compiler_params=pltpu.CompilerParams(vmem_limit_bytes=VMEM_LIMIT),
    )(*parts[0], *parts[1], *w, *m, *v)
    return [res[4 * k:4 * k + 4] for k in range(n)]


def _adamw_sum(parts0, parts1, w, m, v, name):
    _, R, C = w.shape
    tr = R
    while tr * C > 256 * 1024 and tr % 32 == 0:
        tr //= 2

    def body(p0_ref, p1_ref, w_ref, m_ref, v_ref, g_ref, d_ref, mo_ref, vo_ref):
        def update(p_ref):
            g = p_ref[0].astype(F32)
            for s in range(1, N_DEV):
                g = g + p_ref[s].astype(F32)
            g_ref[...] = g
            d_ref[...], mo_ref[...], vo_ref[...] = _adamw_update(g, w_ref[...], m_ref[...], v_ref[...])

        @pl.when(pl.program_id(0) == 0)
        def _():
            update(p0_ref)

        @pl.when(pl.program_id(0) == 1)
        def _():
            update(p1_ref)

    blk = pl.BlockSpec((None, tr, C), lambda l, i: (l, i, 0))
    return pl.pallas_call(
        body, name=name, grid=(2, R // tr),
        in_specs=[pl.BlockSpec((N_DEV, tr, C), lambda l, i: (0, i * (1 - l), 0)),
                  pl.BlockSpec((N_DEV, tr, C), lambda l, i: (0, i * l, 0)), blk, blk, blk],
        out_specs=[blk, blk, blk, blk],
        out_shape=[SDS((2, R, C), F32)] * 4,
        compiler_params=_cparams(("arbitrary", "arbitrary")),
    )(parts0, parts1, w, m, v)


def _layer_fwd(x, P, skew, S, rest, ex, tgt=None):
    z, ht, *got0 = _in_proj(x, P["pre_g"], P["w_in_t"], ex[0])
    P = {**P, **rest(got0)}
    ac, ap, u1, u0, pooled, *got1 = _mix_fwd(z, P["conv_dw"], P["conv_dw_b"], P["conv_ln_g"], P["conv_ln_b"],
                                 P["pool_w"], P["pool_b"], P["pool_scale"], S, ex[1])
    o, *got2 = _attn_fwd(z, skew, S, ex[2])
    out, kept, got3 = _out_fwd(x, ac, o, ap, z, P["w_conv_out"], P["w_attn_out"], P["w_pool_out"], P["w_out"],
                               P["post_g"], ex[3], tgt)
    return out, (x, z, ht, ac, o, ap, u1, u0, pooled, kept), P, (got0, got1, got2, got3)


def _layer_bwd(dout, saved, P, skew, S, ex=(None, None), mix_ex=None, win_ex=None):
    x, z, ht, ac, o, ap, u1, u0, pooled, kept = saved
    (dac, dao, dag, dap, dgm, dwco, dwao, dwpo, dwout, dpostg, *got0) = _out_bwd(
        dout, ac, o, ap, z, kept, P["w_conv_out"], P["w_attn_out"], P["w_pool_out"], P["w_out"], P["post_g"], ex[0])
    grads = dict(post_norm_g=dpostg, w_conv_out=dwco, w_attn_out=dwao, w_pool_out=dwpo, w_out=dwout)
    dq, dk, dv, grads["dskew"], *got1 = _attn_bwd(z, dao, skew, S, ex[1])
    (dzc, dzp, grads["conv_dw"], grads["conv_dw_b"], grads["conv_ln_g"], grads["conv_ln_b"], grads["pool_w"],
     grads["pool_b"], grads["pool_scale"], *got2) = _mix_bwd(
        z, u1, u0, pooled, dac, dap, P["conv_dw"], P["conv_dw_b"], P["conv_ln_g"], P["conv_ln_b"],
        P["pool_w"], P["pool_b"], P["pool_scale"], S, mix_ex(grads) if mix_ex else None)
    pieces = [dzc, dq, dk, dv, dag, dzp, dgm]
    buf, row0 = None, 0
    for group in ([dzc], [dq, dk, dv, dag], [dzp], [dgm]):
        (buf,) = _in_bwd_w(ht, group, row0, buf)
        row0 += sum(p.shape[1] for p in group)
    grads["w_in_t"] = buf
    dx, grads["pre_norm_g"], *got3 = _in_bwd_x(pieces, P["w_in_t"], x, P["pre_g"], dout,
                                               win_ex(grads) if win_ex else None)
    return dx, grads, (got0, got1, got2, got3)


WEIGHT_NAMES = ("pre_norm_g", "post_norm_g", "w_in", "conv_dw", "conv_dw_b", "conv_ln_g", "conv_ln_b",
                "w_conv_out", "rel_bias", "w_attn_out", "pool_w", "pool_b", "pool_scale", "w_pool_out", "w_out")
SHARDED = ("w_in", "w_conv_out", "w_attn_out", "w_pool_out", "w_out", "conv_dw")
OUT_PROJ = ("w_conv_out", "w_attn_out", "w_pool_out", "w_out")
REST = tuple(n for n in WEIGHT_NAMES if n not in ("w_in", "pre_norm_g"))


def _cols_from_slabs(g):
    return g.transpose(1, 0, 2).reshape(g.shape[1], N_DEV * g.shape[2])


def _slabs_from_cols(full):
    r, wd = full.shape
    return full.reshape(r, N_DEV, wd // N_DEV).transpose(1, 0, 2)


def _rest_shards(weights, l):
    return [weights["w_conv_out"][l].astype(BF16), weights["w_attn_out"][l].astype(BF16),
            weights["w_pool_out"][l].astype(BF16), weights["w_out"][l].astype(BF16), weights["conv_dw"][l]]


def _rest_weights(got):
    wco, wao, wpo, wout, cdw = got
    return dict(w_conv_out=_cols_from_slabs(wco), w_attn_out=_cols_from_slabs(wao),
                w_pool_out=_cols_from_slabs(wpo), w_out=wout.reshape(D, D), conv_dw=_cols_from_slabs(cdw))


def _grad_arrays(g, names):
    make = {"w_in": lambda: g["w_in_t"].reshape(N_DEV, NCOL // N_DEV, D),
            "w_conv_out": lambda: _slabs_from_cols(g["w_conv_out"].astype(BF16)),
            "w_attn_out": lambda: _slabs_from_cols(g["w_attn_out"].astype(BF16)),
            "w_pool_out": lambda: _slabs_from_cols(g["w_pool_out"].astype(BF16)),
            "w_out": lambda: g["w_out"].astype(BF16).reshape(N_DEV, D // N_DEV, D),
            "conv_dw": lambda: _slabs_from_cols(g["conv_dw"].astype(BF16)),
            "rel_bias": lambda: jnp.dot(g["dskew"], jnp.asarray(_skew_select().T), precision=lax.Precision.HIGHEST),
            "pool_b": lambda: g["pool_b"].reshape(4, 128)}
    return [make[n]() if n in make else g[n] for n in names]


def _grad_exchange(g, names):
    return _Exchange(_grad_arrays(g, names), [n in SHARDED for n in names])


def kernel(x, pre_norm_g, post_norm_g, w_in, conv_dw, conv_dw_b, conv_ln_g, conv_ln_b, w_conv_out, rel_bias, w_attn_out, pool_w, pool_b, pool_scale, w_pool_out, w_out, loss_target, m_pre_norm_g, m_post_norm_g, m_w_in, m_conv_dw, m_conv_dw_b, m_conv_ln_g, m_conv_ln_b, m_w_conv_out, m_rel_bias, m_w_attn_out, m_pool_w, m_pool_b, m_pool_scale, m_w_pool_out, m_w_out, v_pre_norm_g, v_post_norm_g, v_w_in, v_conv_dw, v_conv_dw_b, v_conv_ln_g, v_conv_ln_b, v_w_conv_out, v_rel_bias, v_w_attn_out, v_pool_w, v_pool_b, v_pool_scale, v_w_pool_out, v_w_out):
    given = dict(locals())
    weights = {n: given[n] for n in WEIGHT_NAMES}
    nb, S, _ = x.shape
    T = nb * S
    L = pre_norm_g.shape[0]
    assert L == 2
    x2 = x.reshape(T, D)
    tgt2 = loss_target.reshape(T, D)
    skews = [_skew_table(rel_bias[l]) for l in range(L)]

    def local_params(l):
        return dict(pre_g=pre_norm_g[l:l + 1], post_g=post_norm_g[l:l + 1], conv_dw_b=conv_dw_b[l:l + 1],
                    conv_ln_g=conv_ln_g[l:l + 1], conv_ln_b=conv_ln_b[l:l + 1], pool_w=pool_w[l].astype(BF16),
                    pool_b=pool_b[l].reshape(1, BW), pool_scale=pool_scale[l:l + 1])

    win0 = w_in[0].T.astype(BF16)
    win1 = w_in[1].T.astype(BF16)
    half = win1.shape[0] // 2
    w_in_t0 = _gather_two_level(win0, "gather_w_in_0")
    gather = lambda arrays: _Exchange(arrays, [False] * len(arrays))
    (h,), saved0, P0, (got_rest0, got_a, got_b, got_rest1) = _layer_fwd(
        x2, {**local_params(0), "w_in_t": w_in_t0.reshape(NCOL, D)}, skews[0], S, _rest_weights,
        (gather(_rest_shards(weights, 0)), gather([win1[:half]]), gather([win1[half:]]),
         gather(_rest_shards(weights, 1))))
    w_in_t1 = jnp.concatenate([got_a[0], got_b[0]], axis=1).reshape(NCOL, D)
    (dout, lsum), saved1, P1, _ = _layer_fwd(h, {**local_params(1), "w_in_t": w_in_t1}, skews[1], S,
                                             lambda _: _rest_weights(got_rest1), (None,) * 4, tgt2)

    dout, g1, _ = _layer_bwd(dout, saved1, P1, skews[1], S)
    g1["loss"] = lsum
    others = REST + ("pre_norm_g",)
    late = ("w_in",) + tuple(n for n in REST if n not in OUT_PROJ)
    dout, g0, (got_others1, got_win1, got_outp0, got_late0) = _layer_bwd(
        dout, saved0, P0, skews[0], S, (_grad_exchange(g1, others + ("loss",)), _grad_exchange(g1, ("w_in",))),
        lambda g: _grad_exchange(g, OUT_PROJ), lambda g: _grad_exchange(g, late))
    (got_pre0,) = _exchange([g0["pre_norm_g"]], [False], "gather_grad_pre_norm_g_0")
    parts = [{"pre_norm_g": got_pre0, **dict(zip(OUT_PROJ, got_outp0)), **dict(zip(late, got_late0))},
             {"w_in": got_win1[0], **dict(zip(others + ("loss",), got_others1))}]
    loss = jnp.sum(parts[1].pop("loss")[:, 0, 0]) * (0.5 / D)
    grad_x = dout.reshape(x.shape)

    outs = {}
    small = [n for n in WEIGHT_NAMES if n not in SHARDED]
    res = _adamw_small([[parts[l][n] for n in small] for l in range(L)], [weights[n] for n in small],
                       [given["m_" + n] for n in small], [given["v_" + n] for n in small])
    outs.update(zip(small, res))
    for n in SHARDED:
        view = (lambda a: a.transpose(0, 2, 1)) if n == "w_in" else (lambda a: a)
        res = _adamw_sum(parts[0][n], parts[1][n], view(weights[n]), view(given["m_" + n]), view(given["v_" + n]),
                         "adamw_" + n)
        outs[n] = [view(a) for a in res]
    return (loss, grad_x, *[outs[n][0] for n in WEIGHT_NAMES], *[outs[n][1] for n in WEIGHT_NAMES],
            *[outs[n][2] for n in WEIGHT_NAMES], *[outs[n][3] for n in WEIGHT_NAMES])
```

```python
import functools

import numpy as np
import jax
import jax.numpy as jnp
from jax import lax
from jax.experimental import pallas as pl
from jax.experimental.pallas import tpu as pltpu

F32 = jnp.float32
BF16 = jnp.bfloat16
SDS = jax.ShapeDtypeStruct

D = 1024
BW = 512
NCOL = 7680
EPS = 1e-6
NEG = -1e30
HEADS = 8
HD = 64
CHUNK = 64
LEFT = 8
MAX_REL = 256
TQ = 256
KW = 768
CONV_K = 31
WINDOWS = (2, 4, 8, 16)
HALO = 32
RC = 32
N_DEV = 8

ADAM_LR = 0.001
ADAM_B1 = 0.9
ADAM_B2 = 0.999
ADAM_EPS = 1e-08
ADAM_WD = 0.01
ADAM_STEP = 10

VMEM_LIMIT = 56 * 1024 * 1024

CB_CA, CB_CB, CB_CG, CB_Q, CB_K, CB_V, CB_AG, CB_PI, CB_PG = range(9)


def _cparams(sem):
    return pltpu.CompilerParams(dimension_semantics=sem, vmem_limit_bytes=VMEM_LIMIT)


def _sig(x):
    return 0.5 * jnp.tanh(0.5 * x) + 0.5


def _dsilu(x, s):
    return s * (1.0 + x * (1.0 - s))


def _nt(a, b):
    return lax.dot_general(a, b, (((1,), (1,)), ((), ())), preferred_element_type=F32)


def _tn(a, b):
    return lax.dot_general(a, b, (((0,), (0,)), ((), ())), preferred_element_type=F32)


def _nn(a, b):
    return jnp.dot(a, b, preferred_element_type=F32)


def _rows8(x):
    return x[0:8] + x[8:16] + x[16:24] + x[24:32]


def _call(body, name, grid, in_specs, out_specs, out_shape, scratch, args, sem, ex=None, aliases=None):
    aliases = aliases or {}
    if ex is None:
        return pl.pallas_call(body, name=name, grid=grid, in_specs=in_specs, out_specs=out_specs,
                              out_shape=out_shape, scratch_shapes=scratch, input_output_aliases=aliases,
                              compiler_params=_cparams(sem))(*args)
    n_in, n_out = len(in_specs), len(out_specs)
    steps = int(np.prod(grid))

    def carrier(*refs):
        own_in, own_out, own_scr, exr = ex.split(refs, n_in, n_out)
        step = pl.program_id(0)
        for axis in range(1, len(grid)):
            step = step * grid[axis] + pl.program_id(axis)

        @pl.when(step == 0)
        def _():
            ex.start(exr)

        body(*own_in, *own_out, *own_scr)

        @pl.when(step == steps - 1)
        def _():
            ex.finish(exr)

    return pl.pallas_call(
        carrier, name=name + "_carrier", grid=grid, in_specs=in_specs + ex.in_specs,
        out_specs=out_specs + ex.out_specs, out_shape=out_shape + ex.out_shape,
        scratch_shapes=scratch + ex.scratch, input_output_aliases=aliases,
        compiler_params=_cparams(("arbitrary",) * len(grid)),
    )(*args, *ex.arrays)


def _in_proj(x, g, wt, ex=None):
    T = x.shape[0]
    tm = 512
    tn = 1536

    def body(x_ref, g_ref, w_ref, z_ref, ht_ref):
        xv = x_ref[...]
        r = lax.rsqrt(jnp.mean(xv * xv, axis=-1, keepdims=True) + EPS)
        h = xv * r * g_ref[...]
        hb = h.astype(BF16)
        ht_ref[...] = h.T.astype(BF16)
        for c in range(NCOL // tn):
            z_ref[:, c * tn:(c + 1) * tn] = _nt(hb, w_ref[c * tn:(c + 1) * tn, :]).astype(BF16)

    return _call(
        body, "in_proj", (T // tm,),
        [pl.BlockSpec((tm, D), lambda i: (i, 0)), pl.BlockSpec((1, D), lambda i: (0, 0)),
         pl.BlockSpec((NCOL, D), lambda i: (0, 0))],
        [pl.BlockSpec((tm, NCOL), lambda i: (i, 0)), pl.BlockSpec((D, tm), lambda i: (0, i))],
        [SDS((T, NCOL), BF16), SDS((D, T), BF16)], [],
        (x, g, wt), ("parallel",), ex)


def _stencil(load, w_ref, lo, hi, tap_of):
    out = None
    for r in range(8):
        n = RC if r == 0 else RC + 8
        v = None
        for q in range((hi - r) // 8 + 1):
            o = 8 * q + r
            if o < lo:
                continue
            j = tap_of(o)
            term = w_ref[j:j + 1, :] * load(q, n)
            v = term if v is None else v + term
        if v is None:
            continue
        if r:
            v = pltpu.roll(v, n - r, axis=0)[0:RC]
        out = v if out is None else out + v
    return out


def _layer_norm_fwd(u1):
    mu = jnp.mean(u1, axis=-1, keepdims=True)
    xc = u1 - mu
    rstd = lax.rsqrt(jnp.mean(xc * xc, axis=-1, keepdims=True) + EPS)
    return xc * rstd, rstd


def _window_sums(x, w, back):
    n = x.shape[0]
    s = x
    k = 1
    while k < w:
        s = s + pltpu.roll(s, k if back else n - k, axis=0)
        k *= 2
    return s


def _pool_chunk(pwin, t_first):
    t = t_first + lax.broadcasted_iota(jnp.int32, (RC, 128), 0)
    outs = []
    for g, w in enumerate(WINDOWS):
        x = pwin[:, g * 128:(g + 1) * 128]
        s = _window_sums(x, w, True)
        cnt = jnp.minimum(t + 1, w).astype(F32)
        outs.append(s[HALO:HALO + RC] / cnt - x[HALO:HALO + RC])
    return outs


def _mix_fwd(z, dw, dwb, lng, lnb, pw, pb, ps, S, ex=None):
    T = z.shape[0]
    tm = min(S, 1024)
    ts = S // tm
    nh = tm // HALO

    def body(ca_ref, cb_ref, cg_ref, pi_ref, pg_ref, cah_ref, cbh_ref, pih_ref,
             dw_ref, dwb_ref, lng_ref, lnb_ref, pw_ref, pb_ref, ps_ref,
             ac_ref, ap_ref, u1_ref, u0_ref, pooled, ubuf, pbuf):
        i = pl.program_id(0)
        keep = jnp.where((i % ts) == 0, 0.0, 1.0)
        ubuf[0:HALO, :] = cah_ref[...].astype(F32) * _sig(cbh_ref[...].astype(F32)) * keep
        u0 = ca_ref[...].astype(F32) * _sig(cb_ref[...].astype(F32))
        ubuf[HALO:HALO + tm, :] = u0
        u0_ref[...] = u0
        pbuf[0:HALO, :] = pih_ref[...].astype(F32) * keep
        pbuf[HALO:HALO + tm, :] = pi_ref[...].astype(F32)
        t0 = (i % ts) * tm

        def chunk(c, carry):
            base = pl.multiple_of(c * RC, RC)
            load = lambda q, n: ubuf[pl.ds(base + 8 * q, n), :]
            u1 = _stencil(load, dw_ref, 2, CONV_K + 1, lambda o: o - 2) + dwb_ref[...]
            u1_ref[pl.ds(base, RC), :] = u1
            n, _ = _layer_norm_fwd(u1)
            u2 = n * lng_ref[...] + lnb_ref[...]
            u3 = u2 * _sig(u2)
            cg = cg_ref[pl.ds(base, RC), :].astype(F32)
            ac_ref[pl.ds(base, RC), :] = (u3 * cg * _sig(cg)).astype(BF16)
            pwin = pbuf[pl.ds(base, RC + HALO), :]
            outs = _pool_chunk(pwin, t0 + base)
            for g in range(4):
                pooled[pl.ds(base, RC), g * 128:(g + 1) * 128] = outs[g].astype(BF16)
            return carry

        lax.fori_loop(0, tm // RC, chunk, 0, unroll=8)
        pg = pg_ref[...].astype(F32)
        spg = pg * _sig(pg)
        for g in range(4):
            sl = slice(g * 128, (g + 1) * 128)
            mixed = (_nn(pooled[:, sl], pw_ref[g]) + pb_ref[:, sl]) * ps_ref[:, sl]
            ap_ref[:, sl] = (mixed * spg[:, sl]).astype(BF16)

    def zmain(cb):
        return pl.BlockSpec((tm, BW), lambda i: (i, cb))

    def zprev(cb):
        return pl.BlockSpec((HALO, BW), lambda i: (jnp.maximum(i * nh - 1, 0), cb))

    full = lambda shape: pl.BlockSpec(shape, lambda i: (0,) * len(shape))
    row = pl.BlockSpec((tm, BW), lambda i: (i, 0))
    return _call(
        body, "mix_fwd", (T // tm,),
        [zmain(CB_CA), zmain(CB_CB), zmain(CB_CG), zmain(CB_PI), zmain(CB_PG),
         zprev(CB_CA), zprev(CB_CB), zprev(CB_PI),
         full((CONV_K, BW)), full((1, BW)), full((1, BW)), full((1, BW)),
         full((4, 128, 128)), full((1, BW)), full((1, BW))],
        [row] * 5, [SDS((T, BW), BF16), SDS((T, BW), BF16), SDS((T, BW), F32), SDS((T, BW), F32), SDS((T, BW), BF16)],
        [pltpu.VMEM((HALO + tm, BW), F32), pltpu.VMEM((HALO + tm, BW), F32)],
        (z, z, z, z, z, z, z, z, dw, dwb, lng, lnb, pw, pb, ps), ("parallel",), ex)


def _attn_specs(nq):
    def kv(cb, off):
        return pl.BlockSpec((TQ, BW), lambda i: (i - jnp.minimum(off, i % nq), cb))
    return [pl.BlockSpec((TQ, BW), lambda i: (i, CB_Q)),
            kv(CB_K, 2), kv(CB_K, 1), kv(CB_K, 0), kv(CB_V, 2), kv(CB_V, 1), kv(CB_V, 0)]


NSKEW = 1024


def _skew_table(table):
    return jnp.dot(table, jnp.asarray(_skew_select()), precision=lax.Precision.HIGHEST)


def _skew_select():
    d = np.arange(TQ + KW - 1)
    idx = np.clip(3 * TQ - 1 - d, -MAX_REL, MAX_REL) + MAX_REL
    sel = np.zeros((2 * MAX_REL + 1, NSKEW), np.float32)
    sel[idx, d] = 1.0
    return sel


def _bias_from_skew(f_ref, bias_scr):
    qi = lax.broadcasted_iota(jnp.int32, (TQ, KW), 0)
    kj = lax.broadcasted_iota(jnp.int32, (TQ, KW), 1)
    lo = (qi // CHUNK) * CHUNK
    band = jnp.where((kj >= lo) & (kj < lo + (LEFT + 1) * CHUNK), 0.0, NEG)
    for h in range(HEADS):
        rows = jnp.broadcast_to(f_ref[h:h + 1, :], (TQ, NSKEW))
        rows = pltpu.roll(rows, NSKEW - (TQ - 1), axis=1, stride=1, stride_axis=0)
        bias_scr[h] = rows[:, 0:KW] + band


def _skew_from_bias(db):
    i = lax.broadcasted_iota(jnp.int32, (TQ, TQ), 0)
    j = lax.broadcasted_iota(jnp.int32, (TQ, TQ), 1)
    flip = jnp.where(i + j == TQ - 1, 1.0, 0.0).astype(BF16)
    hi = db.astype(BF16)
    lo = (db - hi.astype(F32)).astype(BF16)
    rev = _nn(flip, hi) + _nn(flip, lo)
    rev = jnp.concatenate([rev, jnp.zeros((TQ, NSKEW - KW), F32)], axis=1)
    return jnp.sum(pltpu.roll(rev, 0, axis=1, stride=1, stride_axis=0), axis=0, keepdims=True)


def _attn_fwd(z, f, S, ex=None):
    T = z.shape[0]
    nq = S // TQ

    def body(q_ref, k2_ref, k1_ref, k0_ref, v2_ref, v1_ref, v0_ref, f_ref, o_ref, kbuf, vbuf, b_scr):
        @pl.when(pl.program_id(0) == 0)
        def _():
            _bias_from_skew(f_ref, b_scr)

        qb = pl.program_id(0) % nq
        kbuf[0:TQ, :] = k2_ref[...]
        kbuf[TQ:2 * TQ, :] = k1_ref[...]
        kbuf[2 * TQ:KW, :] = k0_ref[...]
        vbuf[0:TQ, :] = v2_ref[...]
        vbuf[TQ:2 * TQ, :] = v1_ref[...]
        vbuf[2 * TQ:KW, :] = v0_ref[...]
        lane = lax.broadcasted_iota(jnp.int32, (1, 128), 1)

        def attend(lo):
            def scores(h):
                sl = slice((h // 2) * 128, (h // 2 + 1) * 128)
                qp = q_ref[:, sl] * 0.125
                qm = jnp.where((lane < HD) if h % 2 == 0 else (lane >= HD), qp, jnp.zeros_like(qp))
                return _nt(qm, kbuf[lo:KW, sl]) + b_scr[h, :, lo:KW]

            s = scores(0)
            acc = None
            for h in range(HEADS):
                s_next = scores(h + 1) if h + 1 < HEADS else None
                sl = slice((h // 2) * 128, (h // 2 + 1) * 128)
                e = jnp.exp(s - jnp.max(s, axis=-1, keepdims=True))
                vp = vbuf[lo:KW, sl]
                vm = jnp.where((lane < HD) if h % 2 == 0 else (lane >= HD), vp, jnp.zeros_like(vp))
                o = _nn(e.astype(BF16), vm) * (1.0 / jnp.sum(e, axis=-1, keepdims=True))
                acc = o if h % 2 == 0 else acc + o
                if h % 2 == 1:
                    o_ref[:, sl] = acc.astype(BF16)
                s = s_next

        for nblk in (1, 2, 3):
            pl.when(jnp.minimum(qb, 2) == nblk - 1)(functools.partial(attend, (3 - nblk) * TQ))

    full = lambda shape: pl.BlockSpec(shape, lambda i: (0,) * len(shape))
    return _call(
        body, "attn_fwd", (T // TQ,),
        _attn_specs(nq) + [full((HEADS, NSKEW))],
        [pl.BlockSpec((TQ, BW), lambda i: (i, 0))], [SDS((T, BW), BF16)],
        [pltpu.VMEM((KW, BW), BF16), pltpu.VMEM((KW, BW), BF16), pltpu.VMEM((HEADS, TQ, KW), F32)],
        (z, z, z, z, z, z, z, f), ("arbitrary",), ex)


def _gates(gl_ref, gh_ref):
    gl = _sig(gl_ref[...].astype(F32))
    gh = _sig(gh_ref[...].astype(F32))
    return (gl[:, 0:D], jnp.concatenate([gl[:, D:1536], gh[:, 0:512]], axis=1), gh[:, 512:1536])


def _out_specs_in(tm):
    row = lambda w: pl.BlockSpec((tm, w), lambda i: (i, 0))
    full = lambda shape: pl.BlockSpec(shape, lambda i: (0,) * len(shape))
    return [row(BW), row(BW), row(BW),
            pl.BlockSpec((tm, BW), lambda i: (i, CB_AG)),
            pl.BlockSpec((tm, 1536), lambda i: (i, 3)),
            pl.BlockSpec((tm, 1536), lambda i: (i, 4)),
            full((BW, D)), full((BW, D)), full((BW, D)), full((D, D)), full((1, D))]


def _out_fwd(x, ac, o, ap, z, wco, wao, wpo, wout, postg, ex=None, tgt=None):
    T = x.shape[0]
    tm = 512
    last = tgt is not None

    def body(ac_ref, o_ref, ap_ref, ag_ref, gl_ref, gh_ref, wco_ref, wao_ref, wpo_ref, wout_ref, pg_ref,
             x_ref, *rest):
        ag = ag_ref[...].astype(F32)
        aat = (o_ref[...].astype(F32) * ag * _sig(ag)).astype(BF16)
        gates = _gates(gl_ref, gh_ref)
        acts = (ac_ref[...], aat, ap_ref[...])
        merged = None
        for b, w_ref in enumerate((wco_ref, wao_ref, wpo_ref)):
            yb = _nn(acts[b], w_ref[...])
            rest[-4 + b][...] = yb.astype(BF16)
            merged = gates[b] * yb if merged is None else merged + gates[b] * yb
        y = _nn(merged.astype(BF16), wout_ref[...])
        rest[-1][...] = y.astype(BF16)
        ry = lax.rsqrt(jnp.mean(y * y, axis=-1, keepdims=True) + EPS)
        out = x_ref[...] + y * ry * pg_ref[...]
        if not last:
            rest[0][...] = out
            return
        t_ref, d_ref, l_ref = rest[:3]

        @pl.when(pl.program_id(0) == 0)
        def _():
            l_ref[...] = jnp.zeros_like(l_ref)
        d = out - t_ref[...]
        d_ref[...] = d * (1.0 / D)
        l_ref[...] += jnp.sum(jnp.sum(d * d, axis=0, keepdims=True), axis=1, keepdims=True)

    row = pl.BlockSpec((tm, D), lambda i: (i, 0))
    kept_specs, kept_shapes = [row] * 4, [SDS((T, D), BF16)] * 4
    if not last:
        res = _call(body, "out_fwd", (T // tm,), _out_specs_in(tm) + [row], [row] + kept_specs,
                    [SDS((T, D), F32)] + kept_shapes, [],
                    (ac, o, ap, z, z, z, wco, wao, wpo, wout, postg, x), ("parallel",), ex)
        return res[:1], res[1:5], res[5:]
    res = _call(body, "out_fwd_loss", (T // tm,), _out_specs_in(tm) + [row, row],
                [row, pl.BlockSpec((1, 128), lambda i: (0, 0))] + kept_specs,
                [SDS((T, D), F32), SDS((1, 128), F32)] + kept_shapes, [],
                (ac, o, ap, z, z, z, wco, wao, wpo, wout, postg, x, tgt), ("arbitrary",), ex)
    return res[:2], res[2:6], res[6:]


def _out_bwd(dout, ac, o, ap, z, kept, wco, wao, wpo, wout, postg, ex=None):
    T = dout.shape[0]
    tm = 256

    def body(ac_ref, o_ref, ap_ref, ag_ref, gl_ref, gh_ref, wco_ref, wao_ref, wpo_ref, wout_ref, pg_ref, do_ref,
             yc_ref, ya_ref, yp_ref, y_ref,
             dac_ref, dao_ref, dag_ref, dap_ref, dgm_ref, dwco_ref, dwao_ref, dwpo_ref, dwout_ref, dpg_ref):
        @pl.when(pl.program_id(0) == 0)
        def _():
            for r in (dwco_ref, dwao_ref, dwpo_ref, dwout_ref, dpg_ref):
                r[...] = jnp.zeros_like(r)

        ag = ag_ref[...].astype(F32)
        sag = _sig(ag)
        ov = o_ref[...].astype(F32)
        acts = (ac_ref[...], (ov * ag * sag).astype(BF16), ap_ref[...])
        ws = (wco_ref, wao_ref, wpo_ref)
        gates = _gates(gl_ref, gh_ref)
        ys = [r[...].astype(F32) for r in (yc_ref, ya_ref, yp_ref)]
        merged = (gates[0] * ys[0] + gates[1] * ys[1] + gates[2] * ys[2]).astype(BF16)
        y = y_ref[...].astype(F32)
        ry = lax.rsqrt(jnp.mean(y * y, axis=-1, keepdims=True) + EPS)
        yn = y * ry
        dout_v = do_ref[...]
        dpg_ref[...] += jnp.sum(dout_v * yn, axis=0, keepdims=True)
        dyn = dout_v * pg_ref[...]
        dy = (ry * (dyn - yn * jnp.mean(dyn * yn, axis=-1, keepdims=True))).astype(BF16)
        dmerged = _nt(dy, wout_ref[...])
        dwout_ref[...] += _tn(merged, dy)
        dws = (dwco_ref, dwao_ref, dwpo_ref)
        das = []
        for b in range(3):
            gb = gates[b]
            dgm_ref[:, b * D:(b + 1) * D] = (dmerged * ys[b] * gb * (1.0 - gb)).astype(BF16)
            dyb = (dmerged * gb).astype(BF16)
            dws[b][...] += _tn(acts[b], dyb)
            das.append(_nt(dyb, ws[b][...]))
        dac_ref[...] = das[0].astype(BF16)
        dap_ref[...] = das[2].astype(BF16)
        dao_ref[...] = (das[1] * ag * sag).astype(BF16)
        dag_ref[...] = (das[1] * ov * _dsilu(ag, sag)).astype(BF16)

    row = lambda w: pl.BlockSpec((tm, w), lambda i: (i, 0))
    full = lambda shape: pl.BlockSpec(shape, lambda i: (0,) * len(shape))
    return _call(
        body, "out_bwd", (T // tm,), _out_specs_in(tm) + [row(D)] * 5,
        [row(BW), row(BW), row(BW), row(BW), row(3 * D),
         full((BW, D)), full((BW, D)), full((BW, D)), full((D, D)), full((1, D))],
        [SDS((T, BW), BF16)] * 4 + [SDS((T, 3 * D), BF16)]
        + [SDS((BW, D), F32)] * 3 + [SDS((D, D), F32), SDS((1, D), F32)], [],
        (ac, o, ap, z, z, z, wco, wao, wpo, wout, postg, dout, *kept), ("arbitrary",), ex)


def _attn_bwd(z, dao, f, S, ex=None):
    T = z.shape[0]
    nq = S // TQ
    nsteps = T // TQ

    def body(q_ref, k2_ref, k1_ref, k0_ref, v2_ref, v1_ref, v0_ref, do_ref, f_ref,
             dq_ref, dk_ref, dv_ref, df_ref, kbuf, vbuf, dkacc, dvacc, b_scr, db_scr):
        i = pl.program_id(0)
        qb = i % nq

        @pl.when(i == 0)
        def _():
            _bias_from_skew(f_ref, b_scr)
            db_scr[...] = jnp.zeros_like(db_scr)

        @pl.when(qb == 0)
        def _():
            dkacc[...] = jnp.zeros_like(dkacc)
            dvacc[...] = jnp.zeros_like(dvacc)

        kbuf[0:TQ, :] = k2_ref[...]
        kbuf[TQ:2 * TQ, :] = k1_ref[...]
        kbuf[2 * TQ:KW, :] = k0_ref[...]
        vbuf[0:TQ, :] = v2_ref[...]
        vbuf[TQ:2 * TQ, :] = v1_ref[...]
        vbuf[2 * TQ:KW, :] = v0_ref[...]
        lane = lax.broadcasted_iota(jnp.int32, (1, 128), 1)
        row0 = pl.multiple_of(qb * TQ, TQ)

        def attend(lo):
            def first_matmuls(h):
                sl = slice((h // 2) * 128, (h // 2 + 1) * 128)
                msk = (lane < HD) if h % 2 == 0 else (lane >= HD)
                qp = q_ref[:, sl] * 0.125
                dop = do_ref[:, sl]
                qm = jnp.where(msk, qp, jnp.zeros_like(qp))
                dom = jnp.where(msk, dop, jnp.zeros_like(dop))
                s = _nt(qm, kbuf[lo:KW, sl]) + b_scr[h, :, lo:KW]
                return s, _nt(dom, vbuf[lo:KW, sl]), qm, dom

            cur = first_matmuls(0)
            dq_acc = dk_acc = dv_acc = None
            for h in range(HEADS):
                nxt = first_matmuls(h + 1) if h + 1 < HEADS else None
                s, dp, qm, dom = cur
                sl = slice((h // 2) * 128, (h // 2 + 1) * 128)
                e = jnp.exp(s - jnp.max(s, axis=-1, keepdims=True))
                p = e * (1.0 / jnp.sum(e, axis=-1, keepdims=True))
                ds = p * (dp - jnp.sum(p * dp, axis=-1, keepdims=True))
                db_scr[h, :, lo:KW] += ds
                dsb = ds.astype(BF16)
                kp = kbuf[lo:KW, sl]
                km = jnp.where((lane < HD) if h % 2 == 0 else (lane >= HD), kp, jnp.zeros_like(kp))
                dq_h = _nn(dsb, km) * 0.125
                dk_h = _tn(dsb, qm)
                dv_h = _tn(p.astype(BF16), dom)
                if h % 2 == 0:
                    dq_acc, dk_acc, dv_acc = dq_h, dk_h, dv_h
                else:
                    dq_ref[:, sl] = (dq_acc + dq_h).astype(BF16)
                    dkacc[pl.ds(row0 + lo, KW - lo), sl] += dk_acc + dk_h
                    dvacc[pl.ds(row0 + lo, KW - lo), sl] += dv_acc + dv_h
                cur = nxt

        for nblk in (1, 2, 3):
            pl.when(jnp.minimum(qb, 2) == nblk - 1)(functools.partial(attend, (3 - nblk) * TQ))

        @pl.when(qb == nq - 1)
        def _():
            dk_ref[...] = dkacc[2 * TQ:2 * TQ + S, :].astype(BF16)
            dv_ref[...] = dvacc[2 * TQ:2 * TQ + S, :].astype(BF16)

        @pl.when(i == nsteps - 1)
        def _():
            for h in range(HEADS):
                df_ref[h:h + 1, :] = _skew_from_bias(db_scr[h])

    full = lambda shape: pl.BlockSpec(shape, lambda i: (0,) * len(shape))
    return _call(
        body, "attn_bwd", (nsteps,),
        _attn_specs(nq) + [pl.BlockSpec((TQ, BW), lambda i: (i, 0)), full((HEADS, NSKEW))],
        [pl.BlockSpec((TQ, BW), lambda i: (i, 0)), pl.BlockSpec((S, BW), lambda i: (i // nq, 0)),
         pl.BlockSpec((S, BW), lambda i: (i // nq, 0)), full((HEADS, NSKEW))],
        [SDS((T, BW), BF16)] * 3 + [SDS((HEADS, NSKEW), F32)],
        [pltpu.VMEM((KW, BW), BF16), pltpu.VMEM((KW, BW), BF16),
         pltpu.VMEM((S + 2 * TQ, BW), F32), pltpu.VMEM((S + 2 * TQ, BW), F32),
         pltpu.VMEM((HEADS, TQ, KW), F32), pltpu.VMEM((HEADS, TQ, KW), F32)],
        (z, z, z, z, z, z, z, dao, f), ("arbitrary",), ex)


def _mix_bwd(z, u1, u0, pooled_kept, dac, dap, dw, dwb, lng, lnb, pw, pb, ps, S, ex=None):
    T = z.shape[0]
    tm = min(S, 1024)
    ts = S // tm
    nh = tm // HALO
    nsteps = T // tm
    nblk32 = T // HALO

    def body(ca_ref, cb_ref, cg_ref, pg_ref, u1_ref, u0_ref, pooled, dac_ref, dap_ref,
             u0h_ref,
             cgn_ref, pgn_ref, u1n_ref, dacn_ref, dapn_ref,
             dw_ref, dwb_ref, lng_ref, lnb_ref, pw_ref, pb_ref, ps_ref,
             dzc_ref, dzp_ref, ddw_ref, ddwb_ref, dlng_ref, dlnb_ref, dpw_ref, dpb_ref, dps_ref,
             ubuf, gbuf, qbuf, *accs):
        tap_acc, (lng_acc, lnb_acc, dwb_acc) = accs[:CONV_K], accs[CONV_K:]
        i = pl.program_id(0)
        keep_prev = jnp.where((i % ts) == 0, 0.0, 1.0)
        keep_next = jnp.where((i % ts) == ts - 1, 0.0, 1.0)
        t0 = (i % ts) * tm

        @pl.when(i == 0)
        def _():
            for a in accs:
                a[...] = jnp.zeros_like(a)
            dpw_ref[...] = jnp.zeros_like(dpw_ref)
            dpb_ref[...] = jnp.zeros_like(dpb_ref)
            dps_ref[...] = jnp.zeros_like(dps_ref)

        ubuf[0:HALO, :] = u0h_ref[...] * keep_prev
        ubuf[HALO:HALO + tm, :] = u0_ref[...]

        def norm_back(u1v, cg, dacv):
            n, rstd = _layer_norm_fwd(u1v)
            u2 = n * lng_ref[...] + lnb_ref[...]
            s2 = _sig(u2)
            scg = _sig(cg)
            du2 = dacv * cg * scg * _dsilu(u2, s2)
            dn = du2 * lng_ref[...]
            du1 = rstd * (dn - jnp.mean(dn, axis=-1, keepdims=True)
                          - n * jnp.mean(dn * n, axis=-1, keepdims=True))
            return du1, du2, n, dacv * u2 * s2 * _dsilu(cg, scg)

        def chunk_a(c, carry):
            base = pl.multiple_of(c * RC, RC)
            du1, du2, n, dcg = norm_back(u1_ref[pl.ds(base, RC), :], cg_ref[pl.ds(base, RC), :].astype(F32),
                                         dac_ref[pl.ds(base, RC), :].astype(F32))
            gbuf[pl.ds(base, RC), :] = du1
            dzc_ref[pl.ds(base, RC), 2 * BW:3 * BW] = dcg.astype(BF16)
            lng_acc[...] += _rows8(du2 * n)
            lnb_acc[...] += _rows8(du2)
            dwb_acc[...] += _rows8(du1)
            padded = jnp.concatenate([du1, jnp.zeros((8, BW), F32)], axis=0)
            for r in range(8):
                nrow = RC if r == 0 else RC + 8
                g = du1 if r == 0 else pltpu.roll(padded, r, axis=0)
                for q in range((CONV_K + 1 - r) // 8 + 1):
                    o = 8 * q + r
                    if o < 2:
                        continue
                    prod = g * ubuf[pl.ds(base + 8 * q, nrow), :]
                    red = prod[0:8]
                    for k in range(1, nrow // 8):
                        red = red + prod[8 * k:8 * k + 8]
                    tap_acc[o - 2][...] += red
            return carry

        lax.fori_loop(0, tm // RC, chunk_a, 0, unroll=8)
        du1n, _, _, _ = norm_back(u1n_ref[...], cgn_ref[...].astype(F32), dacn_ref[...].astype(F32))
        gbuf[tm:tm + HALO, :] = du1n * keep_next

        def cnt_of(t_first, rows, w):
            t = t_first + lax.broadcasted_iota(jnp.int32, (rows, 128), 0)
            return jnp.minimum(t + 1, w).astype(F32)

        pg = pg_ref[...].astype(F32)
        spg_s = _sig(pg)
        dapv = dap_ref[...].astype(F32)
        pgn = pgn_ref[...].astype(F32)
        dmixn = dapn_ref[...].astype(F32) * pgn * _sig(pgn) * ps_ref[...] * keep_next
        for g, w in enumerate(WINDOWS):
            sl = slice(g * 128, (g + 1) * 128)
            mixed_u = _nn(pooled[:, sl], pw_ref[g]) + pb_ref[:, sl]
            dap_g = dapv[:, sl]
            pg_g = pg[:, sl]
            s_g = spg_s[:, sl]
            silu_g = pg_g * s_g
            dps_ref[:, sl] += jnp.sum(dap_g * silu_g * mixed_u, axis=0, keepdims=True)
            dzp_ref[:, BW + g * 128:BW + (g + 1) * 128] = (
                dap_g * mixed_u * ps_ref[:, sl] * _dsilu(pg_g, s_g)).astype(BF16)
            dmix = dap_g * silu_g * ps_ref[:, sl]
            dpb_ref[:, sl] += jnp.sum(dmix, axis=0, keepdims=True)
            dmixb = dmix.astype(BF16)
            dpw_ref[g] += _tn(pooled[:, sl], dmixb)
            qbuf[0:tm, sl] = _nt(dmixb, pw_ref[g]) / cnt_of(t0, tm, w)
            qbuf[tm:tm + HALO, sl] = _nt(dmixn[:, sl].astype(BF16), pw_ref[g]) / cnt_of(t0 + tm, HALO, w)

        def chunk_b(c, carry):
            base = pl.multiple_of(c * RC, RC)
            load = lambda q, n: gbuf[pl.ds(base + 8 * q, n), :]
            du0 = _stencil(load, dw_ref, 0, CONV_K - 1, lambda o: CONV_K - 1 - o)
            ca = ca_ref[pl.ds(base, RC), :].astype(F32)
            sb = _sig(cb_ref[pl.ds(base, RC), :].astype(F32))
            dzc_ref[pl.ds(base, RC), 0:BW] = (du0 * sb).astype(BF16)
            dzc_ref[pl.ds(base, RC), BW:2 * BW] = (du0 * ca * sb * (1.0 - sb)).astype(BF16)
            qwin = qbuf[pl.ds(base, RC + HALO), :]
            t = t0 + base + lax.broadcasted_iota(jnp.int32, (RC, 128), 0)
            for g, w in enumerate(WINDOWS):
                x = qwin[:, g * 128:(g + 1) * 128]
                s = _window_sums(x, w, False)
                cnt = jnp.minimum(t + 1, w).astype(F32)
                dzp_ref[pl.ds(base, RC), g * 128:(g + 1) * 128] = (s[0:RC] - cnt * x[0:RC]).astype(BF16)
            return carry

        lax.fori_loop(0, tm // RC, chunk_b, 0)

        @pl.when(i == nsteps - 1)
        def _():
            dlng_ref[...] = jnp.sum(lng_acc[...], axis=0, keepdims=True)
            dlnb_ref[...] = jnp.sum(lnb_acc[...], axis=0, keepdims=True)
            ddwb_ref[...] = jnp.sum(dwb_acc[...], axis=0, keepdims=True)
            for j in range(CONV_K):
                ddw_ref[j:j + 1, :] = jnp.sum(tap_acc[j][...], axis=0, keepdims=True)

    def zmain(cb):
        return pl.BlockSpec((tm, BW), lambda i: (i, cb))

    def zprev(cb):
        return pl.BlockSpec((HALO, BW), lambda i: (jnp.maximum(i * nh - 1, 0), cb))

    def znext(cb):
        return pl.BlockSpec((HALO, BW), lambda i: (jnp.minimum((i + 1) * nh, nblk32 - 1), cb))

    row = lambda w: pl.BlockSpec((tm, w), lambda i: (i, 0))
    full = lambda shape: pl.BlockSpec(shape, lambda i: (0,) * len(shape))
    return _call(
        body, "mix_bwd", (nsteps,),
        [zmain(CB_CA), zmain(CB_CB), zmain(CB_CG), zmain(CB_PG), row(BW), row(BW), row(BW), row(BW), row(BW),
         zprev(0),
         znext(CB_CG), znext(CB_PG), znext(0), znext(0), znext(0),
         full((CONV_K, BW)), full((1, BW)), full((1, BW)), full((1, BW)),
         full((4, 128, 128)), full((1, BW)), full((1, BW))],
        [row(3 * BW), row(2 * BW), full((CONV_K, BW)), full((1, BW)), full((1, BW)), full((1, BW)),
         full((4, 128, 128)), full((1, BW)), full((1, BW))],
        [SDS((T, 3 * BW), BF16), SDS((T, 2 * BW), BF16), SDS((CONV_K, BW), F32),
         SDS((1, BW), F32), SDS((1, BW), F32), SDS((1, BW), F32),
         SDS((4, 128, 128), F32), SDS((1, BW), F32), SDS((1, BW), F32)],
        [pltpu.VMEM((HALO + tm, BW), F32), pltpu.VMEM((tm + HALO, BW), F32), pltpu.VMEM((tm + HALO, BW), F32)]
        + [pltpu.VMEM((8, BW), F32)] * (CONV_K + 3),
        (z, z, z, z, u1, u0, pooled_kept, dac, dap, u0, z, z, u1, dac, dap, dw, dwb, lng, lnb, pw, pb, ps),
        ("arbitrary",), ex)


def _in_bwd_x(pieces, wt, x, g, dout, ex=None):
    T = x.shape[0]
    tm = 512
    widths = [p.shape[1] for p in pieces]
    offs = np.cumsum([0] + widths)
    npc = len(pieces)

    def body(*refs):
        p_refs = refs[:npc]
        w_ref, x_ref, g_ref, do_ref, dx_ref, dg_ref = refs[npc:]

        @pl.when(pl.program_id(0) == 0)
        def _():
            dg_ref[...] = jnp.zeros_like(dg_ref)

        dh = None
        for k in range(npc):
            t = _nn(p_refs[k][...], w_ref[int(offs[k]):int(offs[k + 1]), :])
            dh = t if dh is None else dh + t
        xv = x_ref[...]
        r = lax.rsqrt(jnp.mean(xv * xv, axis=-1, keepdims=True) + EPS)
        xn = xv * r
        dg_ref[...] += jnp.sum(dh * xn, axis=0, keepdims=True)
        dxn = dh * g_ref[...]
        dx_ref[...] = do_ref[...] + r * (dxn - xn * jnp.mean(dxn * xn, axis=-1, keepdims=True))

    row = lambda wd: pl.BlockSpec((tm, wd), lambda i: (i, 0))
    return _call(
        body, "in_bwd_x", (T // tm,),
        [row(wd) for wd in widths] + [pl.BlockSpec((NCOL, D), lambda i: (0, 0)),
                                      row(D), pl.BlockSpec((1, D), lambda i: (0, 0)), row(D)],
        [row(D), pl.BlockSpec((1, D), lambda i: (0, 0))], [SDS((T, D), F32), SDS((1, D), F32)], [],
        (*pieces, wt, x, g, dout), ("arbitrary",), ex)


def _in_bwd_w(ht, pieces, row0, buf=None, ex=None):
    T = ht.shape[1]
    n = len(pieces)
    wd = pieces[0].shape[1]
    tn = next(t for t in (1536, 1024, 768, 512) if wd % t == 0 and row0 % t == 0)
    per = wd // tn
    tk = min(T, 2048)
    nk = T // tk
    j0 = row0 // tn

    def body(ht_ref, *rest):
        p_refs = rest[:n]
        o_ref, acc = rest[-2:]
        j, k = pl.program_id(0), pl.program_id(1)

        @pl.when(k == 0)
        def _():
            acc[...] = jnp.zeros_like(acc)

        for p in range(n):
            @pl.when(j // per == p)
            def _(p=p):
                acc[...] += _nn(ht_ref[...], p_refs[p][...])

        @pl.when(k == nk - 1)
        def _():
            o_ref[...] = acc[...].T.astype(BF16)

    def piece_spec(p):
        return pl.BlockSpec((tk, tn), lambda j, k: (jnp.where(j // per == p, k, 0), jnp.where(j // per == p, j % per, 0)))

    in_specs = [pl.BlockSpec((D, tk), lambda j, k: (0, k))] + [piece_spec(p) for p in range(n)]
    args = (ht, *pieces)
    if buf is not None:
        in_specs.append(pl.BlockSpec(memory_space=pl.ANY))
        args += (buf,)
    return _call(
        body, "in_bwd_w", (n * per, nk), in_specs,
        [pl.BlockSpec((tn, D), lambda j, k: (j + j0, 0))], [SDS((NCOL, D), BF16)], [pltpu.VMEM((D, tn), F32)],
        args, ("arbitrary", "arbitrary"), ex, None if buf is None else {n + 1: 0})


def _my_id():
    return 4 * lax.axis_index("x") + 2 * lax.axis_index("y") + lax.axis_index("c")


def _peers():
    x, y, c = lax.axis_index("x"), lax.axis_index("y"), lax.axis_index("c")
    out = []
    for k in range(1, N_DEV):
        fx, fy, fc = (k >> 2) & 1, (k >> 1) & 1, k & 1
        px, py, pc = x ^ fx, y ^ fy, c ^ fc
        out.append(((px, py, pc), 4 * px + 2 * py + pc))
    return out


class _Exchange:
    def __init__(self, arrays, scatter):
        self.arrays = list(arrays)
        self.scatter = list(scatter)
        self.n = n = len(arrays)
        hbm = pl.BlockSpec(memory_space=pltpu.HBM)
        self.in_specs = [hbm] * n
        self.out_specs = [hbm] * n
        self.out_shape = [SDS((N_DEV,) + tuple(a.shape[1:] if s else a.shape), a.dtype)
                          for a, s in zip(arrays, scatter)]
        self.scratch = [pltpu.SemaphoreType.DMA((N_DEV - 1, n)), pltpu.SemaphoreType.DMA((N_DEV - 1, n)),
                        pltpu.SemaphoreType.DMA((n,))]

    def split(self, refs, n_in, n_out):
        n = self.n
        own_in = refs[:n_in]
        ex_in = refs[n_in:n_in + n]
        own_out = refs[n_in + n:n_in + n + n_out]
        ex_out = refs[n_in + n + n_out:n_in + 2 * n + n_out]
        rest = refs[n_in + 2 * n + n_out:]
        return own_in, own_out, rest[:-3], (ex_in, ex_out, rest[-3:])

    def _copy(self, ex, k, p, landing):
        in_refs, out_refs, (send_sems, recv_sems, _) = ex
        pos, pid = _peers()[p]
        return pltpu.make_async_remote_copy(
            src_ref=in_refs[k].at[pid] if self.scatter[k] else in_refs[k],
            dst_ref=out_refs[k].at[pid if landing else _my_id()],
            send_sem=send_sems.at[p, k], recv_sem=recv_sems.at[p, k],
            device_id=pos, device_id_type=pl.DeviceIdType.MESH)

    def _own(self, ex, k):
        in_refs, out_refs, (_, _, local_sems) = ex
        me = _my_id()
        return pltpu.make_async_copy(in_refs[k].at[me] if self.scatter[k] else in_refs[k], out_refs[k].at[me],
                                     local_sems.at[k])

    def start(self, ex):
        for k in range(self.n):
            self._own(ex, k).start()
        for p in range(N_DEV - 1):
            for k in range(self.n):
                self._copy(ex, k, p, False).start()

    def finish(self, ex):
        for p in range(N_DEV - 1):
            for k in range(self.n):
                self._copy(ex, k, p, True).wait_recv()
        for p in range(N_DEV - 1):
            for k in range(self.n):
                self._copy(ex, k, p, False).wait_send()
        for k in range(self.n):
            self._own(ex, k).wait()


def _exchange(arrays, scatter, name):
    ex = _Exchange(arrays, scatter)

    def body(*refs):
        _, _, _, exr = ex.split(refs, 0, 0)
        ex.start(exr)
        ex.finish(exr)

    return pl.pallas_call(body, name=name, in_specs=ex.in_specs, out_specs=ex.out_specs,
                          out_shape=ex.out_shape, scratch_shapes=ex.scratch)(*ex.arrays)


def _gather_two_level(shard, name):
    def body(x_ref, out_ref, send_sems, recv_sems, local_sem):
        x, y, c = lax.axis_index("x"), lax.axis_index("y"), lax.axis_index("c")
        me, sibling = (x, y, c), (x, y, 1 - c)
        chips = [(1 - x, y), (x, 1 - y), (1 - x, 1 - y)]
        rows = shard.shape[0] // 2

        def half(ref, h):
            return ref.at[pl.ds(h * rows, rows)]

        slab = lambda px, py, pc: out_ref.at[4 * px + 2 * py + pc]

        def copy(k, h, block, to, src=None):
            return pltpu.make_async_remote_copy(
                src_ref=half(slab(*block) if src is None else src, h), dst_ref=half(slab(*block), h),
                send_sem=send_sems.at[k, h], recv_sem=recv_sems.at[k, h],
                device_id=to, device_id_type=pl.DeviceIdType.MESH)

        mine = pltpu.make_async_copy(x_ref, slab(*me), local_sem)
        mine.start()
        first = [cp for h in range(2)
                 for cp in [copy(1 + j, h, me, (*chip, c), src=x_ref) for j, chip in enumerate(chips)]
                 + [copy(0, h, me, sibling, src=x_ref)]]
        for cp in first:
            cp.start()
        passed = []
        for h in range(2):
            for j, chip in enumerate(chips):
                copy(1 + j, h, (*chip, c), me).wait_recv()
                passed.append(copy(4 + j, h, (*chip, c), sibling))
                passed[-1].start()
        for h in range(2):
            copy(0, h, sibling, me).wait_recv()
            for j, chip in enumerate(chips):
                copy(4 + j, h, (*chip, 1 - c), me).wait_recv()
        for cp in first + passed:
            cp.wait_send()
        mine.wait()

    hbm = pl.BlockSpec(memory_space=pltpu.HBM)
    return pl.pallas_call(
        body, name=name, in_specs=[hbm], out_specs=hbm,
        out_shape=SDS((N_DEV,) + shard.shape, shard.dtype),
        scratch_shapes=[pltpu.SemaphoreType.DMA((N_DEV - 1, 2)), pltpu.SemaphoreType.DMA((N_DEV - 1, 2)),
                        pltpu.SemaphoreType.DMA],
    )(shard)


def _adamw_update(g, w, m, v):
    c1 = 1.0 / (1.0 - ADAM_B1 ** ADAM_STEP)
    c2 = 1.0 / (1.0 - ADAM_B2 ** ADAM_STEP)
    mn = ADAM_B1 * m + (1.0 - ADAM_B1) * g
    vn = ADAM_B2 * v + (1.0 - ADAM_B2) * (g * g)
    return -ADAM_LR * ((mn * c1) / (jnp.sqrt(vn * c2) + ADAM_EPS) + ADAM_WD * w), mn, vn


def _adamw_small(parts, w, m, v):
    n = len(w)

    def body(*refs):
        p_refs = (refs[0:n], refs[n:2 * n])
        w_refs, m_refs, v_refs = refs[2 * n:3 * n], refs[3 * n:4 * n], refs[4 * n:5 * n]
        outs = refs[5 * n:]
        for k in range(n):
            g_ref, d_ref, mo_ref, vo_ref = outs[4 * k:4 * k + 4]
            for l in range(2):
                at = (slice(l, l + 1),) if len(w_refs[k].shape) == 2 else (l,)
                g = p_refs[l][k][0]
                for s in range(1, N_DEV):
                    g = g + p_refs[l][k][s]
                delta, mn, vn = _adamw_update(g, w_refs[k][at], m_refs[k][at], v_refs[k][at])
                g_ref[at] = g
                d_ref[at] = delta
                mo_ref[at] = mn
                vo_ref[at] = vn

    vmem = pl.BlockSpec(memory_space=pltpu.VMEM)
    res = pl.pallas_call(
        body, name="adamw_replicated", in_specs=[vmem] * (5 * n), out_specs=[vmem] * (4 * n),
        out_shape=[SDS(a.shape, F32) for a in w for _ in range(4)],
        compiler_params=pltpu.CompilerParams(vmem_limit_bytes=VMEM_LIMIT),
    )(*parts[0], *parts[1], *w, *m, *v)
    return [res[4 * k:4 * k + 4] for k in range(n)]


def _adamw_sum(parts0, parts1, w, m, v, name):
    _, R, C = w.shape
    tr = R
    while tr * C > 256 * 1024 and tr % 32 == 0:
        tr //= 2

    def body(p0_ref, p1_ref, w_ref, m_ref, v_ref, g_ref, d_ref, mo_ref, vo_ref):
        def update(p_ref):
            g = p_ref[0].astype(F32)
            for s in range(1, N_DEV):
                g = g + p_ref[s].astype(F32)
            g_ref[...] = g
            d_ref[...], mo_ref[...], vo_ref[...] = _adamw_update(g, w_ref[...], m_ref[...], v_ref[...])

        @pl.when(pl.program_id(0) == 0)
        def _():
            update(p0_ref)

        @pl.when(pl.program_id(0) == 1)
        def _():
            update(p1_ref)

    blk = pl.BlockSpec((None, tr, C), lambda l, i: (l, i, 0))
    return pl.pallas_call(
        body, name=name, grid=(2, R // tr),
        in_specs=[pl.BlockSpec((N_DEV, tr, C), lambda l, i: (0, i * (1 - l), 0)),
                  pl.BlockSpec((N_DEV, tr, C), lambda l, i: (0, i * l, 0)), blk, blk, blk],
        out_specs=[blk, blk, blk, blk],
        out_shape=[SDS((2, R, C), F32)] * 4,
        compiler_params=_cparams(("arbitrary", "arbitrary")),
    )(parts0, parts1, w, m, v)


def _layer_fwd(x, P, skew, S, rest, ex, tgt=None):
    z, ht, *got0 = _in_proj(x, P["pre_g"], P["w_in_t"], ex[0])
    P = {**P, **rest(got0)}
    ac, ap, u1, u0, pooled, *got1 = _mix_fwd(z, P["conv_dw"], P["conv_dw_b"], P["conv_ln_g"], P["conv_ln_b"],
                                 P["pool_w"], P["pool_b"], P["pool_scale"], S, ex[1])
    o, *got2 = _attn_fwd(z, skew, S, ex[2])
    out, kept, got3 = _out_fwd(x, ac, o, ap, z, P["w_conv_out"], P["w_attn_out"], P["w_pool_out"], P["w_out"],
                               P["post_g"], ex[3], tgt)
    return out, (x, z, ht, ac, o, ap, u1, u0, pooled, kept), P, (got0, got1, got2, got3)


def _layer_bwd(dout, saved, P, skew, S, ex=(None, None), mix_ex=None, win_ex=None):
    x, z, ht, ac, o, ap, u1, u0, pooled, kept = saved
    (dac, dao, dag, dap, dgm, dwco, dwao, dwpo, dwout, dpostg, *got0) = _out_bwd(
        dout, ac, o, ap, z, kept, P["w_conv_out"], P["w_attn_out"], P["w_pool_out"], P["w_out"], P["post_g"], ex[0])
    grads = dict(post_norm_g=dpostg, w_conv_out=dwco, w_attn_out=dwao, w_pool_out=dwpo, w_out=dwout)
    dq, dk, dv, grads["dskew"], *got1 = _attn_bwd(z, dao, skew, S, ex[1])
    (dzc, dzp, grads["conv_dw"], grads["conv_dw_b"], grads["conv_ln_g"], grads["conv_ln_b"], grads["pool_w"],
     grads["pool_b"], grads["pool_scale"], *got2) = _mix_bwd(
        z, u1, u0, pooled, dac, dap, P["conv_dw"], P["conv_dw_b"], P["conv_ln_g"], P["conv_ln_b"],
        P["pool_w"], P["pool_b"], P["pool_scale"], S, mix_ex(grads) if mix_ex else None)
    pieces = [dzc, dq, dk, dv, dag, dzp, dgm]
    buf, row0 = None, 0
    for group in ([dzc], [dq, dk, dv, dag], [dzp], [dgm]):
        (buf,) = _in_bwd_w(ht, group, row0, buf)
        row0 += sum(p.shape[1] for p in group)
    grads["w_in_t"] = buf
    dx, grads["pre_norm_g"], *got3 = _in_bwd_x(pieces, P["w_in_t"], x, P["pre_g"], dout,
                                               win_ex(grads) if win_ex else None)
    return dx, grads, (got0, got1, got2, got3)


WEIGHT_NAMES = ("pre_norm_g", "post_norm_g", "w_in", "conv_dw", "conv_dw_b", "conv_ln_g", "conv_ln_b",
                "w_conv_out", "rel_bias", "w_attn_out", "pool_w", "pool_b", "pool_scale", "w_pool_out", "w_out")
SHARDED = ("w_in", "w_conv_out", "w_attn_out", "w_pool_out", "w_out", "conv_dw")
OUT_PROJ = ("w_conv_out", "w_attn_out", "w_pool_out", "w_out")
REST = tuple(n for n in WEIGHT_NAMES if n not in ("w_in", "pre_norm_g"))


def _cols_from_slabs(g):
    return g.transpose(1, 0, 2).reshape(g.shape[1], N_DEV * g.shape[2])


def _slabs_from_cols(full):
    r, wd = full.shape
    return full.reshape(r, N_DEV, wd // N_DEV).transpose(1, 0, 2)


def _rest_shards(weights, l):
    return [weights["w_conv_out"][l].astype(BF16), weights["w_attn_out"][l].astype(BF16),
            weights["w_pool_out"][l].astype(BF16), weights["w_out"][l].astype(BF16), weights["conv_dw"][l]]


def _rest_weights(got):
    wco, wao, wpo, wout, cdw = got
    return dict(w_conv_out=_cols_from_slabs(wco), w_attn_out=_cols_from_slabs(wao),
                w_pool_out=_cols_from_slabs(wpo), w_out=wout.reshape(D, D), conv_dw=_cols_from_slabs(cdw))


def _grad_arrays(g, names):
    make = {"w_in": lambda: g["w_in_t"].reshape(N_DEV, NCOL // N_DEV, D),
            "w_conv_out": lambda: _slabs_from_cols(g["w_conv_out"].astype(BF16)),
            "w_attn_out": lambda: _slabs_from_cols(g["w_attn_out"].astype(BF16)),
            "w_pool_out": lambda: _slabs_from_cols(g["w_pool_out"].astype(BF16)),
            "w_out": lambda: g["w_out"].astype(BF16).reshape(N_DEV, D // N_DEV, D),
            "conv_dw": lambda: _slabs_from_cols(g["conv_dw"].astype(BF16)),
            "rel_bias": lambda: jnp.dot(g["dskew"], jnp.asarray(_skew_select().T), precision=lax.Precision.HIGHEST),
            "pool_b": lambda: g["pool_b"].reshape(4, 128)}
    return [make[n]() if n in make else g[n] for n in names]


def _grad_exchange(g, names):
    return _Exchange(_grad_arrays(g, names), [n in SHARDED for n in names])


def kernel(x, pre_norm_g, post_norm_g, w_in, conv_dw, conv_dw_b, conv_ln_g, conv_ln_b, w_conv_out, rel_bias, w_attn_out, pool_w, pool_b, pool_scale, w_pool_out, w_out, loss_target, m_pre_norm_g, m_post_norm_g, m_w_in, m_conv_dw, m_conv_dw_b, m_conv_ln_g, m_conv_ln_b, m_w_conv_out, m_rel_bias, m_w_attn_out, m_pool_w, m_pool_b, m_pool_scale, m_w_pool_out, m_w_out, v_pre_norm_g, v_post_norm_g, v_w_in, v_conv_dw, v_conv_dw_b, v_conv_ln_g, v_conv_ln_b, v_w_conv_out, v_rel_bias, v_w_attn_out, v_pool_w, v_pool_b, v_pool_scale, v_w_pool_out, v_w_out):
    given = dict(locals())
    weights = {n: given[n] for n in WEIGHT_NAMES}
    nb, S, _ = x.shape
    T = nb * S
    L = pre_norm_g.shape[0]
    assert L == 2
    x2 = x.reshape(T, D)
    tgt2 = loss_target.reshape(T, D)
    skews = [_skew_table(rel_bias[l]) for l in range(L)]

    def local_params(l):
        return dict(pre_g=pre_norm_g[l:l + 1], post_g=post_norm_g[l:l + 1], conv_dw_b=conv_dw_b[l:l + 1],
                    conv_ln_g=conv_ln_g[l:l + 1], conv_ln_b=conv_ln_b[l:l + 1], pool_w=pool_w[l].astype(BF16),
                    pool_b=pool_b[l].reshape(1, BW), pool_scale=pool_scale[l:l + 1])

    win0 = w_in[0].T.astype(BF16)
    win1 = w_in[1].T.astype(BF16)
    half = win1.shape[0] // 2
    w_in_t0 = _gather_two_level(win0, "gather_w_in_0")
    gather = lambda arrays: _Exchange(arrays, [False] * len(arrays))
    (h,), saved0, P0, (got_rest0, got_a, got_b, got_rest1) = _layer_fwd(
        x2, {**local_params(0), "w_in_t": w_in_t0.reshape(NCOL, D)}, skews[0], S, _rest_weights,
        (gather(_rest_shards(weights, 0)), gather([win1[:half]]), gather([win1[half:]]),
         gather(_rest_shards(weights, 1))))
    w_in_t1 = jnp.concatenate([got_a[0], got_b[0]], axis=1).reshape(NCOL, D)
    (dout, lsum), saved1, P1, _ = _layer_fwd(h, {**local_params(1), "w_in_t": w_in_t1}, skews[1], S,
                                             lambda _: _rest_weights(got_rest1), (None,) * 4, tgt2)

    dout, g1, _ = _layer_bwd(dout, saved1, P1, skews[1], S)
    g1["loss"] = lsum
    others = REST + ("pre_norm_g",)
    late = ("w_in",) + tuple(n for n in REST if n not in OUT_PROJ)
    dout, g0, (got_others1, got_win1, got_outp0, got_late0) = _layer_bwd(
        dout, saved0, P0, skews[0], S, (_grad_exchange(g1, others + ("loss",)), _grad_exchange(g1, ("w_in",))),
        lambda g: _grad_exchange(g, OUT_PROJ), lambda g: _grad_exchange(g, late))
    (got_pre0,) = _exchange([g0["pre_norm_g"]], [False], "gather_grad_pre_norm_g_0")
    parts = [{"pre_norm_g": got_pre0, **dict(zip(OUT_PROJ, got_outp0)), **dict(zip(late, got_late0))},
             {"w_in": got_win1[0], **dict(zip(others + ("loss",), got_others1))}]
    loss = jnp.sum(parts[1].pop("loss")[:, 0, 0]) * (0.5 / D)
    grad_x = dout.reshape(x.shape)

    outs = {}
    small = [n for n in WEIGHT_NAMES if n not in SHARDED]
    res = _adamw_small([[parts[l][n] for n in small] for l in range(L)], [weights[n] for n in small],
                       [given["m_" + n] for n in small], [given["v_" + n] for n in small])
    outs.update(zip(small, res))
    for n in SHARDED:
        view = (lambda a: a.transpose(0, 2, 1)) if n == "w_in" else (lambda a: a)
        res = _adamw_sum(parts[0][n], parts[1][n], view(weights[n]), view(given["m_" + n]), view(given["v_" + n]),
                         "adamw_" + n)
        outs[n] = [view(a) for a in res]
    return (loss, grad_x, *[outs[n][0] for n in WEIGHT_NAMES], *[outs[n][1] for n in WEIGHT_NAMES],
            *[outs[n][2] for n in WEIGHT_NAMES], *[outs[n][3] for n in WEIGHT_NAMES])
```
